```python
import jax, jax.numpy as jnp
from jax import lax
import numpy as np

D_MODEL = 1024
BATCH = 32
SEQ = 256
DEPTH = 2
DEC_BATCH = 2
DEC_SEQ = 1024
PAST_LEN = 256

GRID_W = 64
RET_HEADS = 8
RET_DK = 64
RET_DV = 64
RET_QK = RET_HEADS * RET_DK
RET_WIDTH = RET_HEADS * RET_DV
RET_CHUNK = 128
RET_DECAY_OFFSET = 5.0
MLA_HEADS = 8
MLA_NOPE = 64
MLA_ROPE = 32
MLA_V = 64
MLA_WIDTH = MLA_HEADS * MLA_V
Q_RANK = 256
KV_RANK = 128
D_MIX = RET_WIDTH + MLA_WIDTH
IN_COLS = 2 * RET_QK + 2 * RET_WIDTH + Q_RANK + KV_RANK + MLA_ROPE
D_FF = ((8 * D_MODEL // 3 + 255) // 256) * 256
N_MOD = 6
ROPE_BASE = 10000.0
ATTN_BLOCK = 128
EPS = 1e-6

kernel_name = 'hybrid_retention_mla_diffusion_step'


def rms_norm(x, g):
    xf = x.astype(jnp.float32)
    y = xf * lax.rsqrt(jnp.mean(xf * xf, axis=-1, keepdims=True) + EPS)
    return (y * g.astype(jnp.float32)).astype(x.dtype)


def heads(t, n_heads):
    b, n, _ = t.shape
    return t.reshape(b, n, n_heads, -1).transpose(0, 2, 1, 3)


def axial_rope(x, row, col):
    d = x.shape[-1]
    half = d // 2
    nf = half // 2
    inv = ROPE_BASE ** (-jnp.arange(nf, dtype=jnp.float32) / nf)

    def rot(xa, pos):
        ang = pos.astype(jnp.float32)[:, None] * inv
        cos, sin = jnp.cos(ang), jnp.sin(ang)
        x1, x2 = xa[..., :nf], xa[..., nf:]
        return jnp.concatenate([x1 * cos - x2 * sin, x1 * sin + x2 * cos], axis=-1)

    xf = x.astype(jnp.float32)
    out = jnp.concatenate([rot(xf[..., :half], row), rot(xf[..., half:], col)], axis=-1)
    return out.astype(x.dtype)


def modulation(cond, w_mod, b_mod):
    m = jax.nn.silu(cond) @ w_mod + b_mod
    return jnp.split(m[..., None, :], N_MOD, axis=-1)


def project(h, w_in, g_q_a, w_q_b, g_kv_a, g_qn, g_qr, g_kr):
    z = h @ w_in
    i1 = RET_QK
    i2 = i1 + RET_QK
    i3 = i2 + RET_WIDTH
    i4 = i3 + RET_WIDTH
    i5 = i4 + Q_RANK
    i6 = i5 + KV_RANK
    rq, rk, rv, rg, qa, kva, kr = jnp.split(z, [i1, i2, i3, i4, i5, i6], axis=-1)
    q = heads(rms_norm(qa, g_q_a) @ w_q_b, MLA_HEADS)
    qn = rms_norm(q[..., :MLA_NOPE], g_qn)
    qr = rms_norm(q[..., MLA_NOPE:], g_qr)
    ckv = rms_norm(kva, g_kv_a)
    kr = rms_norm(kr, g_kr)
    return (heads(rq, RET_HEADS), heads(rk, RET_HEADS) * (RET_DK ** -0.5), heads(rv, RET_HEADS),
            rg, qn, qr, ckv, kr)


def mla_up(ckv, w_kv_b, g_kn):
    kv = heads(ckv @ w_kv_b, MLA_HEADS)
    return rms_norm(kv[..., :MLA_NOPE], g_kn), kv[..., MLA_NOPE:]


def log_gamma(p):
    return jnp.log1p(-jnp.exp2(-p.astype(jnp.float32)))


def retention_chunkwise(q, k, v, lg, s0):
    b, h, n, _ = q.shape
    c = RET_CHUNK
    nc = n // c
    j = jnp.arange(c, dtype=jnp.float32)
    diff = j[:, None] - j[None, :]
    decay_intra = jnp.where(diff >= 0, jnp.exp(jnp.maximum(diff, 0.0)[None] * lg[:, None, None]), 0.0)
    q_decay = jnp.exp((j + 1.0)[None, :] * lg[:, None])[..., None]
    k_decay = jnp.exp((c - 1.0 - j)[None, :] * lg[:, None])[..., None]
    chunk_decay = jnp.exp(c * lg)[:, None, None]

    def to_chunks(t):
        return jnp.moveaxis(t.astype(jnp.float32).reshape(b, h, nc, c, t.shape[-1]), 2, 0)

    def step(s, inp):
        qi, ki, vi = inp
        scores = jnp.einsum('bhid,bhjd->bhij', qi, ki) * decay_intra
        inner = jnp.einsum('bhij,bhjv->bhiv', scores, vi)
        cross = jnp.einsum('bhid,bhdv->bhiv', qi, s) * q_decay
        s_new = s * chunk_decay + jnp.einsum('bhjd,bhjv->bhdv', ki * k_decay, vi)
        return s_new, inner + cross

    s_fin, out = lax.scan(step, s0.astype(jnp.float32), (to_chunks(q), to_chunks(k), to_chunks(v)))
    return jnp.moveaxis(out, 0, 2).reshape(b, h, n, -1), s_fin


def bidir_retention(q, k, v, p_fwd, p_bwd, s_f0, s_b0):
    o_f, s_f = retention_chunkwise(q, k, v, log_gamma(p_fwd), s_f0)
    flip = lambda t: jnp.flip(t, axis=2)
    o_b, s_b = retention_chunkwise(flip(q), flip(k), flip(v), log_gamma(p_bwd), s_b0)
    return o_f + flip(o_b), s_f, s_b


def head_group_norm(o, g, bias):
    mu = jnp.mean(o, axis=-1, keepdims=True)
    var = jnp.mean(jnp.square(o - mu), axis=-1, keepdims=True)
    y = (o - mu) * lax.rsqrt(var + EPS)
    b, h, n, d = y.shape
    y = y.transpose(0, 2, 1, 3).reshape(b, n, h * d)
    return y * g.astype(jnp.float32) + bias.astype(jnp.float32)


def block_attention(qn, qr, kn, kr, v):
    b, h, nq, _ = qn.shape
    nb = nq // ATTN_BLOCK
    scale = (MLA_NOPE + MLA_ROPE) ** -0.5

    def blk(qs):
        qnb, qrb = qs
        s = (jnp.einsum('bhqd,bhkd->bhqk', qnb, kn) + jnp.einsum('bhqd,bkd->bhqk', qrb, kr)).astype(jnp.float32) * scale
        p = jax.nn.softmax(s, axis=-1)
        return jnp.einsum('bhqk,bhkv->bhqv', p.astype(v.dtype), v)

    to_blocks = lambda t: jnp.moveaxis(t.reshape(b, h, nb, ATTN_BLOCK, t.shape[-1]), 2, 0)
    out = lax.map(blk, (to_blocks(qn), to_blocks(qr)))
    return jnp.moveaxis(out, 0, 2).reshape(b, h, nq, -1)


def trunk_layer(x, cond, grid, ctx, w_mod, b_mod, g_norm_mix, g_norm_ffn, w_in, g_q_a, w_q_b,
                g_kv_a, w_kv_b, g_qn, g_qr, g_kn, g_kr, ret_p_fwd, ret_p_bwd, g_ret_gn, b_ret_gn,
                w_o, w_ffn_in, w_ffn_out):
    sh1, sc1, gt1, sh2, sc2, gt2 = modulation(cond, w_mod, b_mod)
    h = rms_norm(x, g_norm_mix) * (1.0 + sc1) + sh1
    rq, rk, rv, rg, qn, qr, ckv, kr = project(h, w_in, g_q_a, w_q_b, g_kv_a, g_qn, g_qr, g_kr)
    kn, v = mla_up(ckv, w_kv_b, g_kn)
    if grid is None:
        zeros = jnp.zeros((x.shape[0], RET_HEADS, RET_DK, RET_DV), jnp.float32)
        s_f0, s_b0 = zeros, zeros
        kr_keys = kr
    else:
        row, col = grid
        rq, rk, qr = axial_rope(rq, row, col), axial_rope(rk, row, col), axial_rope(qr, row, col)
        ckv_c, kr_c, s_f0, s_b0 = ctx
        kn_c, v_c = mla_up(ckv_c, w_kv_b, g_kn)
        kn = jnp.concatenate([kn, kn_c], axis=2)
        v = jnp.concatenate([v, v_c], axis=2)
        kr_keys = jnp.concatenate([axial_rope(kr, row, col), kr_c], axis=1)
    ret, s_f, s_b = bidir_retention(rq, rk, rv, ret_p_fwd, ret_p_bwd, s_f0, s_b0)
    attn = block_attention(qn, qr, kn, kr_keys, v)
    b, n, _ = x.shape
    ret = head_group_norm(ret, g_ret_gn, b_ret_gn).astype(x.dtype) * jax.nn.silu(rg)
    attn = attn.transpose(0, 2, 1, 3).reshape(b, n, MLA_WIDTH)
    x = x + gt1 * (jnp.concatenate([ret, attn], axis=-1) @ w_o)
    h = rms_norm(x, g_norm_ffn) * (1.0 + sc2) + sh2
    gate, up = jnp.split(h @ w_ffn_in, 2, axis=-1)
    x = x + gt2 * ((jax.nn.silu(gate) * up) @ w_ffn_out)
    return x, (ckv, kr, s_f.astype(x.dtype), s_b.astype(x.dtype))


def setup_inputs(seed: int = 0) -> dict:
    key = jax.random.key(seed)
    keys = jax.random.split(key, 32)

    def nrm(i, shape, scale):
        return jax.random.normal(keys[i], shape, jnp.float32) * scale

    def gain(i, shape):
        return 1.0 + 0.05 * jax.random.normal(keys[i], shape, jnp.float32)

    decay_base = RET_DECAY_OFFSET + jnp.arange(RET_HEADS, dtype=jnp.float32)
    st_shape = (DEC_BATCH, DEPTH, RET_HEADS, RET_DK, RET_DV)
    return {
        'x_prompt': nrm(0, (BATCH, SEQ, D_MODEL), 1.0),
        'x_sample': nrm(1, (DEC_BATCH, DEC_SEQ, D_MODEL), 1.0),
        'cache_ckv': nrm(2, (DEC_BATCH, DEPTH, PAST_LEN, KV_RANK), 1.0),
        'cache_krope': nrm(3, (DEC_BATCH, DEPTH, PAST_LEN, MLA_ROPE), 1.0),
        'state_ret_fwd': nrm(4, st_shape, 0.5),
        'state_ret_bwd': nrm(5, st_shape, 0.5),
        'c': nrm(6, (DEC_BATCH, D_MODEL), 1.0),
        'c_ctx': nrm(7, (D_MODEL,), 1.0),
        'w_mod': nrm(8, (DEPTH, D_MODEL, N_MOD * D_MODEL), 0.5 * D_MODEL ** -0.5),
        'b_mod': nrm(9, (DEPTH, N_MOD * D_MODEL), 0.02),
        'g_norm_mix': gain(10, (DEPTH, D_MODEL)),
        'g_norm_ffn': gain(11, (DEPTH, D_MODEL)),
        'w_in': nrm(12, (DEPTH, D_MODEL, IN_COLS), D_MODEL ** -0.5),
        'g_q_a': gain(13, (DEPTH, Q_RANK)),
        'w_q_b': nrm(14, (DEPTH, Q_RANK, MLA_HEADS * (MLA_NOPE + MLA_ROPE)), Q_RANK ** -0.5),
        'g_kv_a': gain(15, (DEPTH, KV_RANK)),
        'w_kv_b': nrm(16, (DEPTH, KV_RANK, MLA_HEADS * (MLA_NOPE + MLA_V)), KV_RANK ** -0.5),
        'g_qn': gain(17, (DEPTH, MLA_NOPE)),
        'g_qr': gain(18, (DEPTH, MLA_ROPE)),
        'g_kn': gain(19, (DEPTH, MLA_NOPE)),
        'g_kr': gain(20, (DEPTH, MLA_ROPE)),
        'ret_p_fwd': decay_base + nrm(21, (DEPTH, RET_HEADS), 0.1),
        'ret_p_bwd': decay_base + nrm(22, (DEPTH, RET_HEADS), 0.1),
        'g_ret_gn': gain(23, (DEPTH, RET_WIDTH)),
        'b_ret_gn': nrm(24, (DEPTH, RET_WIDTH), 0.02),
        'w_o': nrm(25, (DEPTH, D_MIX, D_MODEL), D_MIX ** -0.5),
        'w_ffn_in': nrm(26, (DEPTH, D_MODEL, 2 * D_FF), D_MODEL ** -0.5),
        'w_ffn_out': nrm(27, (DEPTH, D_FF, D_MODEL), D_FF ** -0.5),
    }


def reference(x_prompt, x_sample, cache_ckv, cache_krope, state_ret_fwd, state_ret_bwd, c, c_ctx,
              w_mod, b_mod, g_norm_mix, g_norm_ffn, w_in, g_q_a, w_q_b, g_kv_a, w_kv_b,
              g_qn, g_qr, g_kn, g_kr, ret_p_fwd, ret_p_bwd, g_ret_gn, b_ret_gn, w_o,
              w_ffn_in, w_ffn_out):
    def layer_params(l):
        return (w_mod[l], b_mod[l], g_norm_mix[l], g_norm_ffn[l], w_in[l], g_q_a[l], w_q_b[l],
                g_kv_a[l], w_kv_b[l], g_qn[l], g_qr[l], g_kn[l], g_kr[l], ret_p_fwd[l],
                ret_p_bwd[l], g_ret_gn[l], b_ret_gn[l], w_o[l], w_ffn_in[l], w_ffn_out[l])

    x = x_prompt
    ckv_l, kr_l, sf_l, sb_l = [], [], [], []
    for l in range(DEPTH):
        x, (ckv, kr, s_f, s_b) = trunk_layer(x, c_ctx, None, None, *layer_params(l))
        ckv_l.append(ckv)
        kr_l.append(kr)
        sf_l.append(s_f)
        sb_l.append(s_b)
    y_prompt = x

    n_lat = x_sample.shape[1]
    rows = n_lat // GRID_W
    grid = (jnp.repeat(jnp.arange(rows), GRID_W), jnp.tile(jnp.arange(GRID_W), rows))
    y = x_sample
    for l in range(DEPTH):
        ctx = (cache_ckv[:, l], cache_krope[:, l], state_ret_fwd[:, l], state_ret_bwd[:, l])
        y, _ = trunk_layer(y, c, grid, ctx, *layer_params(l))

    new_ckv = jnp.stack(ckv_l, axis=1)
    new_krope = jnp.stack(kr_l, axis=1)
    new_ret_fwd = jnp.stack(sf_l, axis=1)
    new_ret_bwd = jnp.stack(sb_l, axis=1)
    return (y_prompt, y, new_ckv, new_krope, new_ret_fwd, new_ret_bwd)
```

```python
import functools

import jax
import jax.numpy as jnp
from jax import lax
from jax.experimental import pallas as pl
from jax.experimental.pallas import tpu as pltpu

D_MODEL = 1024
N_MOD = 6
RET_HEADS = 8
RET_DK = 64
RET_WIDTH = 512
RET_CHUNK = 128
MLA_HEADS = 8
MLA_NOPE = 64
MLA_ROPE = 32
MLA_V = 64
Q_RANK = 256
KV_RANK = 128
D_FF = 2816
GRID_W = 64
ROPE_BASE = 10000.0
EPS = 1e-6

LANES = 128
HEAD_PAD = LANES
N_PAIRS = RET_HEADS // 2
IN_COLS_PAD = 4 * RET_WIDTH + Q_RANK + KV_RANK + LANES
ROW_TILE = 256
FF_CHUNK = 512
VMEM_LIMIT = 56 * 1024 * 1024

BF16 = jnp.bfloat16
F32 = jnp.float32
_NT = (((1,), (1,)), ((), ()))


def _dot(a, b):
    return jnp.dot(a, b, preferred_element_type=F32)


def _dot_nt(a, b):
    return lax.dot_general(a, b, _NT, preferred_element_type=F32)


def _rms(x, g):
    return x * lax.rsqrt(jnp.mean(x * x, axis=-1, keepdims=True) + EPS) * g


def _silu(x):
    return x * jax.nn.sigmoid(x)


def _masked_mean_sq(x, mask, n):
    return jnp.sum(jnp.where(mask, x * x, 0.0), axis=-1, keepdims=True) * (1.0 / n)


def _rope(x, c, sa, sb, shift):
    return x * c + pltpu.roll(x, shift, 1) * sa + pltpu.roll(x, LANES - shift, 1) * sb


def _mod_kernel(c_ref, w_ref, b_ref, o_ref):
    a = _silu(c_ref[...])
    w = w_ref[...]
    a_hi = a.astype(BF16)
    a_lo = (a - a_hi.astype(F32)).astype(BF16)
    w_hi = w.astype(BF16)
    w_lo = (w - w_hi.astype(F32)).astype(BF16)
    o_ref[...] = _dot(a_hi, w_hi) + _dot(a_lo, w_hi) + _dot(a_hi, w_lo) + b_ref[...]


def _modulation(conds, w_mod, b_mod):
    depth, _, n = w_mod.shape
    tn = 1536
    return pl.pallas_call(
        _mod_kernel,
        grid=(depth, n // tn),
        in_specs=[
            pl.BlockSpec((8, D_MODEL), lambda l, j: (0, 0)),
            pl.BlockSpec((None, D_MODEL, tn), lambda l, j: (l, 0, j)),
            pl.BlockSpec((None, 1, tn), lambda l, j: (l, 0, j)),
        ],
        out_specs=pl.BlockSpec((None, 8, tn), lambda l, j: (l, 0, j)),
        out_shape=jax.ShapeDtypeStruct((depth, 8, n), F32),
        compiler_params=pltpu.CompilerParams(
            dimension_semantics=("arbitrary", "arbitrary"), vmem_limit_bytes=VMEM_LIMIT),
        name="modulation",
    )(conds, w_mod, b_mod.reshape(depth, 1, n))


def _norm_kn(kv, gkn):
    lane = lax.broadcasted_iota(jnp.int32, (kv.shape[0], LANES), 1)
    lo = lane < MLA_NOPE
    out = []
    for h in range(MLA_HEADS):
        kvh = kv[:, h * HEAD_PAD:(h + 1) * HEAD_PAD]
        rs = lax.rsqrt(_masked_mean_sq(kvh, lo, MLA_NOPE) + EPS)
        out.append((kvh * jnp.where(lo, rs * gkn, 1.0)).astype(BF16))
    return out


def _cache_kernel(ckv_ref, wkvb_ref, gkn_ref, kvn_ref):
    kv = _dot(ckv_ref[...].astype(BF16), wkvb_ref[...])
    for h, kvh in enumerate(_norm_kn(kv, gkn_ref[...])):
        kvn_ref[:, h * HEAD_PAD:(h + 1) * HEAD_PAD] = kvh


def _cache_up(cache_ckv, wkvb, gkn):
    nb, depth, past, _ = cache_ckv.shape
    width = MLA_HEADS * HEAD_PAD
    return pl.pallas_call(
        _cache_kernel,
        grid=(depth, nb),
        in_specs=[
            pl.BlockSpec((None, None, past, KV_RANK), lambda l, b: (b, l, 0, 0)),
            pl.BlockSpec((None, KV_RANK, width), lambda l, b: (l, 0, 0)),
            pl.BlockSpec((None, 1, LANES), lambda l, b: (l, 0, 0)),
        ],
        out_specs=pl.BlockSpec((None, None, past, width), lambda l, b: (l, b, 0, 0)),
        out_shape=jax.ShapeDtypeStruct((depth, nb, past, width), BF16),
        compiler_params=pltpu.CompilerParams(
            dimension_semantics=("arbitrary", "arbitrary"), vmem_limit_bytes=VMEM_LIMIT),
        name="cache_up",
    )(cache_ckv, wkvb, gkn)


def _proj_kernel(*refs, rope):
    (x_ref, mod_ref, gmix_ref, win_ref, gqa_ref, wqb_ref, gkva_ref, wkvb_ref,
     gq_ref, gkn_ref, gkr_ref) = refs[:11]
    if rope:
        c64_ref, sa64_ref, sb64_ref, c32_ref, sa32_ref, sb32_ref = refs[11:17]
        rq_ref, rk_ref, rv_ref, rg_ref, qcat_ref, kvn_ref, krp_ref = refs[17:]
    else:
        rq_ref, rk_ref, rv_ref, rg_ref, qcat_ref, kvn_ref, krp_ref, ckv_ref, kro_ref = refs[11:]

    x = x_ref[...]
    sh1 = mod_ref[0:1, :]
    sc1 = mod_ref[1:2, :]
    h = (_rms(x, gmix_ref[...]) * (1.0 + sc1) + sh1).astype(BF16)
    z = _dot(h, win_ref[...])

    w = RET_WIDTH
    for j in range(w // LANES):
        sl = slice(j * LANES, (j + 1) * LANES)
        q = z[:, j * LANES:(j + 1) * LANES]
        k = z[:, w + j * LANES:w + (j + 1) * LANES] * (RET_DK ** -0.5)
        if rope:
            q = _rope(q, c64_ref[...], sa64_ref[...], sb64_ref[...], 16)
            k = _rope(k, c64_ref[...], sa64_ref[...], sb64_ref[...], 16)
        rq_ref[:, sl] = q.astype(BF16)
        rk_ref[:, sl] = k.astype(BF16)
    rv_ref[...] = z[:, 2 * w:3 * w].astype(BF16)
    rg_ref[...] = z[:, 3 * w:4 * w]

    o = 4 * w
    qa = z[:, o:o + Q_RANK]
    kva = z[:, o + Q_RANK:o + Q_RANK + KV_RANK]
    kr2 = z[:, o + Q_RANK + KV_RANK:]

    lane = lax.broadcasted_iota(jnp.int32, (x.shape[0], LANES), 1)
    nope = lane < MLA_NOPE

    q = _dot(_rms(qa, gqa_ref[...]).astype(BF16), wqb_ref[...])
    is_rope = jnp.logical_and(lane >= MLA_NOPE, lane < MLA_NOPE + MLA_ROPE)
    for hh in range(MLA_HEADS):
        qh = q[:, hh * HEAD_PAD:(hh + 1) * HEAD_PAD]
        rs_n = lax.rsqrt(_masked_mean_sq(qh, nope, MLA_NOPE) + EPS)
        rs_r = lax.rsqrt(_masked_mean_sq(qh, is_rope, MLA_ROPE) + EPS)
        qn = qh * jnp.where(nope, rs_n, rs_r) * gq_ref[...]
        if rope:
            qn = _rope(qn, c32_ref[...], sa32_ref[...], sb32_ref[...], 8)
        qcat_ref[:, hh * HEAD_PAD:(hh + 1) * HEAD_PAD] = qn.astype(BF16)

    ckv = _rms(kva, gkva_ref[...])
    kv = _dot(ckv.astype(BF16), wkvb_ref[...])
    for hh, kvh in enumerate(_norm_kn(kv, gkn_ref[...])):
        kvn_ref[:, hh * HEAD_PAD:(hh + 1) * HEAD_PAD] = kvh

    krn = kr2 * lax.rsqrt(_masked_mean_sq(kr2, lane < MLA_ROPE, MLA_ROPE) + EPS) * gkr_ref[...]
    if rope:
        krn = _rope(krn, c32_ref[...], sa32_ref[...], sb32_ref[...], 8)
    else:
        ckv_ref[...] = ckv
        kro_ref[...] = krn[:, 0:MLA_ROPE]
    krp_ref[...] = krn.astype(BF16)


def _proj(x, mod, layer, cond_row, wts, rope_tabs):
    m = x.shape[0]
    tm = ROW_TILE
    rope = rope_tabs is not None
    width = MLA_HEADS * HEAD_PAD

    def row(i):
        return (i, 0)

    def const2(i):
        return (0, 0)

    def lay3(i):
        return (layer, 0, 0)

    in_specs = [
        pl.BlockSpec((tm, D_MODEL), row),
        pl.BlockSpec((None, None, N_MOD, D_MODEL), lambda i: (layer, cond_row(i), 0, 0)),
        pl.BlockSpec((None, 1, D_MODEL), lay3),
        pl.BlockSpec((None, D_MODEL, IN_COLS_PAD), lay3),
        pl.BlockSpec((None, 1, Q_RANK), lay3),
        pl.BlockSpec((None, Q_RANK, width), lay3),
        pl.BlockSpec((None, 1, KV_RANK), lay3),
        pl.BlockSpec((None, KV_RANK, width), lay3),
        pl.BlockSpec((None, 1, LANES), lay3),
        pl.BlockSpec((None, 1, LANES), lay3),
        pl.BlockSpec((None, 1, LANES), lay3),
    ]
    args = [x, mod, wts["g_mix"], wts["w_in"], wts["g_qa"], wts["w_qb"], wts["g_kva"],
            wts["w_kvb"], wts["g_q"], wts["g_kn"], wts["g_kr"]]
    out_shape = [
        jax.ShapeDtypeStruct((m, RET_WIDTH), BF16),
        jax.ShapeDtypeStruct((m, RET_WIDTH), BF16),
        jax.ShapeDtypeStruct((m, RET_WIDTH), BF16),
        jax.ShapeDtypeStruct((m, RET_WIDTH), F32),
        jax.ShapeDtypeStruct((m, width), BF16),
        jax.ShapeDtypeStruct((m, width), BF16),
        jax.ShapeDtypeStruct((m, LANES), BF16),
    ]
    out_specs = [
        pl.BlockSpec((tm, RET_WIDTH), row), pl.BlockSpec((tm, RET_WIDTH), row),
        pl.BlockSpec((tm, RET_WIDTH), row), pl.BlockSpec((tm, RET_WIDTH), row),
        pl.BlockSpec((tm, width), row), pl.BlockSpec((tm, width), row),
        pl.BlockSpec((tm, LANES), row),
    ]
    if rope:
        n_lat = rope_tabs[0].shape[0]
        tiles = n_lat // tm
        in_specs += [pl.BlockSpec((tm, LANES), lambda i: (i % tiles, 0))] * 6
        args += list(rope_tabs)
    else:
        out_shape += [jax.ShapeDtypeStruct((m, KV_RANK), F32),
                      jax.ShapeDtypeStruct((m, MLA_ROPE), F32)]
        out_specs += [pl.BlockSpec((tm, KV_RANK), row), pl.BlockSpec((tm, MLA_ROPE), row)]

    return pl.pallas_call(
        functools.partial(_proj_kernel, rope=rope),
        grid=(m // tm,),
        in_specs=in_specs,
        out_specs=out_specs,
        out_shape=out_shape,
        compiler_params=pltpu.CompilerParams(
            dimension_semantics=("arbitrary",), vmem_limit_bytes=VMEM_LIMIT),
        name="proj_latent" if rope else "proj_context",
    )(*args)


def _log_gamma(p):
    return jnp.log1p(-jnp.exp2(-p))


def _mix_kernel(*refs, n_seq, latent):
    c = RET_CHUNK
    nc = n_seq // c
    (p_ref, rq_ref, rk_ref, rv_ref, rg_ref, qcat_ref, kvn_ref, krp_ref,
     gng_ref, gnb_ref) = refs[:10]
    if latent:
        kvc_ref, krc_ref, sf0_ref, sb0_ref, ret_ref, attn_ref, of_scr = refs[10:]
    else:
        ret_ref, attn_ref, sf_ref, sb_ref, of_scr = refs[10:]

    pair = pl.program_id(1)
    lane = lax.broadcasted_iota(jnp.int32, (c, LANES), 1)
    rowi = lax.broadcasted_iota(jnp.int32, (c, LANES), 0)
    rowf = rowi.astype(F32)
    lo = lane < RET_DK
    blockdiag = (rowi < RET_DK) == lo
    ri = lax.broadcasted_iota(jnp.int32, (c, c), 0)
    ci = lax.broadcasted_iota(jnp.int32, (c, c), 1)

    def run_direction(d, s0):
        p0 = p_ref[d, 2 * pair]
        p1 = p_ref[d, 2 * pair + 1]
        lgv = _log_gamma(jnp.where(lo, p0, p1))
        if d == 0:
            qdec = jnp.exp((rowf + 1.0) * lgv)
            kdec = jnp.exp((c - 1.0 - rowf) * lgv)
            diff = (ri - ci).astype(F32)
        else:
            qdec = jnp.exp((c - rowf) * lgv)
            kdec = jnp.exp(rowf * lgv)
            diff = (ci - ri).astype(F32)
        cdec = jnp.exp(c * lgv)

        def intra(pe):
            lg = _log_gamma(jnp.full((c, c), pe, F32))
            return jnp.where(diff >= 0, jnp.exp(jnp.maximum(diff, 0.0) * lg), 0.0)

        d0 = intra(p0)
        d1 = intra(p1)

        def body(t, s):
            ch = t if d == 0 else nc - 1 - t
            rows = pl.ds(pl.multiple_of(ch * c, c), c)
            qp = rq_ref[rows, :]
            kp = rk_ref[rows, :]
            vp = rv_ref[rows, :]
            zero = jnp.zeros_like(qp)
            a0 = (_dot_nt(jnp.where(lo, qp, zero), kp) * d0).astype(BF16)
            a1 = (_dot_nt(jnp.where(lo, zero, qp), kp) * d1).astype(BF16)
            inner = jnp.where(lo, _dot(a0, vp), _dot(a1, vp))
            cross = _dot(qp, s.astype(BF16)) * qdec
            o = inner + cross
            kdt = (kp.astype(F32) * kdec).T.astype(BF16)
            s_new = s * cdec + jnp.where(blockdiag, _dot(kdt, vp), 0.0)
            if d == 0:
                of_scr[rows, :] = o
            else:
                tot = of_scr[rows, :] + o
                inv = 1.0 / RET_DK
                m0 = jnp.sum(jnp.where(lo, tot, 0.0), axis=-1, keepdims=True) * inv
                m1 = jnp.sum(jnp.where(lo, 0.0, tot), axis=-1, keepdims=True) * inv
                y = tot - jnp.where(lo, m0, m1)
                v0 = jnp.sum(jnp.where(lo, y * y, 0.0), axis=-1, keepdims=True) * inv
                v1 = jnp.sum(jnp.where(lo, 0.0, y * y), axis=-1, keepdims=True) * inv
                yn = y * lax.rsqrt(jnp.where(lo, v0, v1) + EPS) * gng_ref[...] + gnb_ref[...]
                ret_ref[rows, :] = (yn * _silu(rg_ref[rows, :])).astype(BF16)
            return s_new

        return lax.fori_loop(0, nc, body, s0)

    if latent:
        run_direction(0, sf0_ref[...])
        run_direction(1, sb0_ref[...])
    else:
        zero_state = jnp.zeros((c, LANES), F32)
        for d, st_ref in ((0, sf_ref), (1, sb_ref)):
            s_fin = run_direction(d, zero_state)
            st_ref[0] = s_fin[0:RET_DK, 0:RET_DK]
            st_ref[1] = s_fin[RET_DK:, RET_DK:]

    scale = (MLA_NOPE + MLA_ROPE) ** -0.5
    nope_n = lax.broadcasted_iota(jnp.int32, (n_seq, LANES), 1) < MLA_NOPE
    krp = krp_ref[...]
    kv = [kvn_ref[:, e * HEAD_PAD:(e + 1) * HEAD_PAD] for e in range(2)]
    kc = [jnp.where(nope_n, kv[e], krp) for e in range(2)]
    if latent:
        past = kvc_ref.shape[0]
        nope_c = lax.broadcasted_iota(jnp.int32, (past, LANES), 1) < MLA_NOPE
        krc = krc_ref[...]
        kv_c = [kvc_ref[:, e * HEAD_PAD:(e + 1) * HEAD_PAD] for e in range(2)]
        kc_c = [jnp.where(nope_c, kv_c[e], krc) for e in range(2)]

    bq = min(n_seq, 256)
    lo_q = lax.broadcasted_iota(jnp.int32, (bq, LANES), 1) < MLA_V

    def attn_block(i, carry):
        rows = pl.ds(pl.multiple_of(i * bq, bq), bq)
        outs = []
        for e in range(2):
            qc = qcat_ref[rows, e * HEAD_PAD:(e + 1) * HEAD_PAD]
            s = _dot_nt(qc, kc[e]) * scale
            m = jnp.max(s, axis=-1, keepdims=True)
            if latent:
                s2 = _dot_nt(qc, kc_c[e]) * scale
                m = jnp.maximum(m, jnp.max(s2, axis=-1, keepdims=True))
                p2 = jnp.exp(s2 - m)
            p = jnp.exp(s - m)
            den = jnp.sum(p, axis=-1, keepdims=True)
            acc = _dot(p.astype(BF16), kv[e])
            if latent:
                den = den + jnp.sum(p2, axis=-1, keepdims=True)
                acc = acc + _dot(p2.astype(BF16), kv_c[e])
            outs.append(acc / den)
        attn_ref[rows, :] = jnp.where(lo_q, pltpu.roll(outs[0], MLA_V, 1), outs[1]).astype(BF16)
        return carry

    lax.fori_loop(0, n_seq // bq, attn_block, 0)


def _mix(proj_out, decay_p, gn_g, gn_b, layer, n_batch, n_seq, latent_in):
    rq, rk, rv, rg, qcat, kvn, krp = proj_out[:7]
    latent = latent_in is not None
    m = n_batch * n_seq

    def col(b, p):
        return (b, p)

    def pair_gain(b, p):
        return (layer, 0, p)

    in_specs = [
        pl.BlockSpec(memory_space=pltpu.SMEM),
        pl.BlockSpec((n_seq, LANES), col), pl.BlockSpec((n_seq, LANES), col),
        pl.BlockSpec((n_seq, LANES), col), pl.BlockSpec((n_seq, LANES), col),
        pl.BlockSpec((n_seq, 2 * HEAD_PAD), col), pl.BlockSpec((n_seq, 2 * HEAD_PAD), col),
        pl.BlockSpec((n_seq, LANES), lambda b, p: (b, 0)),
        pl.BlockSpec((None, 1, LANES), pair_gain), pl.BlockSpec((None, 1, LANES), pair_gain),
    ]
    args = [decay_p, rq, rk, rv, rg, qcat, kvn, krp, gn_g, gn_b]
    out_shape = [jax.ShapeDtypeStruct((m, RET_WIDTH), BF16),
                 jax.ShapeDtypeStruct((m, MLA_HEADS * MLA_V), BF16)]
    out_specs = [pl.BlockSpec((n_seq, LANES), col), pl.BlockSpec((n_seq, LANES), col)]
    if latent:
        kvn_c, krp_c, s_f0, s_b0 = latent_in
        past = kvn_c.shape[2]
        in_specs += [
            pl.BlockSpec((None, None, past, 2 * HEAD_PAD), lambda b, p: (layer, b, 0, p)),
            pl.BlockSpec((None, None, past, LANES), lambda b, p: (b, layer, 0, 0)),
            pl.BlockSpec((None, None, None, LANES, LANES), lambda b, p: (b, layer, p, 0, 0)),
            pl.BlockSpec((None, None, None, LANES, LANES), lambda b, p: (b, layer, p, 0, 0)),
        ]
        args += [kvn_c, krp_c, s_f0, s_b0]
    else:
        st = jax.ShapeDtypeStruct((n_batch, RET_HEADS, RET_DK, RET_DK), F32)
        out_shape += [st, st]
        st_spec = pl.BlockSpec((None, 2, RET_DK, RET_DK), lambda b, p: (b, p, 0, 0))
        out_specs += [st_spec, st_spec]

    return pl.pallas_call(
        functools.partial(_mix_kernel, n_seq=n_seq, latent=latent),
        grid=(n_batch, N_PAIRS),
        in_specs=in_specs,
        out_specs=out_specs,
        out_shape=out_shape,
        scratch_shapes=[pltpu.VMEM((n_seq, LANES), F32)],
        compiler_params=pltpu.CompilerParams(
            dimension_semantics=("arbitrary", "arbitrary"), vmem_limit_bytes=VMEM_LIMIT),
        name="mix_latent" if latent else "mix_context",
    )(*args)


def _out_kernel(x_ref, ret_ref, attn_ref, mod_ref, gffn_ref, wo_ref, wfi_ref, wfo_ref, o_ref):
    gt1 = mod_ref[2:3, :]
    sh2 = mod_ref[3:4, :]
    sc2 = mod_ref[4:5, :]
    gt2 = mod_ref[5:6, :]
    mixed = _dot(ret_ref[...], wo_ref[0:RET_WIDTH, :]) + _dot(attn_ref[...], wo_ref[RET_WIDTH:, :])
    x1 = x_ref[...] + gt1 * mixed
    h = (_rms(x1, gffn_ref[...]) * (1.0 + sc2) + sh2).astype(BF16)
    acc = jnp.zeros_like(x1)
    for c0 in range(0, D_FF, FF_CHUNK):
        cw = min(FF_CHUNK, D_FF - c0)
        gate = _dot(h, wfi_ref[:, c0:c0 + cw])
        up = _dot(h, wfi_ref[:, D_FF + c0:D_FF + c0 + cw])
        acc = acc + _dot((_silu(gate) * up).astype(BF16), wfo_ref[c0:c0 + cw, :])
    o_ref[...] = x1 + gt2 * acc


def _out(x, ret, attn, mod, layer, cond_row, wts):
    m = x.shape[0]
    tm = ROW_TILE

    def row(i):
        return (i, 0)

    def lay3(i):
        return (layer, 0, 0)

    once = pl.Buffered(1)
    return pl.pallas_call(
        _out_kernel,
        grid=(m // tm,),
        in_specs=[
            pl.BlockSpec((tm, D_MODEL), row),
            pl.BlockSpec((tm, RET_WIDTH), row),
            pl.BlockSpec((tm, RET_WIDTH), row),
            pl.BlockSpec((None, None, N_MOD, D_MODEL), lambda i: (layer, cond_row(i), 0, 0)),
            pl.BlockSpec((None, 1, D_MODEL), lay3),
            pl.BlockSpec((None, D_MODEL, D_MODEL), lay3, pipeline_mode=once),
            pl.BlockSpec((None, D_MODEL, 2 * D_FF), lay3, pipeline_mode=once),
            pl.BlockSpec((None, D_FF, D_MODEL), lay3, pipeline_mode=once),
        ],
        out_specs=pl.BlockSpec((tm, D_MODEL), row),
        out_shape=jax.ShapeDtypeStruct((m, D_MODEL), F32),
        compiler_params=pltpu.CompilerParams(
            dimension_semantics=("arbitrary",), vmem_limit_bytes=VMEM_LIMIT),
        name="out_ffn",
    )(x, ret, attn, mod, wts["g_ffn"], wts["w_o"], wts["w_ffn_in"], wts["w_ffn_out"])


def _rope_tables(n_lat):
    pos = jnp.arange(n_lat)
    row = (pos // GRID_W).astype(F32)[:, None]
    col = (pos % GRID_W).astype(F32)[:, None]
    lane = jnp.arange(LANES)[None, :]

    def tables(d, start, period):
        rel = (lane - start) % period
        active = jnp.logical_and(lane >= start, rel < d)
        half = d // 2
        nf = half // 2
        inv = ROPE_BASE ** (-((rel % nf).astype(F32)) / nf)
        ang = jnp.where(rel < half, row, col) * inv
        cos, sin = jnp.cos(ang), jnp.sin(ang)
        first = (rel % half) < nf
        c = jnp.where(active, cos, 1.0)
        sa = jnp.where(jnp.logical_and(active, jnp.logical_not(first)), sin, 0.0)
        sb = jnp.where(jnp.logical_and(active, first), -sin, 0.0)
        return c, sa, sb

    return tables(RET_DK, 0, RET_DK) + tables(MLA_ROPE, MLA_NOPE, LANES)


def _prepare_weights(g_norm_mix, g_norm_ffn, w_in, g_q_a, w_q_b, g_kv_a, w_kv_b, g_qn, g_qr, g_kn,
                     g_kr, w_o, w_ffn_in, w_ffn_out):
    depth = w_in.shape[0]
    body = 4 * RET_WIDTH + Q_RANK + KV_RANK
    kr_cols = w_in[:, :, body:]
    pad32 = jnp.zeros_like(kr_cols)
    w_in_p = jnp.concatenate([w_in[:, :, :body], kr_cols, pad32, kr_cols, pad32], axis=-1)
    w_qb = w_q_b.reshape(depth, Q_RANK, MLA_HEADS, MLA_NOPE + MLA_ROPE)
    w_qb = jnp.pad(w_qb, ((0, 0), (0, 0), (0, 0), (0, HEAD_PAD - MLA_NOPE - MLA_ROPE)))
    zeros32 = jnp.zeros((depth, MLA_ROPE), F32)
    return {
        "g_mix": g_norm_mix[:, None, :],
        "g_ffn": g_norm_ffn[:, None, :],
        "w_in": w_in_p.astype(BF16),
        "g_qa": g_q_a[:, None, :],
        "w_qb": w_qb.reshape(depth, Q_RANK, MLA_HEADS * HEAD_PAD).astype(BF16),
        "g_kva": g_kv_a[:, None, :],
        "w_kvb": w_kv_b.astype(BF16),
        "g_q": jnp.concatenate([g_qn, g_qr, zeros32], axis=-1)[:, None, :],
        "g_kn": jnp.concatenate([g_kn, jnp.ones((depth, MLA_V), F32)], axis=-1)[:, None, :],
        "g_kr": jnp.concatenate([g_kr, zeros32, g_kr, zeros32], axis=-1)[:, None, :],
        "w_o": w_o.astype(BF16),
        "w_ffn_in": w_ffn_in.astype(BF16),
        "w_ffn_out": w_ffn_out.astype(BF16),
    }


def _blockdiag_states(s):
    b, l = s.shape[:2]
    s = s.reshape(b, l, N_PAIRS, 2, RET_DK, RET_DK)
    z = jnp.zeros_like(s[:, :, :, 0])
    top = jnp.concatenate([s[:, :, :, 0], z], axis=-1)
    bot = jnp.concatenate([z, s[:, :, :, 1]], axis=-1)
    return jnp.concatenate([top, bot], axis=-2)


def kernel(x_prompt, x_sample, cache_ckv, cache_krope, state_ret_fwd, state_ret_bwd, c, c_ctx,
           w_mod, b_mod, g_norm_mix, g_norm_ffn, w_in, g_q_a, w_q_b, g_kv_a, w_kv_b,
           g_qn, g_qr, g_kn, g_kr, ret_p_fwd, ret_p_bwd, g_ret_gn, b_ret_gn, w_o,
           w_ffn_in, w_ffn_out):
    batch, seq, _ = x_prompt.shape
    dec_batch, dec_seq, _ = x_sample.shape
    depth = w_in.shape[0]

    wts = _prepare_weights(g_norm_mix, g_norm_ffn, w_in, g_q_a, w_q_b, g_kv_a, w_kv_b, g_qn, g_qr,
                           g_kn, g_kr, w_o, w_ffn_in, w_ffn_out)
    conds = jnp.concatenate([c_ctx[None], c, jnp.zeros((8 - 1 - dec_batch, D_MODEL), F32)], axis=0)
    mod = _modulation(conds, w_mod, b_mod).reshape(depth, 8, N_MOD, D_MODEL)
    decay_p = jnp.stack([ret_p_fwd, ret_p_bwd], axis=1)
    gn_g = g_ret_gn[:, None, :]
    gn_b = b_ret_gn[:, None, :]

    rope_tabs = _rope_tables(dec_seq)
    kvn_cache = _cache_up(cache_ckv, wts["w_kvb"], wts["g_kn"])
    krp_cache = jnp.pad(cache_krope, ((0, 0), (0, 0), (0, 0), (MLA_NOPE, LANES - MLA_NOPE - MLA_ROPE)))
    krp_cache = krp_cache.astype(BF16)
    s_f0 = _blockdiag_states(state_ret_fwd)
    s_b0 = _blockdiag_states(state_ret_bwd)

    lat_tiles = dec_seq // ROW_TILE

    def ctx_row(i):
        return 0

    def lat_row(i):
        return 1 + i // lat_tiles

    x = x_prompt.reshape(batch * seq, D_MODEL)
    y = x_sample.reshape(dec_batch * dec_seq, D_MODEL)
    ckv_l, kr_l, sf_l, sb_l = [], [], [], []
    for l in range(depth):
        pr = _proj(x, mod, l, ctx_row, wts, None)
        ret, attn, s_f, s_b = _mix(pr, decay_p[l], gn_g, gn_b, l, batch, seq, None)
        x = _out(x, ret, attn, mod, l, ctx_row, wts)
        ckv_l.append(pr[7].reshape(batch, seq, KV_RANK))
        kr_l.append(pr[8].reshape(batch, seq, MLA_ROPE))
        sf_l.append(s_f)
        sb_l.append(s_b)

        pr = _proj(y, mod, l, lat_row, wts, rope_tabs)
        ret, attn = _mix(pr, decay_p[l], gn_g, gn_b, l, dec_batch, dec_seq,
                         (kvn_cache, krp_cache, s_f0, s_b0))
        y = _out(y, ret, attn, mod, l, lat_row, wts)

    return (x.reshape(batch, seq, D_MODEL), y.reshape(dec_batch, dec_seq, D_MODEL),
            jnp.stack(ckv_l, axis=1), jnp.stack(kr_l, axis=1),
            jnp.stack(sf_l, axis=1), jnp.stack(sb_l, axis=1))
```

```python
import functools

import jax
import jax.numpy as jnp
from jax import lax
from jax.experimental import pallas as pl
from jax.experimental.pallas import tpu as pltpu

D_MODEL = 1024
N_MOD = 6
RET_HEADS = 8
RET_DK = 64
RET_WIDTH = 512
MLA_HEADS = 8
MLA_NOPE = 64
MLA_ROPE = 32
MLA_V = 64
Q_RANK = 256
KV_RANK = 128
D_FF = 2816
GRID_W = 64
ROPE_BASE = 10000.0
EPS = 1e-6

LANES = 128
HEAD_PAD = LANES
N_PAIRS = RET_HEADS // 2
IN_COLS_PAD = 4 * RET_WIDTH + Q_RANK + KV_RANK + LANES
ROW_TILE = 256
FF_CHUNK = 512
MIX_CHUNK = 256
Q_DEC_F, K_DEC_F, Q_DEC_B, K_DEC_B, C_DEC_F, C_DEC_B = range(6)
N_DVEC = 6
VMEM_LIMIT = 56 * 1024 * 1024

BF16 = jnp.bfloat16
F32 = jnp.float32
_NT = (((1,), (1,)), ((), ()))


def _dot(a, b):
    return jnp.dot(a, b, preferred_element_type=F32)


def _dot_nt(a, b):
    return lax.dot_general(a, b, _NT, preferred_element_type=F32)


def _rms(x, g):
    return x * lax.rsqrt(jnp.mean(x * x, axis=-1, keepdims=True) + EPS) * g


def _silu(x):
    return x * jax.nn.sigmoid(x)


def _masked_mean_sq(x, mask, n):
    return jnp.sum(jnp.where(mask, x * x, 0.0), axis=-1, keepdims=True) * (1.0 / n)


def _rope(x, c, sa, sb, shift):
    return x * c + pltpu.roll(x, shift, 1) * sa + pltpu.roll(x, LANES - shift, 1) * sb


def _mod_kernel(c_ref, w_ref, b_ref, o_ref):
    a = _silu(c_ref[...])
    w = w_ref[...]
    a_hi = a.astype(BF16)
    a_lo = (a - a_hi.astype(F32)).astype(BF16)
    w_hi = w.astype(BF16)
    w_lo = (w - w_hi.astype(F32)).astype(BF16)
    o_ref[...] = _dot(a_hi, w_hi) + _dot(a_lo, w_hi) + _dot(a_hi, w_lo) + b_ref[...]


def _modulation(conds, w_mod, b_mod):
    depth, _, n = w_mod.shape
    tn = 1536
    return pl.pallas_call(
        _mod_kernel,
        grid=(depth, n // tn),
        in_specs=[
            pl.BlockSpec((8, D_MODEL), lambda l, j: (0, 0)),
            pl.BlockSpec((None, D_MODEL, tn), lambda l, j: (l, 0, j)),
            pl.BlockSpec((None, 1, tn), lambda l, j: (l, 0, j)),
        ],
        out_specs=pl.BlockSpec((None, 8, tn), lambda l, j: (l, 0, j)),
        out_shape=jax.ShapeDtypeStruct((depth, 8, n), F32),
        compiler_params=pltpu.CompilerParams(
            dimension_semantics=("arbitrary", "arbitrary"), vmem_limit_bytes=VMEM_LIMIT),
        name="modulation",
    )(conds, w_mod, b_mod.reshape(depth, 1, n))


def _norm_kn(kv, gkn):
    lane = lax.broadcasted_iota(jnp.int32, (kv.shape[0], LANES), 1)
    lo = lane < MLA_NOPE
    out = []
    for h in range(MLA_HEADS):
        kvh = kv[:, h * HEAD_PAD:(h + 1) * HEAD_PAD]
        rs = lax.rsqrt(_masked_mean_sq(kvh, lo, MLA_NOPE) + EPS)
        out.append((kvh * jnp.where(lo, rs * gkn, 1.0)).astype(BF16))
    return out


def _cache_kernel(ckv_ref, wkvb_ref, gkn_ref, kvn_ref):
    kv = _dot(ckv_ref[...].astype(BF16), wkvb_ref[...])
    for h, kvh in enumerate(_norm_kn(kv, gkn_ref[...])):
        kvn_ref[:, h * HEAD_PAD:(h + 1) * HEAD_PAD] = kvh


def _cache_up(cache_ckv, wkvb, gkn):
    nb, depth, past, _ = cache_ckv.shape
    width = MLA_HEADS * HEAD_PAD
    return pl.pallas_call(
        _cache_kernel,
        grid=(depth, nb),
        in_specs=[
            pl.BlockSpec((None, None, past, KV_RANK), lambda l, b: (b, l, 0, 0)),
            pl.BlockSpec((None, KV_RANK, width), lambda l, b: (l, 0, 0)),
            pl.BlockSpec((None, 1, LANES), lambda l, b: (l, 0, 0)),
        ],
        out_specs=pl.BlockSpec((None, None, past, width), lambda l, b: (l, b, 0, 0)),
        out_shape=jax.ShapeDtypeStruct((depth, nb, past, width), BF16),
        compiler_params=pltpu.CompilerParams(
            dimension_semantics=("arbitrary", "arbitrary"), vmem_limit_bytes=VMEM_LIMIT),
        name="cache_up",
    )(cache_ckv, wkvb, gkn)


def _proj_kernel(*refs, rope):
    (x_ref, mod_ref, gmix_ref, win_ref, gqa_ref, wqb_ref, gkva_ref, wkvb_ref,
     gq_ref, gkn_ref, gkr_ref) = refs[:11]
    if rope:
        c64_ref, sa64_ref, sb64_ref, c32_ref, sa32_ref, sb32_ref = refs[11:17]
        rq_ref, rk_ref, rv_ref, rg_ref, qcat_ref, kvn_ref, krp_ref = refs[17:]
    else:
        rq_ref, rk_ref, rv_ref, rg_ref, qcat_ref, kvn_ref, krp_ref, ckv_ref, kro_ref = refs[11:]

    x = x_ref[...]
    sh1 = mod_ref[0:1, :]
    sc1 = mod_ref[1:2, :]
    h = (_rms(x, gmix_ref[...]) * (1.0 + sc1) + sh1).astype(BF16)
    z = _dot(h, win_ref[...])

    w = RET_WIDTH
    for j in range(w // LANES):
        sl = slice(j * LANES, (j + 1) * LANES)
        q = z[:, j * LANES:(j + 1) * LANES]
        k = z[:, w + j * LANES:w + (j + 1) * LANES] * (RET_DK ** -0.5)
        if rope:
            q = _rope(q, c64_ref[...], sa64_ref[...], sb64_ref[...], 16)
            k = _rope(k, c64_ref[...], sa64_ref[...], sb64_ref[...], 16)
        rq_ref[:, sl] = q.astype(BF16)
        rk_ref[:, sl] = k.astype(BF16)
    rv_ref[...] = z[:, 2 * w:3 * w].astype(BF16)
    rg_ref[...] = z[:, 3 * w:4 * w]

    o = 4 * w
    qa = z[:, o:o + Q_RANK]
    kva = z[:, o + Q_RANK:o + Q_RANK + KV_RANK]
    kr2 = z[:, o + Q_RANK + KV_RANK:]

    lane = lax.broadcasted_iota(jnp.int32, (x.shape[0], LANES), 1)
    nope = lane < MLA_NOPE

    q = _dot(_rms(qa, gqa_ref[...]).astype(BF16), wqb_ref[...])
    is_rope = jnp.logical_and(lane >= MLA_NOPE, lane < MLA_NOPE + MLA_ROPE)
    for hh in range(MLA_HEADS):
        qh = q[:, hh * HEAD_PAD:(hh + 1) * HEAD_PAD]
        rs_n = lax.rsqrt(_masked_mean_sq(qh, nope, MLA_NOPE) + EPS)
        rs_r = lax.rsqrt(_masked_mean_sq(qh, is_rope, MLA_ROPE) + EPS)
        qn = qh * jnp.where(nope, rs_n, rs_r) * gq_ref[...]
        if rope:
            qn = _rope(qn, c32_ref[...], sa32_ref[...], sb32_ref[...], 8)
        qcat_ref[:, hh * HEAD_PAD:(hh + 1) * HEAD_PAD] = qn.astype(BF16)

    ckv = _rms(kva, gkva_ref[...])
    kv = _dot(ckv.astype(BF16), wkvb_ref[...])
    for hh, kvh in enumerate(_norm_kn(kv, gkn_ref[...])):
        kvn_ref[:, hh * HEAD_PAD:(hh + 1) * HEAD_PAD] = kvh

    krn = kr2 * lax.rsqrt(_masked_mean_sq(kr2, lane < MLA_ROPE, MLA_ROPE) + EPS) * gkr_ref[...]
    if rope:
        krn = _rope(krn, c32_ref[...], sa32_ref[...], sb32_ref[...], 8)
    else:
        ckv_ref[...] = ckv
        kro_ref[...] = krn[:, 0:MLA_ROPE]
    krp_ref[...] = krn.astype(BF16)


def _proj(x, mod, layer, cond_row, wts, rope_tabs):
    m = x.shape[0]
    tm = ROW_TILE
    rope = rope_tabs is not None
    width = MLA_HEADS * HEAD_PAD

    def row(i):
        return (i, 0)

    def const2(i):
        return (0, 0)

    def lay3(i):
        return (layer, 0, 0)

    in_specs = [
        pl.BlockSpec((tm, D_MODEL), row),
        pl.BlockSpec((None, None, N_MOD, D_MODEL), lambda i: (layer, cond_row(i), 0, 0)),
        pl.BlockSpec((None, 1, D_MODEL), lay3),
        pl.BlockSpec((None, D_MODEL, IN_COLS_PAD), lay3),
        pl.BlockSpec((None, 1, Q_RANK), lay3),
        pl.BlockSpec((None, Q_RANK, width), lay3),
        pl.BlockSpec((None, 1, KV_RANK), lay3),
        pl.BlockSpec((None, KV_RANK, width), lay3),
        pl.BlockSpec((None, 1, LANES), lay3),
        pl.BlockSpec((None, 1, LANES), lay3),
        pl.BlockSpec((None, 1, LANES), lay3),
    ]
    args = [x, mod, wts["g_mix"], wts["w_in"], wts["g_qa"], wts["w_qb"], wts["g_kva"],
            wts["w_kvb"], wts["g_q"], wts["g_kn"], wts["g_kr"]]
    out_shape = [
        jax.ShapeDtypeStruct((m, RET_WIDTH), BF16),
        jax.ShapeDtypeStruct((m, RET_WIDTH), BF16),
        jax.ShapeDtypeStruct((m, RET_WIDTH), BF16),
        jax.ShapeDtypeStruct((m, RET_WIDTH), F32),
        jax.ShapeDtypeStruct((m, width), BF16),
        jax.ShapeDtypeStruct((m, width), BF16),
        jax.ShapeDtypeStruct((m, LANES), BF16),
    ]
    out_specs = [
        pl.BlockSpec((tm, RET_WIDTH), row), pl.BlockSpec((tm, RET_WIDTH), row),
        pl.BlockSpec((tm, RET_WIDTH), row), pl.BlockSpec((tm, RET_WIDTH), row),
        pl.BlockSpec((tm, width), row), pl.BlockSpec((tm, width), row),
        pl.BlockSpec((tm, LANES), row),
    ]
    if rope:
        n_lat = rope_tabs[0].shape[0]
        tiles = n_lat // tm
        in_specs += [pl.BlockSpec((tm, LANES), lambda i: (i % tiles, 0))] * 6
        args += list(rope_tabs)
    else:
        out_shape += [jax.ShapeDtypeStruct((m, KV_RANK), F32),
                      jax.ShapeDtypeStruct((m, MLA_ROPE), F32)]
        out_specs += [pl.BlockSpec((tm, KV_RANK), row), pl.BlockSpec((tm, MLA_ROPE), row)]

    return pl.pallas_call(
        functools.partial(_proj_kernel, rope=rope),
        grid=(m // tm,),
        in_specs=in_specs,
        out_specs=out_specs,
        out_shape=out_shape,
        compiler_params=pltpu.CompilerParams(
            dimension_semantics=("arbitrary",), vmem_limit_bytes=VMEM_LIMIT),
        name="proj_latent" if rope else "proj_context",
    )(*args)


def _log_gamma(p):
    return jnp.log1p(-jnp.exp2(-p))


def _decay_kernel(p_ref, dmask_ref, dvec_ref):
    c = MIX_CHUNK
    base = pl.program_id(0) * (2 * RET_HEADS)
    pair = pl.program_id(1)
    ri = lax.broadcasted_iota(jnp.int32, (c, c), 0)
    ci = lax.broadcasted_iota(jnp.int32, (c, c), 1)
    dif = (ri - ci).astype(F32)
    for e in range(2):
        h = 2 * pair + e
        lg_f = _log_gamma(jnp.full((c, c), p_ref[base + h], F32))
        lg_b = _log_gamma(jnp.full((c, c), p_ref[base + RET_HEADS + h], F32))
        fwd = jnp.where(dif >= 0, jnp.exp(jnp.maximum(dif, 0.0) * lg_f), 0.0)
        bwd = jnp.where(dif <= 0, jnp.exp(jnp.maximum(-dif, 0.0) * lg_b), 0.0)
        dmask_ref[e] = fwd + bwd
    lane = lax.broadcasted_iota(jnp.int32, (c, LANES), 1)
    rowf = lax.broadcasted_iota(jnp.int32, (c, LANES), 0).astype(F32)
    lo = lane < RET_DK
    lg_f = _log_gamma(jnp.where(lo, p_ref[base + 2 * pair], p_ref[base + 2 * pair + 1]))
    lg_b = _log_gamma(jnp.where(lo, p_ref[base + RET_HEADS + 2 * pair],
                                p_ref[base + RET_HEADS + 2 * pair + 1]))
    dvec_ref[Q_DEC_F] = jnp.exp((rowf + 1.0) * lg_f)
    dvec_ref[K_DEC_F] = jnp.exp((c - 1.0 - rowf) * lg_f)
    dvec_ref[Q_DEC_B] = jnp.exp((c - rowf) * lg_b)
    dvec_ref[K_DEC_B] = jnp.exp(rowf * lg_b)
    dvec_ref[C_DEC_F] = jnp.exp(c * lg_f)
    dvec_ref[C_DEC_B] = jnp.exp(c * lg_b)


def _decay_tables(decay_p):
    depth = decay_p.shape[0]
    c = MIX_CHUNK
    return pl.pallas_call(
        _decay_kernel,
        grid=(depth, N_PAIRS),
        in_specs=[pl.BlockSpec(memory_space=pltpu.SMEM)],
        out_specs=[
            pl.BlockSpec((None, 2, c, c), lambda l, p: (l, p, 0, 0)),
            pl.BlockSpec((None, None, N_DVEC, c, LANES), lambda l, p: (l, p, 0, 0, 0)),
        ],
        out_shape=[
            jax.ShapeDtypeStruct((depth, RET_HEADS, c, c), F32),
            jax.ShapeDtypeStruct((depth, N_PAIRS, N_DVEC, c, LANES), F32),
        ],
        compiler_params=pltpu.CompilerParams(
            dimension_semantics=("arbitrary", "arbitrary"), vmem_limit_bytes=VMEM_LIMIT),
        name="decay_tables",
    )(decay_p.reshape(-1))


def _mix_kernel(*refs, n_seq, latent):
    c = MIX_CHUNK
    nc = n_seq // c
    (rq_ref, rk_ref, rv_ref, rg_ref, qcat_ref, kvn_ref, krp_ref, gng_ref, gnb_ref,
     dmask_ref, dvec_ref) = refs[:11]
    if latent:
        kvc_ref, krc_ref, sf0_ref, sb0_ref, ret_ref, attn_ref, st_scr = refs[11:]
    else:
        ret_ref, attn_ref, sf_ref, sb_ref = refs[11:]

    lane = lax.broadcasted_iota(jnp.int32, (c, LANES), 1)
    lo = lane < RET_DK
    sq_r = lax.broadcasted_iota(jnp.int32, (LANES, LANES), 0)
    sq_c = lax.broadcasted_iota(jnp.int32, (LANES, LANES), 1)
    blockdiag = (sq_r < RET_DK) == (sq_c < RET_DK)

    def cols(j):
        return slice(j * LANES, (j + 1) * LANES)

    def state_update(pair, d, rows):
        kp = rk_ref[rows, cols(pair)]
        vp = rv_ref[rows, cols(pair)]
        kdt = (kp.astype(F32) * dvec_ref[pair, K_DEC_B if d else K_DEC_F]).T.astype(BF16)
        return jnp.where(blockdiag, _dot(kdt, vp), 0.0)

    for pair in range(N_PAIRS):
        for d in range(2):
            if latent:
                s = (sb0_ref if d else sf0_ref)[pair]
                cdec = dvec_ref[pair, C_DEC_B if d else C_DEC_F][0:LANES, :]
                order = list(range(nc - 1, -1, -1)) if d else list(range(nc))
                for idx, ch in enumerate(order):
                    st_scr[d, pair, ch] = s.astype(BF16)
                    if idx < nc - 1:
                        s = s * cdec + state_update(pair, d, pl.ds(ch * c, c))
            else:
                s = state_update(pair, d, pl.ds(0, c))
                st_ref = sb_ref if d else sf_ref
                st_ref[2 * pair] = s[0:RET_DK, 0:RET_DK]
                st_ref[2 * pair + 1] = s[RET_DK:, RET_DK:]

    scale = (MLA_NOPE + MLA_ROPE) ** -0.5
    nope_n = lax.broadcasted_iota(jnp.int32, (n_seq, LANES), 1) < MLA_NOPE
    if latent:
        nope_c = lax.broadcasted_iota(jnp.int32, (kvc_ref.shape[0], LANES), 1) < MLA_NOPE

    def chunk_body(ch, carry):
        rows = pl.ds(pl.multiple_of(ch * c, c), c)

        for pair in range(N_PAIRS):
            qp = rq_ref[rows, cols(pair)]
            kp = rk_ref[rows, cols(pair)]
            vp = rv_ref[rows, cols(pair)]
            zero = jnp.zeros_like(qp)
            a0 = (_dot_nt(jnp.where(lo, qp, zero), kp) * dmask_ref[2 * pair]).astype(BF16)
            a1 = (_dot_nt(jnp.where(lo, zero, qp), kp) * dmask_ref[2 * pair + 1]).astype(BF16)
            tot = jnp.where(lo, _dot(a0, vp), _dot(a1, vp))
            if latent:
                tot = (tot + _dot(qp, st_scr[0, pair, ch]) * dvec_ref[pair, Q_DEC_F]
                       + _dot(qp, st_scr[1, pair, ch]) * dvec_ref[pair, Q_DEC_B])
            inv = 1.0 / RET_DK
            m0 = jnp.sum(jnp.where(lo, tot, 0.0), axis=-1, keepdims=True) * inv
            m1 = jnp.sum(jnp.where(lo, 0.0, tot), axis=-1, keepdims=True) * inv
            y = tot - jnp.where(lo, m0, m1)
            v0 = jnp.sum(jnp.where(lo, y * y, 0.0), axis=-1, keepdims=True) * inv
            v1 = jnp.sum(jnp.where(lo, 0.0, y * y), axis=-1, keepdims=True) * inv
            yn = (y * lax.rsqrt(jnp.where(lo, v0, v1) + EPS) * gng_ref[:, cols(pair)]
                  + gnb_ref[:, cols(pair)])
            ret_ref[rows, cols(pair)] = (yn * _silu(rg_ref[rows, cols(pair)])).astype(BF16)

        krp = krp_ref[...]
        for pair in range(N_PAIRS):
            outs = []
            for e in range(2):
                h = 2 * pair + e
                qc = qcat_ref[rows, cols(h)]
                kv = kvn_ref[:, cols(h)]
                s = _dot_nt(qc, jnp.where(nope_n, kv, krp)) * scale
                m = jnp.max(s, axis=-1, keepdims=True)
                if latent:
                    kv_c = kvc_ref[:, cols(h)]
                    s2 = _dot_nt(qc, jnp.where(nope_c, kv_c, krc_ref[...])) * scale
                    m = jnp.maximum(m, jnp.max(s2, axis=-1, keepdims=True))
                    p2 = jnp.exp(s2 - m)
                p = jnp.exp(s - m)
                den = jnp.sum(p, axis=-1, keepdims=True)
                acc = _dot(p.astype(BF16), kv)
                if latent:
                    den = den + jnp.sum(p2, axis=-1, keepdims=True)
                    acc = acc + _dot(p2.astype(BF16), kv_c)
                outs.append(acc / den)
            attn_ref[rows, cols(pair)] = jnp.where(
                lo, pltpu.roll(outs[0], MLA_V, 1), outs[1]).astype(BF16)
        return carry

    if nc == 1:
        chunk_body(0, 0)
    else:
        lax.fori_loop(0, nc, chunk_body, 0)


def _mix(proj_out, dmask, dvec, gn_g, gn_b, layer, n_batch, n_seq, latent_in):
    rq, rk, rv, rg, qcat, kvn, krp = proj_out[:7]
    latent = latent_in is not None
    m = n_batch * n_seq
    c = MIX_CHUNK
    width = MLA_HEADS * HEAD_PAD

    def row(b):
        return (b, 0)

    def lay3(b):
        return (layer, 0, 0)

    once = pl.Buffered(1)
    in_specs = [
        pl.BlockSpec((n_seq, RET_WIDTH), row), pl.BlockSpec((n_seq, RET_WIDTH), row),
        pl.BlockSpec((n_seq, RET_WIDTH), row), pl.BlockSpec((n_seq, RET_WIDTH), row),
        pl.BlockSpec((n_seq, width), row), pl.BlockSpec((n_seq, width), row),
        pl.BlockSpec((n_seq, LANES), row),
        pl.BlockSpec((None, 1, RET_WIDTH), lay3), pl.BlockSpec((None, 1, RET_WIDTH), lay3),
        pl.BlockSpec((None, RET_HEADS, c, c), lambda b: (layer, 0, 0, 0), pipeline_mode=once),
        pl.BlockSpec((None, N_PAIRS, N_DVEC, c, LANES), lambda b: (layer, 0, 0, 0, 0),
                     pipeline_mode=once),
    ]
    args = [rq, rk, rv, rg, qcat, kvn, krp, gn_g, gn_b, dmask, dvec]
    out_shape = [jax.ShapeDtypeStruct((m, RET_WIDTH), BF16),
                 jax.ShapeDtypeStruct((m, MLA_HEADS * MLA_V), BF16)]
    out_specs = [pl.BlockSpec((n_seq, RET_WIDTH), row), pl.BlockSpec((n_seq, RET_WIDTH), row)]
    scratch = []
    if latent:
        kvn_c, krp_c, s_f0, s_b0 = latent_in
        past = kvn_c.shape[2]
        st_spec = pl.BlockSpec((None, None, N_PAIRS, LANES, LANES), lambda b: (b, layer, 0, 0, 0))
        in_specs += [
            pl.BlockSpec((None, None, past, width), lambda b: (layer, b, 0, 0)),
            pl.BlockSpec((None, None, past, LANES), lambda b: (b, layer, 0, 0)),
            st_spec, st_spec,
        ]
        args += [kvn_c, krp_c, s_f0, s_b0]
        scratch = [pltpu.VMEM((2, N_PAIRS, n_seq // c, LANES, LANES), BF16)]
    else:
        st = jax.ShapeDtypeStruct((n_batch, RET_HEADS, RET_DK, RET_DK), F32)
        out_shape += [st, st]
        st_spec = pl.BlockSpec((None, RET_HEADS, RET_DK, RET_DK), lambda b: (b, 0, 0, 0))
        out_specs += [st_spec, st_spec]

    return pl.pallas_call(
        functools.partial(_mix_kernel, n_seq=n_seq, latent=latent),
        grid=(n_batch,),
        in_specs=in_specs,
        out_specs=out_specs,
        out_shape=out_shape,
        scratch_shapes=scratch,
        compiler_params=pltpu.CompilerParams(
            dimension_semantics=("arbitrary",), vmem_limit_bytes=VMEM_LIMIT),
        name="mix_latent" if latent else "mix_context",
    )(*args)


def _out_kernel(x_ref, ret_ref, attn_ref, mod_ref, gffn_ref, wo_ref, wfi_ref, wfo_ref, o_ref):
    gt1 = mod_ref[2:3, :]
    sh2 = mod_ref[3:4, :]
    sc2 = mod_ref[4:5, :]
    gt2 = mod_ref[5:6, :]
    mixed = _dot(ret_ref[...], wo_ref[0:RET_WIDTH, :]) + _dot(attn_ref[...], wo_ref[RET_WIDTH:, :])
    x1 = x_ref[...] + gt1 * mixed
    h = (_rms(x1, gffn_ref[...]) * (1.0 + sc2) + sh2).astype(BF16)
    acc = jnp.zeros_like(x1)
    for c0 in range(0, D_FF, FF_CHUNK):
        cw = min(FF_CHUNK, D_FF - c0)
        gate = _dot(h, wfi_ref[:, c0:c0 + cw])
        up = _dot(h, wfi_ref[:, D_FF + c0:D_FF + c0 + cw])
        acc = acc + _dot((_silu(gate) * up).astype(BF16), wfo_ref[c0:c0 + cw, :])
    o_ref[...] = x1 + gt2 * acc


def _out(x, ret, attn, mod, layer, cond_row, wts):
    m = x.shape[0]
    tm = ROW_TILE

    def row(i):
        return (i, 0)

    def lay3(i):
        return (layer, 0, 0)

    once = pl.Buffered(1)
    return pl.pallas_call(
        _out_kernel,
        grid=(m // tm,),
        in_specs=[
            pl.BlockSpec((tm, D_MODEL), row),
            pl.BlockSpec((tm, RET_WIDTH), row),
            pl.BlockSpec((tm, RET_WIDTH), row),
            pl.BlockSpec((None, None, N_MOD, D_MODEL), lambda i: (layer, cond_row(i), 0, 0)),
            pl.BlockSpec((None, 1, D_MODEL), lay3),
            pl.BlockSpec((None, D_MODEL, D_MODEL), lay3, pipeline_mode=once),
            pl.BlockSpec((None, D_MODEL, 2 * D_FF), lay3, pipeline_mode=once),
            pl.BlockSpec((None, D_FF, D_MODEL), lay3, pipeline_mode=once),
        ],
        out_specs=pl.BlockSpec((tm, D_MODEL), row),
        out_shape=jax.ShapeDtypeStruct((m, D_MODEL), F32),
        compiler_params=pltpu.CompilerParams(
            dimension_semantics=("arbitrary",), vmem_limit_bytes=VMEM_LIMIT),
        name="out_ffn",
    )(x, ret, attn, mod, wts["g_ffn"], wts["w_o"], wts["w_ffn_in"], wts["w_ffn_out"])


def _rope_tables(n_lat):
    pos = jnp.arange(n_lat)
    row = (pos // GRID_W).astype(F32)[:, None]
    col = (pos % GRID_W).astype(F32)[:, None]
    lane = jnp.arange(LANES)[None, :]

    def tables(d, start, period):
        rel = (lane - start) % period
        active = jnp.logical_and(lane >= start, rel < d)
        half = d // 2
        nf = half // 2
        inv = ROPE_BASE ** (-((rel % nf).astype(F32)) / nf)
        ang = jnp.where(rel < half, row, col) * inv
        cos, sin = jnp.cos(ang), jnp.sin(ang)
        first = (rel % half) < nf
        c = jnp.where(active, cos, 1.0)
        sa = jnp.where(jnp.logical_and(active, jnp.logical_not(first)), sin, 0.0)
        sb = jnp.where(jnp.logical_and(active, first), -sin, 0.0)
        return c, sa, sb

    return tables(RET_DK, 0, RET_DK) + tables(MLA_ROPE, MLA_NOPE, LANES)


def _prepare_weights(g_norm_mix, g_norm_ffn, w_in, g_q_a, w_q_b, g_kv_a, w_kv_b, g_qn, g_qr, g_kn,
                     g_kr, w_o, w_ffn_in, w_ffn_out):
    depth = w_in.shape[0]
    body = 4 * RET_WIDTH + Q_RANK + KV_RANK
    kr_cols = w_in[:, :, body:]
    pad32 = jnp.zeros_like(kr_cols)
    w_in_p = jnp.concatenate([w_in[:, :, :body], kr_cols, pad32, kr_cols, pad32], axis=-1)
    w_qb = w_q_b.reshape(depth, Q_RANK, MLA_HEADS, MLA_NOPE + MLA_ROPE)
    w_qb = jnp.pad(w_qb, ((0, 0), (0, 0), (0, 0), (0, HEAD_PAD - MLA_NOPE - MLA_ROPE)))
    zeros32 = jnp.zeros((depth, MLA_ROPE), F32)
    return {
        "g_mix": g_norm_mix[:, None, :],
        "g_ffn": g_norm_ffn[:, None, :],
        "w_in": w_in_p.astype(BF16),
        "g_qa": g_q_a[:, None, :],
        "w_qb": w_qb.reshape(depth, Q_RANK, MLA_HEADS * HEAD_PAD).astype(BF16),
        "g_kva": g_kv_a[:, None, :],
        "w_kvb": w_kv_b.astype(BF16),
        "g_q": jnp.concatenate([g_qn, g_qr, zeros32], axis=-1)[:, None, :],
        "g_kn": jnp.concatenate([g_kn, jnp.ones((depth, MLA_V), F32)], axis=-1)[:, None, :],
        "g_kr": jnp.concatenate([g_kr, zeros32, g_kr, zeros32], axis=-1)[:, None, :],
        "w_o": w_o.astype(BF16),
        "w_ffn_in": w_ffn_in.astype(BF16),
        "w_ffn_out": w_ffn_out.astype(BF16),
    }


def _blockdiag_states(s):
    b, l = s.shape[:2]
    s = s.reshape(b, l, N_PAIRS, 2, RET_DK, RET_DK)
    z = jnp.zeros_like(s[:, :, :, 0])
    top = jnp.concatenate([s[:, :, :, 0], z], axis=-1)
    bot = jnp.concatenate([z, s[:, :, :, 1]], axis=-1)
    return jnp.concatenate([top, bot], axis=-2)


def kernel(x_prompt, x_sample, cache_ckv, cache_krope, state_ret_fwd, state_ret_bwd, c, c_ctx,
           w_mod, b_mod, g_norm_mix, g_norm_ffn, w_in, g_q_a, w_q_b, g_kv_a, w_kv_b,
           g_qn, g_qr, g_kn, g_kr, ret_p_fwd, ret_p_bwd, g_ret_gn, b_ret_gn, w_o,
           w_ffn_in, w_ffn_out):
    batch, seq, _ = x_prompt.shape
    dec_batch, dec_seq, _ = x_sample.shape
    depth = w_in.shape[0]

    wts = _prepare_weights(g_norm_mix, g_norm_ffn, w_in, g_q_a, w_q_b, g_kv_a, w_kv_b, g_qn, g_qr,
                           g_kn, g_kr, w_o, w_ffn_in, w_ffn_out)
    conds = jnp.concatenate([c_ctx[None], c, jnp.zeros((8 - 1 - dec_batch, D_MODEL), F32)], axis=0)
    mod = _modulation(conds, w_mod, b_mod).reshape(depth, 8, N_MOD, D_MODEL)
    dmask, dvec = _decay_tables(jnp.stack([ret_p_fwd, ret_p_bwd], axis=1))
    gn_g = g_ret_gn[:, None, :]
    gn_b = b_ret_gn[:, None, :]

    rope_tabs = _rope_tables(dec_seq)
    kvn_cache = _cache_up(cache_ckv, wts["w_kvb"], wts["g_kn"])
    krp_cache = jnp.pad(cache_krope, ((0, 0), (0, 0), (0, 0), (MLA_NOPE, LANES - MLA_NOPE - MLA_ROPE)))
    krp_cache = krp_cache.astype(BF16)
    s_f0 = _blockdiag_states(state_ret_fwd)
    s_b0 = _blockdiag_states(state_ret_bwd)

    lat_tiles = dec_seq // ROW_TILE

    def ctx_row(i):
        return 0

    def lat_row(i):
        return 1 + i // lat_tiles

    x = x_prompt.reshape(batch * seq, D_MODEL)
    y = x_sample.reshape(dec_batch * dec_seq, D_MODEL)
    ckv_l, kr_l, sf_l, sb_l = [], [], [], []
    for l in range(depth):
        pr = _proj(x, mod, l, ctx_row, wts, None)
        ret, attn, s_f, s_b = _mix(pr, dmask, dvec, gn_g, gn_b, l, batch, seq, None)
        x = _out(x, ret, attn, mod, l, ctx_row, wts)
        ckv_l.append(pr[7].reshape(batch, seq, KV_RANK))
        kr_l.append(pr[8].reshape(batch, seq, MLA_ROPE))
        sf_l.append(s_f)
        sb_l.append(s_b)

        pr = _proj(y, mod, l, lat_row, wts, rope_tabs)
        ret, attn = _mix(pr, dmask, dvec, gn_g, gn_b, l, dec_batch, dec_seq,
                         (kvn_cache, krp_cache, s_f0, s_b0))
        y = _out(y, ret, attn, mod, l, lat_row, wts)

    return (x.reshape(batch, seq, D_MODEL), y.reshape(dec_batch, dec_seq, D_MODEL),
            jnp.stack(ckv_l, axis=1), jnp.stack(kr_l, axis=1),
            jnp.stack(sf_l, axis=1), jnp.stack(sb_l, axis=1))
```

```python
import functools

import jax
import jax.numpy as jnp
from jax import lax
from jax.experimental import pallas as pl
from jax.experimental.pallas import tpu as pltpu

D_MODEL = 1024
N_MOD = 6
RET_HEADS = 8
RET_DK = 64
RET_WIDTH = 512
MLA_HEADS = 8
MLA_NOPE = 64
MLA_ROPE = 32
MLA_V = 64
Q_RANK = 256
KV_RANK = 128
D_FF = 2816
GRID_W = 64
ROPE_BASE = 10000.0
EPS = 1e-6

LANES = 128
HEAD_PAD = LANES
N_PAIRS = RET_HEADS // 2
IN_COLS_PAD = 4 * RET_WIDTH + Q_RANK + KV_RANK + LANES
ROW_TILE = 512
SUB_TILE = 256
FF_CHUNK = 512
MIX_CHUNK = 256
Q_DEC_F, K_DEC_F, Q_DEC_B, K_DEC_B, C_DEC_F, C_DEC_B = range(6)
N_DVEC = 6
VMEM_LIMIT = 56 * 1024 * 1024

BF16 = jnp.bfloat16
F32 = jnp.float32
_NT = (((1,), (1,)), ((), ()))


def _dot(a, b):
    return jnp.dot(a, b, preferred_element_type=F32)


def _dot_nt(a, b):
    return lax.dot_general(a, b, _NT, preferred_element_type=F32)


def _rms(x, g):
    return x * lax.rsqrt(jnp.mean(x * x, axis=-1, keepdims=True) + EPS) * g


def _silu(x):
    return x * jax.nn.sigmoid(x)


def _masked_mean_sq(x, mask, n):
    return jnp.sum(jnp.where(mask, x * x, 0.0), axis=-1, keepdims=True) * (1.0 / n)


def _rope(x, c, sa, sb, shift):
    return x * c + pltpu.roll(x, shift, 1) * sa + pltpu.roll(x, LANES - shift, 1) * sb


def _mod_kernel(c_ref, w_ref, b_ref, o_ref):
    a = _silu(c_ref[...])
    w = w_ref[...]
    a_hi = a.astype(BF16)
    a_lo = (a - a_hi.astype(F32)).astype(BF16)
    w_hi = w.astype(BF16)
    w_lo = (w - w_hi.astype(F32)).astype(BF16)
    o_ref[...] = _dot(a_hi, w_hi) + _dot(a_lo, w_hi) + _dot(a_hi, w_lo) + b_ref[...]


def _modulation(conds, w_mod, b_mod):
    depth, _, n = w_mod.shape
    tn = 1536
    return pl.pallas_call(
        _mod_kernel,
        grid=(depth, n // tn),
        in_specs=[
            pl.BlockSpec((8, D_MODEL), lambda l, j: (0, 0)),
            pl.BlockSpec((None, D_MODEL, tn), lambda l, j: (l, 0, j)),
            pl.BlockSpec((None, 1, tn), lambda l, j: (l, 0, j)),
        ],
        out_specs=pl.BlockSpec((None, 8, tn), lambda l, j: (l, 0, j)),
        out_shape=jax.ShapeDtypeStruct((depth, 8, n), F32),
        compiler_params=pltpu.CompilerParams(
            dimension_semantics=("arbitrary", "arbitrary"), vmem_limit_bytes=VMEM_LIMIT),
        name="modulation",
    )(conds, w_mod, b_mod.reshape(depth, 1, n))


def _norm_kn(kv, gkn):
    lane = lax.broadcasted_iota(jnp.int32, (kv.shape[0], LANES), 1)
    lo = lane < MLA_NOPE
    out = []
    for h in range(MLA_HEADS):
        kvh = kv[:, h * HEAD_PAD:(h + 1) * HEAD_PAD]
        rs = lax.rsqrt(_masked_mean_sq(kvh, lo, MLA_NOPE) + EPS)
        out.append((kvh * jnp.where(lo, rs * gkn, 1.0)).astype(BF16))
    return out


def _cache_kernel(ckv_ref, wkvb_ref, gkn_ref, kvn_ref):
    kv = _dot(ckv_ref[...].astype(BF16), wkvb_ref[...])
    for h, kvh in enumerate(_norm_kn(kv, gkn_ref[...])):
        kvn_ref[:, h * HEAD_PAD:(h + 1) * HEAD_PAD] = kvh


def _cache_up(cache_ckv, wkvb, gkn):
    nb, depth, past, _ = cache_ckv.shape
    width = MLA_HEADS * HEAD_PAD
    return pl.pallas_call(
        _cache_kernel,
        grid=(depth, nb),
        in_specs=[
            pl.BlockSpec((None, None, past, KV_RANK), lambda l, b: (b, l, 0, 0)),
            pl.BlockSpec((None, KV_RANK, width), lambda l, b: (l, 0, 0)),
            pl.BlockSpec((None, 1, LANES), lambda l, b: (l, 0, 0)),
        ],
        out_specs=pl.BlockSpec((None, None, past, width), lambda l, b: (l, b, 0, 0)),
        out_shape=jax.ShapeDtypeStruct((depth, nb, past, width), BF16),
        compiler_params=pltpu.CompilerParams(
            dimension_semantics=("arbitrary", "arbitrary"), vmem_limit_bytes=VMEM_LIMIT),
        name="cache_up",
    )(cache_ckv, wkvb, gkn)


def _proj_kernel(*refs, rope):
    (x_ref, mod_ref, gmix_ref, win_ref, gqa_ref, wqb_ref, gkva_ref, wkvb_ref,
     gq_ref, gkn_ref, gkr_ref) = refs[:11]
    if rope:
        c64_ref, sa64_ref, sb64_ref, c32_ref, sa32_ref, sb32_ref = refs[11:17]
        rq_ref, rk_ref, rv_ref, rg_ref, qcat_ref, kvn_ref, krp_ref = refs[17:]
    else:
        rq_ref, rk_ref, rv_ref, rg_ref, qcat_ref, kvn_ref, krp_ref, ckv_ref, kro_ref = refs[11:]

    sh1 = mod_ref[0:1, :]
    sc1 = mod_ref[1:2, :]
    w = RET_WIDTH
    lane = lax.broadcasted_iota(jnp.int32, (SUB_TILE, LANES), 1)
    nope = lane < MLA_NOPE
    is_rope = jnp.logical_and(lane >= MLA_NOPE, lane < MLA_NOPE + MLA_ROPE)

    for r in range(x_ref.shape[0] // SUB_TILE):
        rs = slice(r * SUB_TILE, (r + 1) * SUB_TILE)
        h = (_rms(x_ref[rs, :], gmix_ref[...]) * (1.0 + sc1) + sh1).astype(BF16)
        z = _dot(h, win_ref[...])

        for j in range(w // LANES):
            sl = slice(j * LANES, (j + 1) * LANES)
            q = z[:, j * LANES:(j + 1) * LANES]
            k = z[:, w + j * LANES:w + (j + 1) * LANES] * (RET_DK ** -0.5)
            if rope:
                q = _rope(q, c64_ref[rs, :], sa64_ref[rs, :], sb64_ref[rs, :], 16)
                k = _rope(k, c64_ref[rs, :], sa64_ref[rs, :], sb64_ref[rs, :], 16)
            rq_ref[rs, sl] = q.astype(BF16)
            rk_ref[rs, sl] = k.astype(BF16)
        rv_ref[rs, :] = z[:, 2 * w:3 * w].astype(BF16)
        rg_ref[rs, :] = z[:, 3 * w:4 * w]

        o = 4 * w
        qa = z[:, o:o + Q_RANK]
        kva = z[:, o + Q_RANK:o + Q_RANK + KV_RANK]
        kr2 = z[:, o + Q_RANK + KV_RANK:]

        q = _dot(_rms(qa, gqa_ref[...]).astype(BF16), wqb_ref[...])
        for hh in range(MLA_HEADS):
            qh = q[:, hh * HEAD_PAD:(hh + 1) * HEAD_PAD]
            rs_n = lax.rsqrt(_masked_mean_sq(qh, nope, MLA_NOPE) + EPS)
            rs_r = lax.rsqrt(_masked_mean_sq(qh, is_rope, MLA_ROPE) + EPS)
            qn = qh * jnp.where(nope, rs_n, rs_r) * gq_ref[...]
            if rope:
                qn = _rope(qn, c32_ref[rs, :], sa32_ref[rs, :], sb32_ref[rs, :], 8)
            qcat_ref[rs, hh * HEAD_PAD:(hh + 1) * HEAD_PAD] = qn.astype(BF16)

        ckv = _rms(kva, gkva_ref[...])
        kv = _dot(ckv.astype(BF16), wkvb_ref[...])
        for hh, kvh in enumerate(_norm_kn(kv, gkn_ref[...])):
            kvn_ref[rs, hh * HEAD_PAD:(hh + 1) * HEAD_PAD] = kvh

        krn = kr2 * lax.rsqrt(_masked_mean_sq(kr2, lane < MLA_ROPE, MLA_ROPE) + EPS) * gkr_ref[...]
        if rope:
            krn = _rope(krn, c32_ref[rs, :], sa32_ref[rs, :], sb32_ref[rs, :], 8)
        else:
            ckv_ref[rs, :] = ckv
            kro_ref[rs, :] = krn[:, 0:MLA_ROPE]
        krp_ref[rs, :] = krn.astype(BF16)


def _proj(x, mod, layer, cond_row, wts, rope_tabs):
    m = x.shape[0]
    tm = ROW_TILE
    rope = rope_tabs is not None
    width = MLA_HEADS * HEAD_PAD

    def row(i):
        return (i, 0)

    def const2(i):
        return (0, 0)

    def lay3(i):
        return (layer, 0, 0)

    in_specs = [
        pl.BlockSpec((tm, D_MODEL), row),
        pl.BlockSpec((None, None, N_MOD, D_MODEL), lambda i: (layer, cond_row(i), 0, 0)),
        pl.BlockSpec((None, 1, D_MODEL), lay3),
        pl.BlockSpec((None, D_MODEL, IN_COLS_PAD), lay3),
        pl.BlockSpec((None, 1, Q_RANK), lay3),
        pl.BlockSpec((None, Q_RANK, width), lay3),
        pl.BlockSpec((None, 1, KV_RANK), lay3),
        pl.BlockSpec((None, KV_RANK, width), lay3),
        pl.BlockSpec((None, 1, LANES), lay3),
        pl.BlockSpec((None, 1, LANES), lay3),
        pl.BlockSpec((None, 1, LANES), lay3),
    ]
    args = [x, mod, wts["g_mix"], wts["w_in"], wts["g_qa"], wts["w_qb"], wts["g_kva"],
            wts["w_kvb"], wts["g_q"], wts["g_kn"], wts["g_kr"]]
    out_shape = [
        jax.ShapeDtypeStruct((m, RET_WIDTH), BF16),
        jax.ShapeDtypeStruct((m, RET_WIDTH), BF16),
        jax.ShapeDtypeStruct((m, RET_WIDTH), BF16),
        jax.ShapeDtypeStruct((m, RET_WIDTH), F32),
        jax.ShapeDtypeStruct((m, width), BF16),
        jax.ShapeDtypeStruct((m, width), BF16),
        jax.ShapeDtypeStruct((m, LANES), BF16),
    ]
    out_specs = [
        pl.BlockSpec((tm, RET_WIDTH), row), pl.BlockSpec((tm, RET_WIDTH), row),
        pl.BlockSpec((tm, RET_WIDTH), row), pl.BlockSpec((tm, RET_WIDTH), row),
        pl.BlockSpec((tm, width), row), pl.BlockSpec((tm, width), row),
        pl.BlockSpec((tm, LANES), row),
    ]
    if rope:
        n_lat = rope_tabs[0].shape[0]
        tiles = n_lat // tm
        in_specs += [pl.BlockSpec((tm, LANES), lambda i: (i % tiles, 0))] * 6
        args += list(rope_tabs)
    else:
        out_shape += [jax.ShapeDtypeStruct((m, KV_RANK), F32),
                      jax.ShapeDtypeStruct((m, MLA_ROPE), F32)]
        out_specs += [pl.BlockSpec((tm, KV_RANK), row), pl.BlockSpec((tm, MLA_ROPE), row)]

    return pl.pallas_call(
        functools.partial(_proj_kernel, rope=rope),
        grid=(m // tm,),
        in_specs=in_specs,
        out_specs=out_specs,
        out_shape=out_shape,
        compiler_params=pltpu.CompilerParams(
            dimension_semantics=("arbitrary",), vmem_limit_bytes=VMEM_LIMIT),
        name="proj_latent" if rope else "proj_context",
    )(*args)


def _log_gamma(p):
    return jnp.log1p(-jnp.exp2(-p))


def _decay_kernel(p_ref, dmask_ref, dvec_ref):
    c = MIX_CHUNK
    base = pl.program_id(0) * (2 * RET_HEADS)
    pair = pl.program_id(1)
    ri = lax.broadcasted_iota(jnp.int32, (c, c), 0)
    ci = lax.broadcasted_iota(jnp.int32, (c, c), 1)
    dif = (ri - ci).astype(F32)
    for e in range(2):
        h = 2 * pair + e
        lg_f = _log_gamma(jnp.full((c, c), p_ref[base + h], F32))
        lg_b = _log_gamma(jnp.full((c, c), p_ref[base + RET_HEADS + h], F32))
        fwd = jnp.where(dif >= 0, jnp.exp(jnp.maximum(dif, 0.0) * lg_f), 0.0)
        bwd = jnp.where(dif <= 0, jnp.exp(jnp.maximum(-dif, 0.0) * lg_b), 0.0)
        dmask_ref[e] = fwd + bwd
    lane = lax.broadcasted_iota(jnp.int32, (c, LANES), 1)
    rowf = lax.broadcasted_iota(jnp.int32, (c, LANES), 0).astype(F32)
    lo = lane < RET_DK
    lg_f = _log_gamma(jnp.where(lo, p_ref[base + 2 * pair], p_ref[base + 2 * pair + 1]))
    lg_b = _log_gamma(jnp.where(lo, p_ref[base + RET_HEADS + 2 * pair],
                                p_ref[base + RET_HEADS + 2 * pair + 1]))
    dvec_ref[Q_DEC_F] = jnp.exp((rowf + 1.0) * lg_f)
    dvec_ref[K_DEC_F] = jnp.exp((c - 1.0 - rowf) * lg_f)
    dvec_ref[Q_DEC_B] = jnp.exp((c - rowf) * lg_b)
    dvec_ref[K_DEC_B] = jnp.exp(rowf * lg_b)
    dvec_ref[C_DEC_F] = jnp.exp(c * lg_f)
    dvec_ref[C_DEC_B] = jnp.exp(c * lg_b)


def _decay_tables(decay_p):
    depth = decay_p.shape[0]
    c = MIX_CHUNK
    return pl.pallas_call(
        _decay_kernel,
        grid=(depth, N_PAIRS),
        in_specs=[pl.BlockSpec(memory_space=pltpu.SMEM)],
        out_specs=[
            pl.BlockSpec((None, 2, c, c), lambda l, p: (l, p, 0, 0)),
            pl.BlockSpec((None, None, N_DVEC, c, LANES), lambda l, p: (l, p, 0, 0, 0)),
        ],
        out_shape=[
            jax.ShapeDtypeStruct((depth, RET_HEADS, c, c), F32),
            jax.ShapeDtypeStruct((depth, N_PAIRS, N_DVEC, c, LANES), F32),
        ],
        compiler_params=pltpu.CompilerParams(
            dimension_semantics=("arbitrary", "arbitrary"), vmem_limit_bytes=VMEM_LIMIT),
        name="decay_tables",
    )(decay_p.reshape(-1))


def _mix_kernel(*refs, n_seq, latent):
    c = MIX_CHUNK
    nc = n_seq // c
    (rq_ref, rk_ref, rv_ref, rg_ref, qcat_ref, kvn_ref, krp_ref, gng_ref, gnb_ref,
     dmask_ref, dvec_ref) = refs[:11]
    if latent:
        kvc_ref, krc_ref, sf0_ref, sb0_ref, ret_ref, attn_ref, st_scr = refs[11:]
    else:
        ret_ref, attn_ref, sf_ref, sb_ref = refs[11:]

    lane = lax.broadcasted_iota(jnp.int32, (c, LANES), 1)
    lo = lane < RET_DK
    sq_r = lax.broadcasted_iota(jnp.int32, (LANES, LANES), 0)
    sq_c = lax.broadcasted_iota(jnp.int32, (LANES, LANES), 1)
    blockdiag = (sq_r < RET_DK) == (sq_c < RET_DK)

    def cols(j):
        return slice(j * LANES, (j + 1) * LANES)

    def state_update(pair, d, rows):
        kp = rk_ref[rows, cols(pair)]
        vp = rv_ref[rows, cols(pair)]
        kdt = (kp.astype(F32) * dvec_ref[pair, K_DEC_B if d else K_DEC_F]).T.astype(BF16)
        return jnp.where(blockdiag, _dot(kdt, vp), 0.0)

    for pair in range(N_PAIRS):
        for d in range(2):
            if latent:
                s = (sb0_ref if d else sf0_ref)[pair]
                cdec = dvec_ref[pair, C_DEC_B if d else C_DEC_F][0:LANES, :]
                order = list(range(nc - 1, -1, -1)) if d else list(range(nc))
                for idx, ch in enumerate(order):
                    st_scr[d, pair, ch] = s.astype(BF16)
                    if idx < nc - 1:
                        s = s * cdec + state_update(pair, d, pl.ds(ch * c, c))
            else:
                s = state_update(pair, d, pl.ds(0, c))
                st_ref = sb_ref if d else sf_ref
                st_ref[2 * pair] = s[0:RET_DK, 0:RET_DK]
                st_ref[2 * pair + 1] = s[RET_DK:, RET_DK:]

    scale = (MLA_NOPE + MLA_ROPE) ** -0.5
    nope_n = lax.broadcasted_iota(jnp.int32, (n_seq, LANES), 1) < MLA_NOPE
    if latent:
        nope_c = lax.broadcasted_iota(jnp.int32, (kvc_ref.shape[0], LANES), 1) < MLA_NOPE

    def chunk_body(ch, carry):
        rows = pl.ds(pl.multiple_of(ch * c, c), c)

        for pair in range(N_PAIRS):
            qp = rq_ref[rows, cols(pair)]
            kp = rk_ref[rows, cols(pair)]
            vp = rv_ref[rows, cols(pair)]
            zero = jnp.zeros_like(qp)
            a0 = (_dot_nt(jnp.where(lo, qp, zero), kp) * dmask_ref[2 * pair]).astype(BF16)
            a1 = (_dot_nt(jnp.where(lo, zero, qp), kp) * dmask_ref[2 * pair + 1]).astype(BF16)
            tot = jnp.where(lo, _dot(a0, vp), _dot(a1, vp))
            if latent:
                tot = (tot + _dot(qp, st_scr[0, pair, ch]) * dvec_ref[pair, Q_DEC_F]
                       + _dot(qp, st_scr[1, pair, ch]) * dvec_ref[pair, Q_DEC_B])
            inv = 1.0 / RET_DK
            m0 = jnp.sum(jnp.where(lo, tot, 0.0), axis=-1, keepdims=True) * inv
            m1 = jnp.sum(jnp.where(lo, 0.0, tot), axis=-1, keepdims=True) * inv
            y = tot - jnp.where(lo, m0, m1)
            v0 = jnp.sum(jnp.where(lo, y * y, 0.0), axis=-1, keepdims=True) * inv
            v1 = jnp.sum(jnp.where(lo, 0.0, y * y), axis=-1, keepdims=True) * inv
            yn = (y * lax.rsqrt(jnp.where(lo, v0, v1) + EPS) * gng_ref[:, cols(pair)]
                  + gnb_ref[:, cols(pair)])
            ret_ref[rows, cols(pair)] = (yn * _silu(rg_ref[rows, cols(pair)])).astype(BF16)

        krp = krp_ref[...]
        for pair in range(N_PAIRS):
            outs = []
            for e in range(2):
                h = 2 * pair + e
                qc = qcat_ref[rows, cols(h)]
                kv = kvn_ref[:, cols(h)]
                s = _dot_nt(qc, jnp.where(nope_n, kv, krp)) * scale
                m = jnp.max(s, axis=-1, keepdims=True)
                if latent:
                    kv_c = kvc_ref[:, cols(h)]
                    s2 = _dot_nt(qc, jnp.where(nope_c, kv_c, krc_ref[...])) * scale
                    m = jnp.maximum(m, jnp.max(s2, axis=-1, keepdims=True))
                    p2 = jnp.exp(s2 - m)
                p = jnp.exp(s - m)
                den = jnp.sum(p, axis=-1, keepdims=True)
                acc = _dot(p.astype(BF16), kv)
                if latent:
                    den = den + jnp.sum(p2, axis=-1, keepdims=True)
                    acc = acc + _dot(p2.astype(BF16), kv_c)
                outs.append(acc / den)
            attn_ref[rows, cols(pair)] = jnp.where(
                lo, pltpu.roll(outs[0], MLA_V, 1), outs[1]).astype(BF16)
        return carry

    if nc == 1:
        chunk_body(0, 0)
    else:
        lax.fori_loop(0, nc, chunk_body, 0)


def _mix(proj_out, dmask, dvec, gn_g, gn_b, layer, n_batch, n_seq, latent_in):
    rq, rk, rv, rg, qcat, kvn, krp = proj_out[:7]
    latent = latent_in is not None
    m = n_batch * n_seq
    c = MIX_CHUNK
    width = MLA_HEADS * HEAD_PAD

    def row(b):
        return (b, 0)

    def lay3(b):
        return (layer, 0, 0)

    once = pl.Buffered(1)
    in_specs = [
        pl.BlockSpec((n_seq, RET_WIDTH), row), pl.BlockSpec((n_seq, RET_WIDTH), row),
        pl.BlockSpec((n_seq, RET_WIDTH), row), pl.BlockSpec((n_seq, RET_WIDTH), row),
        pl.BlockSpec((n_seq, width), row), pl.BlockSpec((n_seq, width), row),
        pl.BlockSpec((n_seq, LANES), row),
        pl.BlockSpec((None, 1, RET_WIDTH), lay3), pl.BlockSpec((None, 1, RET_WIDTH), lay3),
        pl.BlockSpec((None, RET_HEADS, c, c), lambda b: (layer, 0, 0, 0), pipeline_mode=once),
        pl.BlockSpec((None, N_PAIRS, N_DVEC, c, LANES), lambda b: (layer, 0, 0, 0, 0),
                     pipeline_mode=once),
    ]
    args = [rq, rk, rv, rg, qcat, kvn, krp, gn_g, gn_b, dmask, dvec]
    out_shape = [jax.ShapeDtypeStruct((m, RET_WIDTH), BF16),
                 jax.ShapeDtypeStruct((m, MLA_HEADS * MLA_V), BF16)]
    out_specs = [pl.BlockSpec((n_seq, RET_WIDTH), row), pl.BlockSpec((n_seq, RET_WIDTH), row)]
    scratch = []
    if latent:
        kvn_c, krp_c, s_f0, s_b0 = latent_in
        past = kvn_c.shape[2]
        st_spec = pl.BlockSpec((None, None, N_PAIRS, LANES, LANES), lambda b: (b, layer, 0, 0, 0))
        in_specs += [
            pl.BlockSpec((None, None, past, width), lambda b: (layer, b, 0, 0)),
            pl.BlockSpec((None, None, past, LANES), lambda b: (b, layer, 0, 0)),
            st_spec, st_spec,
        ]
        args += [kvn_c, krp_c, s_f0, s_b0]
        scratch = [pltpu.VMEM((2, N_PAIRS, n_seq // c, LANES, LANES), BF16)]
    else:
        st = jax.ShapeDtypeStruct((n_batch, RET_HEADS, RET_DK, RET_DK), F32)
        out_shape += [st, st]
        st_spec = pl.BlockSpec((None, RET_HEADS, RET_DK, RET_DK), lambda b: (b, 0, 0, 0))
        out_specs += [st_spec, st_spec]

    return pl.pallas_call(
        functools.partial(_mix_kernel, n_seq=n_seq, latent=latent),
        grid=(n_batch,),
        in_specs=in_specs,
        out_specs=out_specs,
        out_shape=out_shape,
        scratch_shapes=scratch,
        compiler_params=pltpu.CompilerParams(
            dimension_semantics=("arbitrary",), vmem_limit_bytes=VMEM_LIMIT),
        name="mix_latent" if latent else "mix_context",
    )(*args)


def _out_kernel(x_ref, ret_ref, attn_ref, mod_ref, gffn_ref, wo_ref, wfi_ref, wfo_ref, o_ref):
    gt1 = mod_ref[2:3, :]
    sh2 = mod_ref[3:4, :]
    sc2 = mod_ref[4:5, :]
    gt2 = mod_ref[5:6, :]
    for r in range(x_ref.shape[0] // SUB_TILE):
        rs = slice(r * SUB_TILE, (r + 1) * SUB_TILE)
        mixed = (_dot(ret_ref[rs, :], wo_ref[0:RET_WIDTH, :])
                 + _dot(attn_ref[rs, :], wo_ref[RET_WIDTH:, :]))
        x1 = x_ref[rs, :] + gt1 * mixed
        h = (_rms(x1, gffn_ref[...]) * (1.0 + sc2) + sh2).astype(BF16)
        acc = jnp.zeros_like(x1)
        for c0 in range(0, D_FF, FF_CHUNK):
            cw = min(FF_CHUNK, D_FF - c0)
            gate = _dot(h, wfi_ref[:, c0:c0 + cw])
            up = _dot(h, wfi_ref[:, D_FF + c0:D_FF + c0 + cw])
            acc = acc + _dot((_silu(gate) * up).astype(BF16), wfo_ref[c0:c0 + cw, :])
        o_ref[rs, :] = x1 + gt2 * acc


def _out(x, ret, attn, mod, layer, cond_row, wts):
    m = x.shape[0]
    tm = ROW_TILE

    def row(i):
        return (i, 0)

    def lay3(i):
        return (layer, 0, 0)

    once = pl.Buffered(1)
    return pl.pallas_call(
        _out_kernel,
        grid=(m // tm,),
        in_specs=[
            pl.BlockSpec((tm, D_MODEL), row),
            pl.BlockSpec((tm, RET_WIDTH), row),
            pl.BlockSpec((tm, RET_WIDTH), row),
            pl.BlockSpec((None, None, N_MOD, D_MODEL), lambda i: (layer, cond_row(i), 0, 0)),
            pl.BlockSpec((None, 1, D_MODEL), lay3),
            pl.BlockSpec((None, D_MODEL, D_MODEL), lay3, pipeline_mode=once),
            pl.BlockSpec((None, D_MODEL, 2 * D_FF), lay3, pipeline_mode=once),
            pl.BlockSpec((None, D_FF, D_MODEL), lay3, pipeline_mode=once),
        ],
        out_specs=pl.BlockSpec((tm, D_MODEL), row),
        out_shape=jax.ShapeDtypeStruct((m, D_MODEL), F32),
        compiler_params=pltpu.CompilerParams(
            dimension_semantics=("arbitrary",), vmem_limit_bytes=VMEM_LIMIT),
        name="out_ffn",
    )(x, ret, attn, mod, wts["g_ffn"], wts["w_o"], wts["w_ffn_in"], wts["w_ffn_out"])


def _rope_tables(n_lat):
    pos = jnp.arange(n_lat)
    row = (pos // GRID_W).astype(F32)[:, None]
    col = (pos % GRID_W).astype(F32)[:, None]
    lane = jnp.arange(LANES)[None, :]

    def tables(d, start, period):
        rel = (lane - start) % period
        active = jnp.logical_and(lane >= start, rel < d)
        half = d // 2
        nf = half // 2
        inv = ROPE_BASE ** (-((rel % nf).astype(F32)) / nf)
        ang = jnp.where(rel < half, row, col) * inv
        cos, sin = jnp.cos(ang), jnp.sin(ang)
        first = (rel % half) < nf
        c = jnp.where(active, cos, 1.0)
        sa = jnp.where(jnp.logical_and(active, jnp.logical_not(first)), sin, 0.0)
        sb = jnp.where(jnp.logical_and(active, first), -sin, 0.0)
        return c, sa, sb

    return tables(RET_DK, 0, RET_DK) + tables(MLA_ROPE, MLA_NOPE, LANES)


def _prepare_weights(g_norm_mix, g_norm_ffn, w_in, g_q_a, w_q_b, g_kv_a, w_kv_b, g_qn, g_qr, g_kn,
                     g_kr, w_o, w_ffn_in, w_ffn_out):
    depth = w_in.shape[0]
    body = 4 * RET_WIDTH + Q_RANK + KV_RANK
    kr_cols = w_in[:, :, body:]
    pad32 = jnp.zeros_like(kr_cols)
    w_in_p = jnp.concatenate([w_in[:, :, :body], kr_cols, pad32, kr_cols, pad32], axis=-1)
    w_qb = w_q_b.reshape(depth, Q_RANK, MLA_HEADS, MLA_NOPE + MLA_ROPE)
    w_qb = jnp.pad(w_qb, ((0, 0), (0, 0), (0, 0), (0, HEAD_PAD - MLA_NOPE - MLA_ROPE)))
    zeros32 = jnp.zeros((depth, MLA_ROPE), F32)
    return {
        "g_mix": g_norm_mix[:, None, :],
        "g_ffn": g_norm_ffn[:, None, :],
        "w_in": w_in_p.astype(BF16),
        "g_qa": g_q_a[:, None, :],
        "w_qb": w_qb.reshape(depth, Q_RANK, MLA_HEADS * HEAD_PAD).astype(BF16),
        "g_kva": g_kv_a[:, None, :],
        "w_kvb": w_kv_b.astype(BF16),
        "g_q": jnp.concatenate([g_qn, g_qr, zeros32], axis=-1)[:, None, :],
        "g_kn": jnp.concatenate([g_kn, jnp.ones((depth, MLA_V), F32)], axis=-1)[:, None, :],
        "g_kr": jnp.concatenate([g_kr, zeros32, g_kr, zeros32], axis=-1)[:, None, :],
        "w_o": w_o.astype(BF16),
        "w_ffn_in": w_ffn_in.astype(BF16),
        "w_ffn_out": w_ffn_out.astype(BF16),
    }


def _blockdiag_states(s):
    b, l = s.shape[:2]
    s = s.reshape(b, l, N_PAIRS, 2, RET_DK, RET_DK)
    z = jnp.zeros_like(s[:, :, :, 0])
    top = jnp.concatenate([s[:, :, :, 0], z], axis=-1)
    bot = jnp.concatenate([z, s[:, :, :, 1]], axis=-1)
    return jnp.concatenate([top, bot], axis=-2)


def kernel(x_prompt, x_sample, cache_ckv, cache_krope, state_ret_fwd, state_ret_bwd, c, c_ctx,
           w_mod, b_mod, g_norm_mix, g_norm_ffn, w_in, g_q_a, w_q_b, g_kv_a, w_kv_b,
           g_qn, g_qr, g_kn, g_kr, ret_p_fwd, ret_p_bwd, g_ret_gn, b_ret_gn, w_o,
           w_ffn_in, w_ffn_out):
    batch, seq, _ = x_prompt.shape
    dec_batch, dec_seq, _ = x_sample.shape
    depth = w_in.shape[0]

    wts = _prepare_weights(g_norm_mix, g_norm_ffn, w_in, g_q_a, w_q_b, g_kv_a, w_kv_b, g_qn, g_qr,
                           g_kn, g_kr, w_o, w_ffn_in, w_ffn_out)
    conds = jnp.concatenate([c_ctx[None], c, jnp.zeros((8 - 1 - dec_batch, D_MODEL), F32)], axis=0)
    mod = _modulation(conds, w_mod, b_mod).reshape(depth, 8, N_MOD, D_MODEL)
    dmask, dvec = _decay_tables(jnp.stack([ret_p_fwd, ret_p_bwd], axis=1))
    gn_g = g_ret_gn[:, None, :]
    gn_b = b_ret_gn[:, None, :]

    rope_tabs = _rope_tables(dec_seq)
    kvn_cache = _cache_up(cache_ckv, wts["w_kvb"], wts["g_kn"])
    krp_cache = jnp.pad(cache_krope, ((0, 0), (0, 0), (0, 0), (MLA_NOPE, LANES - MLA_NOPE - MLA_ROPE)))
    krp_cache = krp_cache.astype(BF16)
    s_f0 = _blockdiag_states(state_ret_fwd)
    s_b0 = _blockdiag_states(state_ret_bwd)

    lat_tiles = dec_seq // ROW_TILE

    def ctx_row(i):
        return 0

    def lat_row(i):
        return 1 + i // lat_tiles

    x = x_prompt.reshape(batch * seq, D_MODEL)
    y = x_sample.reshape(dec_batch * dec_seq, D_MODEL)
    ckv_l, kr_l, sf_l, sb_l = [], [], [], []
    for l in range(depth):
        pr = _proj(x, mod, l, ctx_row, wts, None)
        ret, attn, s_f, s_b = _mix(pr, dmask, dvec, gn_g, gn_b, l, batch, seq, None)
        x = _out(x, ret, attn, mod, l, ctx_row, wts)
        ckv_l.append(pr[7].reshape(batch, seq, KV_RANK))
        kr_l.append(pr[8].reshape(batch, seq, MLA_ROPE))
        sf_l.append(s_f)
        sb_l.append(s_b)

        pr = _proj(y, mod, l, lat_row, wts, rope_tabs)
        ret, attn = _mix(pr, dmask, dvec, gn_g, gn_b, l, dec_batch, dec_seq,
                         (kvn_cache, krp_cache, s_f0, s_b0))
        y = _out(y, ret, attn, mod, l, lat_row, wts)

    return (x.reshape(batch, seq, D_MODEL), y.reshape(dec_batch, dec_seq, D_MODEL),
            jnp.stack(ckv_l, axis=1), jnp.stack(kr_l, axis=1),
            jnp.stack(sf_l, axis=1), jnp.stack(sb_l, axis=1))
```

```python
import functools

import jax
import jax.numpy as jnp
import numpy as np
from jax import lax
from jax.experimental import pallas as pl
from jax.experimental.pallas import tpu as pltpu

D_MODEL = 1024
N_MOD = 6
RET_HEADS = 8
RET_DK = 64
RET_WIDTH = 512
MLA_HEADS = 8
MLA_NOPE = 64
MLA_ROPE = 32
MLA_V = 64
Q_RANK = 256
KV_RANK = 128
D_FF = 2816
GRID_W = 64
ROPE_BASE = 10000.0
EPS = 1e-6

LANES = 128
HEAD_PAD = LANES
N_PAIRS = RET_HEADS // 2
IN_COLS_PAD = 4 * RET_WIDTH + Q_RANK + KV_RANK + LANES
ROW_TILE = 512
SUB_TILE = 256
FF_CHUNK = 512
MIX_CHUNK = 256
Q_DEC_F, K_DEC_F, Q_DEC_B, K_DEC_B, C_DEC_F, C_DEC_B = range(6)
N_DVEC = 6
VMEM_LIMIT = 56 * 1024 * 1024

BF16 = jnp.bfloat16
F32 = jnp.float32
_NT = (((1,), (1,)), ((), ()))


def _dot(a, b):
    return jnp.dot(a, b, preferred_element_type=F32)


def _dot_nt(a, b):
    return lax.dot_general(a, b, _NT, preferred_element_type=F32)


def _rms(x, g):
    return x * lax.rsqrt(jnp.mean(x * x, axis=-1, keepdims=True) + EPS) * g


def _silu(x):
    return x * jax.nn.sigmoid(x)


def _masked_mean_sq(x, mask, n):
    return jnp.sum(jnp.where(mask, x * x, 0.0), axis=-1, keepdims=True) * (1.0 / n)


def _rope(x, c, sa, sb, shift):
    return x * c + pltpu.roll(x, shift, 1) * sa + pltpu.roll(x, LANES - shift, 1) * sb


def _mod_kernel(c_ref, w_ref, b_ref, o_ref):
    a = _silu(c_ref[...])
    w = w_ref[...]
    a_hi = a.astype(BF16)
    a_lo = (a - a_hi.astype(F32)).astype(BF16)
    w_hi = w.astype(BF16)
    w_lo = (w - w_hi.astype(F32)).astype(BF16)
    o_ref[...] = _dot(a_hi, w_hi) + _dot(a_lo, w_hi) + _dot(a_hi, w_lo) + b_ref[...]


def _modulation(conds, w_mod, b_mod):
    depth, _, n = w_mod.shape
    tn = 1536
    return pl.pallas_call(
        _mod_kernel,
        grid=(depth, n // tn),
        in_specs=[
            pl.BlockSpec((8, D_MODEL), lambda l, j: (0, 0)),
            pl.BlockSpec((None, D_MODEL, tn), lambda l, j: (l, 0, j)),
            pl.BlockSpec((None, 1, tn), lambda l, j: (l, 0, j)),
        ],
        out_specs=pl.BlockSpec((None, 8, tn), lambda l, j: (l, 0, j)),
        out_shape=jax.ShapeDtypeStruct((depth, 8, n), F32),
        compiler_params=pltpu.CompilerParams(
            dimension_semantics=("arbitrary", "arbitrary"), vmem_limit_bytes=VMEM_LIMIT),
        name="modulation",
    )(conds, w_mod, b_mod.reshape(depth, 1, n))


def _norm_kn(kv, gkn):
    lane = lax.broadcasted_iota(jnp.int32, (kv.shape[0], LANES), 1)
    lo = lane < MLA_NOPE
    out = []
    for h in range(MLA_HEADS):
        kvh = kv[:, h * HEAD_PAD:(h + 1) * HEAD_PAD]
        rs = lax.rsqrt(_masked_mean_sq(kvh, lo, MLA_NOPE) + EPS)
        out.append((kvh * jnp.where(lo, rs * gkn, 1.0)).astype(BF16))
    return out


def _cache_kernel(ckv_ref, wkvb_ref, gkn_ref, kvn_ref):
    kv = _dot(ckv_ref[...].astype(BF16), wkvb_ref[...])
    for h, kvh in enumerate(_norm_kn(kv, gkn_ref[...])):
        kvn_ref[:, h * HEAD_PAD:(h + 1) * HEAD_PAD] = kvh


def _cache_up(cache_ckv, wkvb, gkn):
    nb, depth, past, _ = cache_ckv.shape
    width = MLA_HEADS * HEAD_PAD
    return pl.pallas_call(
        _cache_kernel,
        grid=(depth, nb),
        in_specs=[
            pl.BlockSpec((None, None, past, KV_RANK), lambda l, b: (b, l, 0, 0)),
            pl.BlockSpec((None, KV_RANK, width), lambda l, b: (l, 0, 0)),
            pl.BlockSpec((None, 1, LANES), lambda l, b: (l, 0, 0)),
        ],
        out_specs=pl.BlockSpec((None, None, past, width), lambda l, b: (l, b, 0, 0)),
        out_shape=jax.ShapeDtypeStruct((depth, nb, past, width), BF16),
        compiler_params=pltpu.CompilerParams(
            dimension_semantics=("arbitrary", "arbitrary"), vmem_limit_bytes=VMEM_LIMIT),
        name="cache_up",
    )(cache_ckv, wkvb, gkn)


def _mod_vec(mod_ref, cond, k):
    return mod_ref[pl.ds(cond, 1), k * D_MODEL:(k + 1) * D_MODEL]


def _proj_kernel(*refs, rope, cond_row):
    (x_ref, mod_ref, gmix_ref, win_ref, gqa_ref, wqb_ref, gkva_ref, wkvb_ref,
     gq_ref, gkn_ref, gkr_ref) = refs[:11]
    if rope:
        c64_ref, sa64_ref, sb64_ref, c32_ref, sa32_ref, sb32_ref = refs[11:17]
        rq_ref, rk_ref, rv_ref, rg_ref, qcat_ref, kvn_ref, krp_ref = refs[17:]
    else:
        rq_ref, rk_ref, rv_ref, rg_ref, qcat_ref, kvn_ref, krp_ref, ckv_ref, kro_ref = refs[13:]

    cond = cond_row(pl.program_id(0))
    sh1 = _mod_vec(mod_ref, cond, 0)
    sc1 = _mod_vec(mod_ref, cond, 1)
    w = RET_WIDTH
    lane = lax.broadcasted_iota(jnp.int32, (SUB_TILE, LANES), 1)
    nope = lane < MLA_NOPE
    is_rope = jnp.logical_and(lane >= MLA_NOPE, lane < MLA_NOPE + MLA_ROPE)

    for r in range(x_ref.shape[0] // SUB_TILE):
        rs = slice(r * SUB_TILE, (r + 1) * SUB_TILE)
        h = (_rms(x_ref[rs, :], gmix_ref[...]) * (1.0 + sc1) + sh1).astype(BF16)
        z = _dot(h, win_ref[...])

        for j in range(w // LANES):
            sl = slice(j * LANES, (j + 1) * LANES)
            q = z[:, j * LANES:(j + 1) * LANES]
            k = z[:, w + j * LANES:w + (j + 1) * LANES] * (RET_DK ** -0.5)
            if rope:
                q = _rope(q, c64_ref[rs, :], sa64_ref[rs, :], sb64_ref[rs, :], 16)
                k = _rope(k, c64_ref[rs, :], sa64_ref[rs, :], sb64_ref[rs, :], 16)
            rq_ref[rs, sl] = q.astype(BF16)
            rk_ref[rs, sl] = k.astype(BF16)
        rv_ref[rs, :] = z[:, 2 * w:3 * w].astype(BF16)
        rg_ref[rs, :] = z[:, 3 * w:4 * w]

        o = 4 * w
        qa = z[:, o:o + Q_RANK]
        kva = z[:, o + Q_RANK:o + Q_RANK + KV_RANK]
        kr2 = z[:, o + Q_RANK + KV_RANK:]

        q = _dot(_rms(qa, gqa_ref[...]).astype(BF16), wqb_ref[...])
        for hh in range(MLA_HEADS):
            qh = q[:, hh * HEAD_PAD:(hh + 1) * HEAD_PAD]
            rs_n = lax.rsqrt(_masked_mean_sq(qh, nope, MLA_NOPE) + EPS)
            rs_r = lax.rsqrt(_masked_mean_sq(qh, is_rope, MLA_ROPE) + EPS)
            qn = qh * jnp.where(nope, rs_n, rs_r) * gq_ref[...]
            if rope:
                qn = _rope(qn, c32_ref[rs, :], sa32_ref[rs, :], sb32_ref[rs, :], 8)
            qcat_ref[rs, hh * HEAD_PAD:(hh + 1) * HEAD_PAD] = qn.astype(BF16)

        ckv = _rms(kva, gkva_ref[...])
        kv = _dot(ckv.astype(BF16), wkvb_ref[...])
        for hh, kvh in enumerate(_norm_kn(kv, gkn_ref[...])):
            kvn_ref[rs, hh * HEAD_PAD:(hh + 1) * HEAD_PAD] = kvh

        krn = kr2 * lax.rsqrt(_masked_mean_sq(kr2, lane < MLA_ROPE, MLA_ROPE) + EPS) * gkr_ref[...]
        krp = pltpu.roll(krn, MLA_NOPE, 1)
        if rope:
            krp = _rope(krp, c32_ref[rs, :], sa32_ref[rs, :], sb32_ref[rs, :], 8)
        else:
            ckv_ref[r, :, :] = ckv
            kro_ref[r, :, :] = krn[:, 0:MLA_ROPE]
        krp_ref[rs, :] = krp.astype(BF16)


def _proj(x, mod, layer, cond_row, wts, rope_tabs, caches):
    m = x.shape[0]
    tm = ROW_TILE
    rope = rope_tabs is not None
    width = MLA_HEADS * HEAD_PAD

    def row(i):
        return (i, 0)

    def lay3(i):
        return (layer, 0, 0)

    in_specs = [
        pl.BlockSpec((tm, D_MODEL), row),
        pl.BlockSpec((None, 8, N_MOD * D_MODEL), lay3),
        pl.BlockSpec((None, 1, D_MODEL), lay3),
        pl.BlockSpec((None, D_MODEL, IN_COLS_PAD), lay3),
        pl.BlockSpec((None, 1, Q_RANK), lay3),
        pl.BlockSpec((None, Q_RANK, width), lay3),
        pl.BlockSpec((None, 1, KV_RANK), lay3),
        pl.BlockSpec((None, KV_RANK, width), lay3),
        pl.BlockSpec((None, 1, LANES), lay3),
        pl.BlockSpec((None, 1, LANES), lay3),
        pl.BlockSpec((None, 1, LANES), lay3),
    ]
    args = [x, mod, wts["g_mix"], wts["w_in"], wts["g_qa"], wts["w_qb"], wts["g_kva"],
            wts["w_kvb"], wts["g_q"], wts["g_kn"], wts["g_kr"]]
    out_shape = [
        jax.ShapeDtypeStruct((m, RET_WIDTH), BF16),
        jax.ShapeDtypeStruct((m, RET_WIDTH), BF16),
        jax.ShapeDtypeStruct((m, RET_WIDTH), BF16),
        jax.ShapeDtypeStruct((m, RET_WIDTH), F32),
        jax.ShapeDtypeStruct((m, width), BF16),
        jax.ShapeDtypeStruct((m, width), BF16),
        jax.ShapeDtypeStruct((m, LANES), BF16),
    ]
    out_specs = [
        pl.BlockSpec((tm, RET_WIDTH), row), pl.BlockSpec((tm, RET_WIDTH), row),
        pl.BlockSpec((tm, RET_WIDTH), row), pl.BlockSpec((tm, RET_WIDTH), row),
        pl.BlockSpec((tm, width), row), pl.BlockSpec((tm, width), row),
        pl.BlockSpec((tm, LANES), row),
    ]
    aliases = {}
    if rope:
        n_lat = rope_tabs[0].shape[0]
        tiles = n_lat // tm
        in_specs += [pl.BlockSpec((tm, LANES), lambda i: (i % tiles, 0))] * 6
        args += list(rope_tabs)
    else:
        seq = caches[0].shape[2]
        assert seq == SUB_TILE
        nb = tm // seq
        in_specs += [pl.BlockSpec(memory_space=pl.ANY)] * 2
        aliases = {len(args): len(out_shape), len(args) + 1: len(out_shape) + 1}
        args += list(caches)
        out_shape += [jax.ShapeDtypeStruct(c.shape, c.dtype) for c in caches]
        out_specs += [pl.BlockSpec((nb, None, seq, KV_RANK), lambda i: (i, layer, 0, 0)),
                      pl.BlockSpec((nb, None, seq, MLA_ROPE), lambda i: (i, layer, 0, 0))]

    return pl.pallas_call(
        functools.partial(_proj_kernel, rope=rope, cond_row=cond_row),
        grid=(m // tm,),
        in_specs=in_specs,
        out_specs=out_specs,
        out_shape=out_shape,
        input_output_aliases=aliases,
        compiler_params=pltpu.CompilerParams(
            dimension_semantics=("arbitrary",), vmem_limit_bytes=VMEM_LIMIT),
        name="proj_latent" if rope else "proj_context",
    )(*args)


def _log_gamma(p):
    return jnp.log1p(-jnp.exp2(-p))


def _decay_kernel(p_ref, dmask_ref, dvec_ref):
    c = MIX_CHUNK
    base = pl.program_id(0) * (2 * RET_HEADS)
    pair = pl.program_id(1)
    ri = lax.broadcasted_iota(jnp.int32, (c, c), 0)
    ci = lax.broadcasted_iota(jnp.int32, (c, c), 1)
    dif = (ri - ci).astype(F32)
    for e in range(2):
        h = 2 * pair + e
        lg_f = _log_gamma(jnp.full((c, c), p_ref[base + h], F32))
        lg_b = _log_gamma(jnp.full((c, c), p_ref[base + RET_HEADS + h], F32))
        fwd = jnp.where(dif >= 0, jnp.exp(jnp.maximum(dif, 0.0) * lg_f), 0.0)
        bwd = jnp.where(dif <= 0, jnp.exp(jnp.maximum(-dif, 0.0) * lg_b), 0.0)
        dmask_ref[e] = fwd + bwd
    lane = lax.broadcasted_iota(jnp.int32, (c, LANES), 1)
    rowf = lax.broadcasted_iota(jnp.int32, (c, LANES), 0).astype(F32)
    lo = lane < RET_DK
    lg_f = _log_gamma(jnp.where(lo, p_ref[base + 2 * pair], p_ref[base + 2 * pair + 1]))
    lg_b = _log_gamma(jnp.where(lo, p_ref[base + RET_HEADS + 2 * pair],
                                p_ref[base + RET_HEADS + 2 * pair + 1]))
    dvec_ref[Q_DEC_F] = jnp.exp((rowf + 1.0) * lg_f)
    dvec_ref[K_DEC_F] = jnp.exp((c - 1.0 - rowf) * lg_f)
    dvec_ref[Q_DEC_B] = jnp.exp((c - rowf) * lg_b)
    dvec_ref[K_DEC_B] = jnp.exp(rowf * lg_b)
    dvec_ref[C_DEC_F] = jnp.exp(c * lg_f)
    dvec_ref[C_DEC_B] = jnp.exp(c * lg_b)


def _decay_tables(decay_p):
    depth = decay_p.shape[0]
    c = MIX_CHUNK
    return pl.pallas_call(
        _decay_kernel,
        grid=(depth, N_PAIRS),
        in_specs=[pl.BlockSpec(memory_space=pltpu.SMEM)],
        out_specs=[
            pl.BlockSpec((None, 2, c, c), lambda l, p: (l, p, 0, 0)),
            pl.BlockSpec((None, None, N_DVEC, c, LANES), lambda l, p: (l, p, 0, 0, 0)),
        ],
        out_shape=[
            jax.ShapeDtypeStruct((depth, RET_HEADS, c, c), F32),
            jax.ShapeDtypeStruct((depth, N_PAIRS, N_DVEC, c, LANES), F32),
        ],
        compiler_params=pltpu.CompilerParams(
            dimension_semantics=("arbitrary", "arbitrary"), vmem_limit_bytes=VMEM_LIMIT),
        name="decay_tables",
    )(decay_p.reshape(-1))


def _mix_kernel(*refs, n_seq, latent):
    c = MIX_CHUNK
    nc = n_seq // c
    (rq_ref, rk_ref, rv_ref, rg_ref, qcat_ref, kvn_ref, krp_ref, gng_ref, gnb_ref,
     dmask_ref, dvec_ref) = refs[:11]
    if latent:
        kvc_ref, krc_ref, sf0_ref, sb0_ref, ret_ref, attn_ref, st_scr = refs[11:]
    else:
        ret_ref, attn_ref, sf_ref, sb_ref = refs[13:]

    lane = lax.broadcasted_iota(jnp.int32, (c, LANES), 1)
    lo = lane < RET_DK
    sq_r = lax.broadcasted_iota(jnp.int32, (LANES, LANES), 0)
    sq_c = lax.broadcasted_iota(jnp.int32, (LANES, LANES), 1)
    blockdiag = (sq_r < RET_DK) == (sq_c < RET_DK)

    def cols(j):
        return slice(j * LANES, (j + 1) * LANES)

    def state_update(pair, d, rows):
        kp = rk_ref[rows, cols(pair)]
        vp = rv_ref[rows, cols(pair)]
        kdt = (kp.astype(F32) * dvec_ref[pair, K_DEC_B if d else K_DEC_F]).T.astype(BF16)
        return jnp.where(blockdiag, _dot(kdt, vp), 0.0)

    for pair in range(N_PAIRS):
        for d in range(2):
            if latent:
                s = (sb0_ref if d else sf0_ref)[pair]
                cdec = dvec_ref[pair, C_DEC_B if d else C_DEC_F][0:LANES, :]
                order = list(range(nc - 1, -1, -1)) if d else list(range(nc))
                for idx, ch in enumerate(order):
                    st_scr[d, pair, ch] = s.astype(BF16)
                    if idx < nc - 1:
                        s = s * cdec + state_update(pair, d, pl.ds(ch * c, c))
            else:
                s = state_update(pair, d, pl.ds(0, c))
                st_ref = sb_ref if d else sf_ref
                st_ref[2 * pair] = s[0:RET_DK, 0:RET_DK]
                st_ref[2 * pair + 1] = s[RET_DK:, RET_DK:]

    scale = (MLA_NOPE + MLA_ROPE) ** -0.5
    nope_n = lax.broadcasted_iota(jnp.int32, (n_seq, LANES), 1) < MLA_NOPE
    if latent:
        nope_c = lax.broadcasted_iota(jnp.int32, (kvc_ref.shape[0], LANES), 1) < MLA_NOPE

    def chunk_body(ch, carry):
        rows = pl.ds(pl.multiple_of(ch * c, c), c)

        for pair in range(N_PAIRS):
            qp = rq_ref[rows, cols(pair)]
            kp = rk_ref[rows, cols(pair)]
            vp = rv_ref[rows, cols(pair)]
            zero = jnp.zeros_like(qp)
            a0 = (_dot_nt(jnp.where(lo, qp, zero), kp) * dmask_ref[2 * pair]).astype(BF16)
            a1 = (_dot_nt(jnp.where(lo, zero, qp), kp) * dmask_ref[2 * pair + 1]).astype(BF16)
            tot = jnp.where(lo, _dot(a0, vp), _dot(a1, vp))
            if latent:
                tot = (tot + _dot(qp, st_scr[0, pair, ch]) * dvec_ref[pair, Q_DEC_F]
                       + _dot(qp, st_scr[1, pair, ch]) * dvec_ref[pair, Q_DEC_B])
            inv = 1.0 / RET_DK
            m0 = jnp.sum(jnp.where(lo, tot, 0.0), axis=-1, keepdims=True) * inv
            m1 = jnp.sum(jnp.where(lo, 0.0, tot), axis=-1, keepdims=True) * inv
            y = tot - jnp.where(lo, m0, m1)
            v0 = jnp.sum(jnp.where(lo, y * y, 0.0), axis=-1, keepdims=True) * inv
            v1 = jnp.sum(jnp.where(lo, 0.0, y * y), axis=-1, keepdims=True) * inv
            yn = (y * lax.rsqrt(jnp.where(lo, v0, v1) + EPS) * gng_ref[:, cols(pair)]
                  + gnb_ref[:, cols(pair)])
            ret_ref[rows, cols(pair)] = (yn * _silu(rg_ref[rows, cols(pair)])).astype(BF16)

        krp = krp_ref[...]
        for pair in range(N_PAIRS):
            outs = []
            for e in range(2):
                h = 2 * pair + e
                qc = qcat_ref[rows, cols(h)]
                kv = kvn_ref[:, cols(h)]
                s = _dot_nt(qc, jnp.where(nope_n, kv, krp)) * scale
                m = jnp.max(s, axis=-1, keepdims=True)
                if latent:
                    kv_c = kvc_ref[:, cols(h)]
                    s2 = _dot_nt(qc, jnp.where(nope_c, kv_c, krc_ref[...])) * scale
                    m = jnp.maximum(m, jnp.max(s2, axis=-1, keepdims=True))
                    p2 = jnp.exp(s2 - m)
                p = jnp.exp(s - m)
                den = jnp.sum(p, axis=-1, keepdims=True)
                acc = _dot(p.astype(BF16), kv)
                if latent:
                    den = den + jnp.sum(p2, axis=-1, keepdims=True)
                    acc = acc + _dot(p2.astype(BF16), kv_c)
                outs.append(acc / den)
            attn_ref[rows, cols(pair)] = jnp.where(
                lo, pltpu.roll(outs[0], MLA_V, 1), outs[1]).astype(BF16)
        return carry

    if nc == 1:
        chunk_body(0, 0)
    else:
        lax.fori_loop(0, nc, chunk_body, 0)


def _mix(proj_out, dmask, dvec, gn_g, gn_b, layer, n_batch, n_seq, latent_in, states_out):
    rq, rk, rv, rg, qcat, kvn, krp = proj_out[:7]
    latent = latent_in is not None
    aliases = {}
    m = n_batch * n_seq
    c = MIX_CHUNK
    width = MLA_HEADS * HEAD_PAD

    def row(b):
        return (b, 0)

    def lay3(b):
        return (layer, 0, 0)

    once = pl.Buffered(1)
    in_specs = [
        pl.BlockSpec((n_seq, RET_WIDTH), row), pl.BlockSpec((n_seq, RET_WIDTH), row),
        pl.BlockSpec((n_seq, RET_WIDTH), row), pl.BlockSpec((n_seq, RET_WIDTH), row),
        pl.BlockSpec((n_seq, width), row), pl.BlockSpec((n_seq, width), row),
        pl.BlockSpec((n_seq, LANES), row),
        pl.BlockSpec((None, 1, RET_WIDTH), lay3), pl.BlockSpec((None, 1, RET_WIDTH), lay3),
        pl.BlockSpec((None, RET_HEADS, c, c), lambda b: (layer, 0, 0, 0), pipeline_mode=once),
        pl.BlockSpec((None, N_PAIRS, N_DVEC, c, LANES), lambda b: (layer, 0, 0, 0, 0),
                     pipeline_mode=once),
    ]
    args = [rq, rk, rv, rg, qcat, kvn, krp, gn_g, gn_b, dmask, dvec]
    out_shape = [jax.ShapeDtypeStruct((m, RET_WIDTH), BF16),
                 jax.ShapeDtypeStruct((m, MLA_HEADS * MLA_V), BF16)]
    out_specs = [pl.BlockSpec((n_seq, RET_WIDTH), row), pl.BlockSpec((n_seq, RET_WIDTH), row)]
    scratch = []
    if latent:
        kvn_c, krp_c, s_f0, s_b0 = latent_in
        past = kvn_c.shape[2]
        st_spec = pl.BlockSpec((None, None, N_PAIRS, LANES, LANES), lambda b: (b, layer, 0, 0, 0))
        in_specs += [
            pl.BlockSpec((None, None, past, width), lambda b: (layer, b, 0, 0)),
            pl.BlockSpec((None, None, past, LANES), lambda b: (b, layer, 0, 0)),
            st_spec, st_spec,
        ]
        args += [kvn_c, krp_c, s_f0, s_b0]
        scratch = [pltpu.VMEM((2, N_PAIRS, n_seq // c, LANES, LANES), BF16)]
    else:
        in_specs += [pl.BlockSpec(memory_space=pl.ANY)] * 2
        aliases = {len(args): len(out_shape), len(args) + 1: len(out_shape) + 1}
        args += list(states_out)
        out_shape += [jax.ShapeDtypeStruct(s.shape, s.dtype) for s in states_out]
        st_spec = pl.BlockSpec((None, None, RET_HEADS, RET_DK, RET_DK),
                               lambda b: (b, layer, 0, 0, 0))
        out_specs += [st_spec, st_spec]

    return pl.pallas_call(
        functools.partial(_mix_kernel, n_seq=n_seq, latent=latent),
        grid=(n_batch,),
        in_specs=in_specs,
        out_specs=out_specs,
        out_shape=out_shape,
        input_output_aliases=aliases,
        scratch_shapes=scratch,
        compiler_params=pltpu.CompilerParams(
            dimension_semantics=("arbitrary",), vmem_limit_bytes=VMEM_LIMIT),
        name="mix_latent" if latent else "mix_context",
    )(*args)


def _out_kernel(x_ref, ret_ref, attn_ref, mod_ref, gffn_ref, wo_ref, wfi_ref, wfo_ref, o_ref, *,
                cond_row):
    cond = cond_row(pl.program_id(0))
    gt1 = _mod_vec(mod_ref, cond, 2)
    sh2 = _mod_vec(mod_ref, cond, 3)
    sc2 = _mod_vec(mod_ref, cond, 4)
    gt2 = _mod_vec(mod_ref, cond, 5)
    for r in range(x_ref.shape[0] // SUB_TILE):
        rs = slice(r * SUB_TILE, (r + 1) * SUB_TILE)
        mixed = (_dot(ret_ref[rs, :], wo_ref[0:RET_WIDTH, :])
                 + _dot(attn_ref[rs, :], wo_ref[RET_WIDTH:, :]))
        x1 = x_ref[rs, :] + gt1 * mixed
        h = (_rms(x1, gffn_ref[...]) * (1.0 + sc2) + sh2).astype(BF16)
        acc = jnp.zeros_like(x1)
        for c0 in range(0, D_FF, FF_CHUNK):
            cw = min(FF_CHUNK, D_FF - c0)
            gate = _dot(h, wfi_ref[:, c0:c0 + cw])
            up = _dot(h, wfi_ref[:, D_FF + c0:D_FF + c0 + cw])
            acc = acc + _dot((_silu(gate) * up).astype(BF16), wfo_ref[c0:c0 + cw, :])
        o_ref[rs, :] = x1 + gt2 * acc


def _out(x, ret, attn, mod, layer, cond_row, wts):
    m = x.shape[0]
    tm = ROW_TILE

    def row(i):
        return (i, 0)

    def lay3(i):
        return (layer, 0, 0)

    once = pl.Buffered(1)
    return pl.pallas_call(
        functools.partial(_out_kernel, cond_row=cond_row),
        grid=(m // tm,),
        in_specs=[
            pl.BlockSpec((tm, D_MODEL), row),
            pl.BlockSpec((tm, RET_WIDTH), row),
            pl.BlockSpec((tm, RET_WIDTH), row),
            pl.BlockSpec((None, 8, N_MOD * D_MODEL), lay3),
            pl.BlockSpec((None, 1, D_MODEL), lay3),
            pl.BlockSpec((None, D_MODEL, D_MODEL), lay3, pipeline_mode=once),
            pl.BlockSpec((None, D_MODEL, 2 * D_FF), lay3, pipeline_mode=once),
            pl.BlockSpec((None, D_FF, D_MODEL), lay3, pipeline_mode=once),
        ],
        out_specs=pl.BlockSpec((tm, D_MODEL), row),
        out_shape=jax.ShapeDtypeStruct((m, D_MODEL), F32),
        compiler_params=pltpu.CompilerParams(
            dimension_semantics=("arbitrary",), vmem_limit_bytes=VMEM_LIMIT),
        name="out_ffn",
    )(x, ret, attn, mod, wts["g_ffn"], wts["w_o"], wts["w_ffn_in"], wts["w_ffn_out"])


def _rope_tables(n_lat):
    pos = np.arange(n_lat)
    row = (pos // GRID_W).astype(np.float32)[:, None]
    col = (pos % GRID_W).astype(np.float32)[:, None]
    lane = np.arange(LANES)[None, :]

    def tables(d, start, period):
        rel = (lane - start) % period
        active = np.logical_and(lane >= start, rel < d)
        half = d // 2
        nf = half // 2
        inv = np.float32(ROPE_BASE) ** (-((rel % nf).astype(np.float32)) / np.float32(nf))
        ang = (np.where(rel < half, row, col) * inv).astype(np.float32)
        cos, sin = np.cos(ang), np.sin(ang)
        first = (rel % half) < nf
        c = np.where(active, cos, 1.0)
        sa = np.where(np.logical_and(active, np.logical_not(first)), sin, 0.0)
        sb = np.where(np.logical_and(active, first), -sin, 0.0)
        return tuple(jnp.asarray(t, F32) for t in (c, sa, sb))

    return tables(RET_DK, 0, RET_DK) + tables(MLA_ROPE, MLA_NOPE, LANES)


def _prepare_weights(g_norm_mix, g_norm_ffn, w_in, g_q_a, w_q_b, g_kv_a, w_kv_b, g_qn, g_qr, g_kn,
                     g_kr, w_o, w_ffn_in, w_ffn_out):
    depth = w_in.shape[0]
    w_in_p = jnp.pad(w_in.astype(BF16), ((0, 0), (0, 0), (0, IN_COLS_PAD - w_in.shape[2])))
    w_qb = w_q_b.reshape(depth, Q_RANK, MLA_HEADS, MLA_NOPE + MLA_ROPE)
    w_qb = jnp.pad(w_qb, ((0, 0), (0, 0), (0, 0), (0, HEAD_PAD - MLA_NOPE - MLA_ROPE)))
    zeros32 = jnp.zeros((depth, MLA_ROPE), F32)
    return {
        "g_mix": g_norm_mix[:, None, :],
        "g_ffn": g_norm_ffn[:, None, :],
        "w_in": w_in_p,
        "g_qa": g_q_a[:, None, :],
        "w_qb": w_qb.reshape(depth, Q_RANK, MLA_HEADS * HEAD_PAD).astype(BF16),
        "g_kva": g_kv_a[:, None, :],
        "w_kvb": w_kv_b.astype(BF16),
        "g_q": jnp.concatenate([g_qn, g_qr, zeros32], axis=-1)[:, None, :],
        "g_kn": jnp.concatenate([g_kn, jnp.ones((depth, MLA_V), F32)], axis=-1)[:, None, :],
        "g_kr": jnp.pad(g_kr, ((0, 0), (0, LANES - MLA_ROPE)))[:, None, :],
        "w_o": w_o.astype(BF16),
        "w_ffn_in": w_ffn_in.astype(BF16),
        "w_ffn_out": w_ffn_out.astype(BF16),
    }


def _blockdiag_states(s):
    b, l = s.shape[:2]
    s = s.reshape(b, l, N_PAIRS, 2, RET_DK, RET_DK)
    z = jnp.zeros_like(s[:, :, :, 0])
    top = jnp.concatenate([s[:, :, :, 0], z], axis=-1)
    bot = jnp.concatenate([z, s[:, :, :, 1]], axis=-1)
    return jnp.concatenate([top, bot], axis=-2)


def kernel(x_prompt, x_sample, cache_ckv, cache_krope, state_ret_fwd, state_ret_bwd, c, c_ctx,
           w_mod, b_mod, g_norm_mix, g_norm_ffn, w_in, g_q_a, w_q_b, g_kv_a, w_kv_b,
           g_qn, g_qr, g_kn, g_kr, ret_p_fwd, ret_p_bwd, g_ret_gn, b_ret_gn, w_o,
           w_ffn_in, w_ffn_out):
    batch, seq, _ = x_prompt.shape
    dec_batch, dec_seq, _ = x_sample.shape
    depth = w_in.shape[0]

    wts = _prepare_weights(g_norm_mix, g_norm_ffn, w_in, g_q_a, w_q_b, g_kv_a, w_kv_b, g_qn, g_qr,
                           g_kn, g_kr, w_o, w_ffn_in, w_ffn_out)
    conds = jnp.concatenate([c_ctx[None], c, jnp.zeros((8 - 1 - dec_batch, D_MODEL), F32)], axis=0)
    mod = _modulation(conds, w_mod, b_mod)
    dmask, dvec = _decay_tables(jnp.stack([ret_p_fwd, ret_p_bwd], axis=1))
    gn_g = g_ret_gn[:, None, :]
    gn_b = b_ret_gn[:, None, :]

    rope_tabs = _rope_tables(dec_seq)
    kvn_cache = _cache_up(cache_ckv, wts["w_kvb"], wts["g_kn"])
    krp_cache = jnp.pad(cache_krope, ((0, 0), (0, 0), (0, 0), (MLA_NOPE, LANES - MLA_NOPE - MLA_ROPE)))
    krp_cache = krp_cache.astype(BF16)
    s_f0 = _blockdiag_states(state_ret_fwd)
    s_b0 = _blockdiag_states(state_ret_bwd)

    lat_tiles = dec_seq // ROW_TILE

    def ctx_row(i):
        return 0

    def lat_row(i):
        return 1 + i // lat_tiles

    x = x_prompt.reshape(batch * seq, D_MODEL)
    y = x_sample.reshape(dec_batch * dec_seq, D_MODEL)
    caches = (jnp.zeros((batch, depth, seq, KV_RANK), F32),
              jnp.zeros((batch, depth, seq, MLA_ROPE), F32))
    states = (jnp.zeros((batch, depth, RET_HEADS, RET_DK, RET_DK), F32),
              jnp.zeros((batch, depth, RET_HEADS, RET_DK, RET_DK), F32))
    for l in range(depth):
        pr = _proj(x, mod, l, ctx_row, wts, None, caches)
        caches = tuple(pr[7:9])
        ret, attn, s_f, s_b = _mix(pr, dmask, dvec, gn_g, gn_b, l, batch, seq, None, states)
        states = (s_f, s_b)
        x = _out(x, ret, attn, mod, l, ctx_row, wts)

        pr = _proj(y, mod, l, lat_row, wts, rope_tabs, None)
        ret, attn = _mix(pr, dmask, dvec, gn_g, gn_b, l, dec_batch, dec_seq,
                         (kvn_cache, krp_cache, s_f0, s_b0), None)
        y = _out(y, ret, attn, mod, l, lat_row, wts)

    return (x.reshape(batch, seq, D_MODEL), y.reshape(dec_batch, dec_seq, D_MODEL),
            caches[0], caches[1], states[0], states[1])
```

```python
import functools

import jax
import jax.numpy as jnp
import numpy as np
from jax import lax
from jax.experimental import pallas as pl
from jax.experimental.pallas import tpu as pltpu

D_MODEL = 1024
N_MOD = 6
RET_HEADS = 8
RET_DK = 64
RET_WIDTH = 512
MLA_HEADS = 8
MLA_NOPE = 64
MLA_ROPE = 32
MLA_V = 64
Q_RANK = 256
KV_RANK = 128
D_FF = 2816
GRID_W = 64
ROPE_BASE = 10000.0
EPS = 1e-6

LANES = 128
HEAD_PAD = LANES
N_PAIRS = RET_HEADS // 2
IN_COLS_PAD = 4 * RET_WIDTH + Q_RANK + KV_RANK + LANES
ROW_TILE = 512
SUB_TILE = 256
FF_CHUNK = 256
MIX_CHUNK = 256
Q_DEC_F, K_DEC_F, Q_DEC_B, K_DEC_B, C_DEC_F, C_DEC_B = range(6)
N_DVEC = 6
Q_FOLD = (MLA_NOPE + MLA_ROPE) ** -0.5 * 1.4426950408889634
VMEM_LIMIT = 56 * 1024 * 1024

BF16 = jnp.bfloat16
F32 = jnp.float32
_NT = (((1,), (1,)), ((), ()))


def _dot(a, b):
    return jnp.dot(a, b, preferred_element_type=F32)


def _dot_nt(a, b):
    return lax.dot_general(a, b, _NT, preferred_element_type=F32)


def _rms(x, g):
    return x * lax.rsqrt(jnp.mean(x * x, axis=-1, keepdims=True) + EPS) * g


def _silu(x):
    return x * jax.nn.sigmoid(x)


def _masked_mean_sq(x, mask, n):
    return jnp.sum(jnp.where(mask, x * x, 0.0), axis=-1, keepdims=True) * (1.0 / n)


def _rope(x, c, sa, sb, shift):
    return x * c + pltpu.roll(x, shift, 1) * sa + pltpu.roll(x, LANES - shift, 1) * sb


def _mod_kernel(c_ref, w_ref, b_ref, o_ref):
    a = _silu(c_ref[...])
    w = w_ref[...]
    a_hi = a.astype(BF16)
    a_lo = (a - a_hi.astype(F32)).astype(BF16)
    w_hi = w.astype(BF16)
    w_lo = (w - w_hi.astype(F32)).astype(BF16)
    o_ref[...] = _dot(a_hi, w_hi) + _dot(a_lo, w_hi) + _dot(a_hi, w_lo) + b_ref[...]


def _modulation(conds, w_mod, b_mod):
    depth, _, n = w_mod.shape
    tn = 1536
    return pl.pallas_call(
        _mod_kernel,
        grid=(depth, n // tn),
        in_specs=[
            pl.BlockSpec((8, D_MODEL), lambda l, j: (0, 0)),
            pl.BlockSpec((None, D_MODEL, tn), lambda l, j: (l, 0, j)),
            pl.BlockSpec((None, 1, tn), lambda l, j: (l, 0, j)),
        ],
        out_specs=pl.BlockSpec((None, 8, tn), lambda l, j: (l, 0, j)),
        out_shape=jax.ShapeDtypeStruct((depth, 8, n), F32),
        compiler_params=pltpu.CompilerParams(
            dimension_semantics=("arbitrary", "arbitrary"), vmem_limit_bytes=VMEM_LIMIT),
        name="modulation",
    )(conds, w_mod, b_mod.reshape(depth, 1, n))


def _norm_kn(kv, gkn):
    lane = lax.broadcasted_iota(jnp.int32, (kv.shape[0], LANES), 1)
    lo = lane < MLA_NOPE
    out = []
    for h in range(MLA_HEADS):
        kvh = kv[:, h * HEAD_PAD:(h + 1) * HEAD_PAD]
        rs = lax.rsqrt(_masked_mean_sq(kvh, lo, MLA_NOPE) + EPS)
        out.append((kvh * jnp.where(lo, rs * gkn, 1.0)).astype(BF16))
    return out


def _cache_kernel(ckv_ref, wkvb_ref, gkn_ref, kvn_ref):
    kv = _dot(ckv_ref[...].astype(BF16), wkvb_ref[...])
    for h, kvh in enumerate(_norm_kn(kv, gkn_ref[...])):
        kvn_ref[:, h * HEAD_PAD:(h + 1) * HEAD_PAD] = kvh


def _cache_up(cache_ckv, wkvb, gkn):
    nb, depth, past, _ = cache_ckv.shape
    width = MLA_HEADS * HEAD_PAD
    return pl.pallas_call(
        _cache_kernel,
        grid=(depth, nb),
        in_specs=[
            pl.BlockSpec((None, None, past, KV_RANK), lambda l, b: (b, l, 0, 0)),
            pl.BlockSpec((None, KV_RANK, width), lambda l, b: (l, 0, 0)),
            pl.BlockSpec((None, 1, LANES), lambda l, b: (l, 0, 0)),
        ],
        out_specs=pl.BlockSpec((None, None, past, width), lambda l, b: (l, b, 0, 0)),
        out_shape=jax.ShapeDtypeStruct((depth, nb, past, width), BF16),
        compiler_params=pltpu.CompilerParams(
            dimension_semantics=("arbitrary", "arbitrary"), vmem_limit_bytes=VMEM_LIMIT),
        name="cache_up",
    )(cache_ckv, wkvb, gkn)


def _mod_vec(mod_ref, cond, k):
    return mod_ref[pl.ds(cond, 1), k * D_MODEL:(k + 1) * D_MODEL]


def _is_array(x):
    return not isinstance(x, jax.ShapeDtypeStruct)


def _proj_kernel(*refs, rope, cond_row, n_alias):
    (x_ref, mod_ref, gmix_ref, win_ref, gqa_ref, wqb_ref, gkva_ref, wkvb_ref,
     gq_ref, gkn_ref, gkr_ref) = refs[:11]
    if rope:
        c64_ref, sa64_ref, sb64_ref, c32_ref, sa32_ref, sb32_ref = refs[11:17]
        rq_ref, rk_ref, rv_ref, rg_ref, qcat_ref, kvn_ref, krp_ref = refs[17:]
    else:
        (rq_ref, rk_ref, rv_ref, rg_ref, qcat_ref, kvn_ref, krp_ref, ckv_ref,
         kro_ref) = refs[11 + n_alias:]

    cond = cond_row(pl.program_id(0))
    sh1 = _mod_vec(mod_ref, cond, 0)
    sc1 = _mod_vec(mod_ref, cond, 1)
    w = RET_WIDTH
    lane = lax.broadcasted_iota(jnp.int32, (SUB_TILE, LANES), 1)
    nope = lane < MLA_NOPE
    is_rope = jnp.logical_and(lane >= MLA_NOPE, lane < MLA_NOPE + MLA_ROPE)

    for r in range(x_ref.shape[0] // SUB_TILE):
        rs = slice(r * SUB_TILE, (r + 1) * SUB_TILE)
        h = (_rms(x_ref[rs, :], gmix_ref[...]) * (1.0 + sc1) + sh1).astype(BF16)
        z = _dot(h, win_ref[...])

        for j in range(w // LANES):
            sl = slice(j * LANES, (j + 1) * LANES)
            q = z[:, j * LANES:(j + 1) * LANES]
            k = z[:, w + j * LANES:w + (j + 1) * LANES] * (RET_DK ** -0.5)
            if rope:
                q = _rope(q, c64_ref[rs, :], sa64_ref[rs, :], sb64_ref[rs, :], 16)
                k = _rope(k, c64_ref[rs, :], sa64_ref[rs, :], sb64_ref[rs, :], 16)
            rq_ref[rs, sl] = q.astype(BF16)
            rk_ref[rs, sl] = k.astype(BF16)
        rv_ref[rs, :] = z[:, 2 * w:3 * w].astype(BF16)
        rg_ref[rs, :] = z[:, 3 * w:4 * w]

        o = 4 * w
        qa = z[:, o:o + Q_RANK]
        kva = z[:, o + Q_RANK:o + Q_RANK + KV_RANK]
        kr2 = z[:, o + Q_RANK + KV_RANK:]

        q = _dot(_rms(qa, gqa_ref[...]).astype(BF16), wqb_ref[...])
        for hh in range(MLA_HEADS):
            qh = q[:, hh * HEAD_PAD:(hh + 1) * HEAD_PAD]
            rs_n = lax.rsqrt(_masked_mean_sq(qh, nope, MLA_NOPE) + EPS)
            rs_r = lax.rsqrt(_masked_mean_sq(qh, is_rope, MLA_ROPE) + EPS)
            qn = qh * jnp.where(nope, rs_n, rs_r) * gq_ref[...]
            if rope:
                qn = _rope(qn, c32_ref[rs, :], sa32_ref[rs, :], sb32_ref[rs, :], 8)
            qcat_ref[rs, hh * HEAD_PAD:(hh + 1) * HEAD_PAD] = qn.astype(BF16)

        ckv = _rms(kva, gkva_ref[...])
        kv = _dot(ckv.astype(BF16), wkvb_ref[...])
        for hh, kvh in enumerate(_norm_kn(kv, gkn_ref[...])):
            kvn_ref[rs, hh * HEAD_PAD:(hh + 1) * HEAD_PAD] = kvh

        krn = kr2 * lax.rsqrt(_masked_mean_sq(kr2, lane < MLA_ROPE, MLA_ROPE) + EPS) * gkr_ref[...]
        krp = pltpu.roll(krn, MLA_NOPE, 1)
        if rope:
            krp = _rope(krp, c32_ref[rs, :], sa32_ref[rs, :], sb32_ref[rs, :], 8)
        else:
            ckv_ref[r, :, :] = ckv
            kro_ref[r, :, :] = krn[:, 0:MLA_ROPE]
        krp_ref[rs, :] = krp.astype(BF16)


def _proj(x, mod, layer, cond_row, wts, rope_tabs, caches):
    m = x.shape[0]
    tm = ROW_TILE
    rope = rope_tabs is not None
    width = MLA_HEADS * HEAD_PAD

    def row(i):
        return (i, 0)

    def lay3(i):
        return (layer, 0, 0)

    in_specs = [
        pl.BlockSpec((tm, D_MODEL), row),
        pl.BlockSpec((None, 8, N_MOD * D_MODEL), lay3),
        pl.BlockSpec((None, 1, D_MODEL), lay3),
        pl.BlockSpec((None, D_MODEL, IN_COLS_PAD), lay3),
        pl.BlockSpec((None, 1, Q_RANK), lay3),
        pl.BlockSpec((None, Q_RANK, width), lay3),
        pl.BlockSpec((None, 1, KV_RANK), lay3),
        pl.BlockSpec((None, KV_RANK, width), lay3),
        pl.BlockSpec((None, 1, LANES), lay3),
        pl.BlockSpec((None, 1, LANES), lay3),
        pl.BlockSpec((None, 1, LANES), lay3),
    ]
    args = [x, mod, wts["g_mix"], wts["w_in"], wts["g_qa"], wts["w_qb"], wts["g_kva"],
            wts["w_kvb"], wts["g_q"], wts["g_kn"], wts["g_kr"]]
    out_shape = [
        jax.ShapeDtypeStruct((m, RET_WIDTH), BF16),
        jax.ShapeDtypeStruct((m, RET_WIDTH), BF16),
        jax.ShapeDtypeStruct((m, RET_WIDTH), BF16),
        jax.ShapeDtypeStruct((m, RET_WIDTH), F32),
        jax.ShapeDtypeStruct((m, width), BF16),
        jax.ShapeDtypeStruct((m, width), BF16),
        jax.ShapeDtypeStruct((m, LANES), BF16),
    ]
    out_specs = [
        pl.BlockSpec((tm, RET_WIDTH), row), pl.BlockSpec((tm, RET_WIDTH), row),
        pl.BlockSpec((tm, RET_WIDTH), row), pl.BlockSpec((tm, RET_WIDTH), row),
        pl.BlockSpec((tm, width), row), pl.BlockSpec((tm, width), row),
        pl.BlockSpec((tm, LANES), row),
    ]
    aliases = {}
    if rope:
        n_lat = rope_tabs[0].shape[0]
        tiles = n_lat // tm
        in_specs += [pl.BlockSpec((tm, LANES), lambda i: (i % tiles, 0))] * 6
        args += list(rope_tabs)
    else:
        seq = caches[0].shape[2]
        assert seq == SUB_TILE
        nb = tm // seq
        if _is_array(caches[0]):
            in_specs += [pl.BlockSpec(memory_space=pl.ANY)] * 2
            aliases = {len(args): len(out_shape), len(args) + 1: len(out_shape) + 1}
            args += list(caches)
        out_shape += [jax.ShapeDtypeStruct(c.shape, c.dtype) for c in caches]
        out_specs += [pl.BlockSpec((nb, None, seq, KV_RANK), lambda i: (i, layer, 0, 0)),
                      pl.BlockSpec((nb, None, seq, MLA_ROPE), lambda i: (i, layer, 0, 0))]

    return pl.pallas_call(
        functools.partial(_proj_kernel, rope=rope, cond_row=cond_row, n_alias=len(aliases)),
        grid=(m // tm,),
        in_specs=in_specs,
        out_specs=out_specs,
        out_shape=out_shape,
        input_output_aliases=aliases,
        compiler_params=pltpu.CompilerParams(
            dimension_semantics=("arbitrary",), vmem_limit_bytes=VMEM_LIMIT),
        name="proj_latent" if rope else "proj_context",
    )(*args)


def _log_gamma(p):
    return jnp.log1p(-jnp.exp2(-p))


def _decay_kernel(p_ref, dmask_ref, dvec_ref):
    c = MIX_CHUNK
    base = pl.program_id(0) * (2 * RET_HEADS)
    pair = pl.program_id(1)
    ri = lax.broadcasted_iota(jnp.int32, (c, c), 0)
    ci = lax.broadcasted_iota(jnp.int32, (c, c), 1)
    dif = (ri - ci).astype(F32)
    for e in range(2):
        h = 2 * pair + e
        lg_f = _log_gamma(jnp.full((c, c), p_ref[base + h], F32))
        lg_b = _log_gamma(jnp.full((c, c), p_ref[base + RET_HEADS + h], F32))
        fwd = jnp.where(dif >= 0, jnp.exp(jnp.maximum(dif, 0.0) * lg_f), 0.0)
        bwd = jnp.where(dif <= 0, jnp.exp(jnp.maximum(-dif, 0.0) * lg_b), 0.0)
        dmask_ref[e] = fwd + bwd
    lane = lax.broadcasted_iota(jnp.int32, (c, LANES), 1)
    rowf = lax.broadcasted_iota(jnp.int32, (c, LANES), 0).astype(F32)
    lo = lane < RET_DK
    lg_f = _log_gamma(jnp.where(lo, p_ref[base + 2 * pair], p_ref[base + 2 * pair + 1]))
    lg_b = _log_gamma(jnp.where(lo, p_ref[base + RET_HEADS + 2 * pair],
                                p_ref[base + RET_HEADS + 2 * pair + 1]))
    dvec_ref[Q_DEC_F] = jnp.exp((rowf + 1.0) * lg_f)
    dvec_ref[K_DEC_F] = jnp.exp((c - 1.0 - rowf) * lg_f)
    dvec_ref[Q_DEC_B] = jnp.exp((c - rowf) * lg_b)
    dvec_ref[K_DEC_B] = jnp.exp(rowf * lg_b)
    dvec_ref[C_DEC_F] = jnp.exp(c * lg_f)
    dvec_ref[C_DEC_B] = jnp.exp(c * lg_b)


def _decay_tables(decay_p):
    depth = decay_p.shape[0]
    c = MIX_CHUNK
    return pl.pallas_call(
        _decay_kernel,
        grid=(depth, N_PAIRS),
        in_specs=[pl.BlockSpec(memory_space=pltpu.SMEM)],
        out_specs=[
            pl.BlockSpec((None, 2, c, c), lambda l, p: (l, p, 0, 0)),
            pl.BlockSpec((None, None, N_DVEC, c, LANES), lambda l, p: (l, p, 0, 0, 0)),
        ],
        out_shape=[
            jax.ShapeDtypeStruct((depth, RET_HEADS, c, c), F32),
            jax.ShapeDtypeStruct((depth, N_PAIRS, N_DVEC, c, LANES), F32),
        ],
        compiler_params=pltpu.CompilerParams(
            dimension_semantics=("arbitrary", "arbitrary"), vmem_limit_bytes=VMEM_LIMIT),
        name="decay_tables",
    )(decay_p.reshape(-1))


def _mix_kernel(*refs, n_seq, latent, n_alias):
    c = MIX_CHUNK
    nc = n_seq // c
    (rq_ref, rk_ref, rv_ref, rg_ref, qcat_ref, kvn_ref, krp_ref, gng_ref, gnb_ref,
     dmask_ref, dvec_ref) = refs[:11]
    if latent:
        kvc_ref, krc_ref, sf0_ref, sb0_ref, ret_ref, attn_ref, st_scr = refs[11:]
    else:
        ret_ref, attn_ref, sf_ref, sb_ref = refs[11 + n_alias:]

    lane = lax.broadcasted_iota(jnp.int32, (c, LANES), 1)
    lo = lane < RET_DK
    sq_r = lax.broadcasted_iota(jnp.int32, (LANES, LANES), 0)
    sq_c = lax.broadcasted_iota(jnp.int32, (LANES, LANES), 1)
    blockdiag = (sq_r < RET_DK) == (sq_c < RET_DK)

    def cols(j):
        return slice(j * LANES, (j + 1) * LANES)

    def state_update(pair, d, rows):
        kp = rk_ref[rows, cols(pair)]
        vp = rv_ref[rows, cols(pair)]
        kdt = (kp.astype(F32) * dvec_ref[pair, K_DEC_B if d else K_DEC_F]).T.astype(BF16)
        return jnp.where(blockdiag, _dot(kdt, vp), 0.0)

    for pair in range(N_PAIRS):
        for d in range(2):
            if latent:
                s = (sb0_ref if d else sf0_ref)[pair]
                cdec = dvec_ref[pair, C_DEC_B if d else C_DEC_F][0:LANES, :]
                order = list(range(nc - 1, -1, -1)) if d else list(range(nc))
                for idx, ch in enumerate(order):
                    st_scr[d, pair, ch] = s.astype(BF16)
                    if idx < nc - 1:
                        s = s * cdec + state_update(pair, d, pl.ds(ch * c, c))
            else:
                s = state_update(pair, d, pl.ds(0, c))
                st_ref = sb_ref if d else sf_ref
                st_ref[2 * pair] = s[0:RET_DK, 0:RET_DK]
                st_ref[2 * pair + 1] = s[RET_DK:, RET_DK:]

    nope_n = lax.broadcasted_iota(jnp.int32, (n_seq, LANES), 1) < MLA_NOPE
    if latent:
        nope_c = lax.broadcasted_iota(jnp.int32, (kvc_ref.shape[0], LANES), 1) < MLA_NOPE

    def chunk_body(ch, carry):
        rows = pl.ds(pl.multiple_of(ch * c, c), c)

        for pair in range(N_PAIRS):
            qp = rq_ref[rows, cols(pair)]
            kp = rk_ref[rows, cols(pair)]
            vp = rv_ref[rows, cols(pair)]
            zero = jnp.zeros_like(qp)
            a0 = (_dot_nt(jnp.where(lo, qp, zero), kp) * dmask_ref[2 * pair]).astype(BF16)
            a1 = (_dot_nt(jnp.where(lo, zero, qp), kp) * dmask_ref[2 * pair + 1]).astype(BF16)
            tot = jnp.where(lo, _dot(a0, vp), _dot(a1, vp))
            if latent:
                tot = (tot + _dot(qp, st_scr[0, pair, ch]) * dvec_ref[pair, Q_DEC_F]
                       + _dot(qp, st_scr[1, pair, ch]) * dvec_ref[pair, Q_DEC_B])
            inv = 1.0 / RET_DK
            m0 = jnp.sum(jnp.where(lo, tot, 0.0), axis=-1, keepdims=True) * inv
            m1 = jnp.sum(jnp.where(lo, 0.0, tot), axis=-1, keepdims=True) * inv
            y = tot - jnp.where(lo, m0, m1)
            v0 = jnp.sum(jnp.where(lo, y * y, 0.0), axis=-1, keepdims=True) * inv
            v1 = jnp.sum(jnp.where(lo, 0.0, y * y), axis=-1, keepdims=True) * inv
            yn = (y * lax.rsqrt(jnp.where(lo, v0, v1) + EPS) * gng_ref[:, cols(pair)]
                  + gnb_ref[:, cols(pair)])
            ret_ref[rows, cols(pair)] = (yn * _silu(rg_ref[rows, cols(pair)])).astype(BF16)

        krp = krp_ref[...]
        for pair in range(N_PAIRS):
            outs = []
            for e in range(2):
                h = 2 * pair + e
                qc = qcat_ref[rows, cols(h)]
                kv = kvn_ref[:, cols(h)]
                s = _dot_nt(qc, jnp.where(nope_n, kv, krp))
                m = jnp.max(s, axis=-1, keepdims=True)
                if latent:
                    kv_c = kvc_ref[:, cols(h)]
                    s2 = _dot_nt(qc, jnp.where(nope_c, kv_c, krc_ref[...]))
                    m = jnp.maximum(m, jnp.max(s2, axis=-1, keepdims=True))
                    p2 = jnp.exp2(s2 - m)
                p = jnp.exp2(s - m)
                den = jnp.sum(p, axis=-1, keepdims=True)
                acc = _dot(p.astype(BF16), kv)
                if latent:
                    den = den + jnp.sum(p2, axis=-1, keepdims=True)
                    acc = acc + _dot(p2.astype(BF16), kv_c)
                outs.append(acc / den)
            attn_ref[rows, cols(pair)] = jnp.where(
                lo, pltpu.roll(outs[0], MLA_V, 1), outs[1]).astype(BF16)
        return carry

    if nc == 1:
        chunk_body(0, 0)
    else:
        lax.fori_loop(0, nc, chunk_body, 0)


def _mix(proj_out, dmask, dvec, gn_g, gn_b, layer, n_batch, n_seq, latent_in, states_out):
    rq, rk, rv, rg, qcat, kvn, krp = proj_out[:7]
    latent = latent_in is not None
    aliases = {}
    m = n_batch * n_seq
    c = MIX_CHUNK
    width = MLA_HEADS * HEAD_PAD

    def row(b):
        return (b, 0)

    def lay3(b):
        return (layer, 0, 0)

    once = pl.Buffered(1)
    in_specs = [
        pl.BlockSpec((n_seq, RET_WIDTH), row), pl.BlockSpec((n_seq, RET_WIDTH), row),
        pl.BlockSpec((n_seq, RET_WIDTH), row), pl.BlockSpec((n_seq, RET_WIDTH), row),
        pl.BlockSpec((n_seq, width), row), pl.BlockSpec((n_seq, width), row),
        pl.BlockSpec((n_seq, LANES), row),
        pl.BlockSpec((None, 1, RET_WIDTH), lay3), pl.BlockSpec((None, 1, RET_WIDTH), lay3),
        pl.BlockSpec((None, RET_HEADS, c, c), lambda b: (layer, 0, 0, 0), pipeline_mode=once),
        pl.BlockSpec((None, N_PAIRS, N_DVEC, c, LANES), lambda b: (layer, 0, 0, 0, 0),
                     pipeline_mode=once),
    ]
    args = [rq, rk, rv, rg, qcat, kvn, krp, gn_g, gn_b, dmask, dvec]
    out_shape = [jax.ShapeDtypeStruct((m, RET_WIDTH), BF16),
                 jax.ShapeDtypeStruct((m, MLA_HEADS * MLA_V), BF16)]
    out_specs = [pl.BlockSpec((n_seq, RET_WIDTH), row), pl.BlockSpec((n_seq, RET_WIDTH), row)]
    scratch = []
    if latent:
        kvn_c, krp_c, s_f0, s_b0 = latent_in
        past = kvn_c.shape[2]
        st_spec = pl.BlockSpec((None, None, N_PAIRS, LANES, LANES), lambda b: (b, layer, 0, 0, 0))
        in_specs += [
            pl.BlockSpec((None, None, past, width), lambda b: (layer, b, 0, 0)),
            pl.BlockSpec((None, None, past, LANES), lambda b: (b, layer, 0, 0)),
            st_spec, st_spec,
        ]
        args += [kvn_c, krp_c, s_f0, s_b0]
        scratch = [pltpu.VMEM((2, N_PAIRS, n_seq // c, LANES, LANES), BF16)]
    else:
        if _is_array(states_out[0]):
            in_specs += [pl.BlockSpec(memory_space=pl.ANY)] * 2
            aliases = {len(args): len(out_shape), len(args) + 1: len(out_shape) + 1}
            args += list(states_out)
        out_shape += [jax.ShapeDtypeStruct(s.shape, s.dtype) for s in states_out]
        st_spec = pl.BlockSpec((None, None, RET_HEADS, RET_DK, RET_DK),
                               lambda b: (b, layer, 0, 0, 0))
        out_specs += [st_spec, st_spec]

    return pl.pallas_call(
        functools.partial(_mix_kernel, n_seq=n_seq, latent=latent, n_alias=len(aliases)),
        grid=(n_batch,),
        in_specs=in_specs,
        out_specs=out_specs,
        out_shape=out_shape,
        input_output_aliases=aliases,
        scratch_shapes=scratch,
        compiler_params=pltpu.CompilerParams(
            dimension_semantics=("arbitrary",), vmem_limit_bytes=VMEM_LIMIT),
        name="mix_latent" if latent else "mix_context",
    )(*args)


def _out_kernel(x_ref, ret_ref, attn_ref, mod_ref, gffn_ref, wo_ref, wfi_ref, wfo_ref, o_ref,
                act_scr, *, cond_row):
    cond = cond_row(pl.program_id(0))
    gt1 = _mod_vec(mod_ref, cond, 2)
    sh2 = _mod_vec(mod_ref, cond, 3)
    sc2 = _mod_vec(mod_ref, cond, 4)
    gt2 = _mod_vec(mod_ref, cond, 5)
    mixed = _dot(ret_ref[...], wo_ref[0:RET_WIDTH, :]) + _dot(attn_ref[...], wo_ref[RET_WIDTH:, :])
    x1 = x_ref[...] + gt1 * mixed
    h = (_rms(x1, gffn_ref[...]) * (1.0 + sc2) + sh2).astype(BF16)
    for c0 in range(0, D_FF, FF_CHUNK):
        cw = min(FF_CHUNK, D_FF - c0)
        gate = _dot(h, wfi_ref[:, c0:c0 + cw])
        up = _dot(h, wfi_ref[:, D_FF + c0:D_FF + c0 + cw])
        act_scr[:, c0:c0 + cw] = (_silu(gate) * up).astype(BF16)
    o_ref[...] = x1 + gt2 * _dot(act_scr[...], wfo_ref[...])


def _out(x, ret, attn, mod, layer, cond_row, wts):
    m = x.shape[0]
    tm = ROW_TILE

    def row(i):
        return (i, 0)

    def lay3(i):
        return (layer, 0, 0)

    once = pl.Buffered(1)
    return pl.pallas_call(
        functools.partial(_out_kernel, cond_row=cond_row),
        grid=(m // tm,),
        in_specs=[
            pl.BlockSpec((tm, D_MODEL), row),
            pl.BlockSpec((tm, RET_WIDTH), row),
            pl.BlockSpec((tm, RET_WIDTH), row),
            pl.BlockSpec((None, 8, N_MOD * D_MODEL), lay3),
            pl.BlockSpec((None, 1, D_MODEL), lay3),
            pl.BlockSpec((None, D_MODEL, D_MODEL), lay3, pipeline_mode=once),
            pl.BlockSpec((None, D_MODEL, 2 * D_FF), lay3, pipeline_mode=once),
            pl.BlockSpec((None, D_FF, D_MODEL), lay3, pipeline_mode=once),
        ],
        out_specs=pl.BlockSpec((tm, D_MODEL), row),
        out_shape=jax.ShapeDtypeStruct((m, D_MODEL), F32),
        scratch_shapes=[pltpu.VMEM((tm, D_FF), BF16)],
        compiler_params=pltpu.CompilerParams(
            dimension_semantics=("arbitrary",), vmem_limit_bytes=VMEM_LIMIT),
        name="out_ffn",
    )(x, ret, attn, mod, wts["g_ffn"], wts["w_o"], wts["w_ffn_in"], wts["w_ffn_out"])


def _rope_tables(n_lat):
    pos = np.arange(n_lat)
    row = (pos // GRID_W).astype(np.float32)[:, None]
    col = (pos % GRID_W).astype(np.float32)[:, None]
    lane = np.arange(LANES)[None, :]

    def tables(d, start, period):
        rel = (lane - start) % period
        active = np.logical_and(lane >= start, rel < d)
        half = d // 2
        nf = half // 2
        inv = np.float32(ROPE_BASE) ** (-((rel % nf).astype(np.float32)) / np.float32(nf))
        ang = (np.where(rel < half, row, col) * inv).astype(np.float32)
        cos, sin = np.cos(ang), np.sin(ang)
        first = (rel % half) < nf
        c = np.where(active, cos, 1.0)
        sa = np.where(np.logical_and(active, np.logical_not(first)), sin, 0.0)
        sb = np.where(np.logical_and(active, first), -sin, 0.0)
        return tuple(jnp.asarray(t, F32) for t in (c, sa, sb))

    return tables(RET_DK, 0, RET_DK) + tables(MLA_ROPE, MLA_NOPE, LANES)


def _prepare_weights(g_norm_mix, g_norm_ffn, w_in, g_q_a, w_q_b, g_kv_a, w_kv_b, g_qn, g_qr, g_kn,
                     g_kr, w_o, w_ffn_in, w_ffn_out):
    depth = w_in.shape[0]
    w_in_p = jnp.pad(w_in.astype(BF16), ((0, 0), (0, 0), (0, IN_COLS_PAD - w_in.shape[2])))
    w_qb = w_q_b.reshape(depth, Q_RANK, MLA_HEADS, MLA_NOPE + MLA_ROPE)
    w_qb = jnp.pad(w_qb, ((0, 0), (0, 0), (0, 0), (0, HEAD_PAD - MLA_NOPE - MLA_ROPE)))
    zeros32 = jnp.zeros((depth, MLA_ROPE), F32)
    return {
        "g_mix": g_norm_mix[:, None, :],
        "g_ffn": g_norm_ffn[:, None, :],
        "w_in": w_in_p,
        "g_qa": g_q_a[:, None, :],
        "w_qb": w_qb.reshape(depth, Q_RANK, MLA_HEADS * HEAD_PAD).astype(BF16),
        "g_kva": g_kv_a[:, None, :],
        "w_kvb": w_kv_b.astype(BF16),
        "g_q": (jnp.concatenate([g_qn, g_qr, zeros32], axis=-1) * Q_FOLD)[:, None, :],
        "g_kn": jnp.concatenate([g_kn, jnp.ones((depth, MLA_V), F32)], axis=-1)[:, None, :],
        "g_kr": jnp.pad(g_kr, ((0, 0), (0, LANES - MLA_ROPE)))[:, None, :],
        "w_o": w_o.astype(BF16),
        "w_ffn_in": w_ffn_in.astype(BF16),
        "w_ffn_out": w_ffn_out.astype(BF16),
    }


def _blockdiag_states(s):
    b, l = s.shape[:2]
    s = s.reshape(b, l, N_PAIRS, 2, RET_DK, RET_DK)
    z = jnp.zeros_like(s[:, :, :, 0])
    top = jnp.concatenate([s[:, :, :, 0], z], axis=-1)
    bot = jnp.concatenate([z, s[:, :, :, 1]], axis=-1)
    return jnp.concatenate([top, bot], axis=-2)


def kernel(x_prompt, x_sample, cache_ckv, cache_krope, state_ret_fwd, state_ret_bwd, c, c_ctx,
           w_mod, b_mod, g_norm_mix, g_norm_ffn, w_in, g_q_a, w_q_b, g_kv_a, w_kv_b,
           g_qn, g_qr, g_kn, g_kr, ret_p_fwd, ret_p_bwd, g_ret_gn, b_ret_gn, w_o,
           w_ffn_in, w_ffn_out):
    batch, seq, _ = x_prompt.shape
    dec_batch, dec_seq, _ = x_sample.shape
    depth = w_in.shape[0]

    wts = _prepare_weights(g_norm_mix, g_norm_ffn, w_in, g_q_a, w_q_b, g_kv_a, w_kv_b, g_qn, g_qr,
                           g_kn, g_kr, w_o, w_ffn_in, w_ffn_out)
    conds = jnp.concatenate([c_ctx[None], c, jnp.zeros((8 - 1 - dec_batch, D_MODEL), F32)], axis=0)
    mod = _modulation(conds, w_mod, b_mod)
    dmask, dvec = _decay_tables(jnp.stack([ret_p_fwd, ret_p_bwd], axis=1))
    gn_g = g_ret_gn[:, None, :]
    gn_b = b_ret_gn[:, None, :]

    rope_tabs = _rope_tables(dec_seq)
    kvn_cache = _cache_up(cache_ckv, wts["w_kvb"], wts["g_kn"])
    krp_cache = jnp.pad(cache_krope, ((0, 0), (0, 0), (0, 0), (MLA_NOPE, LANES - MLA_NOPE - MLA_ROPE)))
    krp_cache = krp_cache.astype(BF16)
    s_f0 = _blockdiag_states(state_ret_fwd)
    s_b0 = _blockdiag_states(state_ret_bwd)

    lat_tiles = dec_seq // ROW_TILE

    def ctx_row(i):
        return 0

    def lat_row(i):
        return 1 + i // lat_tiles

    x = x_prompt.reshape(batch * seq, D_MODEL)
    y = x_sample.reshape(dec_batch * dec_seq, D_MODEL)
    caches = (jax.ShapeDtypeStruct((batch, depth, seq, KV_RANK), F32),
              jax.ShapeDtypeStruct((batch, depth, seq, MLA_ROPE), F32))
    states = (jax.ShapeDtypeStruct((batch, depth, RET_HEADS, RET_DK, RET_DK), F32),) * 2
    for l in range(depth):
        pr = _proj(x, mod, l, ctx_row, wts, None, caches)
        caches = tuple(pr[7:9])
        ret, attn, s_f, s_b = _mix(pr, dmask, dvec, gn_g, gn_b, l, batch, seq, None, states)
        states = (s_f, s_b)
        x = _out(x, ret, attn, mod, l, ctx_row, wts)

        pr = _proj(y, mod, l, lat_row, wts, rope_tabs, None)
        ret, attn = _mix(pr, dmask, dvec, gn_g, gn_b, l, dec_batch, dec_seq,
                         (kvn_cache, krp_cache, s_f0, s_b0), None)
        y = _out(y, ret, attn, mod, l, lat_row, wts)

    return (x.reshape(batch, seq, D_MODEL), y.reshape(dec_batch, dec_seq, D_MODEL),
            caches[0], caches[1], states[0], states[1])
```

```python
import functools

import jax
import jax.numpy as jnp
import numpy as np
from jax import lax
from jax.experimental import pallas as pl
from jax.experimental.pallas import tpu as pltpu

D_MODEL = 1024
N_MOD = 6
RET_HEADS = 8
RET_DK = 64
RET_WIDTH = 512
MLA_HEADS = 8
MLA_NOPE = 64
MLA_ROPE = 32
MLA_V = 64
Q_RANK = 256
KV_RANK = 128
D_FF = 2816
GRID_W = 64
ROPE_BASE = 10000.0
EPS = 1e-6

LANES = 128
HEAD_PAD = LANES
N_PAIRS = RET_HEADS // 2
IN_COLS_PAD = 4 * RET_WIDTH + Q_RANK + KV_RANK + LANES
ROW_TILE = 512
SUB_TILE = 256
FF_CHUNK = 256
MIX_CHUNK = 256
Q_DEC_F, K_DEC_F, Q_DEC_B, K_DEC_B, C_DEC_F, C_DEC_B = range(6)
N_DVEC = 6
Q_FOLD = (MLA_NOPE + MLA_ROPE) ** -0.5 * 1.4426950408889634
VMEM_LIMIT = 56 * 1024 * 1024

BF16 = jnp.bfloat16
F32 = jnp.float32
_NT = (((1,), (1,)), ((), ()))


def _dot(a, b):
    return jnp.dot(a, b, preferred_element_type=F32)


def _dot_nt(a, b):
    return lax.dot_general(a, b, _NT, preferred_element_type=F32)


def _rms(x, g):
    return x * lax.rsqrt(jnp.mean(x * x, axis=-1, keepdims=True) + EPS) * g


def _silu(x):
    return x * jax.nn.sigmoid(x)


def _masked_mean_sq(x, mask, n):
    return jnp.sum(jnp.where(mask, x * x, 0.0), axis=-1, keepdims=True) * (1.0 / n)


def _rope(x, c, sa, sb, shift):
    return x * c + pltpu.roll(x, shift, 1) * sa + pltpu.roll(x, LANES - shift, 1) * sb


def _mod_kernel(c_ref, w_ref, b_ref, o_ref):
    a = _silu(c_ref[...])
    w = w_ref[...]
    a_hi = a.astype(BF16)
    a_lo = (a - a_hi.astype(F32)).astype(BF16)
    w_hi = w.astype(BF16)
    w_lo = (w - w_hi.astype(F32)).astype(BF16)
    o_ref[...] = _dot(a_hi, w_hi) + _dot(a_lo, w_hi) + _dot(a_hi, w_lo) + b_ref[...]


def _modulation(conds, w_mod, b_mod):
    depth, _, n = w_mod.shape
    tn = 1536
    return pl.pallas_call(
        _mod_kernel,
        grid=(depth, n // tn),
        in_specs=[
            pl.BlockSpec((8, D_MODEL), lambda l, j: (0, 0)),
            pl.BlockSpec((None, D_MODEL, tn), lambda l, j: (l, 0, j)),
            pl.BlockSpec((None, 1, tn), lambda l, j: (l, 0, j)),
        ],
        out_specs=pl.BlockSpec((None, 8, tn), lambda l, j: (l, 0, j)),
        out_shape=jax.ShapeDtypeStruct((depth, 8, n), F32),
        compiler_params=pltpu.CompilerParams(
            dimension_semantics=("arbitrary", "arbitrary"), vmem_limit_bytes=VMEM_LIMIT),
        name="modulation",
    )(conds, w_mod, b_mod.reshape(depth, 1, n))


def _norm_kn(kv, gkn):
    lane = lax.broadcasted_iota(jnp.int32, (kv.shape[0], LANES), 1)
    lo = lane < MLA_NOPE
    out = []
    for h in range(MLA_HEADS):
        kvh = kv[:, h * HEAD_PAD:(h + 1) * HEAD_PAD]
        rs = lax.rsqrt(_masked_mean_sq(kvh, lo, MLA_NOPE) + EPS)
        out.append((kvh * jnp.where(lo, rs * gkn, 1.0)).astype(BF16))
    return out


def _cache_kernel(ckv_ref, wkvb_ref, gkn_ref, kvn_ref):
    kv = _dot(ckv_ref[...].astype(BF16), wkvb_ref[...])
    for h, kvh in enumerate(_norm_kn(kv, gkn_ref[...])):
        kvn_ref[:, h * HEAD_PAD:(h + 1) * HEAD_PAD] = kvh


def _cache_up(cache_ckv, wkvb, gkn):
    nb, depth, past, _ = cache_ckv.shape
    width = MLA_HEADS * HEAD_PAD
    return pl.pallas_call(
        _cache_kernel,
        grid=(depth, nb),
        in_specs=[
            pl.BlockSpec((None, None, past, KV_RANK), lambda l, b: (b, l, 0, 0)),
            pl.BlockSpec((None, KV_RANK, width), lambda l, b: (l, 0, 0)),
            pl.BlockSpec((None, 1, LANES), lambda l, b: (l, 0, 0)),
        ],
        out_specs=pl.BlockSpec((None, None, past, width), lambda l, b: (l, b, 0, 0)),
        out_shape=jax.ShapeDtypeStruct((depth, nb, past, width), BF16),
        compiler_params=pltpu.CompilerParams(
            dimension_semantics=("arbitrary", "arbitrary"), vmem_limit_bytes=VMEM_LIMIT),
        name="cache_up",
    )(cache_ckv, wkvb, gkn)


def _mod_vec(mod_ref, cond, k):
    return mod_ref[pl.ds(cond, 1), k * D_MODEL:(k + 1) * D_MODEL]


def _is_array(x):
    return not isinstance(x, jax.ShapeDtypeStruct)


def _proj_kernel(*refs, rope, cond_row, n_alias):
    (x_ref, mod_ref, gmix_ref, win_ref, gqa_ref, wqb_ref, gkva_ref, wkvb_ref,
     gq_ref, gkn_ref, gkr_ref) = refs[:11]
    if rope:
        c64_ref, sa64_ref, sb64_ref, c32_ref, sa32_ref, sb32_ref = refs[11:17]
        rq_ref, rk_ref, rv_ref, rg_ref, qcat_ref, kvn_ref, krp_ref = refs[17:]
    else:
        (rq_ref, rk_ref, rv_ref, rg_ref, qcat_ref, kvn_ref, krp_ref, ckv_ref,
         kro_ref) = refs[11 + n_alias:]

    cond = cond_row(pl.program_id(0))
    sh1 = _mod_vec(mod_ref, cond, 0)
    sc1 = _mod_vec(mod_ref, cond, 1)
    w = RET_WIDTH
    lane = lax.broadcasted_iota(jnp.int32, (SUB_TILE, LANES), 1)
    nope = lane < MLA_NOPE
    is_rope = jnp.logical_and(lane >= MLA_NOPE, lane < MLA_NOPE + MLA_ROPE)

    for r in range(x_ref.shape[0] // SUB_TILE):
        rs = slice(r * SUB_TILE, (r + 1) * SUB_TILE)
        h = (_rms(x_ref[rs, :], gmix_ref[...]) * (1.0 + sc1) + sh1).astype(BF16)
        z = _dot(h, win_ref[...])

        for j in range(w // LANES):
            sl = slice(j * LANES, (j + 1) * LANES)
            q = z[:, j * LANES:(j + 1) * LANES]
            k = z[:, w + j * LANES:w + (j + 1) * LANES] * (RET_DK ** -0.5)
            if rope:
                q = _rope(q, c64_ref[rs, :], sa64_ref[rs, :], sb64_ref[rs, :], 16)
                k = _rope(k, c64_ref[rs, :], sa64_ref[rs, :], sb64_ref[rs, :], 16)
            rq_ref[rs, sl] = q.astype(BF16)
            rk_ref[rs, sl] = k.astype(BF16)
        rv_ref[rs, :] = z[:, 2 * w:3 * w].astype(BF16)
        rg_ref[rs, :] = z[:, 3 * w:4 * w]

        o = 4 * w
        qa = z[:, o:o + Q_RANK]
        kva = z[:, o + Q_RANK:o + Q_RANK + KV_RANK]
        kr2 = z[:, o + Q_RANK + KV_RANK:]

        q = _dot(_rms(qa, gqa_ref[...]).astype(BF16), wqb_ref[...])
        for hh in range(MLA_HEADS):
            qh = q[:, hh * HEAD_PAD:(hh + 1) * HEAD_PAD]
            rs_n = lax.rsqrt(_masked_mean_sq(qh, nope, MLA_NOPE) + EPS)
            rs_r = lax.rsqrt(_masked_mean_sq(qh, is_rope, MLA_ROPE) + EPS)
            qn = qh * jnp.where(nope, rs_n, rs_r) * gq_ref[...]
            if rope:
                qn = _rope(qn, c32_ref[rs, :], sa32_ref[rs, :], sb32_ref[rs, :], 8)
            qcat_ref[rs, hh * HEAD_PAD:(hh + 1) * HEAD_PAD] = qn.astype(BF16)

        ckv = _rms(kva, gkva_ref[...])
        kv = _dot(ckv.astype(BF16), wkvb_ref[...])
        for hh, kvh in enumerate(_norm_kn(kv, gkn_ref[...])):
            kvn_ref[rs, hh * HEAD_PAD:(hh + 1) * HEAD_PAD] = kvh

        krn = kr2 * lax.rsqrt(_masked_mean_sq(kr2, lane < MLA_ROPE, MLA_ROPE) + EPS) * gkr_ref[...]
        krp = pltpu.roll(krn, MLA_NOPE, 1)
        if rope:
            krp = _rope(krp, c32_ref[rs, :], sa32_ref[rs, :], sb32_ref[rs, :], 8)
        else:
            ckv_ref[r, :, :] = ckv
            kro_ref[r, :, :] = krn[:, 0:MLA_ROPE]
        krp_ref[rs, :] = krp.astype(BF16)


def _proj(x, mod, layer, cond_row, wts, w_in_bf, rope_tabs, caches):
    m = x.shape[0]
    tm = ROW_TILE
    rope = rope_tabs is not None
    width = MLA_HEADS * HEAD_PAD

    def row(i):
        return (i, 0)

    def lay3(i):
        return (layer, 0, 0)

    in_specs = [
        pl.BlockSpec((tm, D_MODEL), row),
        pl.BlockSpec((None, 8, N_MOD * D_MODEL), lay3),
        pl.BlockSpec((None, 1, D_MODEL), lay3),
        pl.BlockSpec((D_MODEL, IN_COLS_PAD), lambda i: (0, 0)),
        pl.BlockSpec((None, 1, Q_RANK), lay3),
        pl.BlockSpec((None, Q_RANK, width), lay3),
        pl.BlockSpec((None, 1, KV_RANK), lay3),
        pl.BlockSpec((None, KV_RANK, width), lay3),
        pl.BlockSpec((None, 1, LANES), lay3),
        pl.BlockSpec((None, 1, LANES), lay3),
        pl.BlockSpec((None, 1, LANES), lay3),
    ]
    args = [x, mod, wts["g_mix"], w_in_bf, wts["g_qa"], wts["w_qb"], wts["g_kva"],
            wts["w_kvb"], wts["g_q"], wts["g_kn"], wts["g_kr"]]
    out_shape = [
        jax.ShapeDtypeStruct((m, RET_WIDTH), BF16),
        jax.ShapeDtypeStruct((m, RET_WIDTH), BF16),
        jax.ShapeDtypeStruct((m, RET_WIDTH), BF16),
        jax.ShapeDtypeStruct((m, RET_WIDTH), F32),
        jax.ShapeDtypeStruct((m, width), BF16),
        jax.ShapeDtypeStruct((m, width), BF16),
        jax.ShapeDtypeStruct((m, LANES), BF16),
    ]
    out_specs = [
        pl.BlockSpec((tm, RET_WIDTH), row), pl.BlockSpec((tm, RET_WIDTH), row),
        pl.BlockSpec((tm, RET_WIDTH), row), pl.BlockSpec((tm, RET_WIDTH), row),
        pl.BlockSpec((tm, width), row), pl.BlockSpec((tm, width), row),
        pl.BlockSpec((tm, LANES), row),
    ]
    aliases = {}
    if rope:
        n_lat = rope_tabs[0].shape[0]
        tiles = n_lat // tm
        in_specs += [pl.BlockSpec((tm, LANES), lambda i: (i % tiles, 0))] * 6
        args += list(rope_tabs)
    else:
        seq = caches[0].shape[2]
        assert seq == SUB_TILE
        nb = tm // seq
        if _is_array(caches[0]):
            in_specs += [pl.BlockSpec(memory_space=pl.ANY)] * 2
            aliases = {len(args): len(out_shape), len(args) + 1: len(out_shape) + 1}
            args += list(caches)
        out_shape += [jax.ShapeDtypeStruct(c.shape, c.dtype) for c in caches]
        out_specs += [pl.BlockSpec((nb, None, seq, KV_RANK), lambda i: (i, layer, 0, 0)),
                      pl.BlockSpec((nb, None, seq, MLA_ROPE), lambda i: (i, layer, 0, 0))]

    return pl.pallas_call(
        functools.partial(_proj_kernel, rope=rope, cond_row=cond_row, n_alias=len(aliases)),
        grid=(m // tm,),
        in_specs=in_specs,
        out_specs=out_specs,
        out_shape=out_shape,
        input_output_aliases=aliases,
        compiler_params=pltpu.CompilerParams(
            dimension_semantics=("arbitrary",), vmem_limit_bytes=VMEM_LIMIT),
        name="proj_latent" if rope else "proj_context",
    )(*args)


def _log_gamma(p):
    return jnp.log1p(-jnp.exp2(-p))


def _decay_kernel(p_ref, dmask_ref, dvec_ref):
    c = MIX_CHUNK
    base = pl.program_id(0) * (2 * RET_HEADS)
    pair = pl.program_id(1)
    ri = lax.broadcasted_iota(jnp.int32, (c, c), 0)
    ci = lax.broadcasted_iota(jnp.int32, (c, c), 1)
    dif = (ri - ci).astype(F32)
    for e in range(2):
        h = 2 * pair + e
        lg_f = _log_gamma(jnp.full((c, c), p_ref[base + h], F32))
        lg_b = _log_gamma(jnp.full((c, c), p_ref[base + RET_HEADS + h], F32))
        fwd = jnp.where(dif >= 0, jnp.exp(jnp.maximum(dif, 0.0) * lg_f), 0.0)
        bwd = jnp.where(dif <= 0, jnp.exp(jnp.maximum(-dif, 0.0) * lg_b), 0.0)
        dmask_ref[e] = fwd + bwd
    lane = lax.broadcasted_iota(jnp.int32, (c, LANES), 1)
    rowf = lax.broadcasted_iota(jnp.int32, (c, LANES), 0).astype(F32)
    lo = lane < RET_DK
    lg_f = _log_gamma(jnp.where(lo, p_ref[base + 2 * pair], p_ref[base + 2 * pair + 1]))
    lg_b = _log_gamma(jnp.where(lo, p_ref[base + RET_HEADS + 2 * pair],
                                p_ref[base + RET_HEADS + 2 * pair + 1]))
    dvec_ref[Q_DEC_F] = jnp.exp((rowf + 1.0) * lg_f)
    dvec_ref[K_DEC_F] = jnp.exp((c - 1.0 - rowf) * lg_f)
    dvec_ref[Q_DEC_B] = jnp.exp((c - rowf) * lg_b)
    dvec_ref[K_DEC_B] = jnp.exp(rowf * lg_b)
    dvec_ref[C_DEC_F] = jnp.exp(c * lg_f)
    dvec_ref[C_DEC_B] = jnp.exp(c * lg_b)


def _decay_tables(decay_p):
    depth = decay_p.shape[0]
    c = MIX_CHUNK
    return pl.pallas_call(
        _decay_kernel,
        grid=(depth, N_PAIRS),
        in_specs=[pl.BlockSpec(memory_space=pltpu.SMEM)],
        out_specs=[
            pl.BlockSpec((None, 2, c, c), lambda l, p: (l, p, 0, 0)),
            pl.BlockSpec((None, None, N_DVEC, c, LANES), lambda l, p: (l, p, 0, 0, 0)),
        ],
        out_shape=[
            jax.ShapeDtypeStruct((depth, RET_HEADS, c, c), F32),
            jax.ShapeDtypeStruct((depth, N_PAIRS, N_DVEC, c, LANES), F32),
        ],
        compiler_params=pltpu.CompilerParams(
            dimension_semantics=("arbitrary", "arbitrary"), vmem_limit_bytes=VMEM_LIMIT),
        name="decay_tables",
    )(decay_p.reshape(-1))


def _mix_kernel(*refs, n_seq, latent, n_alias, n_cast):
    c = MIX_CHUNK
    nc = n_seq // c
    (rq_ref, rk_ref, rv_ref, rg_ref, qcat_ref, kvn_ref, krp_ref, gng_ref, gnb_ref,
     dmask_ref, dvec_ref) = refs[:11]
    if latent:
        kvc_ref, krc_ref, sf0_ref, sb0_ref, ret_ref, attn_ref, st_scr = refs[11:]
    else:
        n_in = 11 + n_alias
        cast_src = refs[n_in:n_in + n_cast]
        ret_ref, attn_ref, sf_ref, sb_ref = refs[n_in + n_cast:n_in + n_cast + 4]
        cast_dst = refs[n_in + n_cast + 4:]
        for src, dst in zip(cast_src, cast_dst):
            w = src.shape[1]
            dst[:, 0:w] = src[...].astype(BF16)
            if dst.shape[1] > w:
                dst[:, w:] = jnp.zeros((dst.shape[0], dst.shape[1] - w), BF16)

    lane = lax.broadcasted_iota(jnp.int32, (c, LANES), 1)
    lo = lane < RET_DK
    sq_r = lax.broadcasted_iota(jnp.int32, (LANES, LANES), 0)
    sq_c = lax.broadcasted_iota(jnp.int32, (LANES, LANES), 1)
    blockdiag = (sq_r < RET_DK) == (sq_c < RET_DK)

    def cols(j):
        return slice(j * LANES, (j + 1) * LANES)

    def state_update(pair, d, rows):
        kp = rk_ref[rows, cols(pair)]
        vp = rv_ref[rows, cols(pair)]
        kdt = (kp.astype(F32) * dvec_ref[pair, K_DEC_B if d else K_DEC_F]).T.astype(BF16)
        return jnp.where(blockdiag, _dot(kdt, vp), 0.0)

    for pair in range(N_PAIRS):
        for d in range(2):
            if latent:
                s = (sb0_ref if d else sf0_ref)[pair]
                cdec = dvec_ref[pair, C_DEC_B if d else C_DEC_F][0:LANES, :]
                order = list(range(nc - 1, -1, -1)) if d else list(range(nc))
                for idx, ch in enumerate(order):
                    st_scr[d, pair, ch] = s.astype(BF16)
                    if idx < nc - 1:
                        s = s * cdec + state_update(pair, d, pl.ds(ch * c, c))
            else:
                s = state_update(pair, d, pl.ds(0, c))
                st_ref = sb_ref if d else sf_ref
                st_ref[2 * pair] = s[0:RET_DK, 0:RET_DK]
                st_ref[2 * pair + 1] = s[RET_DK:, RET_DK:]

    nope_n = lax.broadcasted_iota(jnp.int32, (n_seq, LANES), 1) < MLA_NOPE
    if latent:
        nope_c = lax.broadcasted_iota(jnp.int32, (kvc_ref.shape[0], LANES), 1) < MLA_NOPE

    def chunk_body(ch, carry):
        rows = pl.ds(pl.multiple_of(ch * c, c), c)

        for pair in range(N_PAIRS):
            qp = rq_ref[rows, cols(pair)]
            kp = rk_ref[rows, cols(pair)]
            vp = rv_ref[rows, cols(pair)]
            zero = jnp.zeros_like(qp)
            a0 = (_dot_nt(jnp.where(lo, qp, zero), kp) * dmask_ref[2 * pair]).astype(BF16)
            a1 = (_dot_nt(jnp.where(lo, zero, qp), kp) * dmask_ref[2 * pair + 1]).astype(BF16)
            tot = jnp.where(lo, _dot(a0, vp), _dot(a1, vp))
            if latent:
                tot = (tot + _dot(qp, st_scr[0, pair, ch]) * dvec_ref[pair, Q_DEC_F]
                       + _dot(qp, st_scr[1, pair, ch]) * dvec_ref[pair, Q_DEC_B])
            inv = 1.0 / RET_DK
            m0 = jnp.sum(jnp.where(lo, tot, 0.0), axis=-1, keepdims=True) * inv
            m1 = jnp.sum(jnp.where(lo, 0.0, tot), axis=-1, keepdims=True) * inv
            y = tot - jnp.where(lo, m0, m1)
            v0 = jnp.sum(jnp.where(lo, y * y, 0.0), axis=-1, keepdims=True) * inv
            v1 = jnp.sum(jnp.where(lo, 0.0, y * y), axis=-1, keepdims=True) * inv
            yn = (y * lax.rsqrt(jnp.where(lo, v0, v1) + EPS) * gng_ref[:, cols(pair)]
                  + gnb_ref[:, cols(pair)])
            ret_ref[rows, cols(pair)] = (yn * _silu(rg_ref[rows, cols(pair)])).astype(BF16)

        krp = krp_ref[...]
        for pair in range(N_PAIRS):
            outs = []
            for e in range(2):
                h = 2 * pair + e
                qc = qcat_ref[rows, cols(h)]
                kv = kvn_ref[:, cols(h)]
                s = _dot_nt(qc, jnp.where(nope_n, kv, krp))
                m = jnp.max(s, axis=-1, keepdims=True)
                if latent:
                    kv_c = kvc_ref[:, cols(h)]
                    s2 = _dot_nt(qc, jnp.where(nope_c, kv_c, krc_ref[...]))
                    m = jnp.maximum(m, jnp.max(s2, axis=-1, keepdims=True))
                    p2 = jnp.exp2(s2 - m)
                p = jnp.exp2(s - m)
                den = jnp.sum(p, axis=-1, keepdims=True)
                acc = _dot(p.astype(BF16), kv)
                if latent:
                    den = den + jnp.sum(p2, axis=-1, keepdims=True)
                    acc = acc + _dot(p2.astype(BF16), kv_c)
                outs.append(acc / den)
            attn_ref[rows, cols(pair)] = jnp.where(
                lo, pltpu.roll(outs[0], MLA_V, 1), outs[1]).astype(BF16)
        return carry

    if nc == 1:
        chunk_body(0, 0)
    else:
        lax.fori_loop(0, nc, chunk_body, 0)


def _mix(proj_out, dmask, dvec, gn_g, gn_b, layer, n_batch, n_seq, latent_in, states_out,
         casts=()):
    rq, rk, rv, rg, qcat, kvn, krp = proj_out[:7]
    latent = latent_in is not None
    aliases = {}
    m = n_batch * n_seq
    c = MIX_CHUNK
    width = MLA_HEADS * HEAD_PAD

    def row(b):
        return (b, 0)

    def lay3(b):
        return (layer, 0, 0)

    once = pl.Buffered(1)
    in_specs = [
        pl.BlockSpec((n_seq, RET_WIDTH), row), pl.BlockSpec((n_seq, RET_WIDTH), row),
        pl.BlockSpec((n_seq, RET_WIDTH), row), pl.BlockSpec((n_seq, RET_WIDTH), row),
        pl.BlockSpec((n_seq, width), row), pl.BlockSpec((n_seq, width), row),
        pl.BlockSpec((n_seq, LANES), row),
        pl.BlockSpec((None, 1, RET_WIDTH), lay3), pl.BlockSpec((None, 1, RET_WIDTH), lay3),
        pl.BlockSpec((None, RET_HEADS, c, c), lambda b: (layer, 0, 0, 0), pipeline_mode=once),
        pl.BlockSpec((None, N_PAIRS, N_DVEC, c, LANES), lambda b: (layer, 0, 0, 0, 0),
                     pipeline_mode=once),
    ]
    args = [rq, rk, rv, rg, qcat, kvn, krp, gn_g, gn_b, dmask, dvec]
    out_shape = [jax.ShapeDtypeStruct((m, RET_WIDTH), BF16),
                 jax.ShapeDtypeStruct((m, MLA_HEADS * MLA_V), BF16)]
    out_specs = [pl.BlockSpec((n_seq, RET_WIDTH), row), pl.BlockSpec((n_seq, RET_WIDTH), row)]
    scratch = []
    if latent:
        kvn_c, krp_c, s_f0, s_b0 = latent_in
        past = kvn_c.shape[2]
        st_spec = pl.BlockSpec((None, None, N_PAIRS, LANES, LANES), lambda b: (b, layer, 0, 0, 0))
        in_specs += [
            pl.BlockSpec((None, None, past, width), lambda b: (layer, b, 0, 0)),
            pl.BlockSpec((None, None, past, LANES), lambda b: (b, layer, 0, 0)),
            st_spec, st_spec,
        ]
        args += [kvn_c, krp_c, s_f0, s_b0]
        scratch = [pltpu.VMEM((2, N_PAIRS, n_seq // c, LANES, LANES), BF16)]
    else:
        if _is_array(states_out[0]):
            in_specs += [pl.BlockSpec(memory_space=pl.ANY)] * 2
            aliases = {len(args): len(out_shape), len(args) + 1: len(out_shape) + 1}
            args += list(states_out)
        out_shape += [jax.ShapeDtypeStruct(s.shape, s.dtype) for s in states_out]
        st_spec = pl.BlockSpec((None, None, RET_HEADS, RET_DK, RET_DK),
                               lambda b: (b, layer, 0, 0, 0))
        out_specs += [st_spec, st_spec]
        for w, w_layer, out_cols, blocks in casts:
            rows = w.shape[1] // blocks
            stride = n_batch // blocks
            assert rows * blocks == w.shape[1] and rows % 16 == 0 and stride * blocks == n_batch
            in_specs.append(pl.BlockSpec(
                (None, rows, w.shape[2]),
                lambda b, w_layer=w_layer, stride=stride: (w_layer, b // stride, 0)))
            args.append(w)
            out_shape.append(jax.ShapeDtypeStruct((w.shape[1], out_cols), BF16))
            out_specs.append(pl.BlockSpec((rows, out_cols),
                                          lambda b, stride=stride: (b // stride, 0)))

    return pl.pallas_call(
        functools.partial(_mix_kernel, n_seq=n_seq, latent=latent, n_alias=len(aliases),
                          n_cast=len(casts)),
        grid=(n_batch,),
        in_specs=in_specs,
        out_specs=out_specs,
        out_shape=out_shape,
        input_output_aliases=aliases,
        scratch_shapes=scratch,
        compiler_params=pltpu.CompilerParams(
            dimension_semantics=("arbitrary",), vmem_limit_bytes=VMEM_LIMIT),
        name="mix_latent" if latent else "mix_context",
    )(*args)


def _out_kernel(x_ref, ret_ref, attn_ref, mod_ref, gffn_ref, wo_ref, wfi_ref, wfo_ref, o_ref,
                act_scr, *, cond_row):
    cond = cond_row(pl.program_id(0))
    gt1 = _mod_vec(mod_ref, cond, 2)
    sh2 = _mod_vec(mod_ref, cond, 3)
    sc2 = _mod_vec(mod_ref, cond, 4)
    gt2 = _mod_vec(mod_ref, cond, 5)
    mixed = _dot(ret_ref[...], wo_ref[0:RET_WIDTH, :]) + _dot(attn_ref[...], wo_ref[RET_WIDTH:, :])
    x1 = x_ref[...] + gt1 * mixed
    h = (_rms(x1, gffn_ref[...]) * (1.0 + sc2) + sh2).astype(BF16)
    for c0 in range(0, D_FF, FF_CHUNK):
        cw = min(FF_CHUNK, D_FF - c0)
        gate = _dot(h, wfi_ref[:, c0:c0 + cw])
        up = _dot(h, wfi_ref[:, D_FF + c0:D_FF + c0 + cw])
        act_scr[:, c0:c0 + cw] = (_silu(gate) * up).astype(BF16)
    o_ref[...] = x1 + gt2 * _dot(act_scr[...], wfo_ref[...])


def _out(x, ret, attn, mod, layer, cond_row, wts, big_w):
    m = x.shape[0]
    tm = ROW_TILE

    def row(i):
        return (i, 0)

    def lay3(i):
        return (layer, 0, 0)

    def whole(i):
        return (0, 0)

    once = pl.Buffered(1)
    return pl.pallas_call(
        functools.partial(_out_kernel, cond_row=cond_row),
        grid=(m // tm,),
        in_specs=[
            pl.BlockSpec((tm, D_MODEL), row),
            pl.BlockSpec((tm, RET_WIDTH), row),
            pl.BlockSpec((tm, RET_WIDTH), row),
            pl.BlockSpec((None, 8, N_MOD * D_MODEL), lay3),
            pl.BlockSpec((None, 1, D_MODEL), lay3),
            pl.BlockSpec((D_MODEL, D_MODEL), whole, pipeline_mode=once),
            pl.BlockSpec((D_MODEL, 2 * D_FF), whole, pipeline_mode=once),
            pl.BlockSpec((D_FF, D_MODEL), whole, pipeline_mode=once),
        ],
        out_specs=pl.BlockSpec((tm, D_MODEL), row),
        out_shape=jax.ShapeDtypeStruct((m, D_MODEL), F32),
        scratch_shapes=[pltpu.VMEM((tm, D_FF), BF16)],
        compiler_params=pltpu.CompilerParams(
            dimension_semantics=("arbitrary",), vmem_limit_bytes=VMEM_LIMIT),
        name="out_ffn",
    )(x, ret, attn, mod, wts["g_ffn"], *big_w)


def _rope_tables(n_lat):
    pos = np.arange(n_lat)
    row = (pos // GRID_W).astype(np.float32)[:, None]
    col = (pos % GRID_W).astype(np.float32)[:, None]
    lane = np.arange(LANES)[None, :]

    def tables(d, start, period):
        rel = (lane - start) % period
        active = np.logical_and(lane >= start, rel < d)
        half = d // 2
        nf = half // 2
        inv = np.float32(ROPE_BASE) ** (-((rel % nf).astype(np.float32)) / np.float32(nf))
        ang = (np.where(rel < half, row, col) * inv).astype(np.float32)
        cos, sin = np.cos(ang), np.sin(ang)
        first = (rel % half) < nf
        c = np.where(active, cos, 1.0)
        sa = np.where(np.logical_and(active, np.logical_not(first)), sin, 0.0)
        sb = np.where(np.logical_and(active, first), -sin, 0.0)
        return tuple(jnp.asarray(t, F32) for t in (c, sa, sb))

    return tables(RET_DK, 0, RET_DK) + tables(MLA_ROPE, MLA_NOPE, LANES)


def _prepare_weights(g_norm_mix, g_norm_ffn, w_in, g_q_a, w_q_b, g_kv_a, w_kv_b, g_qn, g_qr, g_kn,
                     g_kr):
    depth = w_in.shape[0]
    w_in0 = jnp.pad(w_in[0].astype(BF16), ((0, 0), (0, IN_COLS_PAD - w_in.shape[2])))
    w_qb = w_q_b.reshape(depth, Q_RANK, MLA_HEADS, MLA_NOPE + MLA_ROPE)
    w_qb = jnp.pad(w_qb, ((0, 0), (0, 0), (0, 0), (0, HEAD_PAD - MLA_NOPE - MLA_ROPE)))
    zeros32 = jnp.zeros((depth, MLA_ROPE), F32)
    return {
        "g_mix": g_norm_mix[:, None, :],
        "g_ffn": g_norm_ffn[:, None, :],
        "w_in0": w_in0,
        "g_qa": g_q_a[:, None, :],
        "w_qb": w_qb.reshape(depth, Q_RANK, MLA_HEADS * HEAD_PAD).astype(BF16),
        "g_kva": g_kv_a[:, None, :],
        "w_kvb": w_kv_b.astype(BF16),
        "g_q": (jnp.concatenate([g_qn, g_qr, zeros32], axis=-1) * Q_FOLD)[:, None, :],
        "g_kn": jnp.concatenate([g_kn, jnp.ones((depth, MLA_V), F32)], axis=-1)[:, None, :],
        "g_kr": jnp.pad(g_kr, ((0, 0), (0, LANES - MLA_ROPE)))[:, None, :],
    }


def _blockdiag_states(s):
    b, l = s.shape[:2]
    s = s.reshape(b, l, N_PAIRS, 2, RET_DK, RET_DK)
    z = jnp.zeros_like(s[:, :, :, 0])
    top = jnp.concatenate([s[:, :, :, 0], z], axis=-1)
    bot = jnp.concatenate([z, s[:, :, :, 1]], axis=-1)
    return jnp.concatenate([top, bot], axis=-2)


def kernel(x_prompt, x_sample, cache_ckv, cache_krope, state_ret_fwd, state_ret_bwd, c, c_ctx,
           w_mod, b_mod, g_norm_mix, g_norm_ffn, w_in, g_q_a, w_q_b, g_kv_a, w_kv_b,
           g_qn, g_qr, g_kn, g_kr, ret_p_fwd, ret_p_bwd, g_ret_gn, b_ret_gn, w_o,
           w_ffn_in, w_ffn_out):
    batch, seq, _ = x_prompt.shape
    dec_batch, dec_seq, _ = x_sample.shape
    depth = w_in.shape[0]

    wts = _prepare_weights(g_norm_mix, g_norm_ffn, w_in, g_q_a, w_q_b, g_kv_a, w_kv_b, g_qn, g_qr,
                           g_kn, g_kr)
    conds = jnp.concatenate([c_ctx[None], c, jnp.zeros((8 - 1 - dec_batch, D_MODEL), F32)], axis=0)
    mod = _modulation(conds, w_mod, b_mod)
    dmask, dvec = _decay_tables(jnp.stack([ret_p_fwd, ret_p_bwd], axis=1))
    gn_g = g_ret_gn[:, None, :]
    gn_b = b_ret_gn[:, None, :]

    rope_tabs = _rope_tables(dec_seq)
    kvn_cache = _cache_up(cache_ckv, wts["w_kvb"], wts["g_kn"])
    krp_cache = jnp.pad(cache_krope, ((0, 0), (0, 0), (0, 0), (MLA_NOPE, LANES - MLA_NOPE - MLA_ROPE)))
    krp_cache = krp_cache.astype(BF16)
    s_f0 = _blockdiag_states(state_ret_fwd)
    s_b0 = _blockdiag_states(state_ret_bwd)

    lat_tiles = dec_seq // ROW_TILE

    def ctx_row(i):
        return 0

    def lat_row(i):
        return 1 + i // lat_tiles

    x = x_prompt.reshape(batch * seq, D_MODEL)
    y = x_sample.reshape(dec_batch * dec_seq, D_MODEL)
    caches = (jax.ShapeDtypeStruct((batch, depth, seq, KV_RANK), F32),
              jax.ShapeDtypeStruct((batch, depth, seq, MLA_ROPE), F32))
    states = (jax.ShapeDtypeStruct((batch, depth, RET_HEADS, RET_DK, RET_DK), F32),) * 2
    w_in_bf = wts["w_in0"]
    for l in range(depth):
        pr = _proj(x, mod, l, ctx_row, wts, w_in_bf, None, caches)
        caches = tuple(pr[7:9])
        casts = [(w_o, l, D_MODEL, batch), (w_ffn_in, l, 2 * D_FF, batch),
                 (w_ffn_out, l, D_MODEL, batch // 2)]
        if l + 1 < depth:
            casts.append((w_in, l + 1, IN_COLS_PAD, batch))
        mixed = _mix(pr, dmask, dvec, gn_g, gn_b, l, batch, seq, None, states, casts)
        ret, attn = mixed[:2]
        states = tuple(mixed[2:4])
        big_w = tuple(mixed[4:7])
        x = _out(x, ret, attn, mod, l, ctx_row, wts, big_w)

        pr = _proj(y, mod, l, lat_row, wts, w_in_bf, rope_tabs, None)
        ret, attn = _mix(pr, dmask, dvec, gn_g, gn_b, l, dec_batch, dec_seq,
                         (kvn_cache, krp_cache, s_f0, s_b0), None)
        y = _out(y, ret, attn, mod, l, lat_row, wts, big_w)
        if l + 1 < depth:
            w_in_bf = mixed[7]

    return (x.reshape(batch, seq, D_MODEL), y.reshape(dec_batch, dec_seq, D_MODEL),
            caches[0], caches[1], states[0], states[1])
```

```python
import functools

import jax
import jax.numpy as jnp
import numpy as np
from jax import lax
from jax.experimental import pallas as pl
from jax.experimental.pallas import tpu as pltpu

D_MODEL = 1024
N_MOD = 6
RET_HEADS = 8
RET_DK = 64
RET_WIDTH = 512
MLA_HEADS = 8
MLA_NOPE = 64
MLA_ROPE = 32
MLA_V = 64
Q_RANK = 256
KV_RANK = 128
D_FF = 2816
GRID_W = 64
ROPE_BASE = 10000.0
EPS = 1e-6

LANES = 128
HEAD_PAD = LANES
N_PAIRS = RET_HEADS // 2
IN_COLS_PAD = 4 * RET_WIDTH + Q_RANK + KV_RANK + LANES
ROW_TILE = 512
SUB_TILE = 256
FF_CHUNK = 256
MIX_CHUNK = 256
Q_DEC_F, K_DEC_F, Q_DEC_B, K_DEC_B, C_DEC_F, C_DEC_B = range(6)
N_DVEC = 6
Q_FOLD = (MLA_NOPE + MLA_ROPE) ** -0.5 * 1.4426950408889634
VMEM_LIMIT = 56 * 1024 * 1024

BF16 = jnp.bfloat16
F32 = jnp.float32
_NT = (((1,), (1,)), ((), ()))


def _dot(a, b):
    return jnp.dot(a, b, preferred_element_type=F32)


def _dot_nt(a, b):
    return lax.dot_general(a, b, _NT, preferred_element_type=F32)


def _rms(x, g):
    return x * lax.rsqrt(jnp.mean(x * x, axis=-1, keepdims=True) + EPS) * g


def _silu(x):
    return x * jax.nn.sigmoid(x)


def _masked_mean_sq(x, mask, n):
    return jnp.sum(jnp.where(mask, x * x, 0.0), axis=-1, keepdims=True) * (1.0 / n)


def _rope(x, c, sa, sb, shift):
    return x * c + pltpu.roll(x, shift, 1) * sa + pltpu.roll(x, LANES - shift, 1) * sb


def _mod_kernel(c_ref, w_ref, b_ref, o_ref):
    a = _silu(c_ref[...])
    w = w_ref[...]
    a_hi = a.astype(BF16)
    a_lo = (a - a_hi.astype(F32)).astype(BF16)
    w_hi = w.astype(BF16)
    w_lo = (w - w_hi.astype(F32)).astype(BF16)
    both = _dot(jnp.concatenate([a_hi, a_lo], axis=0), w_hi)
    o_ref[...] = both[0:8] + both[8:16] + _dot(a_hi, w_lo) + b_ref[...]


def _modulation(conds, w_mod, b_mod):
    depth, _, n = w_mod.shape
    tn = 1536
    return pl.pallas_call(
        _mod_kernel,
        grid=(depth, n // tn),
        in_specs=[
            pl.BlockSpec((8, D_MODEL), lambda l, j: (0, 0)),
            pl.BlockSpec((None, D_MODEL, tn), lambda l, j: (l, 0, j)),
            pl.BlockSpec((None, 1, tn), lambda l, j: (l, 0, j)),
        ],
        out_specs=pl.BlockSpec((None, 8, tn), lambda l, j: (l, 0, j)),
        out_shape=jax.ShapeDtypeStruct((depth, 8, n), F32),
        compiler_params=pltpu.CompilerParams(
            dimension_semantics=("arbitrary", "arbitrary"), vmem_limit_bytes=VMEM_LIMIT),
        name="modulation",
    )(conds, w_mod, b_mod.reshape(depth, 1, n))


def _norm_kn(kv, gkn):
    lane = lax.broadcasted_iota(jnp.int32, (kv.shape[0], LANES), 1)
    lo = lane < MLA_NOPE
    out = []
    for h in range(MLA_HEADS):
        kvh = kv[:, h * HEAD_PAD:(h + 1) * HEAD_PAD]
        rs = lax.rsqrt(_masked_mean_sq(kvh, lo, MLA_NOPE) + EPS)
        out.append((kvh * jnp.where(lo, rs * gkn, 1.0)).astype(BF16))
    return out


def _cache_kernel(ckv_ref, wkvb_ref, gkn_ref, kvn_ref):
    kv = _dot(ckv_ref[...].astype(BF16), wkvb_ref[...])
    for h, kvh in enumerate(_norm_kn(kv, gkn_ref[...])):
        kvn_ref[:, h * HEAD_PAD:(h + 1) * HEAD_PAD] = kvh


def _cache_up(cache_ckv, wkvb, gkn):
    nb, depth, past, _ = cache_ckv.shape
    width = MLA_HEADS * HEAD_PAD
    return pl.pallas_call(
        _cache_kernel,
        grid=(depth, nb),
        in_specs=[
            pl.BlockSpec((None, None, past, KV_RANK), lambda l, b: (b, l, 0, 0)),
            pl.BlockSpec((None, KV_RANK, width), lambda l, b: (l, 0, 0)),
            pl.BlockSpec((None, 1, LANES), lambda l, b: (l, 0, 0)),
        ],
        out_specs=pl.BlockSpec((None, None, past, width), lambda l, b: (l, b, 0, 0)),
        out_shape=jax.ShapeDtypeStruct((depth, nb, past, width), BF16),
        compiler_params=pltpu.CompilerParams(
            dimension_semantics=("arbitrary", "arbitrary"), vmem_limit_bytes=VMEM_LIMIT),
        name="cache_up",
    )(cache_ckv, wkvb, gkn)


def _mod_vec(mod_ref, cond, k):
    return mod_ref[pl.ds(cond, 1), k * D_MODEL:(k + 1) * D_MODEL]


def _is_array(x):
    return not isinstance(x, jax.ShapeDtypeStruct)


def _proj_kernel(*refs, rope, cond_row, n_alias):
    (x_ref, mod_ref, gmix_ref, win_ref, gqa_ref, wqb_ref, gkva_ref, wkvb_ref,
     gq_ref, gkn_ref, gkr_ref) = refs[:11]
    if rope:
        c64_ref, sa64_ref, sb64_ref, c32_ref, sa32_ref, sb32_ref = refs[11:17]
        rq_ref, rk_ref, rv_ref, rg_ref, qcat_ref, kvn_ref, krp_ref = refs[17:]
    else:
        (rq_ref, rk_ref, rv_ref, rg_ref, qcat_ref, kvn_ref, krp_ref, ckv_ref,
         kro_ref) = refs[11 + n_alias:]

    cond = cond_row(pl.program_id(0))
    sh1 = _mod_vec(mod_ref, cond, 0)
    sc1 = _mod_vec(mod_ref, cond, 1)
    w = RET_WIDTH
    lane = lax.broadcasted_iota(jnp.int32, (SUB_TILE, LANES), 1)
    nope = lane < MLA_NOPE
    is_rope = jnp.logical_and(lane >= MLA_NOPE, lane < MLA_NOPE + MLA_ROPE)

    for r in range(x_ref.shape[0] // SUB_TILE):
        rs = slice(r * SUB_TILE, (r + 1) * SUB_TILE)
        h = (_rms(x_ref[rs, :], gmix_ref[...]) * (1.0 + sc1) + sh1).astype(BF16)
        z = _dot_nt(h, win_ref[...])

        for j in range(w // LANES):
            sl = slice(j * LANES, (j + 1) * LANES)
            q = z[:, j * LANES:(j + 1) * LANES]
            k = z[:, w + j * LANES:w + (j + 1) * LANES] * (RET_DK ** -0.5)
            if rope:
                q = _rope(q, c64_ref[rs, :], sa64_ref[rs, :], sb64_ref[rs, :], 16)
                k = _rope(k, c64_ref[rs, :], sa64_ref[rs, :], sb64_ref[rs, :], 16)
            rq_ref[rs, sl] = q.astype(BF16)
            rk_ref[rs, sl] = k.astype(BF16)
        rv_ref[rs, :] = z[:, 2 * w:3 * w].astype(BF16)
        rg_ref[rs, :] = z[:, 3 * w:4 * w]

        o = 4 * w
        qa = z[:, o:o + Q_RANK]
        kva = z[:, o + Q_RANK:o + Q_RANK + KV_RANK]
        kr2 = z[:, o + Q_RANK + KV_RANK:]

        q = _dot(_rms(qa, gqa_ref[...]).astype(BF16), wqb_ref[...])
        for hh in range(MLA_HEADS):
            qh = q[:, hh * HEAD_PAD:(hh + 1) * HEAD_PAD]
            rs_n = lax.rsqrt(_masked_mean_sq(qh, nope, MLA_NOPE) + EPS)
            rs_r = lax.rsqrt(_masked_mean_sq(qh, is_rope, MLA_ROPE) + EPS)
            qn = qh * jnp.where(nope, rs_n, rs_r) * gq_ref[...]
            if rope:
                qn = _rope(qn, c32_ref[rs, :], sa32_ref[rs, :], sb32_ref[rs, :], 8)
            qcat_ref[rs, hh * HEAD_PAD:(hh + 1) * HEAD_PAD] = qn.astype(BF16)

        ckv = _rms(kva, gkva_ref[...])
        kv = _dot(ckv.astype(BF16), wkvb_ref[...])
        for hh, kvh in enumerate(_norm_kn(kv, gkn_ref[...])):
            kvn_ref[rs, hh * HEAD_PAD:(hh + 1) * HEAD_PAD] = kvh

        krn = kr2 * lax.rsqrt(_masked_mean_sq(kr2, lane < MLA_ROPE, MLA_ROPE) + EPS) * gkr_ref[...]
        krp = pltpu.roll(krn, MLA_NOPE, 1)
        if rope:
            krp = _rope(krp, c32_ref[rs, :], sa32_ref[rs, :], sb32_ref[rs, :], 8)
        else:
            ckv_ref[r, :, :] = ckv
            kro_ref[r, :, :] = krn.T[0:MLA_ROPE, :]
        krp_ref[rs, :] = krp.astype(BF16)


def _proj(x, mod, layer, cond_row, wts, rope_tabs, caches):
    m = x.shape[0]
    tm = ROW_TILE
    rope = rope_tabs is not None
    width = MLA_HEADS * HEAD_PAD

    def row(i):
        return (i, 0)

    def lay3(i):
        return (layer, 0, 0)

    in_specs = [
        pl.BlockSpec((tm, D_MODEL), row),
        pl.BlockSpec((None, 8, N_MOD * D_MODEL), lay3),
        pl.BlockSpec((None, 1, D_MODEL), lay3),
        pl.BlockSpec((None, IN_COLS_PAD, D_MODEL), lay3),
        pl.BlockSpec((None, 1, Q_RANK), lay3),
        pl.BlockSpec((None, Q_RANK, width), lay3),
        pl.BlockSpec((None, 1, KV_RANK), lay3),
        pl.BlockSpec((None, KV_RANK, width), lay3),
        pl.BlockSpec((None, 1, LANES), lay3),
        pl.BlockSpec((None, 1, LANES), lay3),
        pl.BlockSpec((None, 1, LANES), lay3),
    ]
    args = [x, mod, wts["g_mix"], wts["w_in_t"], wts["g_qa"], wts["w_qb"], wts["g_kva"],
            wts["w_kvb"], wts["g_q"], wts["g_kn"], wts["g_kr"]]
    out_shape = [
        jax.ShapeDtypeStruct((m, RET_WIDTH), BF16),
        jax.ShapeDtypeStruct((m, RET_WIDTH), BF16),
        jax.ShapeDtypeStruct((m, RET_WIDTH), BF16),
        jax.ShapeDtypeStruct((m, RET_WIDTH), F32),
        jax.ShapeDtypeStruct((m, width), BF16),
        jax.ShapeDtypeStruct((m, width), BF16),
        jax.ShapeDtypeStruct((m, LANES), BF16),
    ]
    out_specs = [
        pl.BlockSpec((tm, RET_WIDTH), row), pl.BlockSpec((tm, RET_WIDTH), row),
        pl.BlockSpec((tm, RET_WIDTH), row), pl.BlockSpec((tm, RET_WIDTH), row),
        pl.BlockSpec((tm, width), row), pl.BlockSpec((tm, width), row),
        pl.BlockSpec((tm, LANES), row),
    ]
    aliases = {}
    if rope:
        n_lat = rope_tabs[0].shape[0]
        tiles = n_lat // tm
        in_specs += [pl.BlockSpec((tm, LANES), lambda i: (i % tiles, 0))] * 6
        args += list(rope_tabs)
    else:
        seq = caches[0].shape[2]
        assert seq == SUB_TILE
        nb = tm // seq
        if _is_array(caches[0]):
            in_specs += [pl.BlockSpec(memory_space=pl.ANY)] * 2
            aliases = {len(args): len(out_shape), len(args) + 1: len(out_shape) + 1}
            args += list(caches)
        out_shape += [jax.ShapeDtypeStruct(c.shape, c.dtype) for c in caches]
        out_specs += [pl.BlockSpec((nb, None, seq, KV_RANK), lambda i: (i, layer, 0, 0)),
                      pl.BlockSpec((nb, None, MLA_ROPE, seq), lambda i: (i, layer, 0, 0))]

    return pl.pallas_call(
        functools.partial(_proj_kernel, rope=rope, cond_row=cond_row, n_alias=len(aliases)),
        grid=(m // tm,),
        in_specs=in_specs,
        out_specs=out_specs,
        out_shape=out_shape,
        input_output_aliases=aliases,
        compiler_params=pltpu.CompilerParams(
            dimension_semantics=("arbitrary",), vmem_limit_bytes=VMEM_LIMIT),
        name="proj_latent" if rope else "proj_context",
    )(*args)


def _log_gamma(p):
    return jnp.log1p(-jnp.exp2(-p))


def _decay_kernel(p_ref, dmask_ref, dvec_ref):
    c = MIX_CHUNK
    base = pl.program_id(0) * (2 * RET_HEADS)
    pair = pl.program_id(1)
    ri = lax.broadcasted_iota(jnp.int32, (c, c), 0)
    ci = lax.broadcasted_iota(jnp.int32, (c, c), 1)
    dif = (ri - ci).astype(F32)
    for e in range(2):
        h = 2 * pair + e
        lg_f = _log_gamma(jnp.full((c, c), p_ref[base + h], F32))
        lg_b = _log_gamma(jnp.full((c, c), p_ref[base + RET_HEADS + h], F32))
        fwd = jnp.where(dif >= 0, jnp.exp(jnp.maximum(dif, 0.0) * lg_f), 0.0)
        bwd = jnp.where(dif <= 0, jnp.exp(jnp.maximum(-dif, 0.0) * lg_b), 0.0)
        dmask_ref[e] = fwd + bwd
    lane = lax.broadcasted_iota(jnp.int32, (c, LANES), 1)
    rowf = lax.broadcasted_iota(jnp.int32, (c, LANES), 0).astype(F32)
    lo = lane < RET_DK
    lg_f = _log_gamma(jnp.where(lo, p_ref[base + 2 * pair], p_ref[base + 2 * pair + 1]))
    lg_b = _log_gamma(jnp.where(lo, p_ref[base + RET_HEADS + 2 * pair],
                                p_ref[base + RET_HEADS + 2 * pair + 1]))
    dvec_ref[Q_DEC_F] = jnp.exp((rowf + 1.0) * lg_f)
    dvec_ref[K_DEC_F] = jnp.exp((c - 1.0 - rowf) * lg_f)
    dvec_ref[Q_DEC_B] = jnp.exp((c - rowf) * lg_b)
    dvec_ref[K_DEC_B] = jnp.exp(rowf * lg_b)
    dvec_ref[C_DEC_F] = jnp.exp(c * lg_f)
    dvec_ref[C_DEC_B] = jnp.exp(c * lg_b)


def _decay_tables(decay_p):
    depth = decay_p.shape[0]
    c = MIX_CHUNK
    return pl.pallas_call(
        _decay_kernel,
        grid=(depth, N_PAIRS),
        in_specs=[pl.BlockSpec(memory_space=pltpu.SMEM)],
        out_specs=[
            pl.BlockSpec((None, 2, c, c), lambda l, p: (l, p, 0, 0)),
            pl.BlockSpec((None, None, N_DVEC, c, LANES), lambda l, p: (l, p, 0, 0, 0)),
        ],
        out_shape=[
            jax.ShapeDtypeStruct((depth, RET_HEADS, c, c), F32),
            jax.ShapeDtypeStruct((depth, N_PAIRS, N_DVEC, c, LANES), F32),
        ],
        compiler_params=pltpu.CompilerParams(
            dimension_semantics=("arbitrary", "arbitrary"), vmem_limit_bytes=VMEM_LIMIT),
        name="decay_tables",
    )(decay_p.reshape(-1))


def _mix_kernel(*refs, n_seq, latent, n_alias, n_cast):
    c = MIX_CHUNK
    nc = n_seq // c
    (rq_ref, rk_ref, rv_ref, rg_ref, qcat_ref, kvn_ref, krp_ref, gng_ref, gnb_ref,
     dmask_ref, dvec_ref) = refs[:11]
    if latent:
        kvc_ref, krc_ref, sf0_ref, sb0_ref, ret_ref, attn_ref, st_scr = refs[11:]
    else:
        n_in = 11 + n_alias
        cast_src = refs[n_in:n_in + n_cast]
        ret_ref, attn_ref, sf_ref, sb_ref = refs[n_in + n_cast:n_in + n_cast + 4]
        cast_dst = refs[n_in + n_cast + 4:]
        for src, dst in zip(cast_src, cast_dst):
            w = src.shape[1]
            dst[:, 0:w] = src[...].astype(BF16)
            if dst.shape[1] > w:
                dst[:, w:] = jnp.zeros((dst.shape[0], dst.shape[1] - w), BF16)

    lane = lax.broadcasted_iota(jnp.int32, (c, LANES), 1)
    lo = lane < RET_DK
    sq_r = lax.broadcasted_iota(jnp.int32, (LANES, LANES), 0)
    sq_c = lax.broadcasted_iota(jnp.int32, (LANES, LANES), 1)
    blockdiag = (sq_r < RET_DK) == (sq_c < RET_DK)

    def cols(j):
        return slice(j * LANES, (j + 1) * LANES)

    def state_update(pair, d, rows):
        kp = rk_ref[rows, cols(pair)]
        vp = rv_ref[rows, cols(pair)]
        kdt = (kp.astype(F32) * dvec_ref[pair, K_DEC_B if d else K_DEC_F]).T.astype(BF16)
        return jnp.where(blockdiag, _dot(kdt, vp), 0.0)

    for pair in range(N_PAIRS):
        for d in range(2):
            if latent:
                s = (sb0_ref if d else sf0_ref)[pair]
                cdec = dvec_ref[pair, C_DEC_B if d else C_DEC_F][0:LANES, :]
                order = list(range(nc - 1, -1, -1)) if d else list(range(nc))
                for idx, ch in enumerate(order):
                    st_scr[d, pair, ch] = s.astype(BF16)
                    if idx < nc - 1:
                        s = s * cdec + state_update(pair, d, pl.ds(ch * c, c))
            else:
                s = state_update(pair, d, pl.ds(0, c))
                st_ref = sb_ref if d else sf_ref
                st_ref[2 * pair] = s[0:RET_DK, 0:RET_DK]
                st_ref[2 * pair + 1] = s[RET_DK:, RET_DK:]

    nope_n = lax.broadcasted_iota(jnp.int32, (n_seq, LANES), 1) < MLA_NOPE
    if latent:
        nope_c = lax.broadcasted_iota(jnp.int32, (kvc_ref.shape[0], LANES), 1) < MLA_NOPE

    def chunk_body(ch, carry):
        rows = pl.ds(pl.multiple_of(ch * c, c), c)

        for pair in range(N_PAIRS):
            qp = rq_ref[rows, cols(pair)]
            kp = rk_ref[rows, cols(pair)]
            vp = rv_ref[rows, cols(pair)]
            zero = jnp.zeros_like(qp)
            a0 = (_dot_nt(jnp.where(lo, qp, zero), kp) * dmask_ref[2 * pair]).astype(BF16)
            a1 = (_dot_nt(jnp.where(lo, zero, qp), kp) * dmask_ref[2 * pair + 1]).astype(BF16)
            tot = jnp.where(lo, _dot(a0, vp), _dot(a1, vp))
            if latent:
                tot = (tot + _dot(qp, st_scr[0, pair, ch]) * dvec_ref[pair, Q_DEC_F]
                       + _dot(qp, st_scr[1, pair, ch]) * dvec_ref[pair, Q_DEC_B])
            inv = 1.0 / RET_DK
            m0 = jnp.sum(jnp.where(lo, tot, 0.0), axis=-1, keepdims=True) * inv
            m1 = jnp.sum(jnp.where(lo, 0.0, tot), axis=-1, keepdims=True) * inv
            y = tot - jnp.where(lo, m0, m1)
            v0 = jnp.sum(jnp.where(lo, y * y, 0.0), axis=-1, keepdims=True) * inv
            v1 = jnp.sum(jnp.where(lo, 0.0, y * y), axis=-1, keepdims=True) * inv
            yn = (y * lax.rsqrt(jnp.where(lo, v0, v1) + EPS) * gng_ref[:, cols(pair)]
                  + gnb_ref[:, cols(pair)])
            ret_ref[rows, cols(pair)] = (yn * _silu(rg_ref[rows, cols(pair)])).astype(BF16)

        krp = krp_ref[...]
        for pair in range(N_PAIRS):
            outs = []
            for e in range(2):
                h = 2 * pair + e
                qc = qcat_ref[rows, cols(h)]
                kv = kvn_ref[:, cols(h)]
                s = _dot_nt(qc, jnp.where(nope_n, kv, krp))
                m = jnp.max(s, axis=-1, keepdims=True)
                if latent:
                    kv_c = kvc_ref[:, cols(h)]
                    s2 = _dot_nt(qc, jnp.where(nope_c, kv_c, krc_ref[...]))
                    m = jnp.maximum(m, jnp.max(s2, axis=-1, keepdims=True))
                    p2 = jnp.exp2(s2 - m)
                p = jnp.exp2(s - m)
                den = jnp.sum(p, axis=-1, keepdims=True)
                acc = _dot(p.astype(BF16), kv)
                if latent:
                    den = den + jnp.sum(p2, axis=-1, keepdims=True)
                    acc = acc + _dot(p2.astype(BF16), kv_c)
                outs.append(acc / den)
            attn_ref[rows, cols(pair)] = jnp.where(
                lo, pltpu.roll(outs[0], MLA_V, 1), outs[1]).astype(BF16)
        return carry

    if nc == 1:
        chunk_body(0, 0)
    else:
        lax.fori_loop(0, nc, chunk_body, 0)


def _mix(proj_out, dmask, dvec, gn_g, gn_b, layer, n_batch, n_seq, latent_in, states_out,
         casts=()):
    rq, rk, rv, rg, qcat, kvn, krp = proj_out[:7]
    latent = latent_in is not None
    aliases = {}
    m = n_batch * n_seq
    c = MIX_CHUNK
    width = MLA_HEADS * HEAD_PAD

    def row(b):
        return (b, 0)

    def lay3(b):
        return (layer, 0, 0)

    once = pl.Buffered(1)
    in_specs = [
        pl.BlockSpec((n_seq, RET_WIDTH), row), pl.BlockSpec((n_seq, RET_WIDTH), row),
        pl.BlockSpec((n_seq, RET_WIDTH), row), pl.BlockSpec((n_seq, RET_WIDTH), row),
        pl.BlockSpec((n_seq, width), row), pl.BlockSpec((n_seq, width), row),
        pl.BlockSpec((n_seq, LANES), row),
        pl.BlockSpec((None, 1, RET_WIDTH), lay3), pl.BlockSpec((None, 1, RET_WIDTH), lay3),
        pl.BlockSpec((None, RET_HEADS, c, c), lambda b: (layer, 0, 0, 0), pipeline_mode=once),
        pl.BlockSpec((None, N_PAIRS, N_DVEC, c, LANES), lambda b: (layer, 0, 0, 0, 0),
                     pipeline_mode=once),
    ]
    args = [rq, rk, rv, rg, qcat, kvn, krp, gn_g, gn_b, dmask, dvec]
    out_shape = [jax.ShapeDtypeStruct((m, RET_WIDTH), BF16),
                 jax.ShapeDtypeStruct((m, MLA_HEADS * MLA_V), BF16)]
    out_specs = [pl.BlockSpec((n_seq, RET_WIDTH), row), pl.BlockSpec((n_seq, RET_WIDTH), row)]
    scratch = []
    if latent:
        kvn_c, krp_c, s_f0, s_b0 = latent_in
        past = kvn_c.shape[2]
        st_spec = pl.BlockSpec((None, None, N_PAIRS, LANES, LANES), lambda b: (b, layer, 0, 0, 0))
        in_specs += [
            pl.BlockSpec((None, None, past, width), lambda b: (layer, b, 0, 0)),
            pl.BlockSpec((None, None, past, LANES), lambda b: (b, layer, 0, 0)),
            st_spec, st_spec,
        ]
        args += [kvn_c, krp_c, s_f0, s_b0]
        scratch = [pltpu.VMEM((2, N_PAIRS, n_seq // c, LANES, LANES), BF16)]
    else:
        if _is_array(states_out[0]):
            in_specs += [pl.BlockSpec(memory_space=pl.ANY)] * 2
            aliases = {len(args): len(out_shape), len(args) + 1: len(out_shape) + 1}
            args += list(states_out)
        out_shape += [jax.ShapeDtypeStruct(s.shape, s.dtype) for s in states_out]
        st_spec = pl.BlockSpec((None, None, RET_HEADS, RET_DK, RET_DK),
                               lambda b: (b, layer, 0, 0, 0))
        out_specs += [st_spec, st_spec]
        for w, w_layer, out_cols, blocks in casts:
            rows = w.shape[1] // blocks
            stride = n_batch // blocks
            assert rows * blocks == w.shape[1] and rows % 16 == 0 and stride * blocks == n_batch
            in_specs.append(pl.BlockSpec(
                (None, rows, w.shape[2]),
                lambda b, w_layer=w_layer, stride=stride: (w_layer, b // stride, 0)))
            args.append(w)
            out_shape.append(jax.ShapeDtypeStruct((w.shape[1], out_cols), BF16))
            out_specs.append(pl.BlockSpec((rows, out_cols),
                                          lambda b, stride=stride: (b // stride, 0)))

    return pl.pallas_call(
        functools.partial(_mix_kernel, n_seq=n_seq, latent=latent, n_alias=len(aliases),
                          n_cast=len(casts)),
        grid=(n_batch,),
        in_specs=in_specs,
        out_specs=out_specs,
        out_shape=out_shape,
        input_output_aliases=aliases,
        scratch_shapes=scratch,
        compiler_params=pltpu.CompilerParams(
            dimension_semantics=("arbitrary",), vmem_limit_bytes=VMEM_LIMIT),
        name="mix_latent" if latent else "mix_context",
    )(*args)


def _out_kernel(x_ref, ret_ref, attn_ref, mod_ref, gffn_ref, wo_ref, wfi_ref, wfo_ref, o_ref,
                act_scr, *, cond_row):
    cond = cond_row(pl.program_id(0))
    gt1 = _mod_vec(mod_ref, cond, 2)
    sh2 = _mod_vec(mod_ref, cond, 3)
    sc2 = _mod_vec(mod_ref, cond, 4)
    gt2 = _mod_vec(mod_ref, cond, 5)
    mixed = _dot(ret_ref[...], wo_ref[0:RET_WIDTH, :]) + _dot(attn_ref[...], wo_ref[RET_WIDTH:, :])
    x1 = x_ref[...] + gt1 * mixed
    h = (_rms(x1, gffn_ref[...]) * (1.0 + sc2) + sh2).astype(BF16)
    for c0 in range(0, D_FF, FF_CHUNK):
        cw = min(FF_CHUNK, D_FF - c0)
        gate = _dot(h, wfi_ref[:, c0:c0 + cw])
        up = _dot(h, wfi_ref[:, D_FF + c0:D_FF + c0 + cw])
        act_scr[:, c0:c0 + cw] = (_silu(gate) * up).astype(BF16)
    o_ref[...] = x1 + gt2 * _dot(act_scr[...], wfo_ref[...])


def _out(x, ret, attn, mod, layer, cond_row, wts, big_w):
    m = x.shape[0]
    tm = ROW_TILE

    def row(i):
        return (i, 0)

    def lay3(i):
        return (layer, 0, 0)

    def whole(i):
        return (0, 0)

    once = pl.Buffered(1)
    return pl.pallas_call(
        functools.partial(_out_kernel, cond_row=cond_row),
        grid=(m // tm,),
        in_specs=[
            pl.BlockSpec((tm, D_MODEL), row),
            pl.BlockSpec((tm, RET_WIDTH), row),
            pl.BlockSpec((tm, RET_WIDTH), row),
            pl.BlockSpec((None, 8, N_MOD * D_MODEL), lay3),
            pl.BlockSpec((None, 1, D_MODEL), lay3),
            pl.BlockSpec((D_MODEL, D_MODEL), whole, pipeline_mode=once),
            pl.BlockSpec((D_MODEL, 2 * D_FF), whole, pipeline_mode=once),
            pl.BlockSpec((D_FF, D_MODEL), whole, pipeline_mode=once),
        ],
        out_specs=pl.BlockSpec((tm, D_MODEL), row),
        out_shape=jax.ShapeDtypeStruct((m, D_MODEL), F32),
        scratch_shapes=[pltpu.VMEM((tm, D_FF), BF16)],
        compiler_params=pltpu.CompilerParams(
            dimension_semantics=("arbitrary",), vmem_limit_bytes=VMEM_LIMIT),
        name="out_ffn",
    )(x, ret, attn, mod, wts["g_ffn"], *big_w)


def _rope_tables(n_lat):
    pos = np.arange(n_lat)
    row = (pos // GRID_W).astype(np.float32)[:, None]
    col = (pos % GRID_W).astype(np.float32)[:, None]
    lane = np.arange(LANES)[None, :]

    def tables(d, start, period):
        rel = (lane - start) % period
        active = np.logical_and(lane >= start, rel < d)
        half = d // 2
        nf = half // 2
        inv = np.float32(ROPE_BASE) ** (-((rel % nf).astype(np.float32)) / np.float32(nf))
        ang = (np.where(rel < half, row, col) * inv).astype(np.float32)
        cos, sin = np.cos(ang), np.sin(ang)
        first = (rel % half) < nf
        c = np.where(active, cos, 1.0)
        sa = np.where(np.logical_and(active, np.logical_not(first)), sin, 0.0)
        sb = np.where(np.logical_and(active, first), -sin, 0.0)
        return tuple(jnp.asarray(t, F32) for t in (c, sa, sb))

    return tables(RET_DK, 0, RET_DK) + tables(MLA_ROPE, MLA_NOPE, LANES)


def _prepare_weights(g_norm_mix, g_norm_ffn, w_in, g_q_a, w_q_b, g_kv_a, w_kv_b, g_qn, g_qr, g_kn,
                     g_kr):
    depth = w_in.shape[0]
    w_in_t = jnp.pad(jnp.swapaxes(w_in, 1, 2).astype(BF16),
                     ((0, 0), (0, IN_COLS_PAD - w_in.shape[2]), (0, 0)))
    w_qb = w_q_b.reshape(depth, Q_RANK, MLA_HEADS, MLA_NOPE + MLA_ROPE)
    w_qb = jnp.pad(w_qb, ((0, 0), (0, 0), (0, 0), (0, HEAD_PAD - MLA_NOPE - MLA_ROPE)))
    zeros32 = jnp.zeros((depth, MLA_ROPE), F32)
    return {
        "g_mix": g_norm_mix[:, None, :],
        "g_ffn": g_norm_ffn[:, None, :],
        "w_in_t": w_in_t,
        "g_qa": g_q_a[:, None, :],
        "w_qb": w_qb.reshape(depth, Q_RANK, MLA_HEADS * HEAD_PAD).astype(BF16),
        "g_kva": g_kv_a[:, None, :],
        "w_kvb": w_kv_b.astype(BF16),
        "g_q": (jnp.concatenate([g_qn, g_qr, zeros32], axis=-1) * Q_FOLD)[:, None, :],
        "g_kn": jnp.concatenate([g_kn, jnp.ones((depth, MLA_V), F32)], axis=-1)[:, None, :],
        "g_kr": jnp.pad(g_kr, ((0, 0), (0, LANES - MLA_ROPE)))[:, None, :],
    }


def _blockdiag_states(s):
    b, l = s.shape[:2]
    s = s.reshape(b, l, N_PAIRS, 2, RET_DK, RET_DK)
    z = jnp.zeros_like(s[:, :, :, 0])
    top = jnp.concatenate([s[:, :, :, 0], z], axis=-1)
    bot = jnp.concatenate([z, s[:, :, :, 1]], axis=-1)
    return jnp.concatenate([top, bot], axis=-2)


def kernel(x_prompt, x_sample, cache_ckv, cache_krope, state_ret_fwd, state_ret_bwd, c, c_ctx,
           w_mod, b_mod, g_norm_mix, g_norm_ffn, w_in, g_q_a, w_q_b, g_kv_a, w_kv_b,
           g_qn, g_qr, g_kn, g_kr, ret_p_fwd, ret_p_bwd, g_ret_gn, b_ret_gn, w_o,
           w_ffn_in, w_ffn_out):
    batch, seq, _ = x_prompt.shape
    dec_batch, dec_seq, _ = x_sample.shape
    depth = w_in.shape[0]

    wts = _prepare_weights(g_norm_mix, g_norm_ffn, w_in, g_q_a, w_q_b, g_kv_a, w_kv_b, g_qn, g_qr,
                           g_kn, g_kr)
    conds = jnp.concatenate([c_ctx[None], c, jnp.zeros((8 - 1 - dec_batch, D_MODEL), F32)], axis=0)
    mod = _modulation(conds, w_mod, b_mod)
    dmask, dvec = _decay_tables(jnp.stack([ret_p_fwd, ret_p_bwd], axis=1))
    gn_g = g_ret_gn[:, None, :]
    gn_b = b_ret_gn[:, None, :]

    rope_tabs = _rope_tables(dec_seq)
    kvn_cache = _cache_up(cache_ckv, wts["w_kvb"], wts["g_kn"])
    krp_cache = jnp.pad(cache_krope, ((0, 0), (0, 0), (0, 0), (MLA_NOPE, LANES - MLA_NOPE - MLA_ROPE)))
    krp_cache = krp_cache.astype(BF16)
    s_f0 = _blockdiag_states(state_ret_fwd)
    s_b0 = _blockdiag_states(state_ret_bwd)

    lat_tiles = dec_seq // ROW_TILE

    def ctx_row(i):
        return 0

    def lat_row(i):
        return 1 + i // lat_tiles

    x = x_prompt.reshape(batch * seq, D_MODEL)
    y = x_sample.reshape(dec_batch * dec_seq, D_MODEL)
    caches = (jax.ShapeDtypeStruct((batch, depth, seq, KV_RANK), F32),
              jax.ShapeDtypeStruct((batch, depth, MLA_ROPE, seq), F32))
    states = (jax.ShapeDtypeStruct((batch, depth, RET_HEADS, RET_DK, RET_DK), F32),) * 2
    for l in range(depth):
        pr = _proj(x, mod, l, ctx_row, wts, None, caches)
        caches = tuple(pr[7:9])
        casts = [(w_o, l, D_MODEL, batch), (w_ffn_in, l, 2 * D_FF, batch),
                 (w_ffn_out, l, D_MODEL, batch // 2)]
        mixed = _mix(pr, dmask, dvec, gn_g, gn_b, l, batch, seq, None, states, casts)
        ret, attn = mixed[:2]
        states = tuple(mixed[2:4])
        big_w = tuple(mixed[4:7])
        x = _out(x, ret, attn, mod, l, ctx_row, wts, big_w)

        pr = _proj(y, mod, l, lat_row, wts, rope_tabs, None)
        ret, attn = _mix(pr, dmask, dvec, gn_g, gn_b, l, dec_batch, dec_seq,
                         (kvn_cache, krp_cache, s_f0, s_b0), None)
        y = _out(y, ret, attn, mod, l, lat_row, wts, big_w)

    return (x.reshape(batch, seq, D_MODEL), y.reshape(dec_batch, dec_seq, D_MODEL),
            caches[0], jnp.swapaxes(caches[1], 2, 3), states[0], states[1])
```

```python
import functools

import jax
import jax.numpy as jnp
import numpy as np
from jax import lax
from jax.experimental import pallas as pl
from jax.experimental.pallas import tpu as pltpu

D_MODEL = 1024
N_MOD = 6
RET_HEADS = 8
RET_DK = 64
RET_WIDTH = 512
MLA_HEADS = 8
MLA_NOPE = 64
MLA_ROPE = 32
MLA_V = 64
Q_RANK = 256
KV_RANK = 128
D_FF = 2816
GRID_W = 64
ROPE_BASE = 10000.0
EPS = 1e-6

LANES = 128
HEAD_PAD = LANES
N_PAIRS = RET_HEADS // 2
IN_COLS_PAD = 4 * RET_WIDTH + Q_RANK + KV_RANK + LANES
ROW_TILE = 512
SUB_TILE = 256
FF_CHUNK = 256
MIX_CHUNK = 256
MIX_SEQS = 4
Q_DEC_F, K_DEC_F, Q_DEC_B, K_DEC_B, C_DEC_F, C_DEC_B = range(6)
N_DVEC = 6
Q_FOLD = (MLA_NOPE + MLA_ROPE) ** -0.5 * 1.4426950408889634
VMEM_LIMIT = 56 * 1024 * 1024

BF16 = jnp.bfloat16
F32 = jnp.float32
_NT = (((1,), (1,)), ((), ()))


def _dot(a, b):
    return jnp.dot(a, b, preferred_element_type=F32)


def _dot_nt(a, b):
    return lax.dot_general(a, b, _NT, preferred_element_type=F32)


def _rms(x, g):
    return x * lax.rsqrt(jnp.mean(x * x, axis=-1, keepdims=True) + EPS) * g


def _silu(x):
    return x * jax.nn.sigmoid(x)


def _masked_mean_sq(x, mask, n):
    return jnp.sum(jnp.where(mask, x * x, 0.0), axis=-1, keepdims=True) * (1.0 / n)


def _rope(x, c, sa, sb, shift):
    return x * c + pltpu.roll(x, shift, 1) * sa + pltpu.roll(x, LANES - shift, 1) * sb


def _mod_kernel(c_ref, w_ref, b_ref, o_ref):
    a = _silu(c_ref[...])
    w = w_ref[...]
    a_hi = a.astype(BF16)
    a_lo = (a - a_hi.astype(F32)).astype(BF16)
    w_hi = w.astype(BF16)
    w_lo = (w - w_hi.astype(F32)).astype(BF16)
    both = _dot(jnp.concatenate([a_hi, a_lo], axis=0), w_hi)
    o_ref[...] = both[0:8] + both[8:16] + _dot(a_hi, w_lo) + b_ref[...]


def _modulation(conds, w_mod, b_mod):
    depth, _, n = w_mod.shape
    tn = 1536
    return pl.pallas_call(
        _mod_kernel,
        grid=(depth, n // tn),
        in_specs=[
            pl.BlockSpec((8, D_MODEL), lambda l, j: (0, 0)),
            pl.BlockSpec((None, D_MODEL, tn), lambda l, j: (l, 0, j)),
            pl.BlockSpec((None, 1, tn), lambda l, j: (l, 0, j)),
        ],
        out_specs=pl.BlockSpec((None, 8, tn), lambda l, j: (l, 0, j)),
        out_shape=jax.ShapeDtypeStruct((depth, 8, n), F32),
        compiler_params=pltpu.CompilerParams(
            dimension_semantics=("arbitrary", "arbitrary"), vmem_limit_bytes=VMEM_LIMIT),
        name="modulation",
    )(conds, w_mod, b_mod.reshape(depth, 1, n))


def _norm_kn(kv, gkn):
    lane = lax.broadcasted_iota(jnp.int32, (kv.shape[0], LANES), 1)
    lo = lane < MLA_NOPE
    out = []
    for h in range(MLA_HEADS):
        kvh = kv[:, h * HEAD_PAD:(h + 1) * HEAD_PAD]
        rs = lax.rsqrt(_masked_mean_sq(kvh, lo, MLA_NOPE) + EPS)
        out.append((kvh * jnp.where(lo, rs * gkn, 1.0)).astype(BF16))
    return out


def _cache_kernel(ckv_ref, wkvb_ref, gkn_ref, kvn_ref):
    kv = _dot(ckv_ref[...].astype(BF16), wkvb_ref[...])
    for h, kvh in enumerate(_norm_kn(kv, gkn_ref[...])):
        kvn_ref[:, h * HEAD_PAD:(h + 1) * HEAD_PAD] = kvh


def _cache_up(cache_ckv, wkvb, gkn):
    nb, depth, past, _ = cache_ckv.shape
    width = MLA_HEADS * HEAD_PAD
    return pl.pallas_call(
        _cache_kernel,
        grid=(depth, nb),
        in_specs=[
            pl.BlockSpec((None, None, past, KV_RANK), lambda l, b: (b, l, 0, 0)),
            pl.BlockSpec((None, KV_RANK, width), lambda l, b: (l, 0, 0)),
            pl.BlockSpec((None, 1, LANES), lambda l, b: (l, 0, 0)),
        ],
        out_specs=pl.BlockSpec((None, None, past, width), lambda l, b: (l, b, 0, 0)),
        out_shape=jax.ShapeDtypeStruct((depth, nb, past, width), BF16),
        compiler_params=pltpu.CompilerParams(
            dimension_semantics=("arbitrary", "arbitrary"), vmem_limit_bytes=VMEM_LIMIT),
        name="cache_up",
    )(cache_ckv, wkvb, gkn)


def _mod_vec(mod_ref, cond, k):
    return mod_ref[pl.ds(cond, 1), k * D_MODEL:(k + 1) * D_MODEL]


def _is_array(x):
    return not isinstance(x, jax.ShapeDtypeStruct)


def _proj_kernel(*refs, rope, cond_row, n_alias):
    (x_ref, mod_ref, gmix_ref, win_ref, gqa_ref, wqb_ref, gkva_ref, wkvb_ref,
     gq_ref, gkn_ref, gkr_ref) = refs[:11]
    if rope:
        c64_ref, sa64_ref, sb64_ref, c32_ref, sa32_ref, sb32_ref = refs[11:17]
        rq_ref, rk_ref, rv_ref, rg_ref, qcat_ref, kvn_ref, krp_ref = refs[17:]
    else:
        (rq_ref, rk_ref, rv_ref, rg_ref, qcat_ref, kvn_ref, krp_ref, ckv_ref,
         kro_ref) = refs[11 + n_alias:]

    cond = cond_row(pl.program_id(0))
    sh1 = _mod_vec(mod_ref, cond, 0)
    sc1 = _mod_vec(mod_ref, cond, 1)
    w = RET_WIDTH
    lane = lax.broadcasted_iota(jnp.int32, (SUB_TILE, LANES), 1)
    nope = lane < MLA_NOPE
    is_rope = jnp.logical_and(lane >= MLA_NOPE, lane < MLA_NOPE + MLA_ROPE)

    for r in range(x_ref.shape[0] // SUB_TILE):
        rs = slice(r * SUB_TILE, (r + 1) * SUB_TILE)
        h = (_rms(x_ref[rs, :], gmix_ref[...]) * (1.0 + sc1) + sh1).astype(BF16)
        z = _dot_nt(h, win_ref[...])

        for j in range(w // LANES):
            sl = slice(j * LANES, (j + 1) * LANES)
            q = z[:, j * LANES:(j + 1) * LANES]
            k = z[:, w + j * LANES:w + (j + 1) * LANES] * (RET_DK ** -0.5)
            if rope:
                q = _rope(q, c64_ref[rs, :], sa64_ref[rs, :], sb64_ref[rs, :], 16)
                k = _rope(k, c64_ref[rs, :], sa64_ref[rs, :], sb64_ref[rs, :], 16)
            rq_ref[rs, sl] = q.astype(BF16)
            rk_ref[rs, sl] = k.astype(BF16)
        rv_ref[rs, :] = z[:, 2 * w:3 * w].astype(BF16)
        rg_ref[rs, :] = z[:, 3 * w:4 * w]

        o = 4 * w
        qa = z[:, o:o + Q_RANK]
        kva = z[:, o + Q_RANK:o + Q_RANK + KV_RANK]
        kr2 = z[:, o + Q_RANK + KV_RANK:]

        q = _dot(_rms(qa, gqa_ref[...]).astype(BF16), wqb_ref[...])
        for hh in range(MLA_HEADS):
            qh = q[:, hh * HEAD_PAD:(hh + 1) * HEAD_PAD]
            rs_n = lax.rsqrt(_masked_mean_sq(qh, nope, MLA_NOPE) + EPS)
            rs_r = lax.rsqrt(_masked_mean_sq(qh, is_rope, MLA_ROPE) + EPS)
            qn = qh * jnp.where(nope, rs_n, rs_r) * gq_ref[...]
            if rope:
                qn = _rope(qn, c32_ref[rs, :], sa32_ref[rs, :], sb32_ref[rs, :], 8)
            qcat_ref[rs, hh * HEAD_PAD:(hh + 1) * HEAD_PAD] = qn.astype(BF16)

        ckv = _rms(kva, gkva_ref[...])
        kv = _dot(ckv.astype(BF16), wkvb_ref[...])
        for hh, kvh in enumerate(_norm_kn(kv, gkn_ref[...])):
            kvn_ref[rs, hh * HEAD_PAD:(hh + 1) * HEAD_PAD] = kvh

        krn = kr2 * lax.rsqrt(_masked_mean_sq(kr2, lane < MLA_ROPE, MLA_ROPE) + EPS) * gkr_ref[...]
        krp = pltpu.roll(krn, MLA_NOPE, 1)
        if rope:
            krp = _rope(krp, c32_ref[rs, :], sa32_ref[rs, :], sb32_ref[rs, :], 8)
        else:
            ckv_ref[r, :, :] = ckv
            kro_ref[r, :, :] = krn.T[0:MLA_ROPE, :]
        krp_ref[rs, :] = krp.astype(BF16)


def _proj(x, mod, layer, cond_row, wts, rope_tabs, caches):
    m = x.shape[0]
    tm = ROW_TILE
    rope = rope_tabs is not None
    width = MLA_HEADS * HEAD_PAD

    def row(i):
        return (i, 0)

    def lay3(i):
        return (layer, 0, 0)

    in_specs = [
        pl.BlockSpec((tm, D_MODEL), row),
        pl.BlockSpec((None, 8, N_MOD * D_MODEL), lay3),
        pl.BlockSpec((None, 1, D_MODEL), lay3),
        pl.BlockSpec((None, IN_COLS_PAD, D_MODEL), lay3),
        pl.BlockSpec((None, 1, Q_RANK), lay3),
        pl.BlockSpec((None, Q_RANK, width), lay3),
        pl.BlockSpec((None, 1, KV_RANK), lay3),
        pl.BlockSpec((None, KV_RANK, width), lay3),
        pl.BlockSpec((None, 1, LANES), lay3),
        pl.BlockSpec((None, 1, LANES), lay3),
        pl.BlockSpec((None, 1, LANES), lay3),
    ]
    args = [x, mod, wts["g_mix"], wts["w_in_t"], wts["g_qa"], wts["w_qb"], wts["g_kva"],
            wts["w_kvb"], wts["g_q"], wts["g_kn"], wts["g_kr"]]
    out_shape = [
        jax.ShapeDtypeStruct((m, RET_WIDTH), BF16),
        jax.ShapeDtypeStruct((m, RET_WIDTH), BF16),
        jax.ShapeDtypeStruct((m, RET_WIDTH), BF16),
        jax.ShapeDtypeStruct((m, RET_WIDTH), F32),
        jax.ShapeDtypeStruct((m, width), BF16),
        jax.ShapeDtypeStruct((m, width), BF16),
        jax.ShapeDtypeStruct((m, LANES), BF16),
    ]
    out_specs = [
        pl.BlockSpec((tm, RET_WIDTH), row), pl.BlockSpec((tm, RET_WIDTH), row),
        pl.BlockSpec((tm, RET_WIDTH), row), pl.BlockSpec((tm, RET_WIDTH), row),
        pl.BlockSpec((tm, width), row), pl.BlockSpec((tm, width), row),
        pl.BlockSpec((tm, LANES), row),
    ]
    aliases = {}
    if rope:
        n_lat = rope_tabs[0].shape[0]
        tiles = n_lat // tm
        in_specs += [pl.BlockSpec((tm, LANES), lambda i: (i % tiles, 0))] * 6
        args += list(rope_tabs)
    else:
        seq = caches[0].shape[2]
        assert seq == SUB_TILE
        nb = tm // seq
        if _is_array(caches[0]):
            in_specs += [pl.BlockSpec(memory_space=pl.ANY)] * 2
            aliases = {len(args): len(out_shape), len(args) + 1: len(out_shape) + 1}
            args += list(caches)
        out_shape += [jax.ShapeDtypeStruct(c.shape, c.dtype) for c in caches]
        out_specs += [pl.BlockSpec((nb, None, seq, KV_RANK), lambda i: (i, layer, 0, 0)),
                      pl.BlockSpec((nb, None, MLA_ROPE, seq), lambda i: (i, layer, 0, 0))]

    return pl.pallas_call(
        functools.partial(_proj_kernel, rope=rope, cond_row=cond_row, n_alias=len(aliases)),
        grid=(m // tm,),
        in_specs=in_specs,
        out_specs=out_specs,
        out_shape=out_shape,
        input_output_aliases=aliases,
        compiler_params=pltpu.CompilerParams(
            dimension_semantics=("arbitrary",), vmem_limit_bytes=VMEM_LIMIT),
        name="proj_latent" if rope else "proj_context",
    )(*args)


def _log_gamma(p):
    return jnp.log1p(-jnp.exp2(-p))


def _decay_kernel(p_ref, dmask_ref, dvec_ref):
    c = MIX_CHUNK
    base = pl.program_id(0) * (2 * RET_HEADS)
    pair = pl.program_id(1)
    ri = lax.broadcasted_iota(jnp.int32, (c, c), 0)
    ci = lax.broadcasted_iota(jnp.int32, (c, c), 1)
    dif = (ri - ci).astype(F32)
    for e in range(2):
        h = 2 * pair + e
        lg_f = _log_gamma(jnp.full((c, c), p_ref[base + h], F32))
        lg_b = _log_gamma(jnp.full((c, c), p_ref[base + RET_HEADS + h], F32))
        fwd = jnp.where(dif >= 0, jnp.exp(jnp.maximum(dif, 0.0) * lg_f), 0.0)
        bwd = jnp.where(dif <= 0, jnp.exp(jnp.maximum(-dif, 0.0) * lg_b), 0.0)
        dmask_ref[e] = fwd + bwd
    lane = lax.broadcasted_iota(jnp.int32, (c, LANES), 1)
    rowf = lax.broadcasted_iota(jnp.int32, (c, LANES), 0).astype(F32)
    lo = lane < RET_DK
    lg_f = _log_gamma(jnp.where(lo, p_ref[base + 2 * pair], p_ref[base + 2 * pair + 1]))
    lg_b = _log_gamma(jnp.where(lo, p_ref[base + RET_HEADS + 2 * pair],
                                p_ref[base + RET_HEADS + 2 * pair + 1]))
    dvec_ref[Q_DEC_F] = jnp.exp((rowf + 1.0) * lg_f)
    dvec_ref[K_DEC_F] = jnp.exp((c - 1.0 - rowf) * lg_f)
    dvec_ref[Q_DEC_B] = jnp.exp((c - rowf) * lg_b)
    dvec_ref[K_DEC_B] = jnp.exp(rowf * lg_b)
    dvec_ref[C_DEC_F] = jnp.exp(c * lg_f)
    dvec_ref[C_DEC_B] = jnp.exp(c * lg_b)


def _decay_tables(decay_p):
    depth = decay_p.shape[0]
    c = MIX_CHUNK
    return pl.pallas_call(
        _decay_kernel,
        grid=(depth, N_PAIRS),
        in_specs=[pl.BlockSpec(memory_space=pltpu.SMEM)],
        out_specs=[
            pl.BlockSpec((None, 2, c, c), lambda l, p: (l, p, 0, 0)),
            pl.BlockSpec((None, None, N_DVEC, c, LANES), lambda l, p: (l, p, 0, 0, 0)),
        ],
        out_shape=[
            jax.ShapeDtypeStruct((depth, RET_HEADS, c, c), F32),
            jax.ShapeDtypeStruct((depth, N_PAIRS, N_DVEC, c, LANES), F32),
        ],
        compiler_params=pltpu.CompilerParams(
            dimension_semantics=("arbitrary", "arbitrary"), vmem_limit_bytes=VMEM_LIMIT),
        name="decay_tables",
    )(decay_p.reshape(-1))


def _mix_kernel(*refs, n_seq, n_sub, latent, n_alias, n_cast):
    c = MIX_CHUNK
    nc = n_seq // c
    (rq_ref, rk_ref, rv_ref, rg_ref, qcat_ref, kvn_ref, krp_ref, gng_ref, gnb_ref,
     dmask_ref, dvec_ref) = refs[:11]
    if latent:
        kvc_ref, krc_ref, sf0_ref, sb0_ref, ret_ref, attn_ref, st_scr = refs[11:]
    else:
        n_in = 11 + n_alias
        cast_src = refs[n_in:n_in + n_cast]
        ret_ref, attn_ref, sf_ref, sb_ref = refs[n_in + n_cast:n_in + n_cast + 4]
        cast_dst = refs[n_in + n_cast + 4:]
        for src, dst in zip(cast_src, cast_dst):
            w = src.shape[1]
            dst[:, 0:w] = src[...].astype(BF16)
            if dst.shape[1] > w:
                dst[:, w:] = jnp.zeros((dst.shape[0], dst.shape[1] - w), BF16)

    lane = lax.broadcasted_iota(jnp.int32, (c, LANES), 1)
    lo = lane < RET_DK
    sq_r = lax.broadcasted_iota(jnp.int32, (LANES, LANES), 0)
    sq_c = lax.broadcasted_iota(jnp.int32, (LANES, LANES), 1)
    blockdiag = (sq_r < RET_DK) == (sq_c < RET_DK)

    def cols(j):
        return slice(j * LANES, (j + 1) * LANES)

    def state_update(pair, d, rows):
        kp = rk_ref[rows, cols(pair)]
        vp = rv_ref[rows, cols(pair)]
        kdt = (kp.astype(F32) * dvec_ref[pair, K_DEC_B if d else K_DEC_F]).T.astype(BF16)
        return jnp.where(blockdiag, _dot(kdt, vp), 0.0)

    nope_n = lax.broadcasted_iota(jnp.int32, (n_seq, LANES), 1) < MLA_NOPE
    if latent:
        nope_c = lax.broadcasted_iota(jnp.int32, (kvc_ref.shape[0], LANES), 1) < MLA_NOPE

    def states(sub):
        base = sub * n_seq
        for pair in range(N_PAIRS):
            for d in range(2):
                if latent:
                    s = (sb0_ref if d else sf0_ref)[pair]
                    cdec = dvec_ref[pair, C_DEC_B if d else C_DEC_F][0:LANES, :]
                    order = list(range(nc - 1, -1, -1)) if d else list(range(nc))
                    for idx, ch in enumerate(order):
                        st_scr[d, pair, ch] = s.astype(BF16)
                        if idx < nc - 1:
                            s = s * cdec + state_update(pair, d, pl.ds(base + ch * c, c))
                else:
                    s = state_update(pair, d, pl.ds(base, c))
                    st_ref = sb_ref if d else sf_ref
                    st_ref[sub, 2 * pair] = s[0:RET_DK, 0:RET_DK]
                    st_ref[sub, 2 * pair + 1] = s[RET_DK:, RET_DK:]

    def chunk_body(ch, base):
        rows = pl.ds(pl.multiple_of(base + ch * c, c), c)
        keys = pl.ds(base, n_seq)

        for pair in range(N_PAIRS):
            qp = rq_ref[rows, cols(pair)]
            kp = rk_ref[rows, cols(pair)]
            vp = rv_ref[rows, cols(pair)]
            zero = jnp.zeros_like(qp)
            a0 = (_dot_nt(jnp.where(lo, qp, zero), kp) * dmask_ref[2 * pair]).astype(BF16)
            a1 = (_dot_nt(jnp.where(lo, zero, qp), kp) * dmask_ref[2 * pair + 1]).astype(BF16)
            tot = jnp.where(lo, _dot(a0, vp), _dot(a1, vp))
            if latent:
                tot = (tot + _dot(qp, st_scr[0, pair, ch]) * dvec_ref[pair, Q_DEC_F]
                       + _dot(qp, st_scr[1, pair, ch]) * dvec_ref[pair, Q_DEC_B])
            inv = 1.0 / RET_DK
            m0 = jnp.sum(jnp.where(lo, tot, 0.0), axis=-1, keepdims=True) * inv
            m1 = jnp.sum(jnp.where(lo, 0.0, tot), axis=-1, keepdims=True) * inv
            y = tot - jnp.where(lo, m0, m1)
            v0 = jnp.sum(jnp.where(lo, y * y, 0.0), axis=-1, keepdims=True) * inv
            v1 = jnp.sum(jnp.where(lo, 0.0, y * y), axis=-1, keepdims=True) * inv
            yn = (y * lax.rsqrt(jnp.where(lo, v0, v1) + EPS) * gng_ref[:, cols(pair)]
                  + gnb_ref[:, cols(pair)])
            ret_ref[rows, cols(pair)] = (yn * _silu(rg_ref[rows, cols(pair)])).astype(BF16)

        krp = krp_ref[keys, :]
        for pair in range(N_PAIRS):
            outs = []
            for e in range(2):
                h = 2 * pair + e
                qc = qcat_ref[rows, cols(h)]
                kv = kvn_ref[keys, cols(h)]
                s = _dot_nt(qc, jnp.where(nope_n, kv, krp))
                m = jnp.max(s, axis=-1, keepdims=True)
                if latent:
                    kv_c = kvc_ref[:, cols(h)]
                    s2 = _dot_nt(qc, jnp.where(nope_c, kv_c, krc_ref[...]))
                    m = jnp.maximum(m, jnp.max(s2, axis=-1, keepdims=True))
                    p2 = jnp.exp2(s2 - m)
                p = jnp.exp2(s - m)
                den = jnp.sum(p, axis=-1, keepdims=True)
                acc = _dot(p.astype(BF16), kv)
                if latent:
                    den = den + jnp.sum(p2, axis=-1, keepdims=True)
                    acc = acc + _dot(p2.astype(BF16), kv_c)
                outs.append(acc / den)
            attn_ref[rows, cols(pair)] = jnp.where(
                lo, pltpu.roll(outs[0], MLA_V, 1), outs[1]).astype(BF16)
        return base

    for sub in range(n_sub):
        states(sub)
    for sub in range(n_sub):
        if nc == 1:
            chunk_body(0, sub * n_seq)
        else:
            lax.fori_loop(0, nc, chunk_body, sub * n_seq)


def _mix(proj_out, dmask, dvec, gn_g, gn_b, layer, n_batch, n_seq, latent_in, states_out,
         casts=()):
    rq, rk, rv, rg, qcat, kvn, krp = proj_out[:7]
    latent = latent_in is not None
    aliases = {}
    m = n_batch * n_seq
    c = MIX_CHUNK
    width = MLA_HEADS * HEAD_PAD

    def row(b):
        return (b, 0)

    def lay3(b):
        return (layer, 0, 0)

    once = pl.Buffered(1)
    n_sub = 1 if latent else MIX_SEQS
    n_steps = n_batch // n_sub
    blk = n_sub * n_seq
    in_specs = [
        pl.BlockSpec((blk, RET_WIDTH), row), pl.BlockSpec((blk, RET_WIDTH), row),
        pl.BlockSpec((blk, RET_WIDTH), row), pl.BlockSpec((blk, RET_WIDTH), row),
        pl.BlockSpec((blk, width), row), pl.BlockSpec((blk, width), row),
        pl.BlockSpec((blk, LANES), row),
        pl.BlockSpec((None, 1, RET_WIDTH), lay3), pl.BlockSpec((None, 1, RET_WIDTH), lay3),
        pl.BlockSpec((None, RET_HEADS, c, c), lambda b: (layer, 0, 0, 0), pipeline_mode=once),
        pl.BlockSpec((None, N_PAIRS, N_DVEC, c, LANES), lambda b: (layer, 0, 0, 0, 0),
                     pipeline_mode=once),
    ]
    args = [rq, rk, rv, rg, qcat, kvn, krp, gn_g, gn_b, dmask, dvec]
    out_shape = [jax.ShapeDtypeStruct((m, RET_WIDTH), BF16),
                 jax.ShapeDtypeStruct((m, MLA_HEADS * MLA_V), BF16)]
    out_specs = [pl.BlockSpec((blk, RET_WIDTH), row), pl.BlockSpec((blk, RET_WIDTH), row)]
    scratch = []
    if latent:
        kvn_c, krp_c, s_f0, s_b0 = latent_in
        past = kvn_c.shape[2]
        st_spec = pl.BlockSpec((None, None, N_PAIRS, LANES, LANES), lambda b: (b, layer, 0, 0, 0))
        in_specs += [
            pl.BlockSpec((None, None, past, width), lambda b: (layer, b, 0, 0)),
            pl.BlockSpec((None, None, past, LANES), lambda b: (b, layer, 0, 0)),
            st_spec, st_spec,
        ]
        args += [kvn_c, krp_c, s_f0, s_b0]
        scratch = [pltpu.VMEM((2, N_PAIRS, n_seq // c, LANES, LANES), BF16)]
    else:
        if _is_array(states_out[0]):
            in_specs += [pl.BlockSpec(memory_space=pl.ANY)] * 2
            aliases = {len(args): len(out_shape), len(args) + 1: len(out_shape) + 1}
            args += list(states_out)
        out_shape += [jax.ShapeDtypeStruct(s.shape, s.dtype) for s in states_out]
        st_spec = pl.BlockSpec((n_sub, None, RET_HEADS, RET_DK, RET_DK),
                               lambda b: (b, layer, 0, 0, 0))
        out_specs += [st_spec, st_spec]
        for w, w_layer, out_cols, stride in casts:
            blocks = n_steps // stride
            rows = w.shape[1] // blocks
            assert rows * blocks == w.shape[1] and rows % 16 == 0 and stride * blocks == n_steps
            in_specs.append(pl.BlockSpec(
                (None, rows, w.shape[2]),
                lambda b, w_layer=w_layer, stride=stride: (w_layer, b // stride, 0)))
            args.append(w)
            out_shape.append(jax.ShapeDtypeStruct((w.shape[1], out_cols), BF16))
            out_specs.append(pl.BlockSpec((rows, out_cols),
                                          lambda b, stride=stride: (b // stride, 0)))

    return pl.pallas_call(
        functools.partial(_mix_kernel, n_seq=n_seq, n_sub=n_sub, latent=latent,
                          n_alias=len(aliases), n_cast=len(casts)),
        grid=(n_steps,),
        in_specs=in_specs,
        out_specs=out_specs,
        out_shape=out_shape,
        input_output_aliases=aliases,
        scratch_shapes=scratch,
        compiler_params=pltpu.CompilerParams(
            dimension_semantics=("arbitrary",), vmem_limit_bytes=VMEM_LIMIT),
        name="mix_latent" if latent else "mix_context",
    )(*args)


def _out_kernel(x_ref, ret_ref, attn_ref, mod_ref, gffn_ref, wo_ref, wfi_ref, wfo_ref, o_ref,
                act_scr, *, cond_row):
    cond = cond_row(pl.program_id(0))
    gt1 = _mod_vec(mod_ref, cond, 2)
    sh2 = _mod_vec(mod_ref, cond, 3)
    sc2 = _mod_vec(mod_ref, cond, 4)
    gt2 = _mod_vec(mod_ref, cond, 5)
    mixed = _dot(ret_ref[...], wo_ref[0:RET_WIDTH, :]) + _dot(attn_ref[...], wo_ref[RET_WIDTH:, :])
    x1 = x_ref[...] + gt1 * mixed
    h = (_rms(x1, gffn_ref[...]) * (1.0 + sc2) + sh2).astype(BF16)
    for c0 in range(0, D_FF, FF_CHUNK):
        cw = min(FF_CHUNK, D_FF - c0)
        gate = _dot(h, wfi_ref[:, c0:c0 + cw])
        up = _dot(h, wfi_ref[:, D_FF + c0:D_FF + c0 + cw])
        act_scr[:, c0:c0 + cw] = (_silu(gate) * up).astype(BF16)
    o_ref[...] = x1 + gt2 * _dot(act_scr[...], wfo_ref[...])


def _out(x, ret, attn, mod, layer, cond_row, wts, big_w):
    m = x.shape[0]
    tm = ROW_TILE

    def row(i):
        return (i, 0)

    def lay3(i):
        return (layer, 0, 0)

    def whole(i):
        return (0, 0)

    once = pl.Buffered(1)
    return pl.pallas_call(
        functools.partial(_out_kernel, cond_row=cond_row),
        grid=(m // tm,),
        in_specs=[
            pl.BlockSpec((tm, D_MODEL), row),
            pl.BlockSpec((tm, RET_WIDTH), row),
            pl.BlockSpec((tm, RET_WIDTH), row),
            pl.BlockSpec((None, 8, N_MOD * D_MODEL), lay3),
            pl.BlockSpec((None, 1, D_MODEL), lay3),
            pl.BlockSpec((D_MODEL, D_MODEL), whole, pipeline_mode=once),
            pl.BlockSpec((D_MODEL, 2 * D_FF), whole, pipeline_mode=once),
            pl.BlockSpec((D_FF, D_MODEL), whole, pipeline_mode=once),
        ],
        out_specs=pl.BlockSpec((tm, D_MODEL), row),
        out_shape=jax.ShapeDtypeStruct((m, D_MODEL), F32),
        scratch_shapes=[pltpu.VMEM((tm, D_FF), BF16)],
        compiler_params=pltpu.CompilerParams(
            dimension_semantics=("arbitrary",), vmem_limit_bytes=VMEM_LIMIT),
        name="out_ffn",
    )(x, ret, attn, mod, wts["g_ffn"], *big_w)


def _rope_tables(n_lat):
    pos = np.arange(n_lat)
    row = (pos // GRID_W).astype(np.float32)[:, None]
    col = (pos % GRID_W).astype(np.float32)[:, None]
    lane = np.arange(LANES)[None, :]

    def tables(d, start, period):
        rel = (lane - start) % period
        active = np.logical_and(lane >= start, rel < d)
        half = d // 2
        nf = half // 2
        inv = np.float32(ROPE_BASE) ** (-((rel % nf).astype(np.float32)) / np.float32(nf))
        ang = (np.where(rel < half, row, col) * inv).astype(np.float32)
        cos, sin = np.cos(ang), np.sin(ang)
        first = (rel % half) < nf
        c = np.where(active, cos, 1.0)
        sa = np.where(np.logical_and(active, np.logical_not(first)), sin, 0.0)
        sb = np.where(np.logical_and(active, first), -sin, 0.0)
        return tuple(jnp.asarray(t, F32) for t in (c, sa, sb))

    return tables(RET_DK, 0, RET_DK) + tables(MLA_ROPE, MLA_NOPE, LANES)


def _prepare_weights(g_norm_mix, g_norm_ffn, w_in, g_q_a, w_q_b, g_kv_a, w_kv_b, g_qn, g_qr, g_kn,
                     g_kr):
    depth = w_in.shape[0]
    w_in_t = jnp.pad(jnp.swapaxes(w_in, 1, 2).astype(BF16),
                     ((0, 0), (0, IN_COLS_PAD - w_in.shape[2]), (0, 0)))
    w_qb = w_q_b.reshape(depth, Q_RANK, MLA_HEADS, MLA_NOPE + MLA_ROPE)
    w_qb = jnp.pad(w_qb, ((0, 0), (0, 0), (0, 0), (0, HEAD_PAD - MLA_NOPE - MLA_ROPE)))
    zeros32 = jnp.zeros((depth, MLA_ROPE), F32)
    return {
        "g_mix": g_norm_mix[:, None, :],
        "g_ffn": g_norm_ffn[:, None, :],
        "w_in_t": w_in_t,
        "g_qa": g_q_a[:, None, :],
        "w_qb": w_qb.reshape(depth, Q_RANK, MLA_HEADS * HEAD_PAD).astype(BF16),
        "g_kva": g_kv_a[:, None, :],
        "w_kvb": w_kv_b.astype(BF16),
        "g_q": (jnp.concatenate([g_qn, g_qr, zeros32], axis=-1) * Q_FOLD)[:, None, :],
        "g_kn": jnp.concatenate([g_kn, jnp.ones((depth, MLA_V), F32)], axis=-1)[:, None, :],
        "g_kr": jnp.pad(g_kr, ((0, 0), (0, LANES - MLA_ROPE)))[:, None, :],
    }


def _blockdiag_states(s):
    b, l = s.shape[:2]
    s = s.reshape(b, l, N_PAIRS, 2, RET_DK, RET_DK)
    z = jnp.zeros_like(s[:, :, :, 0])
    top = jnp.concatenate([s[:, :, :, 0], z], axis=-1)
    bot = jnp.concatenate([z, s[:, :, :, 1]], axis=-1)
    return jnp.concatenate([top, bot], axis=-2)


def kernel(x_prompt, x_sample, cache_ckv, cache_krope, state_ret_fwd, state_ret_bwd, c, c_ctx,
           w_mod, b_mod, g_norm_mix, g_norm_ffn, w_in, g_q_a, w_q_b, g_kv_a, w_kv_b,
           g_qn, g_qr, g_kn, g_kr, ret_p_fwd, ret_p_bwd, g_ret_gn, b_ret_gn, w_o,
           w_ffn_in, w_ffn_out):
    batch, seq, _ = x_prompt.shape
    dec_batch, dec_seq, _ = x_sample.shape
    depth = w_in.shape[0]

    wts = _prepare_weights(g_norm_mix, g_norm_ffn, w_in, g_q_a, w_q_b, g_kv_a, w_kv_b, g_qn, g_qr,
                           g_kn, g_kr)
    conds = jnp.concatenate([c_ctx[None], c, jnp.zeros((8 - 1 - dec_batch, D_MODEL), F32)], axis=0)
    mod = _modulation(conds, w_mod, b_mod)
    dmask, dvec = _decay_tables(jnp.stack([ret_p_fwd, ret_p_bwd], axis=1))
    gn_g = g_ret_gn[:, None, :]
    gn_b = b_ret_gn[:, None, :]

    rope_tabs = _rope_tables(dec_seq)
    kvn_cache = _cache_up(cache_ckv, wts["w_kvb"], wts["g_kn"])
    krp_cache = jnp.pad(cache_krope, ((0, 0), (0, 0), (0, 0), (MLA_NOPE, LANES - MLA_NOPE - MLA_ROPE)))
    krp_cache = krp_cache.astype(BF16)
    s_f0 = _blockdiag_states(state_ret_fwd)
    s_b0 = _blockdiag_states(state_ret_bwd)

    lat_tiles = dec_seq // ROW_TILE

    def ctx_row(i):
        return 0

    def lat_row(i):
        return 1 + i // lat_tiles

    x = x_prompt.reshape(batch * seq, D_MODEL)
    y = x_sample.reshape(dec_batch * dec_seq, D_MODEL)
    caches = (jax.ShapeDtypeStruct((batch, depth, seq, KV_RANK), F32),
              jax.ShapeDtypeStruct((batch, depth, MLA_ROPE, seq), F32))
    states = (jax.ShapeDtypeStruct((batch, depth, RET_HEADS, RET_DK, RET_DK), F32),) * 2
    for l in range(depth):
        pr = _proj(x, mod, l, ctx_row, wts, None, caches)
        caches = tuple(pr[7:9])
        casts = [(w_o, l, D_MODEL, 1), (w_ffn_in, l, 2 * D_FF, 1), (w_ffn_out, l, D_MODEL, 2)]
        mixed = _mix(pr, dmask, dvec, gn_g, gn_b, l, batch, seq, None, states, casts)
        ret, attn = mixed[:2]
        states = tuple(mixed[2:4])
        big_w = tuple(mixed[4:7])
        x = _out(x, ret, attn, mod, l, ctx_row, wts, big_w)

        pr = _proj(y, mod, l, lat_row, wts, rope_tabs, None)
        ret, attn = _mix(pr, dmask, dvec, gn_g, gn_b, l, dec_batch, dec_seq,
                         (kvn_cache, krp_cache, s_f0, s_b0), None)
        y = _out(y, ret, attn, mod, l, lat_row, wts, big_w)

    return (x.reshape(batch, seq, D_MODEL), y.reshape(dec_batch, dec_seq, D_MODEL),
            caches[0], jnp.swapaxes(caches[1], 2, 3), states[0], states[1])
```

```python
import functools

import jax
import jax.numpy as jnp
import numpy as np
from jax import lax
from jax.experimental import pallas as pl
from jax.experimental.pallas import tpu as pltpu

D_MODEL = 1024
N_MOD = 6
RET_HEADS = 8
RET_DK = 64
RET_WIDTH = 512
MLA_HEADS = 8
MLA_NOPE = 64
MLA_ROPE = 32
MLA_V = 64
Q_RANK = 256
KV_RANK = 128
D_FF = 2816
GRID_W = 64
ROPE_BASE = 10000.0
EPS = 1e-6

LANES = 128
HEAD_PAD = LANES
N_PAIRS = RET_HEADS // 2
IN_COLS_PAD = 4 * RET_WIDTH + Q_RANK + KV_RANK + LANES
ROW_TILE = 512
PROJ_TILE = 1024
LATENT_PROJ_TILE = 512
SUB_TILE = 256
FF_CHUNK = 256
MIX_CHUNK = 256
MIX_SEQS = 4
Q_DEC_F, K_DEC_F, Q_DEC_B, K_DEC_B, C_DEC_F, C_DEC_B = range(6)
N_DVEC = 6
Q_FOLD = (MLA_NOPE + MLA_ROPE) ** -0.5 * 1.4426950408889634
VMEM_LIMIT = 56 * 1024 * 1024

BF16 = jnp.bfloat16
F32 = jnp.float32
_NT = (((1,), (1,)), ((), ()))


def _dot(a, b):
    return jnp.dot(a, b, preferred_element_type=F32)


def _dot_nt(a, b):
    return lax.dot_general(a, b, _NT, preferred_element_type=F32)


def _rms(x, g):
    return x * lax.rsqrt(jnp.mean(x * x, axis=-1, keepdims=True) + EPS) * g


def _silu(x):
    return x * jax.nn.sigmoid(x)


def _masked_mean_sq(x, mask, n):
    return jnp.sum(jnp.where(mask, x * x, 0.0), axis=-1, keepdims=True) * (1.0 / n)


def _rope(x, c, sa, sb, shift):
    return x * c + pltpu.roll(x, shift, 1) * sa + pltpu.roll(x, LANES - shift, 1) * sb


def _mod_kernel(c_ref, w_ref, b_ref, o_ref):
    a = _silu(c_ref[...])
    w = w_ref[...]
    a_hi = a.astype(BF16)
    a_lo = (a - a_hi.astype(F32)).astype(BF16)
    w_hi = w.astype(BF16)
    w_lo = (w - w_hi.astype(F32)).astype(BF16)
    both = _dot(jnp.concatenate([a_hi, a_lo], axis=0), w_hi)
    o_ref[...] = both[0:8] + both[8:16] + _dot(a_hi, w_lo) + b_ref[...]


def _modulation(conds, w_mod, b_mod):
    depth, _, n = w_mod.shape
    tn = 1536
    return pl.pallas_call(
        _mod_kernel,
        grid=(depth, n // tn),
        in_specs=[
            pl.BlockSpec((8, D_MODEL), lambda l, j: (0, 0)),
            pl.BlockSpec((None, D_MODEL, tn), lambda l, j: (l, 0, j)),
            pl.BlockSpec((None, 1, tn), lambda l, j: (l, 0, j)),
        ],
        out_specs=pl.BlockSpec((None, 8, tn), lambda l, j: (l, 0, j)),
        out_shape=jax.ShapeDtypeStruct((depth, 8, n), F32),
        compiler_params=pltpu.CompilerParams(
            dimension_semantics=("arbitrary", "arbitrary"), vmem_limit_bytes=VMEM_LIMIT),
        name="modulation",
    )(conds, w_mod, b_mod.reshape(depth, 1, n))


def _norm_kn(kv, gkn):
    lane = lax.broadcasted_iota(jnp.int32, (kv.shape[0], LANES), 1)
    lo = lane < MLA_NOPE
    out = []
    for h in range(MLA_HEADS):
        kvh = kv[:, h * HEAD_PAD:(h + 1) * HEAD_PAD]
        rs = lax.rsqrt(_masked_mean_sq(kvh, lo, MLA_NOPE) + EPS)
        out.append((kvh * jnp.where(lo, rs * gkn, 1.0)).astype(BF16))
    return out


def _cache_kernel(ckv_ref, wkvb_ref, gkn_ref, kvn_ref):
    kv = _dot(ckv_ref[...].astype(BF16), wkvb_ref[...])
    for h, kvh in enumerate(_norm_kn(kv, gkn_ref[...])):
        kvn_ref[:, h * HEAD_PAD:(h + 1) * HEAD_PAD] = kvh


def _cache_up(cache_ckv, wkvb, gkn):
    nb, depth, past, _ = cache_ckv.shape
    width = MLA_HEADS * HEAD_PAD
    return pl.pallas_call(
        _cache_kernel,
        grid=(depth, nb),
        in_specs=[
            pl.BlockSpec((None, None, past, KV_RANK), lambda l, b: (b, l, 0, 0)),
            pl.BlockSpec((None, KV_RANK, width), lambda l, b: (l, 0, 0)),
            pl.BlockSpec((None, 1, LANES), lambda l, b: (l, 0, 0)),
        ],
        out_specs=pl.BlockSpec((None, None, past, width), lambda l, b: (l, b, 0, 0)),
        out_shape=jax.ShapeDtypeStruct((depth, nb, past, width), BF16),
        compiler_params=pltpu.CompilerParams(
            dimension_semantics=("arbitrary", "arbitrary"), vmem_limit_bytes=VMEM_LIMIT),
        name="cache_up",
    )(cache_ckv, wkvb, gkn)


def _mod_vec(mod_ref, cond, k):
    return mod_ref[pl.ds(cond, 1), k * D_MODEL:(k + 1) * D_MODEL]


def _is_array(x):
    return not isinstance(x, jax.ShapeDtypeStruct)


def _layer_spec(n, shape, layer, aliased):
    rest = tuple(shape[2:])
    zeros = (0,) * len(rest)
    if aliased:
        return pl.BlockSpec((n, None) + rest, lambda i: (i, layer) + zeros)
    return pl.BlockSpec((n, shape[1]) + rest, lambda i: (i, 0) + zeros)


def _put_layer(ref, i, tail, layer, fresh, value):
    if not fresh:
        ref[(i,) + tail] = value
        return
    for l in range(ref.shape[1]):
        ref[(i, l) + tail] = value if l == layer else jnp.zeros_like(value)


def _proj_kernel(*refs, rope, cond_row, n_alias, layer):
    (x_ref, mod_ref, gmix_ref, win_ref, gqa_ref, wqb_ref, gkva_ref, wkvb_ref,
     gq_ref, gkn_ref, gkr_ref) = refs[:11]
    if rope:
        c64_ref, sa64_ref, sb64_ref, c32_ref, sa32_ref, sb32_ref = refs[11:17]
        rq_ref, rk_ref, rv_ref, rg_ref, qcat_ref, kvn_ref, krp_ref = refs[17:]
    else:
        (rq_ref, rk_ref, rv_ref, rg_ref, qcat_ref, kvn_ref, krp_ref, ckv_ref,
         kro_ref) = refs[11 + n_alias:]

    cond = cond_row(pl.program_id(0) * x_ref.shape[0])
    sh1 = _mod_vec(mod_ref, cond, 0)
    sc1 = _mod_vec(mod_ref, cond, 1)
    w = RET_WIDTH
    lane = lax.broadcasted_iota(jnp.int32, (SUB_TILE, LANES), 1)
    nope = lane < MLA_NOPE
    is_rope = jnp.logical_and(lane >= MLA_NOPE, lane < MLA_NOPE + MLA_ROPE)

    for r in range(x_ref.shape[0] // SUB_TILE):
        rs = slice(r * SUB_TILE, (r + 1) * SUB_TILE)
        h = (_rms(x_ref[rs, :], gmix_ref[...]) * (1.0 + sc1) + sh1).astype(BF16)
        z = _dot_nt(h, win_ref[...])

        for j in range(w // LANES):
            sl = slice(j * LANES, (j + 1) * LANES)
            q = z[:, j * LANES:(j + 1) * LANES]
            k = z[:, w + j * LANES:w + (j + 1) * LANES] * (RET_DK ** -0.5)
            if rope:
                q = _rope(q, c64_ref[rs, :], sa64_ref[rs, :], sb64_ref[rs, :], 16)
                k = _rope(k, c64_ref[rs, :], sa64_ref[rs, :], sb64_ref[rs, :], 16)
            rq_ref[rs, sl] = q.astype(BF16)
            rk_ref[rs, sl] = k.astype(BF16)
        rv_ref[rs, :] = z[:, 2 * w:3 * w].astype(BF16)
        rg_ref[rs, :] = z[:, 3 * w:4 * w]

        o = 4 * w
        qa = z[:, o:o + Q_RANK]
        kva = z[:, o + Q_RANK:o + Q_RANK + KV_RANK]
        kr2 = z[:, o + Q_RANK + KV_RANK:]

        q = _dot(_rms(qa, gqa_ref[...]).astype(BF16), wqb_ref[...])
        for hh in range(MLA_HEADS):
            qh = q[:, hh * HEAD_PAD:(hh + 1) * HEAD_PAD]
            rs_n = lax.rsqrt(_masked_mean_sq(qh, nope, MLA_NOPE) + EPS)
            rs_r = lax.rsqrt(_masked_mean_sq(qh, is_rope, MLA_ROPE) + EPS)
            qn = qh * jnp.where(nope, rs_n, rs_r) * gq_ref[...]
            if rope:
                qn = _rope(qn, c32_ref[rs, :], sa32_ref[rs, :], sb32_ref[rs, :], 8)
            qcat_ref[rs, hh * HEAD_PAD:(hh + 1) * HEAD_PAD] = qn.astype(BF16)

        ckv = _rms(kva, gkva_ref[...])
        kv = _dot(ckv.astype(BF16), wkvb_ref[...])
        for hh, kvh in enumerate(_norm_kn(kv, gkn_ref[...])):
            kvn_ref[rs, hh * HEAD_PAD:(hh + 1) * HEAD_PAD] = kvh

        krn = kr2 * lax.rsqrt(_masked_mean_sq(kr2, lane < MLA_ROPE, MLA_ROPE) + EPS) * gkr_ref[...]
        krp = pltpu.roll(krn, MLA_NOPE, 1)
        if rope:
            krp = _rope(krp, c32_ref[rs, :], sa32_ref[rs, :], sb32_ref[rs, :], 8)
        else:
            _put_layer(ckv_ref, r, (), layer, n_alias == 0, ckv)
            _put_layer(kro_ref, r, (), layer, n_alias == 0, krn.T[0:MLA_ROPE, :])
        krp_ref[rs, :] = krp.astype(BF16)


def _proj(x, mod, layer, cond_row, wts, rope_tabs, caches):
    m = x.shape[0]
    rope = rope_tabs is not None
    tm = LATENT_PROJ_TILE if rope else PROJ_TILE
    width = MLA_HEADS * HEAD_PAD

    def row(i):
        return (i, 0)

    def lay3(i):
        return (layer, 0, 0)

    in_specs = [
        pl.BlockSpec((tm, D_MODEL), row),
        pl.BlockSpec((None, 8, N_MOD * D_MODEL), lay3),
        pl.BlockSpec((None, 1, D_MODEL), lay3),
        pl.BlockSpec((None, IN_COLS_PAD, D_MODEL), lay3),
        pl.BlockSpec((None, 1, Q_RANK), lay3),
        pl.BlockSpec((None, Q_RANK, width), lay3),
        pl.BlockSpec((None, 1, KV_RANK), lay3),
        pl.BlockSpec((None, KV_RANK, width), lay3),
        pl.BlockSpec((None, 1, LANES), lay3),
        pl.BlockSpec((None, 1, LANES), lay3),
        pl.BlockSpec((None, 1, LANES), lay3),
    ]
    args = [x, mod, wts["g_mix"], wts["w_in_t"], wts["g_qa"], wts["w_qb"], wts["g_kva"],
            wts["w_kvb"], wts["g_q"], wts["g_kn"], wts["g_kr"]]
    out_shape = [
        jax.ShapeDtypeStruct((m, RET_WIDTH), BF16),
        jax.ShapeDtypeStruct((m, RET_WIDTH), BF16),
        jax.ShapeDtypeStruct((m, RET_WIDTH), BF16),
        jax.ShapeDtypeStruct((m, RET_WIDTH), F32),
        jax.ShapeDtypeStruct((m, width), BF16),
        jax.ShapeDtypeStruct((m, width), BF16),
        jax.ShapeDtypeStruct((m, LANES), BF16),
    ]
    out_specs = [
        pl.BlockSpec((tm, RET_WIDTH), row), pl.BlockSpec((tm, RET_WIDTH), row),
        pl.BlockSpec((tm, RET_WIDTH), row), pl.BlockSpec((tm, RET_WIDTH), row),
        pl.BlockSpec((tm, width), row), pl.BlockSpec((tm, width), row),
        pl.BlockSpec((tm, LANES), row),
    ]
    aliases = {}
    if rope:
        n_lat = rope_tabs[0].shape[0]
        tiles = n_lat // tm
        in_specs += [pl.BlockSpec((tm, LANES), lambda i: (i % tiles, 0))] * 6
        args += list(rope_tabs)
    else:
        seq = caches[0].shape[2]
        assert seq == SUB_TILE
        nb = tm // seq
        if _is_array(caches[0]):
            in_specs += [pl.BlockSpec(memory_space=pl.ANY)] * 2
            aliases = {len(args): len(out_shape), len(args) + 1: len(out_shape) + 1}
            args += list(caches)
        out_shape += [jax.ShapeDtypeStruct(c.shape, c.dtype) for c in caches]
        out_specs += [_layer_spec(nb, c.shape, layer, bool(aliases)) for c in caches]

    return pl.pallas_call(
        functools.partial(_proj_kernel, rope=rope, cond_row=cond_row, n_alias=len(aliases),
                          layer=layer),
        grid=(m // tm,),
        in_specs=in_specs,
        out_specs=out_specs,
        out_shape=out_shape,
        input_output_aliases=aliases,
        compiler_params=pltpu.CompilerParams(
            dimension_semantics=("arbitrary",), vmem_limit_bytes=VMEM_LIMIT),
        name="proj_latent" if rope else "proj_context",
    )(*args)


def _log_gamma(p):
    return jnp.log1p(-jnp.exp2(-p))


def _decay_kernel(p_ref, dmask_ref, dvec_ref):
    c = MIX_CHUNK
    base = pl.program_id(0) * (2 * RET_HEADS)
    pair = pl.program_id(1)
    ri = lax.broadcasted_iota(jnp.int32, (c, c), 0)
    ci = lax.broadcasted_iota(jnp.int32, (c, c), 1)
    dif = (ri - ci).astype(F32)
    for e in range(2):
        h = 2 * pair + e
        lg_f = _log_gamma(jnp.full((c, c), p_ref[base + h], F32))
        lg_b = _log_gamma(jnp.full((c, c), p_ref[base + RET_HEADS + h], F32))
        fwd = jnp.where(dif >= 0, jnp.exp(jnp.maximum(dif, 0.0) * lg_f), 0.0)
        bwd = jnp.where(dif <= 0, jnp.exp(jnp.maximum(-dif, 0.0) * lg_b), 0.0)
        dmask_ref[e] = fwd + bwd
    lane = lax.broadcasted_iota(jnp.int32, (c, LANES), 1)
    rowf = lax.broadcasted_iota(jnp.int32, (c, LANES), 0).astype(F32)
    lo = lane < RET_DK
    lg_f = _log_gamma(jnp.where(lo, p_ref[base + 2 * pair], p_ref[base + 2 * pair + 1]))
    lg_b = _log_gamma(jnp.where(lo, p_ref[base + RET_HEADS + 2 * pair],
                                p_ref[base + RET_HEADS + 2 * pair + 1]))
    dvec_ref[Q_DEC_F] = jnp.exp((rowf + 1.0) * lg_f)
    dvec_ref[K_DEC_F] = jnp.exp((c - 1.0 - rowf) * lg_f)
    dvec_ref[Q_DEC_B] = jnp.exp((c - rowf) * lg_b)
    dvec_ref[K_DEC_B] = jnp.exp(rowf * lg_b)
    dvec_ref[C_DEC_F] = jnp.exp(c * lg_f)
    dvec_ref[C_DEC_B] = jnp.exp(c * lg_b)


def _decay_tables(decay_p):
    depth = decay_p.shape[0]
    c = MIX_CHUNK
    return pl.pallas_call(
        _decay_kernel,
        grid=(depth, N_PAIRS),
        in_specs=[pl.BlockSpec(memory_space=pltpu.SMEM)],
        out_specs=[
            pl.BlockSpec((None, 2, c, c), lambda l, p: (l, p, 0, 0)),
            pl.BlockSpec((None, None, N_DVEC, c, LANES), lambda l, p: (l, p, 0, 0, 0)),
        ],
        out_shape=[
            jax.ShapeDtypeStruct((depth, RET_HEADS, c, c), F32),
            jax.ShapeDtypeStruct((depth, N_PAIRS, N_DVEC, c, LANES), F32),
        ],
        compiler_params=pltpu.CompilerParams(
            dimension_semantics=("arbitrary", "arbitrary"), vmem_limit_bytes=VMEM_LIMIT),
        name="decay_tables",
    )(decay_p.reshape(-1))


def _mix_kernel(*refs, n_seq, n_sub, latent, n_alias, n_cast, layer):
    c = MIX_CHUNK
    nc = n_seq // c
    (rq_ref, rk_ref, rv_ref, rg_ref, qcat_ref, kvn_ref, krp_ref, gng_ref, gnb_ref,
     dmask_ref, dvec_ref) = refs[:11]
    if latent:
        kvc_ref, krc_ref, sf0_ref, sb0_ref, ret_ref, attn_ref, st_scr = refs[11:]
    else:
        n_in = 11 + n_alias
        cast_src = refs[n_in:n_in + n_cast]
        ret_ref, attn_ref, sf_ref, sb_ref = refs[n_in + n_cast:n_in + n_cast + 4]
        cast_dst = refs[n_in + n_cast + 4:]
        for src, dst in zip(cast_src, cast_dst):
            w = src.shape[1]
            dst[:, 0:w] = src[...].astype(BF16)
            if dst.shape[1] > w:
                dst[:, w:] = jnp.zeros((dst.shape[0], dst.shape[1] - w), BF16)

    lane = lax.broadcasted_iota(jnp.int32, (c, LANES), 1)
    lo = lane < RET_DK
    sq_r = lax.broadcasted_iota(jnp.int32, (LANES, LANES), 0)
    sq_c = lax.broadcasted_iota(jnp.int32, (LANES, LANES), 1)
    blockdiag = (sq_r < RET_DK) == (sq_c < RET_DK)

    def cols(j):
        return slice(j * LANES, (j + 1) * LANES)

    def state_update(pair, d, rows):
        kp = rk_ref[rows, cols(pair)]
        vp = rv_ref[rows, cols(pair)]
        kdt = (kp.astype(F32) * dvec_ref[pair, K_DEC_B if d else K_DEC_F]).T.astype(BF16)
        return jnp.where(blockdiag, _dot(kdt, vp), 0.0)

    nope_n = lax.broadcasted_iota(jnp.int32, (n_seq, LANES), 1) < MLA_NOPE
    if latent:
        nope_c = lax.broadcasted_iota(jnp.int32, (kvc_ref.shape[0], LANES), 1) < MLA_NOPE

    def states(sub):
        base = sub * n_seq
        for pair in range(N_PAIRS):
            for d in range(2):
                if latent:
                    s = (sb0_ref if d else sf0_ref)[pair]
                    cdec = dvec_ref[pair, C_DEC_B if d else C_DEC_F][0:LANES, :]
                    order = list(range(nc - 1, -1, -1)) if d else list(range(nc))
                    for idx, ch in enumerate(order):
                        st_scr[d, pair, ch] = s.astype(BF16)
                        if idx < nc - 1:
                            s = s * cdec + state_update(pair, d, pl.ds(base + ch * c, c))
                else:
                    s = state_update(pair, d, pl.ds(base, c))
                    st_ref = sb_ref if d else sf_ref
                    fresh = n_alias == 0
                    _put_layer(st_ref, sub, (2 * pair,), layer, fresh, s[0:RET_DK, 0:RET_DK])
                    _put_layer(st_ref, sub, (2 * pair + 1,), layer, fresh, s[RET_DK:, RET_DK:])

    def chunk_body(ch, base):
        rows = pl.ds(pl.multiple_of(base + ch * c, c), c)
        keys = pl.ds(base, n_seq)

        for pair in range(N_PAIRS):
            qp = rq_ref[rows, cols(pair)]
            kp = rk_ref[rows, cols(pair)]
            vp = rv_ref[rows, cols(pair)]
            zero = jnp.zeros_like(qp)
            a0 = (_dot_nt(jnp.where(lo, qp, zero), kp) * dmask_ref[2 * pair]).astype(BF16)
            a1 = (_dot_nt(jnp.where(lo, zero, qp), kp) * dmask_ref[2 * pair + 1]).astype(BF16)
            tot = jnp.where(lo, _dot(a0, vp), _dot(a1, vp))
            if latent:
                tot = (tot + _dot(qp, st_scr[0, pair, ch]) * dvec_ref[pair, Q_DEC_F]
                       + _dot(qp, st_scr[1, pair, ch]) * dvec_ref[pair, Q_DEC_B])
            inv = 1.0 / RET_DK
            m0 = jnp.sum(jnp.where(lo, tot, 0.0), axis=-1, keepdims=True) * inv
            m1 = jnp.sum(jnp.where(lo, 0.0, tot), axis=-1, keepdims=True) * inv
            y = tot - jnp.where(lo, m0, m1)
            v0 = jnp.sum(jnp.where(lo, y * y, 0.0), axis=-1, keepdims=True) * inv
            v1 = jnp.sum(jnp.where(lo, 0.0, y * y), axis=-1, keepdims=True) * inv
            yn = (y * lax.rsqrt(jnp.where(lo, v0, v1) + EPS) * gng_ref[:, cols(pair)]
                  + gnb_ref[:, cols(pair)])
            ret_ref[rows, cols(pair)] = (yn * _silu(rg_ref[rows, cols(pair)])).astype(BF16)

        krp = krp_ref[keys, :]
        for pair in range(N_PAIRS):
            outs = []
            for e in range(2):
                h = 2 * pair + e
                qc = qcat_ref[rows, cols(h)]
                kv = kvn_ref[keys, cols(h)]
                s = _dot_nt(qc, jnp.where(nope_n, kv, krp))
                m = jnp.max(s, axis=-1, keepdims=True)
                if latent:
                    kv_c = kvc_ref[:, cols(h)]
                    s2 = _dot_nt(qc, jnp.where(nope_c, kv_c, krc_ref[...]))
                    m = jnp.maximum(m, jnp.max(s2, axis=-1, keepdims=True))
                    p2 = jnp.exp2(s2 - m)
                p = jnp.exp2(s - m)
                den = jnp.sum(p, axis=-1, keepdims=True)
                acc = _dot(p.astype(BF16), kv)
                if latent:
                    den = den + jnp.sum(p2, axis=-1, keepdims=True)
                    acc = acc + _dot(p2.astype(BF16), kv_c)
                outs.append(acc / den)
            attn_ref[rows, cols(pair)] = jnp.where(
                lo, pltpu.roll(outs[0], MLA_V, 1), outs[1]).astype(BF16)
        return base

    for sub in range(n_sub):
        states(sub)
    for sub in range(n_sub):
        if nc == 1:
            chunk_body(0, sub * n_seq)
        else:
            lax.fori_loop(0, nc, chunk_body, sub * n_seq)


def _mix(proj_out, dmask, dvec, gn_g, gn_b, layer, n_batch, n_seq, latent_in, states_out,
         casts=()):
    rq, rk, rv, rg, qcat, kvn, krp = proj_out[:7]
    latent = latent_in is not None
    aliases = {}
    m = n_batch * n_seq
    c = MIX_CHUNK
    width = MLA_HEADS * HEAD_PAD

    def row(b):
        return (b, 0)

    def lay3(b):
        return (layer, 0, 0)

    once = pl.Buffered(1)
    n_sub = 1 if latent else MIX_SEQS
    n_steps = n_batch // n_sub
    blk = n_sub * n_seq
    in_specs = [
        pl.BlockSpec((blk, RET_WIDTH), row), pl.BlockSpec((blk, RET_WIDTH), row),
        pl.BlockSpec((blk, RET_WIDTH), row), pl.BlockSpec((blk, RET_WIDTH), row),
        pl.BlockSpec((blk, width), row), pl.BlockSpec((blk, width), row),
        pl.BlockSpec((blk, LANES), row),
        pl.BlockSpec((None, 1, RET_WIDTH), lay3), pl.BlockSpec((None, 1, RET_WIDTH), lay3),
        pl.BlockSpec((None, RET_HEADS, c, c), lambda b: (layer, 0, 0, 0), pipeline_mode=once),
        pl.BlockSpec((None, N_PAIRS, N_DVEC, c, LANES), lambda b: (layer, 0, 0, 0, 0),
                     pipeline_mode=once),
    ]
    args = [rq, rk, rv, rg, qcat, kvn, krp, gn_g, gn_b, dmask, dvec]
    out_shape = [jax.ShapeDtypeStruct((m, RET_WIDTH), BF16),
                 jax.ShapeDtypeStruct((m, MLA_HEADS * MLA_V), BF16)]
    out_specs = [pl.BlockSpec((blk, RET_WIDTH), row), pl.BlockSpec((blk, RET_WIDTH), row)]
    scratch = []
    if latent:
        kvn_c, krp_c, s_f0, s_b0 = latent_in
        past = kvn_c.shape[2]
        st_spec = pl.BlockSpec((None, None, N_PAIRS, LANES, LANES), lambda b: (b, layer, 0, 0, 0))
        in_specs += [
            pl.BlockSpec((None, None, past, width), lambda b: (layer, b, 0, 0)),
            pl.BlockSpec((None, None, past, LANES), lambda b: (b, layer, 0, 0)),
            st_spec, st_spec,
        ]
        args += [kvn_c, krp_c, s_f0, s_b0]
        scratch = [pltpu.VMEM((2, N_PAIRS, n_seq // c, LANES, LANES), BF16)]
    else:
        if _is_array(states_out[0]):
            in_specs += [pl.BlockSpec(memory_space=pl.ANY)] * 2
            aliases = {len(args): len(out_shape), len(args) + 1: len(out_shape) + 1}
            args += list(states_out)
        out_shape += [jax.ShapeDtypeStruct(s.shape, s.dtype) for s in states_out]
        out_specs += [_layer_spec(n_sub, s.shape, layer, bool(aliases)) for s in states_out]
        for w, w_layer, out_cols, stride in casts:
            blocks = n_steps // stride
            rows = w.shape[1] // blocks
            assert rows * blocks == w.shape[1] and rows % 16 == 0 and stride * blocks == n_steps
            in_specs.append(pl.BlockSpec(
                (None, rows, w.shape[2]),
                lambda b, w_layer=w_layer, stride=stride: (w_layer, b // stride, 0)))
            args.append(w)
            out_shape.append(jax.ShapeDtypeStruct((w.shape[1], out_cols), BF16))
            out_specs.append(pl.BlockSpec((rows, out_cols),
                                          lambda b, stride=stride: (b // stride, 0)))

    return pl.pallas_call(
        functools.partial(_mix_kernel, n_seq=n_seq, n_sub=n_sub, latent=latent, layer=layer,
                          n_alias=len(aliases), n_cast=len(casts)),
        grid=(n_steps,),
        in_specs=in_specs,
        out_specs=out_specs,
        out_shape=out_shape,
        input_output_aliases=aliases,
        scratch_shapes=scratch,
        compiler_params=pltpu.CompilerParams(
            dimension_semantics=("arbitrary",), vmem_limit_bytes=VMEM_LIMIT),
        name="mix_latent" if latent else "mix_context",
    )(*args)


def _out_kernel(x_ref, ret_ref, attn_ref, mod_ref, gffn_ref, wo_ref, wfi_ref, wfo_ref, o_ref,
                act_scr, *, cond_row):
    cond = cond_row(pl.program_id(0) * x_ref.shape[0])
    gt1 = _mod_vec(mod_ref, cond, 2)
    sh2 = _mod_vec(mod_ref, cond, 3)
    sc2 = _mod_vec(mod_ref, cond, 4)
    gt2 = _mod_vec(mod_ref, cond, 5)
    mixed = _dot(ret_ref[...], wo_ref[0:RET_WIDTH, :]) + _dot(attn_ref[...], wo_ref[RET_WIDTH:, :])
    x1 = x_ref[...] + gt1 * mixed
    h = (_rms(x1, gffn_ref[...]) * (1.0 + sc2) + sh2).astype(BF16)
    for c0 in range(0, D_FF, FF_CHUNK):
        cw = min(FF_CHUNK, D_FF - c0)
        gate = _dot(h, wfi_ref[:, c0:c0 + cw])
        up = _dot(h, wfi_ref[:, D_FF + c0:D_FF + c0 + cw])
        act_scr[:, c0:c0 + cw] = (_silu(gate) * up).astype(BF16)
    o_ref[...] = x1 + gt2 * _dot(act_scr[...], wfo_ref[...])


def _out(x, ret, attn, mod, layer, cond_row, wts, big_w):
    m = x.shape[0]
    tm = ROW_TILE

    def row(i):
        return (i, 0)

    def lay3(i):
        return (layer, 0, 0)

    def whole(i):
        return (0, 0)

    once = pl.Buffered(1)
    return pl.pallas_call(
        functools.partial(_out_kernel, cond_row=cond_row),
        grid=(m // tm,),
        in_specs=[
            pl.BlockSpec((tm, D_MODEL), row),
            pl.BlockSpec((tm, RET_WIDTH), row),
            pl.BlockSpec((tm, RET_WIDTH), row),
            pl.BlockSpec((None, 8, N_MOD * D_MODEL), lay3),
            pl.BlockSpec((None, 1, D_MODEL), lay3),
            pl.BlockSpec((D_MODEL, D_MODEL), whole, pipeline_mode=once),
            pl.BlockSpec((D_MODEL, 2 * D_FF), whole, pipeline_mode=once),
            pl.BlockSpec((D_FF, D_MODEL), whole, pipeline_mode=once),
        ],
        out_specs=pl.BlockSpec((tm, D_MODEL), row),
        out_shape=jax.ShapeDtypeStruct((m, D_MODEL), F32),
        scratch_shapes=[pltpu.VMEM((tm, D_FF), BF16)],
        compiler_params=pltpu.CompilerParams(
            dimension_semantics=("arbitrary",), vmem_limit_bytes=VMEM_LIMIT),
        name="out_ffn",
    )(x, ret, attn, mod, wts["g_ffn"], *big_w)


def _rope_tables(n_lat):
    pos = np.arange(n_lat)
    row = (pos // GRID_W).astype(np.float32)[:, None]
    col = (pos % GRID_W).astype(np.float32)[:, None]
    lane = np.arange(LANES)[None, :]

    def tables(d, start, period):
        rel = (lane - start) % period
        active = np.logical_and(lane >= start, rel < d)
        half = d // 2
        nf = half // 2
        inv = np.float32(ROPE_BASE) ** (-((rel % nf).astype(np.float32)) / np.float32(nf))
        ang = (np.where(rel < half, row, col) * inv).astype(np.float32)
        cos, sin = np.cos(ang), np.sin(ang)
        first = (rel % half) < nf
        c = np.where(active, cos, 1.0)
        sa = np.where(np.logical_and(active, np.logical_not(first)), sin, 0.0)
        sb = np.where(np.logical_and(active, first), -sin, 0.0)
        return tuple(jnp.asarray(t, F32) for t in (c, sa, sb))

    return tables(RET_DK, 0, RET_DK) + tables(MLA_ROPE, MLA_NOPE, LANES)


def _prepare_weights(g_norm_mix, g_norm_ffn, w_in, g_q_a, w_q_b, g_kv_a, w_kv_b, g_qn, g_qr, g_kn,
                     g_kr):
    depth = w_in.shape[0]
    w_in_t = jnp.pad(jnp.swapaxes(w_in, 1, 2).astype(BF16),
                     ((0, 0), (0, IN_COLS_PAD - w_in.shape[2]), (0, 0)))
    w_qb = w_q_b.reshape(depth, Q_RANK, MLA_HEADS, MLA_NOPE + MLA_ROPE)
    w_qb = jnp.pad(w_qb, ((0, 0), (0, 0), (0, 0), (0, HEAD_PAD - MLA_NOPE - MLA_ROPE)))
    zeros32 = jnp.zeros((depth, MLA_ROPE), F32)
    return {
        "g_mix": g_norm_mix[:, None, :],
        "g_ffn": g_norm_ffn[:, None, :],
        "w_in_t": w_in_t,
        "g_qa": g_q_a[:, None, :],
        "w_qb": w_qb.reshape(depth, Q_RANK, MLA_HEADS * HEAD_PAD).astype(BF16),
        "g_kva": g_kv_a[:, None, :],
        "w_kvb": w_kv_b.astype(BF16),
        "g_q": (jnp.concatenate([g_qn, g_qr, zeros32], axis=-1) * Q_FOLD)[:, None, :],
        "g_kn": jnp.concatenate([g_kn, jnp.ones((depth, MLA_V), F32)], axis=-1)[:, None, :],
        "g_kr": jnp.pad(g_kr, ((0, 0), (0, LANES - MLA_ROPE)))[:, None, :],
    }


def _blockdiag_states(s):
    b, l = s.shape[:2]
    s = s.reshape(b, l, N_PAIRS, 2, RET_DK, RET_DK)
    z = jnp.zeros_like(s[:, :, :, 0])
    top = jnp.concatenate([s[:, :, :, 0], z], axis=-1)
    bot = jnp.concatenate([z, s[:, :, :, 1]], axis=-1)
    return jnp.concatenate([top, bot], axis=-2)


def kernel(x_prompt, x_sample, cache_ckv, cache_krope, state_ret_fwd, state_ret_bwd, c, c_ctx,
           w_mod, b_mod, g_norm_mix, g_norm_ffn, w_in, g_q_a, w_q_b, g_kv_a, w_kv_b,
           g_qn, g_qr, g_kn, g_kr, ret_p_fwd, ret_p_bwd, g_ret_gn, b_ret_gn, w_o,
           w_ffn_in, w_ffn_out):
    batch, seq, _ = x_prompt.shape
    dec_batch, dec_seq, _ = x_sample.shape
    depth = w_in.shape[0]

    wts = _prepare_weights(g_norm_mix, g_norm_ffn, w_in, g_q_a, w_q_b, g_kv_a, w_kv_b, g_qn, g_qr,
                           g_kn, g_kr)
    conds = jnp.concatenate([c_ctx[None], c, jnp.zeros((8 - 1 - dec_batch, D_MODEL), F32)], axis=0)
    mod = _modulation(conds, w_mod, b_mod)
    dmask, dvec = _decay_tables(jnp.stack([ret_p_fwd, ret_p_bwd], axis=1))
    gn_g = g_ret_gn[:, None, :]
    gn_b = b_ret_gn[:, None, :]

    rope_tabs = _rope_tables(dec_seq)
    kvn_cache = _cache_up(cache_ckv, wts["w_kvb"], wts["g_kn"])
    krp_cache = jnp.pad(cache_krope, ((0, 0), (0, 0), (0, 0), (MLA_NOPE, LANES - MLA_NOPE - MLA_ROPE)))
    krp_cache = krp_cache.astype(BF16)
    s_f0 = _blockdiag_states(state_ret_fwd)
    s_b0 = _blockdiag_states(state_ret_bwd)

    def ctx_row(row):
        return 0

    def lat_row(row):
        return 1 + row // dec_seq

    x = x_prompt.reshape(batch * seq, D_MODEL)
    y = x_sample.reshape(dec_batch * dec_seq, D_MODEL)
    caches = (jax.ShapeDtypeStruct((batch, depth, seq, KV_RANK), F32),
              jax.ShapeDtypeStruct((batch, depth, MLA_ROPE, seq), F32))
    states = (jax.ShapeDtypeStruct((batch, depth, RET_HEADS, RET_DK, RET_DK), F32),) * 2
    for l in range(depth):
        pr = _proj(x, mod, l, ctx_row, wts, None, caches)
        caches = tuple(pr[7:9])
        casts = [(w_o, l, D_MODEL, 1), (w_ffn_in, l, 2 * D_FF, 1), (w_ffn_out, l, D_MODEL, 2)]
        mixed = _mix(pr, dmask, dvec, gn_g, gn_b, l, batch, seq, None, states, casts)
        ret, attn = mixed[:2]
        states = tuple(mixed[2:4])
        big_w = tuple(mixed[4:7])
        x = _out(x, ret, attn, mod, l, ctx_row, wts, big_w)

        pr = _proj(y, mod, l, lat_row, wts, rope_tabs, None)
        ret, attn = _mix(pr, dmask, dvec, gn_g, gn_b, l, dec_batch, dec_seq,
                         (kvn_cache, krp_cache, s_f0, s_b0), None)
        y = _out(y, ret, attn, mod, l, lat_row, wts, big_w)

    return (x.reshape(batch, seq, D_MODEL), y.reshape(dec_batch, dec_seq, D_MODEL),
            caches[0], jnp.swapaxes(caches[1], 2, 3), states[0], states[1])
```

```python
import functools

import jax
import jax.numpy as jnp
import numpy as np
from jax import lax
from jax.experimental import pallas as pl
from jax.experimental.pallas import tpu as pltpu

D_MODEL = 1024
N_MOD = 6
RET_HEADS = 8
RET_DK = 64
RET_WIDTH = 512
MLA_HEADS = 8
MLA_NOPE = 64
MLA_ROPE = 32
MLA_V = 64
Q_RANK = 256
KV_RANK = 128
D_FF = 2816
GRID_W = 64
ROPE_BASE = 10000.0
EPS = 1e-6

LANES = 128
HEAD_PAD = LANES
N_PAIRS = RET_HEADS // 2
IN_COLS_PAD = 4 * RET_WIDTH + Q_RANK + KV_RANK + LANES
ROW_TILE = 512
PROJ_TILE = 1024
LATENT_PROJ_TILE = 512
SUB_TILE = 256
FF_CHUNK = 256
MIX_CHUNK = 256
MIX_SEQS = 4
Q_DEC_F, K_DEC_F, Q_DEC_B, K_DEC_B, C_DEC_F, C_DEC_B = range(6)
N_DVEC = 6
Q_FOLD = (MLA_NOPE + MLA_ROPE) ** -0.5 * 1.4426950408889634
VMEM_LIMIT = 56 * 1024 * 1024

BF16 = jnp.bfloat16
F32 = jnp.float32
_NT = (((1,), (1,)), ((), ()))


def _dot(a, b):
    return jnp.dot(a, b, preferred_element_type=F32)


def _dot_nt(a, b):
    return lax.dot_general(a, b, _NT, preferred_element_type=F32)


def _rms(x, g):
    return x * lax.rsqrt(jnp.mean(x * x, axis=-1, keepdims=True) + EPS) * g


def _silu(x):
    return x * jax.nn.sigmoid(x)


def _masked_mean_sq(x, mask, n):
    return jnp.sum(jnp.where(mask, x * x, 0.0), axis=-1, keepdims=True) * (1.0 / n)


def _mod_kernel(c_ref, w_ref, b_ref, o_ref):
    a = _silu(c_ref[...])
    w = w_ref[...]
    a_hi = a.astype(BF16)
    a_lo = (a - a_hi.astype(F32)).astype(BF16)
    w_hi = w.astype(BF16)
    w_lo = (w - w_hi.astype(F32)).astype(BF16)
    both = _dot(jnp.concatenate([a_hi, a_lo], axis=0), w_hi)
    o_ref[...] = both[0:8] + both[8:16] + _dot(a_hi, w_lo) + b_ref[...]


def _modulation(conds, w_mod, b_mod):
    depth, _, n = w_mod.shape
    tn = 1536
    return pl.pallas_call(
        _mod_kernel,
        grid=(depth, n // tn),
        in_specs=[
            pl.BlockSpec((8, D_MODEL), lambda l, j: (0, 0)),
            pl.BlockSpec((None, D_MODEL, tn), lambda l, j: (l, 0, j)),
            pl.BlockSpec((None, 1, tn), lambda l, j: (l, 0, j)),
        ],
        out_specs=pl.BlockSpec((None, 8, tn), lambda l, j: (l, 0, j)),
        out_shape=jax.ShapeDtypeStruct((depth, 8, n), F32),
        compiler_params=pltpu.CompilerParams(
            dimension_semantics=("arbitrary", "arbitrary"), vmem_limit_bytes=VMEM_LIMIT),
        name="modulation",
    )(conds, w_mod, b_mod.reshape(depth, 1, n))


def _norm_kn(kv, gkn):
    lane = lax.broadcasted_iota(jnp.int32, (kv.shape[0], LANES), 1)
    lo = lane < MLA_NOPE
    out = []
    for h in range(MLA_HEADS):
        kvh = kv[:, h * HEAD_PAD:(h + 1) * HEAD_PAD]
        rs = lax.rsqrt(_masked_mean_sq(kvh, lo, MLA_NOPE) + EPS)
        out.append((kvh * jnp.where(lo, rs * gkn, 1.0)).astype(BF16))
    return out


def _cache_kernel(ckv_ref, wkvb_ref, gkn_ref, kvn_ref):
    kv = _dot(ckv_ref[...].astype(BF16), wkvb_ref[...])
    for h, kvh in enumerate(_norm_kn(kv, gkn_ref[...])):
        kvn_ref[:, h * HEAD_PAD:(h + 1) * HEAD_PAD] = kvh


def _cache_up(cache_ckv, wkvb, gkn):
    nb, depth, past, _ = cache_ckv.shape
    width = MLA_HEADS * HEAD_PAD
    return pl.pallas_call(
        _cache_kernel,
        grid=(depth, nb),
        in_specs=[
            pl.BlockSpec((None, None, past, KV_RANK), lambda l, b: (b, l, 0, 0)),
            pl.BlockSpec((None, KV_RANK, width), lambda l, b: (l, 0, 0)),
            pl.BlockSpec((None, 1, LANES), lambda l, b: (l, 0, 0)),
        ],
        out_specs=pl.BlockSpec((None, None, past, width), lambda l, b: (l, b, 0, 0)),
        out_shape=jax.ShapeDtypeStruct((depth, nb, past, width), BF16),
        compiler_params=pltpu.CompilerParams(
            dimension_semantics=("arbitrary", "arbitrary"), vmem_limit_bytes=VMEM_LIMIT),
        name="cache_up",
    )(cache_ckv, wkvb, gkn)


def _mod_vec(mod_ref, cond, k):
    return mod_ref[pl.ds(cond, 1), k * D_MODEL:(k + 1) * D_MODEL]


def _is_array(x):
    return not isinstance(x, jax.ShapeDtypeStruct)


def _layer_spec(n, shape, layer, aliased):
    rest = tuple(shape[2:])
    zeros = (0,) * len(rest)
    if aliased:
        return pl.BlockSpec((n, None) + rest, lambda i: (i, layer) + zeros)
    return pl.BlockSpec((n, shape[1]) + rest, lambda i: (i, 0) + zeros)


def _put_layer(ref, i, tail, layer, fresh, value):
    if not fresh:
        ref[(i,) + tail] = value
        return
    for l in range(ref.shape[1]):
        ref[(i, l) + tail] = value if l == layer else jnp.zeros_like(value)


def _proj_kernel(*refs, rope, cond_row, n_alias, layer):
    (x_ref, mod_ref, gmix_ref, win_ref, gqa_ref, wqb_ref, gkva_ref, wkvb_ref,
     gq_ref, gkn_ref, gkr_ref) = refs[:11]
    if rope:
        wqb_sw_ref, gq_sw_ref, gkr_sw_ref, c64_ref, s64_ref, c32_ref, s32_ref = refs[11:18]
        rq_ref, rk_ref, rv_ref, rg_ref, qcat_ref, kvn_ref, krp_ref = refs[18:]
    else:
        (rq_ref, rk_ref, rv_ref, rg_ref, qcat_ref, kvn_ref, krp_ref, ckv_ref,
         kro_ref) = refs[11 + n_alias:]

    cond = cond_row(pl.program_id(0) * x_ref.shape[0])
    sh1 = _mod_vec(mod_ref, cond, 0)
    sc1 = _mod_vec(mod_ref, cond, 1)
    w = RET_WIDTH
    lane = lax.broadcasted_iota(jnp.int32, (SUB_TILE, LANES), 1)
    nope = lane < MLA_NOPE
    is_rope = jnp.logical_and(lane >= MLA_NOPE, lane < MLA_NOPE + MLA_ROPE)

    for r in range(x_ref.shape[0] // SUB_TILE):
        rs = slice(r * SUB_TILE, (r + 1) * SUB_TILE)
        h = (_rms(x_ref[rs, :], gmix_ref[...]) * (1.0 + sc1) + sh1).astype(BF16)
        z = _dot_nt(h, win_ref[...])
        sw = IN_COLS_PAD

        for j in range(w // LANES):
            sl = slice(j * LANES, (j + 1) * LANES)
            q = z[:, j * LANES:(j + 1) * LANES]
            k = z[:, w + j * LANES:w + (j + 1) * LANES] * (RET_DK ** -0.5)
            if rope:
                q_sw = z[:, sw + j * LANES:sw + (j + 1) * LANES]
                k_sw = z[:, sw + w + j * LANES:sw + w + (j + 1) * LANES] * (RET_DK ** -0.5)
                q = q * c64_ref[rs, :] + q_sw * s64_ref[rs, :]
                k = k * c64_ref[rs, :] + k_sw * s64_ref[rs, :]
            rq_ref[rs, sl] = q.astype(BF16)
            rk_ref[rs, sl] = k.astype(BF16)
        rv_ref[rs, :] = z[:, 2 * w:3 * w].astype(BF16)
        rg_ref[rs, :] = z[:, 3 * w:4 * w]

        o = 4 * w
        qa = z[:, o:o + Q_RANK]
        kva = z[:, o + Q_RANK:o + Q_RANK + KV_RANK]
        kr2 = z[:, o + Q_RANK + KV_RANK:IN_COLS_PAD]

        qa_n = _rms(qa, gqa_ref[...]).astype(BF16)
        q = _dot(qa_n, wqb_ref[...])
        if rope:
            q_sw = _dot(qa_n, wqb_sw_ref[...])
        for hh in range(MLA_HEADS):
            hs = slice(hh * HEAD_PAD, (hh + 1) * HEAD_PAD)
            qh = q[:, hs]
            rs_n = lax.rsqrt(_masked_mean_sq(qh, nope, MLA_NOPE) + EPS)
            rs_r = lax.rsqrt(_masked_mean_sq(qh, is_rope, MLA_ROPE) + EPS)
            qn = qh * jnp.where(nope, rs_n, rs_r) * gq_ref[...]
            if rope:
                qn = qn * c32_ref[rs, :] + q_sw[:, hs] * rs_r * gq_sw_ref[...] * s32_ref[rs, :]
            qcat_ref[rs, hs] = qn.astype(BF16)

        ckv = _rms(kva, gkva_ref[...])
        kv = _dot(ckv.astype(BF16), wkvb_ref[...])
        for hh, kvh in enumerate(_norm_kn(kv, gkn_ref[...])):
            kvn_ref[rs, hh * HEAD_PAD:(hh + 1) * HEAD_PAD] = kvh

        rs_k = lax.rsqrt(_masked_mean_sq(kr2, lane < MLA_ROPE, MLA_ROPE) + EPS)
        krn = kr2 * rs_k * gkr_ref[...]
        krp = pltpu.roll(krn, MLA_NOPE, 1)
        if rope:
            kr_sw = z[:, sw + 2 * w:] * rs_k * gkr_sw_ref[...]
            krp = krp * c32_ref[rs, :] + pltpu.roll(kr_sw, MLA_NOPE, 1) * s32_ref[rs, :]
        else:
            _put_layer(ckv_ref, r, (), layer, n_alias == 0, ckv)
            _put_layer(kro_ref, r, (), layer, n_alias == 0, krn.T[0:MLA_ROPE, :])
        krp_ref[rs, :] = krp.astype(BF16)


def _proj(x, mod, layer, cond_row, wts, rope_tabs, caches):
    m = x.shape[0]
    rope = rope_tabs is not None
    tm = LATENT_PROJ_TILE if rope else PROJ_TILE
    width = MLA_HEADS * HEAD_PAD

    def row(i):
        return (i, 0)

    def lay3(i):
        return (layer, 0, 0)

    in_specs = [
        pl.BlockSpec((tm, D_MODEL), row),
        pl.BlockSpec((None, 8, N_MOD * D_MODEL), lay3),
        pl.BlockSpec((None, 1, D_MODEL), lay3),
        pl.BlockSpec((None, wts["w_in_t"].shape[1] if rope else IN_COLS_PAD, D_MODEL), lay3),
        pl.BlockSpec((None, 1, Q_RANK), lay3),
        pl.BlockSpec((None, Q_RANK, width), lay3),
        pl.BlockSpec((None, 1, KV_RANK), lay3),
        pl.BlockSpec((None, KV_RANK, width), lay3),
        pl.BlockSpec((None, 1, LANES), lay3),
        pl.BlockSpec((None, 1, LANES), lay3),
        pl.BlockSpec((None, 1, LANES), lay3),
    ]
    args = [x, mod, wts["g_mix"], wts["w_in_t"], wts["g_qa"], wts["w_qb"], wts["g_kva"],
            wts["w_kvb"], wts["g_q"], wts["g_kn"], wts["g_kr"]]
    out_shape = [
        jax.ShapeDtypeStruct((m, RET_WIDTH), BF16),
        jax.ShapeDtypeStruct((m, RET_WIDTH), BF16),
        jax.ShapeDtypeStruct((m, RET_WIDTH), BF16),
        jax.ShapeDtypeStruct((m, RET_WIDTH), F32),
        jax.ShapeDtypeStruct((m, width), BF16),
        jax.ShapeDtypeStruct((m, width), BF16),
        jax.ShapeDtypeStruct((m, LANES), BF16),
    ]
    out_specs = [
        pl.BlockSpec((tm, RET_WIDTH), row), pl.BlockSpec((tm, RET_WIDTH), row),
        pl.BlockSpec((tm, RET_WIDTH), row), pl.BlockSpec((tm, RET_WIDTH), row),
        pl.BlockSpec((tm, width), row), pl.BlockSpec((tm, width), row),
        pl.BlockSpec((tm, LANES), row),
    ]
    aliases = {}
    if rope:
        n_lat = rope_tabs[0].shape[0]
        tiles = n_lat // tm
        in_specs += [pl.BlockSpec((None, Q_RANK, width), lay3),
                     pl.BlockSpec((None, 1, LANES), lay3), pl.BlockSpec((None, 1, LANES), lay3)]
        args += [wts["w_qb_sw"], wts["g_q_sw"], wts["g_kr_sw"]]
        in_specs += [pl.BlockSpec((tm, LANES), lambda i: (i % tiles, 0))] * len(rope_tabs)
        args += list(rope_tabs)
    else:
        seq = caches[0].shape[2]
        assert seq == SUB_TILE
        nb = tm // seq
        if _is_array(caches[0]):
            in_specs += [pl.BlockSpec(memory_space=pl.ANY)] * 2
            aliases = {len(args): len(out_shape), len(args) + 1: len(out_shape) + 1}
            args += list(caches)
        out_shape += [jax.ShapeDtypeStruct(c.shape, c.dtype) for c in caches]
        out_specs += [_layer_spec(nb, c.shape, layer, bool(aliases)) for c in caches]

    return pl.pallas_call(
        functools.partial(_proj_kernel, rope=rope, cond_row=cond_row, n_alias=len(aliases),
                          layer=layer),
        grid=(m // tm,),
        in_specs=in_specs,
        out_specs=out_specs,
        out_shape=out_shape,
        input_output_aliases=aliases,
        compiler_params=pltpu.CompilerParams(
            dimension_semantics=("arbitrary",), vmem_limit_bytes=VMEM_LIMIT),
        name="proj_latent" if rope else "proj_context",
    )(*args)


def _log_gamma(p):
    return jnp.log1p(-jnp.exp2(-p))


def _decay_kernel(p_ref, dmask_ref, dvec_ref):
    c = MIX_CHUNK
    base = pl.program_id(0) * (2 * RET_HEADS)
    pair = pl.program_id(1)
    ri = lax.broadcasted_iota(jnp.int32, (c, c), 0)
    ci = lax.broadcasted_iota(jnp.int32, (c, c), 1)
    dif = (ri - ci).astype(F32)
    for e in range(2):
        h = 2 * pair + e
        lg_f = _log_gamma(jnp.full((c, c), p_ref[base + h], F32))
        lg_b = _log_gamma(jnp.full((c, c), p_ref[base + RET_HEADS + h], F32))
        fwd = jnp.where(dif >= 0, jnp.exp(jnp.maximum(dif, 0.0) * lg_f), 0.0)
        bwd = jnp.where(dif <= 0, jnp.exp(jnp.maximum(-dif, 0.0) * lg_b), 0.0)
        dmask_ref[e] = fwd + bwd
    lane = lax.broadcasted_iota(jnp.int32, (c, LANES), 1)
    rowf = lax.broadcasted_iota(jnp.int32, (c, LANES), 0).astype(F32)
    lo = lane < RET_DK
    lg_f = _log_gamma(jnp.where(lo, p_ref[base + 2 * pair], p_ref[base + 2 * pair + 1]))
    lg_b = _log_gamma(jnp.where(lo, p_ref[base + RET_HEADS + 2 * pair],
                                p_ref[base + RET_HEADS + 2 * pair + 1]))
    dvec_ref[Q_DEC_F] = jnp.exp((rowf + 1.0) * lg_f)
    dvec_ref[K_DEC_F] = jnp.exp((c - 1.0 - rowf) * lg_f)
    dvec_ref[Q_DEC_B] = jnp.exp((c - rowf) * lg_b)
    dvec_ref[K_DEC_B] = jnp.exp(rowf * lg_b)
    dvec_ref[C_DEC_F] = jnp.exp(c * lg_f)
    dvec_ref[C_DEC_B] = jnp.exp(c * lg_b)


def _decay_tables(decay_p):
    depth = decay_p.shape[0]
    c = MIX_CHUNK
    return pl.pallas_call(
        _decay_kernel,
        grid=(depth, N_PAIRS),
        in_specs=[pl.BlockSpec(memory_space=pltpu.SMEM)],
        out_specs=[
            pl.BlockSpec((None, 2, c, c), lambda l, p: (l, p, 0, 0)),
            pl.BlockSpec((None, None, N_DVEC, c, LANES), lambda l, p: (l, p, 0, 0, 0)),
        ],
        out_shape=[
            jax.ShapeDtypeStruct((depth, RET_HEADS, c, c), F32),
            jax.ShapeDtypeStruct((depth, N_PAIRS, N_DVEC, c, LANES), F32),
        ],
        compiler_params=pltpu.CompilerParams(
            dimension_semantics=("arbitrary", "arbitrary"), vmem_limit_bytes=VMEM_LIMIT),
        name="decay_tables",
    )(decay_p.reshape(-1))


def _mix_kernel(*refs, n_seq, n_sub, latent, n_alias, n_cast, layer):
    c = MIX_CHUNK
    nc = n_seq // c
    (rq_ref, rk_ref, rv_ref, rg_ref, qcat_ref, kvn_ref, krp_ref, gng_ref, gnb_ref,
     dmask_ref, dvec_ref) = refs[:11]
    if latent:
        kvc_ref, krc_ref, sf0_ref, sb0_ref, ret_ref, attn_ref, st_scr = refs[11:]
    else:
        n_in = 11 + n_alias
        cast_src = refs[n_in:n_in + n_cast]
        ret_ref, attn_ref, sf_ref, sb_ref = refs[n_in + n_cast:n_in + n_cast + 4]
        cast_dst = refs[n_in + n_cast + 4:]
        for src, dst in zip(cast_src, cast_dst):
            w = src.shape[1]
            dst[:, 0:w] = src[...].astype(BF16)
            if dst.shape[1] > w:
                dst[:, w:] = jnp.zeros((dst.shape[0], dst.shape[1] - w), BF16)

    lane = lax.broadcasted_iota(jnp.int32, (c, LANES), 1)
    lo = lane < RET_DK
    sq_r = lax.broadcasted_iota(jnp.int32, (LANES, LANES), 0)
    sq_c = lax.broadcasted_iota(jnp.int32, (LANES, LANES), 1)
    blockdiag = (sq_r < RET_DK) == (sq_c < RET_DK)

    def cols(j):
        return slice(j * LANES, (j + 1) * LANES)

    def state_update(pair, d, rows):
        kp = rk_ref[rows, cols(pair)]
        vp = rv_ref[rows, cols(pair)]
        kdt = (kp.astype(F32) * dvec_ref[pair, K_DEC_B if d else K_DEC_F]).T.astype(BF16)
        return jnp.where(blockdiag, _dot(kdt, vp), 0.0)

    nope_n = lax.broadcasted_iota(jnp.int32, (n_seq, LANES), 1) < MLA_NOPE
    if latent:
        nope_c = lax.broadcasted_iota(jnp.int32, (kvc_ref.shape[0], LANES), 1) < MLA_NOPE

    def states(sub):
        base = sub * n_seq
        for pair in range(N_PAIRS):
            for d in range(2):
                if latent:
                    s = (sb0_ref if d else sf0_ref)[pair]
                    cdec = dvec_ref[pair, C_DEC_B if d else C_DEC_F][0:LANES, :]
                    order = list(range(nc - 1, -1, -1)) if d else list(range(nc))
                    for idx, ch in enumerate(order):
                        st_scr[d, pair, ch] = s.astype(BF16)
                        if idx < nc - 1:
                            s = s * cdec + state_update(pair, d, pl.ds(base + ch * c, c))
                else:
                    s = state_update(pair, d, pl.ds(base, c))
                    st_ref = sb_ref if d else sf_ref
                    fresh = n_alias == 0
                    _put_layer(st_ref, sub, (2 * pair,), layer, fresh, s[0:RET_DK, 0:RET_DK])
                    _put_layer(st_ref, sub, (2 * pair + 1,), layer, fresh, s[RET_DK:, RET_DK:])

    def chunk_body(ch, base):
        rows = pl.ds(pl.multiple_of(base + ch * c, c), c)
        keys = pl.ds(base, n_seq)

        for pair in range(N_PAIRS):
            qp = rq_ref[rows, cols(pair)]
            kp = rk_ref[rows, cols(pair)]
            vp = rv_ref[rows, cols(pair)]
            zero = jnp.zeros_like(qp)
            a0 = (_dot_nt(jnp.where(lo, qp, zero), kp) * dmask_ref[2 * pair]).astype(BF16)
            a1 = (_dot_nt(jnp.where(lo, zero, qp), kp) * dmask_ref[2 * pair + 1]).astype(BF16)
            tot = jnp.where(lo, _dot(a0, vp), _dot(a1, vp))
            if latent:
                tot = (tot + _dot(qp, st_scr[0, pair, ch]) * dvec_ref[pair, Q_DEC_F]
                       + _dot(qp, st_scr[1, pair, ch]) * dvec_ref[pair, Q_DEC_B])
            inv = 1.0 / RET_DK
            m0 = jnp.sum(jnp.where(lo, tot, 0.0), axis=-1, keepdims=True) * inv
            m1 = jnp.sum(jnp.where(lo, 0.0, tot), axis=-1, keepdims=True) * inv
            y = tot - jnp.where(lo, m0, m1)
            v0 = jnp.sum(jnp.where(lo, y * y, 0.0), axis=-1, keepdims=True) * inv
            v1 = jnp.sum(jnp.where(lo, 0.0, y * y), axis=-1, keepdims=True) * inv
            yn = (y * lax.rsqrt(jnp.where(lo, v0, v1) + EPS) * gng_ref[:, cols(pair)]
                  + gnb_ref[:, cols(pair)])
            ret_ref[rows, cols(pair)] = (yn * _silu(rg_ref[rows, cols(pair)])).astype(BF16)

        krp = krp_ref[keys, :]
        for pair in range(N_PAIRS):
            outs = []
            for e in range(2):
                h = 2 * pair + e
                qc = qcat_ref[rows, cols(h)]
                kv = kvn_ref[keys, cols(h)]
                s = _dot_nt(qc, jnp.where(nope_n, kv, krp))
                m = jnp.max(s, axis=-1, keepdims=True)
                if latent:
                    kv_c = kvc_ref[:, cols(h)]
                    s2 = _dot_nt(qc, jnp.where(nope_c, kv_c, krc_ref[...]))
                    m = jnp.maximum(m, jnp.max(s2, axis=-1, keepdims=True))
                    p2 = jnp.exp2(s2 - m)
                p = jnp.exp2(s - m)
                den = jnp.sum(p, axis=-1, keepdims=True)
                acc = _dot(p.astype(BF16), kv)
                if latent:
                    den = den + jnp.sum(p2, axis=-1, keepdims=True)
                    acc = acc + _dot(p2.astype(BF16), kv_c)
                outs.append(acc / den)
            attn_ref[rows, cols(pair)] = jnp.where(
                lo, pltpu.roll(outs[0], MLA_V, 1), outs[1]).astype(BF16)
        return base

    for sub in range(n_sub):
        states(sub)
    for sub in range(n_sub):
        if nc == 1:
            chunk_body(0, sub * n_seq)
        else:
            lax.fori_loop(0, nc, chunk_body, sub * n_seq)


def _mix(proj_out, dmask, dvec, gn_g, gn_b, layer, n_batch, n_seq, latent_in, states_out,
         casts=()):
    rq, rk, rv, rg, qcat, kvn, krp = proj_out[:7]
    latent = latent_in is not None
    aliases = {}
    m = n_batch * n_seq
    c = MIX_CHUNK
    width = MLA_HEADS * HEAD_PAD

    def row(b):
        return (b, 0)

    def lay3(b):
        return (layer, 0, 0)

    once = pl.Buffered(1)
    n_sub = 1 if latent else MIX_SEQS
    n_steps = n_batch // n_sub
    blk = n_sub * n_seq
    in_specs = [
        pl.BlockSpec((blk, RET_WIDTH), row), pl.BlockSpec((blk, RET_WIDTH), row),
        pl.BlockSpec((blk, RET_WIDTH), row), pl.BlockSpec((blk, RET_WIDTH), row),
        pl.BlockSpec((blk, width), row), pl.BlockSpec((blk, width), row),
        pl.BlockSpec((blk, LANES), row),
        pl.BlockSpec((None, 1, RET_WIDTH), lay3), pl.BlockSpec((None, 1, RET_WIDTH), lay3),
        pl.BlockSpec((None, RET_HEADS, c, c), lambda b: (layer, 0, 0, 0), pipeline_mode=once),
        pl.BlockSpec((None, N_PAIRS, N_DVEC, c, LANES), lambda b: (layer, 0, 0, 0, 0),
                     pipeline_mode=once),
    ]
    args = [rq, rk, rv, rg, qcat, kvn, krp, gn_g, gn_b, dmask, dvec]
    out_shape = [jax.ShapeDtypeStruct((m, RET_WIDTH), BF16),
                 jax.ShapeDtypeStruct((m, MLA_HEADS * MLA_V), BF16)]
    out_specs = [pl.BlockSpec((blk, RET_WIDTH), row), pl.BlockSpec((blk, RET_WIDTH), row)]
    scratch = []
    if latent:
        kvn_c, krp_c, s_f0, s_b0 = latent_in
        past = kvn_c.shape[2]
        st_spec = pl.BlockSpec((None, None, N_PAIRS, LANES, LANES), lambda b: (b, layer, 0, 0, 0))
        in_specs += [
            pl.BlockSpec((None, None, past, width), lambda b: (layer, b, 0, 0)),
            pl.BlockSpec((None, None, past, LANES), lambda b: (b, layer, 0, 0)),
            st_spec, st_spec,
        ]
        args += [kvn_c, krp_c, s_f0, s_b0]
        scratch = [pltpu.VMEM((2, N_PAIRS, n_seq // c, LANES, LANES), BF16)]
    else:
        if _is_array(states_out[0]):
            in_specs += [pl.BlockSpec(memory_space=pl.ANY)] * 2
            aliases = {len(args): len(out_shape), len(args) + 1: len(out_shape) + 1}
            args += list(states_out)
        out_shape += [jax.ShapeDtypeStruct(s.shape, s.dtype) for s in states_out]
        out_specs += [_layer_spec(n_sub, s.shape, layer, bool(aliases)) for s in states_out]
        for w, w_layer, out_cols, stride in casts:
            blocks = n_steps // stride
            rows = w.shape[1] // blocks
            assert rows * blocks == w.shape[1] and rows % 16 == 0 and stride * blocks == n_steps
            in_specs.append(pl.BlockSpec(
                (None, rows, w.shape[2]),
                lambda b, w_layer=w_layer, stride=stride: (w_layer, b // stride, 0)))
            args.append(w)
            out_shape.append(jax.ShapeDtypeStruct((w.shape[1], out_cols), BF16))
            out_specs.append(pl.BlockSpec((rows, out_cols),
                                          lambda b, stride=stride: (b // stride, 0)))

    return pl.pallas_call(
        functools.partial(_mix_kernel, n_seq=n_seq, n_sub=n_sub, latent=latent, layer=layer,
                          n_alias=len(aliases), n_cast=len(casts)),
        grid=(n_steps,),
        in_specs=in_specs,
        out_specs=out_specs,
        out_shape=out_shape,
        input_output_aliases=aliases,
        scratch_shapes=scratch,
        compiler_params=pltpu.CompilerParams(
            dimension_semantics=("arbitrary",), vmem_limit_bytes=VMEM_LIMIT),
        name="mix_latent" if latent else "mix_context",
    )(*args)


def _out_kernel(x_ref, ret_ref, attn_ref, mod_ref, gffn_ref, wo_ref, wfi_ref, wfo_ref, o_ref,
                act_scr, *, cond_row):
    cond = cond_row(pl.program_id(0) * x_ref.shape[0])
    gt1 = _mod_vec(mod_ref, cond, 2)
    sh2 = _mod_vec(mod_ref, cond, 3)
    sc2 = _mod_vec(mod_ref, cond, 4)
    gt2 = _mod_vec(mod_ref, cond, 5)
    mixed = _dot(ret_ref[...], wo_ref[0:RET_WIDTH, :]) + _dot(attn_ref[...], wo_ref[RET_WIDTH:, :])
    x1 = x_ref[...] + gt1 * mixed
    h = (_rms(x1, gffn_ref[...]) * (1.0 + sc2) + sh2).astype(BF16)
    for c0 in range(0, D_FF, FF_CHUNK):
        cw = min(FF_CHUNK, D_FF - c0)
        gate = _dot(h, wfi_ref[:, c0:c0 + cw])
        up = _dot(h, wfi_ref[:, D_FF + c0:D_FF + c0 + cw])
        act_scr[:, c0:c0 + cw] = (_silu(gate) * up).astype(BF16)
    o_ref[...] = x1 + gt2 * _dot(act_scr[...], wfo_ref[...])


def _out(x, ret, attn, mod, layer, cond_row, wts, big_w):
    m = x.shape[0]
    tm = ROW_TILE

    def row(i):
        return (i, 0)

    def lay3(i):
        return (layer, 0, 0)

    def whole(i):
        return (0, 0)

    once = pl.Buffered(1)
    return pl.pallas_call(
        functools.partial(_out_kernel, cond_row=cond_row),
        grid=(m // tm,),
        in_specs=[
            pl.BlockSpec((tm, D_MODEL), row),
            pl.BlockSpec((tm, RET_WIDTH), row),
            pl.BlockSpec((tm, RET_WIDTH), row),
            pl.BlockSpec((None, 8, N_MOD * D_MODEL), lay3),
            pl.BlockSpec((None, 1, D_MODEL), lay3),
            pl.BlockSpec((D_MODEL, D_MODEL), whole, pipeline_mode=once),
            pl.BlockSpec((D_MODEL, 2 * D_FF), whole, pipeline_mode=once),
            pl.BlockSpec((D_FF, D_MODEL), whole, pipeline_mode=once),
        ],
        out_specs=pl.BlockSpec((tm, D_MODEL), row),
        out_shape=jax.ShapeDtypeStruct((m, D_MODEL), F32),
        scratch_shapes=[pltpu.VMEM((tm, D_FF), BF16)],
        compiler_params=pltpu.CompilerParams(
            dimension_semantics=("arbitrary",), vmem_limit_bytes=VMEM_LIMIT),
        name="out_ffn",
    )(x, ret, attn, mod, wts["g_ffn"], *big_w)


def _rope_tables(n_lat):
    pos = np.arange(n_lat)
    row = (pos // GRID_W).astype(np.float32)[:, None]
    col = (pos % GRID_W).astype(np.float32)[:, None]
    lane = np.arange(LANES)[None, :]

    def tables(d, start, period):
        rel = (lane - start) % period
        active = np.logical_and(lane >= start, rel < d)
        half = d // 2
        nf = half // 2
        inv = np.float32(ROPE_BASE) ** (-((rel % nf).astype(np.float32)) / np.float32(nf))
        ang = (np.where(rel < half, row, col) * inv).astype(np.float32)
        cos, sin = np.cos(ang), np.sin(ang)
        first = (rel % half) < nf
        c = np.where(active, cos, 1.0)
        s = np.where(active, np.where(first, -sin, sin), 0.0)
        return tuple(jnp.asarray(t, F32) for t in (c, s))

    return tables(RET_DK, 0, RET_DK) + tables(MLA_ROPE, MLA_NOPE, LANES)


def _swap_partners(a, axis, d):
    shape = a.shape
    split = shape[:axis] + (shape[axis] // d, 2, 2, d // 4) + shape[axis + 1:]
    return jnp.flip(a.reshape(split), axis=axis + 2).reshape(shape)


def _prepare_weights(g_norm_mix, g_norm_ffn, w_in, g_q_a, w_q_b, g_kv_a, w_kv_b, g_qn, g_qr, g_kn,
                     g_kr):
    depth = w_in.shape[0]
    n_in = w_in.shape[2]
    wt = jnp.swapaxes(w_in, 1, 2)
    w_in_t = jnp.concatenate([
        wt, jnp.zeros((depth, IN_COLS_PAD - n_in, D_MODEL), F32),
        _swap_partners(wt[:, 0:RET_WIDTH], 1, RET_DK),
        _swap_partners(wt[:, RET_WIDTH:2 * RET_WIDTH], 1, RET_DK),
        _swap_partners(wt[:, n_in - MLA_ROPE:], 1, MLA_ROPE),
        jnp.zeros((depth, LANES - MLA_ROPE, D_MODEL), F32)], axis=1).astype(BF16)
    w_qb = w_q_b.reshape(depth, Q_RANK, MLA_HEADS, MLA_NOPE + MLA_ROPE)
    w_qb_sw = jnp.pad(_swap_partners(w_qb[..., MLA_NOPE:], 3, MLA_ROPE),
                      ((0, 0), (0, 0), (0, 0), (MLA_NOPE, HEAD_PAD - MLA_NOPE - MLA_ROPE)))
    w_qb = jnp.pad(w_qb, ((0, 0), (0, 0), (0, 0), (0, HEAD_PAD - MLA_NOPE - MLA_ROPE)))
    zeros32 = jnp.zeros((depth, MLA_ROPE), F32)
    zeros64 = jnp.zeros((depth, MLA_NOPE), F32)
    return {
        "g_mix": g_norm_mix[:, None, :],
        "g_ffn": g_norm_ffn[:, None, :],
        "w_in_t": w_in_t,
        "g_qa": g_q_a[:, None, :],
        "w_qb": w_qb.reshape(depth, Q_RANK, MLA_HEADS * HEAD_PAD).astype(BF16),
        "g_kva": g_kv_a[:, None, :],
        "w_kvb": w_kv_b.astype(BF16),
        "g_q": (jnp.concatenate([g_qn, g_qr, zeros32], axis=-1) * Q_FOLD)[:, None, :],
        "g_kn": jnp.concatenate([g_kn, jnp.ones((depth, MLA_V), F32)], axis=-1)[:, None, :],
        "g_kr": jnp.pad(g_kr, ((0, 0), (0, LANES - MLA_ROPE)))[:, None, :],
        "w_qb_sw": w_qb_sw.reshape(depth, Q_RANK, MLA_HEADS * HEAD_PAD).astype(BF16),
        "g_q_sw": (jnp.concatenate([zeros64, _swap_partners(g_qr, 1, MLA_ROPE), zeros32], axis=-1)
                   * Q_FOLD)[:, None, :],
        "g_kr_sw": jnp.pad(_swap_partners(g_kr, 1, MLA_ROPE),
                           ((0, 0), (0, LANES - MLA_ROPE)))[:, None, :],
    }


def _blockdiag_states(s):
    b, l = s.shape[:2]
    s = s.reshape(b, l, N_PAIRS, 2, RET_DK, RET_DK)
    z = jnp.zeros_like(s[:, :, :, 0])
    top = jnp.concatenate([s[:, :, :, 0], z], axis=-1)
    bot = jnp.concatenate([z, s[:, :, :, 1]], axis=-1)
    return jnp.concatenate([top, bot], axis=-2)


def kernel(x_prompt, x_sample, cache_ckv, cache_krope, state_ret_fwd, state_ret_bwd, c, c_ctx,
           w_mod, b_mod, g_norm_mix, g_norm_ffn, w_in, g_q_a, w_q_b, g_kv_a, w_kv_b,
           g_qn, g_qr, g_kn, g_kr, ret_p_fwd, ret_p_bwd, g_ret_gn, b_ret_gn, w_o,
           w_ffn_in, w_ffn_out):
    batch, seq, _ = x_prompt.shape
    dec_batch, dec_seq, _ = x_sample.shape
    depth = w_in.shape[0]

    wts = _prepare_weights(g_norm_mix, g_norm_ffn, w_in, g_q_a, w_q_b, g_kv_a, w_kv_b, g_qn, g_qr,
                           g_kn, g_kr)
    conds = jnp.concatenate([c_ctx[None], c, jnp.zeros((8 - 1 - dec_batch, D_MODEL), F32)], axis=0)
    mod = _modulation(conds, w_mod, b_mod)
    dmask, dvec = _decay_tables(jnp.stack([ret_p_fwd, ret_p_bwd], axis=1))
    gn_g = g_ret_gn[:, None, :]
    gn_b = b_ret_gn[:, None, :]

    rope_tabs = _rope_tables(dec_seq)
    kvn_cache = _cache_up(cache_ckv, wts["w_kvb"], wts["g_kn"])
    krp_cache = jnp.pad(cache_krope, ((0, 0), (0, 0), (0, 0), (MLA_NOPE, LANES - MLA_NOPE - MLA_ROPE)))
    krp_cache = krp_cache.astype(BF16)
    s_f0 = _blockdiag_states(state_ret_fwd)
    s_b0 = _blockdiag_states(state_ret_bwd)

    def ctx_row(row):
        return 0

    def lat_row(row):
        return 1 + row // dec_seq

    x = x_prompt.reshape(batch * seq, D_MODEL)
    y = x_sample.reshape(dec_batch * dec_seq, D_MODEL)
    caches = (jax.ShapeDtypeStruct((batch, depth, seq, KV_RANK), F32),
              jax.ShapeDtypeStruct((batch, depth, MLA_ROPE, seq), F32))
    states = (jax.ShapeDtypeStruct((batch, depth, RET_HEADS, RET_DK, RET_DK), F32),) * 2
    for l in range(depth):
        pr = _proj(x, mod, l, ctx_row, wts, None, caches)
        caches = tuple(pr[7:9])
        casts = [(w_o, l, D_MODEL, 1), (w_ffn_in, l, 2 * D_FF, 1), (w_ffn_out, l, D_MODEL, 2)]
        mixed = _mix(pr, dmask, dvec, gn_g, gn_b, l, batch, seq, None, states, casts)
        ret, attn = mixed[:2]
        states = tuple(mixed[2:4])
        big_w = tuple(mixed[4:7])
        x = _out(x, ret, attn, mod, l, ctx_row, wts, big_w)

        pr = _proj(y, mod, l, lat_row, wts, rope_tabs, None)
        ret, attn = _mix(pr, dmask, dvec, gn_g, gn_b, l, dec_batch, dec_seq,
                         (kvn_cache, krp_cache, s_f0, s_b0), None)
        y = _out(y, ret, attn, mod, l, lat_row, wts, big_w)

    return (x.reshape(batch, seq, D_MODEL), y.reshape(dec_batch, dec_seq, D_MODEL),
            caches[0], jnp.swapaxes(caches[1], 2, 3), states[0], states[1])
```

```python
import functools

import jax
import jax.numpy as jnp
import numpy as np
from jax import lax
from jax.experimental import pallas as pl
from jax.experimental.pallas import tpu as pltpu

D_MODEL = 1024
N_MOD = 6
RET_HEADS = 8
RET_DK = 64
RET_WIDTH = 512
MLA_HEADS = 8
MLA_NOPE = 64
MLA_ROPE = 32
MLA_V = 64
Q_RANK = 256
KV_RANK = 128
D_FF = 2816
GRID_W = 64
ROPE_BASE = 10000.0
EPS = 1e-6

LANES = 128
HEAD_PAD = LANES
N_PAIRS = RET_HEADS // 2
IN_COLS_PAD = 4 * RET_WIDTH + Q_RANK + KV_RANK + LANES
ROW_TILE = 512
PROJ_TILE = 1024
LATENT_PROJ_TILE = 512
SUB_TILE = 256
FF_CHUNK = 256
MIX_CHUNK = 256
MIX_SEQS = 4
Q_DEC_F, K_DEC_F, Q_DEC_B, K_DEC_B, C_DEC_F, C_DEC_B = range(6)
N_DVEC = 6
Q_FOLD = (MLA_NOPE + MLA_ROPE) ** -0.5 * 1.4426950408889634
VMEM_LIMIT = 56 * 1024 * 1024

BF16 = jnp.bfloat16
F32 = jnp.float32
_NT = (((1,), (1,)), ((), ()))


def _dot(a, b):
    return jnp.dot(a, b, preferred_element_type=F32)


def _dot_nt(a, b):
    return lax.dot_general(a, b, _NT, preferred_element_type=F32)


def _rms(x, g):
    return x * lax.rsqrt(jnp.mean(x * x, axis=-1, keepdims=True) + EPS) * g


def _silu(x):
    return x * jax.nn.sigmoid(x)


def _masked_mean_sq(x, mask, n):
    return jnp.sum(jnp.where(mask, x * x, 0.0), axis=-1, keepdims=True) * (1.0 / n)


def _mod_kernel(c_ref, w_ref, b_ref, o_ref):
    a = _silu(c_ref[...])
    w = w_ref[...]
    a_hi = a.astype(BF16)
    a_lo = (a - a_hi.astype(F32)).astype(BF16)
    w_hi = w.astype(BF16)
    w_lo = (w - w_hi.astype(F32)).astype(BF16)
    both = _dot(jnp.concatenate([a_hi, a_lo], axis=0), w_hi)
    o_ref[...] = both[0:8] + both[8:16] + _dot(a_hi, w_lo) + b_ref[...]


def _modulation(conds, w_mod, b_mod):
    depth, _, n = w_mod.shape
    tn = 1536
    return pl.pallas_call(
        _mod_kernel,
        grid=(depth, n // tn),
        in_specs=[
            pl.BlockSpec((8, D_MODEL), lambda l, j: (0, 0)),
            pl.BlockSpec((None, D_MODEL, tn), lambda l, j: (l, 0, j)),
            pl.BlockSpec((None, 1, tn), lambda l, j: (l, 0, j)),
        ],
        out_specs=pl.BlockSpec((None, 8, tn), lambda l, j: (l, 0, j)),
        out_shape=jax.ShapeDtypeStruct((depth, 8, n), F32),
        compiler_params=pltpu.CompilerParams(
            dimension_semantics=("arbitrary", "arbitrary"), vmem_limit_bytes=VMEM_LIMIT),
        name="modulation",
    )(conds, w_mod, b_mod.reshape(depth, 1, n))


def _norm_kn(kv, gkn):
    lane = lax.broadcasted_iota(jnp.int32, (kv.shape[0], LANES), 1)
    lo = lane < MLA_NOPE
    out = []
    for h in range(MLA_HEADS):
        kvh = kv[:, h * HEAD_PAD:(h + 1) * HEAD_PAD]
        rs = lax.rsqrt(_masked_mean_sq(kvh, lo, MLA_NOPE) + EPS)
        out.append((kvh * jnp.where(lo, rs * gkn, 1.0)).astype(BF16))
    return out


def _cache_kernel(ckv_ref, wkvb_ref, gkn_ref, kvn_ref):
    kv = _dot(ckv_ref[...].astype(BF16), wkvb_ref[...])
    for h, kvh in enumerate(_norm_kn(kv, gkn_ref[...])):
        kvn_ref[:, h * HEAD_PAD:(h + 1) * HEAD_PAD] = kvh


def _cache_up(cache_ckv, wkvb, gkn):
    nb, depth, past, _ = cache_ckv.shape
    width = MLA_HEADS * HEAD_PAD
    return pl.pallas_call(
        _cache_kernel,
        grid=(depth, nb),
        in_specs=[
            pl.BlockSpec((None, None, past, KV_RANK), lambda l, b: (b, l, 0, 0)),
            pl.BlockSpec((None, KV_RANK, width), lambda l, b: (l, 0, 0)),
            pl.BlockSpec((None, 1, LANES), lambda l, b: (l, 0, 0)),
        ],
        out_specs=pl.BlockSpec((None, None, past, width), lambda l, b: (l, b, 0, 0)),
        out_shape=jax.ShapeDtypeStruct((depth, nb, past, width), BF16),
        compiler_params=pltpu.CompilerParams(
            dimension_semantics=("arbitrary", "arbitrary"), vmem_limit_bytes=VMEM_LIMIT),
        name="cache_up",
    )(cache_ckv, wkvb, gkn)


def _mod_vec(mod_ref, cond, k):
    return mod_ref[pl.ds(cond, 1), k * D_MODEL:(k + 1) * D_MODEL]


def _is_array(x):
    return not isinstance(x, jax.ShapeDtypeStruct)


def _layer_spec(n, shape, layer, aliased):
    rest = tuple(shape[2:])
    zeros = (0,) * len(rest)
    if aliased:
        return pl.BlockSpec((n, None) + rest, lambda i: (i, layer) + zeros)
    return pl.BlockSpec((n, shape[1]) + rest, lambda i: (i, 0) + zeros)


def _put_layer(ref, i, tail, layer, fresh, value):
    if not fresh:
        ref[(i,) + tail] = value
        return
    for l in range(ref.shape[1]):
        ref[(i, l) + tail] = value if l == layer else jnp.zeros_like(value)


def _proj_kernel(*refs, rope, cond_row, n_alias, layer):
    (x_ref, mod_ref, gmix_ref, win_ref, gqa_ref, wqb_ref, gkva_ref, wkvb_ref,
     gq_ref, gkn_ref, gkr_ref) = refs[:11]
    if rope:
        (win_sw_ref, wqb_sw_ref, gq_sw_ref, gkr_sw_ref, c64_ref, s64_ref, c32_ref,
         s32_ref) = refs[11:19]
        rq_ref, rk_ref, rv_ref, rg_ref, qcat_ref, kvn_ref, krp_ref = refs[19:]
    else:
        (rq_ref, rk_ref, rv_ref, rg_ref, qcat_ref, kvn_ref, krp_ref, ckv_ref,
         kro_ref) = refs[11 + n_alias:]

    cond = cond_row(pl.program_id(0) * x_ref.shape[0])
    sh1 = _mod_vec(mod_ref, cond, 0)
    sc1 = _mod_vec(mod_ref, cond, 1)
    w = RET_WIDTH
    lane = lax.broadcasted_iota(jnp.int32, (SUB_TILE, LANES), 1)
    nope = lane < MLA_NOPE
    is_rope = jnp.logical_and(lane >= MLA_NOPE, lane < MLA_NOPE + MLA_ROPE)

    for r in range(x_ref.shape[0] // SUB_TILE):
        rs = slice(r * SUB_TILE, (r + 1) * SUB_TILE)
        h = (_rms(x_ref[rs, :], gmix_ref[...]) * (1.0 + sc1) + sh1).astype(BF16)
        z = _dot_nt(h, win_ref[...])
        if rope:
            z_sw = _dot_nt(h, win_sw_ref[...])

        for j in range(w // LANES):
            sl = slice(j * LANES, (j + 1) * LANES)
            q = z[:, j * LANES:(j + 1) * LANES]
            k = z[:, w + j * LANES:w + (j + 1) * LANES] * (RET_DK ** -0.5)
            if rope:
                q_sw = z_sw[:, j * LANES:(j + 1) * LANES]
                k_sw = z_sw[:, w + j * LANES:w + (j + 1) * LANES] * (RET_DK ** -0.5)
                q = q * c64_ref[rs, :] + q_sw * s64_ref[rs, :]
                k = k * c64_ref[rs, :] + k_sw * s64_ref[rs, :]
            rq_ref[rs, sl] = q.astype(BF16)
            rk_ref[rs, sl] = k.astype(BF16)
        rv_ref[rs, :] = z[:, 2 * w:3 * w].astype(BF16)
        rg_ref[rs, :] = z[:, 3 * w:4 * w]

        o = 4 * w
        qa = z[:, o:o + Q_RANK]
        kva = z[:, o + Q_RANK:o + Q_RANK + KV_RANK]
        kr2 = z[:, o + Q_RANK + KV_RANK:]

        qa_n = _rms(qa, gqa_ref[...]).astype(BF16)
        q = _dot(qa_n, wqb_ref[...])
        if rope:
            q_sw = _dot(qa_n, wqb_sw_ref[...])
        for hh in range(MLA_HEADS):
            hs = slice(hh * HEAD_PAD, (hh + 1) * HEAD_PAD)
            qh = q[:, hs]
            rs_n = lax.rsqrt(_masked_mean_sq(qh, nope, MLA_NOPE) + EPS)
            rs_r = lax.rsqrt(_masked_mean_sq(qh, is_rope, MLA_ROPE) + EPS)
            qn = qh * jnp.where(nope, rs_n, rs_r) * gq_ref[...]
            if rope:
                qn = qn * c32_ref[rs, :] + q_sw[:, hs] * rs_r * gq_sw_ref[...] * s32_ref[rs, :]
            qcat_ref[rs, hs] = qn.astype(BF16)

        ckv = _rms(kva, gkva_ref[...])
        kv = _dot(ckv.astype(BF16), wkvb_ref[...])
        for hh, kvh in enumerate(_norm_kn(kv, gkn_ref[...])):
            kvn_ref[rs, hh * HEAD_PAD:(hh + 1) * HEAD_PAD] = kvh

        rs_k = lax.rsqrt(_masked_mean_sq(kr2, lane < MLA_ROPE, MLA_ROPE) + EPS)
        krn = kr2 * rs_k * gkr_ref[...]
        krp = pltpu.roll(krn, MLA_NOPE, 1)
        if rope:
            kr_sw = z_sw[:, 2 * w:] * rs_k * gkr_sw_ref[...]
            krp = krp * c32_ref[rs, :] + pltpu.roll(kr_sw, MLA_NOPE, 1) * s32_ref[rs, :]
        else:
            _put_layer(ckv_ref, r, (), layer, n_alias == 0, ckv)
            _put_layer(kro_ref, r, (), layer, n_alias == 0, krn.T[0:MLA_ROPE, :])
        krp_ref[rs, :] = krp.astype(BF16)


def _proj(x, mod, layer, cond_row, wts, rope_tabs, caches):
    m = x.shape[0]
    rope = rope_tabs is not None
    tm = LATENT_PROJ_TILE if rope else PROJ_TILE
    width = MLA_HEADS * HEAD_PAD

    def row(i):
        return (i, 0)

    def lay3(i):
        return (layer, 0, 0)

    in_specs = [
        pl.BlockSpec((tm, D_MODEL), row),
        pl.BlockSpec((None, 8, N_MOD * D_MODEL), lay3),
        pl.BlockSpec((None, 1, D_MODEL), lay3),
        pl.BlockSpec((None, IN_COLS_PAD, D_MODEL), lay3),
        pl.BlockSpec((None, 1, Q_RANK), lay3),
        pl.BlockSpec((None, Q_RANK, width), lay3),
        pl.BlockSpec((None, 1, KV_RANK), lay3),
        pl.BlockSpec((None, KV_RANK, width), lay3),
        pl.BlockSpec((None, 1, LANES), lay3),
        pl.BlockSpec((None, 1, LANES), lay3),
        pl.BlockSpec((None, 1, LANES), lay3),
    ]
    args = [x, mod, wts["g_mix"], wts["w_in_t"], wts["g_qa"], wts["w_qb"], wts["g_kva"],
            wts["w_kvb"], wts["g_q"], wts["g_kn"], wts["g_kr"]]
    out_shape = [
        jax.ShapeDtypeStruct((m, RET_WIDTH), BF16),
        jax.ShapeDtypeStruct((m, RET_WIDTH), BF16),
        jax.ShapeDtypeStruct((m, RET_WIDTH), BF16),
        jax.ShapeDtypeStruct((m, RET_WIDTH), F32),
        jax.ShapeDtypeStruct((m, width), BF16),
        jax.ShapeDtypeStruct((m, width), BF16),
        jax.ShapeDtypeStruct((m, LANES), BF16),
    ]
    out_specs = [
        pl.BlockSpec((tm, RET_WIDTH), row), pl.BlockSpec((tm, RET_WIDTH), row),
        pl.BlockSpec((tm, RET_WIDTH), row), pl.BlockSpec((tm, RET_WIDTH), row),
        pl.BlockSpec((tm, width), row), pl.BlockSpec((tm, width), row),
        pl.BlockSpec((tm, LANES), row),
    ]
    aliases = {}
    if rope:
        n_lat = rope_tabs[0].shape[0]
        tiles = n_lat // tm
        in_specs += [pl.BlockSpec((None, wts["w_in_sw"].shape[1], D_MODEL), lay3),
                     pl.BlockSpec((None, Q_RANK, width), lay3),
                     pl.BlockSpec((None, 1, LANES), lay3), pl.BlockSpec((None, 1, LANES), lay3)]
        args += [wts["w_in_sw"], wts["w_qb_sw"], wts["g_q_sw"], wts["g_kr_sw"]]
        in_specs += [pl.BlockSpec((tm, LANES), lambda i: (i % tiles, 0))] * len(rope_tabs)
        args += list(rope_tabs)
    else:
        seq = caches[0].shape[2]
        assert seq == SUB_TILE
        nb = tm // seq
        if _is_array(caches[0]):
            in_specs += [pl.BlockSpec(memory_space=pl.ANY)] * 2
            aliases = {len(args): len(out_shape), len(args) + 1: len(out_shape) + 1}
            args += list(caches)
        out_shape += [jax.ShapeDtypeStruct(c.shape, c.dtype) for c in caches]
        out_specs += [_layer_spec(nb, c.shape, layer, bool(aliases)) for c in caches]

    return pl.pallas_call(
        functools.partial(_proj_kernel, rope=rope, cond_row=cond_row, n_alias=len(aliases),
                          layer=layer),
        grid=(m // tm,),
        in_specs=in_specs,
        out_specs=out_specs,
        out_shape=out_shape,
        input_output_aliases=aliases,
        compiler_params=pltpu.CompilerParams(
            dimension_semantics=("arbitrary",), vmem_limit_bytes=VMEM_LIMIT),
        name="proj_latent" if rope else "proj_context",
    )(*args)


def _log_gamma(p):
    return jnp.log1p(-jnp.exp2(-p))


def _decay_kernel(p_ref, dmask_ref, dvec_ref):
    c = MIX_CHUNK
    base = pl.program_id(0) * (2 * RET_HEADS)
    pair = pl.program_id(1)
    ri = lax.broadcasted_iota(jnp.int32, (c, c), 0)
    ci = lax.broadcasted_iota(jnp.int32, (c, c), 1)
    dif = (ri - ci).astype(F32)
    for e in range(2):
        h = 2 * pair + e
        lg_f = _log_gamma(jnp.full((c, c), p_ref[base + h], F32))
        lg_b = _log_gamma(jnp.full((c, c), p_ref[base + RET_HEADS + h], F32))
        fwd = jnp.where(dif >= 0, jnp.exp(jnp.maximum(dif, 0.0) * lg_f), 0.0)
        bwd = jnp.where(dif <= 0, jnp.exp(jnp.maximum(-dif, 0.0) * lg_b), 0.0)
        dmask_ref[e] = fwd + bwd
    lane = lax.broadcasted_iota(jnp.int32, (c, LANES), 1)
    rowf = lax.broadcasted_iota(jnp.int32, (c, LANES), 0).astype(F32)
    lo = lane < RET_DK
    lg_f = _log_gamma(jnp.where(lo, p_ref[base + 2 * pair], p_ref[base + 2 * pair + 1]))
    lg_b = _log_gamma(jnp.where(lo, p_ref[base + RET_HEADS + 2 * pair],
                                p_ref[base + RET_HEADS + 2 * pair + 1]))
    dvec_ref[Q_DEC_F] = jnp.exp((rowf + 1.0) * lg_f)
    dvec_ref[K_DEC_F] = jnp.exp((c - 1.0 - rowf) * lg_f)
    dvec_ref[Q_DEC_B] = jnp.exp((c - rowf) * lg_b)
    dvec_ref[K_DEC_B] = jnp.exp(rowf * lg_b)
    dvec_ref[C_DEC_F] = jnp.exp(c * lg_f)
    dvec_ref[C_DEC_B] = jnp.exp(c * lg_b)


def _decay_tables(decay_p):
    depth = decay_p.shape[0]
    c = MIX_CHUNK
    return pl.pallas_call(
        _decay_kernel,
        grid=(depth, N_PAIRS),
        in_specs=[pl.BlockSpec(memory_space=pltpu.SMEM)],
        out_specs=[
            pl.BlockSpec((None, 2, c, c), lambda l, p: (l, p, 0, 0)),
            pl.BlockSpec((None, None, N_DVEC, c, LANES), lambda l, p: (l, p, 0, 0, 0)),
        ],
        out_shape=[
            jax.ShapeDtypeStruct((depth, RET_HEADS, c, c), F32),
            jax.ShapeDtypeStruct((depth, N_PAIRS, N_DVEC, c, LANES), F32),
        ],
        compiler_params=pltpu.CompilerParams(
            dimension_semantics=("arbitrary", "arbitrary"), vmem_limit_bytes=VMEM_LIMIT),
        name="decay_tables",
    )(decay_p.reshape(-1))


def _mix_kernel(*refs, n_seq, n_sub, latent, n_alias, n_cast, layer):
    c = MIX_CHUNK
    nc = n_seq // c
    (rq_ref, rk_ref, rv_ref, rg_ref, qcat_ref, kvn_ref, krp_ref, gng_ref, gnb_ref,
     dmask_ref, dvec_ref) = refs[:11]
    if latent:
        kvc_ref, krc_ref, sf0_ref, sb0_ref, ret_ref, attn_ref, st_scr = refs[11:]
    else:
        n_in = 11 + n_alias
        cast_src = refs[n_in:n_in + n_cast]
        ret_ref, attn_ref, sf_ref, sb_ref = refs[n_in + n_cast:n_in + n_cast + 4]
        cast_dst = refs[n_in + n_cast + 4:]
        for src, dst in zip(cast_src, cast_dst):
            w = src.shape[1]
            dst[:, 0:w] = src[...].astype(BF16)
            if dst.shape[1] > w:
                dst[:, w:] = jnp.zeros((dst.shape[0], dst.shape[1] - w), BF16)

    lane = lax.broadcasted_iota(jnp.int32, (c, LANES), 1)
    lo = lane < RET_DK
    sq_r = lax.broadcasted_iota(jnp.int32, (LANES, LANES), 0)
    sq_c = lax.broadcasted_iota(jnp.int32, (LANES, LANES), 1)
    blockdiag = (sq_r < RET_DK) == (sq_c < RET_DK)

    def cols(j):
        return slice(j * LANES, (j + 1) * LANES)

    def state_update(pair, d, rows):
        kp = rk_ref[rows, cols(pair)]
        vp = rv_ref[rows, cols(pair)]
        kdt = (kp.astype(F32) * dvec_ref[pair, K_DEC_B if d else K_DEC_F]).T.astype(BF16)
        return jnp.where(blockdiag, _dot(kdt, vp), 0.0)

    nope_n = lax.broadcasted_iota(jnp.int32, (n_seq, LANES), 1) < MLA_NOPE
    if latent:
        nope_c = lax.broadcasted_iota(jnp.int32, (kvc_ref.shape[0], LANES), 1) < MLA_NOPE

    def states(sub):
        base = sub * n_seq
        for pair in range(N_PAIRS):
            for d in range(2):
                if latent:
                    s = (sb0_ref if d else sf0_ref)[pair]
                    cdec = dvec_ref[pair, C_DEC_B if d else C_DEC_F][0:LANES, :]
                    order = list(range(nc - 1, -1, -1)) if d else list(range(nc))
                    for idx, ch in enumerate(order):
                        st_scr[d, pair, ch] = s.astype(BF16)
                        if idx < nc - 1:
                            s = s * cdec + state_update(pair, d, pl.ds(base + ch * c, c))
                else:
                    s = state_update(pair, d, pl.ds(base, c))
                    st_ref = sb_ref if d else sf_ref
                    fresh = n_alias == 0
                    _put_layer(st_ref, sub, (2 * pair,), layer, fresh, s[0:RET_DK, 0:RET_DK])
                    _put_layer(st_ref, sub, (2 * pair + 1,), layer, fresh, s[RET_DK:, RET_DK:])

    def chunk_body(ch, base):
        rows = pl.ds(pl.multiple_of(base + ch * c, c), c)
        keys = pl.ds(base, n_seq)

        for pair in range(N_PAIRS):
            qp = rq_ref[rows, cols(pair)]
            kp = rk_ref[rows, cols(pair)]
            vp = rv_ref[rows, cols(pair)]
            zero = jnp.zeros_like(qp)
            a0 = (_dot_nt(jnp.where(lo, qp, zero), kp) * dmask_ref[2 * pair]).astype(BF16)
            a1 = (_dot_nt(jnp.where(lo, zero, qp), kp) * dmask_ref[2 * pair + 1]).astype(BF16)
            tot = jnp.where(lo, _dot(a0, vp), _dot(a1, vp))
            if latent:
                tot = (tot + _dot(qp, st_scr[0, pair, ch]) * dvec_ref[pair, Q_DEC_F]
                       + _dot(qp, st_scr[1, pair, ch]) * dvec_ref[pair, Q_DEC_B])
            inv = 1.0 / RET_DK
            m0 = jnp.sum(jnp.where(lo, tot, 0.0), axis=-1, keepdims=True) * inv
            m1 = jnp.sum(jnp.where(lo, 0.0, tot), axis=-1, keepdims=True) * inv
            y = tot - jnp.where(lo, m0, m1)
            v0 = jnp.sum(jnp.where(lo, y * y, 0.0), axis=-1, keepdims=True) * inv
            v1 = jnp.sum(jnp.where(lo, 0.0, y * y), axis=-1, keepdims=True) * inv
            yn = (y * lax.rsqrt(jnp.where(lo, v0, v1) + EPS) * gng_ref[:, cols(pair)]
                  + gnb_ref[:, cols(pair)])
            ret_ref[rows, cols(pair)] = (yn * _silu(rg_ref[rows, cols(pair)])).astype(BF16)

        krp = krp_ref[keys, :]
        for pair in range(N_PAIRS):
            outs = []
            for e in range(2):
                h = 2 * pair + e
                qc = qcat_ref[rows, cols(h)]
                kv = kvn_ref[keys, cols(h)]
                s = _dot_nt(qc, jnp.where(nope_n, kv, krp))
                m = jnp.max(s, axis=-1, keepdims=True)
                if latent:
                    kv_c = kvc_ref[:, cols(h)]
                    s2 = _dot_nt(qc, jnp.where(nope_c, kv_c, krc_ref[...]))
                    m = jnp.maximum(m, jnp.max(s2, axis=-1, keepdims=True))
                    p2 = jnp.exp2(s2 - m)
                p = jnp.exp2(s - m)
                den = jnp.sum(p, axis=-1, keepdims=True)
                acc = _dot(p.astype(BF16), kv)
                if latent:
                    den = den + jnp.sum(p2, axis=-1, keepdims=True)
                    acc = acc + _dot(p2.astype(BF16), kv_c)
                outs.append(acc / den)
            attn_ref[rows, cols(pair)] = jnp.where(
                lo, pltpu.roll(outs[0], MLA_V, 1), outs[1]).astype(BF16)
        return base

    for sub in range(n_sub):
        states(sub)
    for sub in range(n_sub):
        if nc == 1:
            chunk_body(0, sub * n_seq)
        else:
            lax.fori_loop(0, nc, chunk_body, sub * n_seq)


def _mix(proj_out, dmask, dvec, gn_g, gn_b, layer, n_batch, n_seq, latent_in, states_out,
         casts=()):
    rq, rk, rv, rg, qcat, kvn, krp = proj_out[:7]
    latent = latent_in is not None
    aliases = {}
    m = n_batch * n_seq
    c = MIX_CHUNK
    width = MLA_HEADS * HEAD_PAD

    def row(b):
        return (b, 0)

    def lay3(b):
        return (layer, 0, 0)

    once = pl.Buffered(1)
    n_sub = 1 if latent else MIX_SEQS
    n_steps = n_batch // n_sub
    blk = n_sub * n_seq
    in_specs = [
        pl.BlockSpec((blk, RET_WIDTH), row), pl.BlockSpec((blk, RET_WIDTH), row),
        pl.BlockSpec((blk, RET_WIDTH), row), pl.BlockSpec((blk, RET_WIDTH), row),
        pl.BlockSpec((blk, width), row), pl.BlockSpec((blk, width), row),
        pl.BlockSpec((blk, LANES), row),
        pl.BlockSpec((None, 1, RET_WIDTH), lay3), pl.BlockSpec((None, 1, RET_WIDTH), lay3),
        pl.BlockSpec((None, RET_HEADS, c, c), lambda b: (layer, 0, 0, 0), pipeline_mode=once),
        pl.BlockSpec((None, N_PAIRS, N_DVEC, c, LANES), lambda b: (layer, 0, 0, 0, 0),
                     pipeline_mode=once),
    ]
    args = [rq, rk, rv, rg, qcat, kvn, krp, gn_g, gn_b, dmask, dvec]
    out_shape = [jax.ShapeDtypeStruct((m, RET_WIDTH), BF16),
                 jax.ShapeDtypeStruct((m, MLA_HEADS * MLA_V), BF16)]
    out_specs = [pl.BlockSpec((blk, RET_WIDTH), row), pl.BlockSpec((blk, RET_WIDTH), row)]
    scratch = []
    if latent:
        kvn_c, krp_c, s_f0, s_b0 = latent_in
        past = kvn_c.shape[2]
        st_spec = pl.BlockSpec((None, None, N_PAIRS, LANES, LANES), lambda b: (b, layer, 0, 0, 0))
        in_specs += [
            pl.BlockSpec((None, None, past, width), lambda b: (layer, b, 0, 0)),
            pl.BlockSpec((None, None, past, LANES), lambda b: (b, layer, 0, 0)),
            st_spec, st_spec,
        ]
        args += [kvn_c, krp_c, s_f0, s_b0]
        scratch = [pltpu.VMEM((2, N_PAIRS, n_seq // c, LANES, LANES), BF16)]
    else:
        if _is_array(states_out[0]):
            in_specs += [pl.BlockSpec(memory_space=pl.ANY)] * 2
            aliases = {len(args): len(out_shape), len(args) + 1: len(out_shape) + 1}
            args += list(states_out)
        out_shape += [jax.ShapeDtypeStruct(s.shape, s.dtype) for s in states_out]
        out_specs += [_layer_spec(n_sub, s.shape, layer, bool(aliases)) for s in states_out]
        for w, w_layer, out_cols, stride in casts:
            blocks = n_steps // stride
            rows = w.shape[1] // blocks
            assert rows * blocks == w.shape[1] and rows % 16 == 0 and stride * blocks == n_steps
            in_specs.append(pl.BlockSpec(
                (None, rows, w.shape[2]),
                lambda b, w_layer=w_layer, stride=stride: (w_layer, b // stride, 0)))
            args.append(w)
            out_shape.append(jax.ShapeDtypeStruct((w.shape[1], out_cols), BF16))
            out_specs.append(pl.BlockSpec((rows, out_cols),
                                          lambda b, stride=stride: (b // stride, 0)))

    return pl.pallas_call(
        functools.partial(_mix_kernel, n_seq=n_seq, n_sub=n_sub, latent=latent, layer=layer,
                          n_alias=len(aliases), n_cast=len(casts)),
        grid=(n_steps,),
        in_specs=in_specs,
        out_specs=out_specs,
        out_shape=out_shape,
        input_output_aliases=aliases,
        scratch_shapes=scratch,
        compiler_params=pltpu.CompilerParams(
            dimension_semantics=("arbitrary",), vmem_limit_bytes=VMEM_LIMIT),
        name="mix_latent" if latent else "mix_context",
    )(*args)


def _out_kernel(x_ref, ret_ref, attn_ref, mod_ref, gffn_ref, wo_ref, wfi_ref, wfo_ref, o_ref,
                act_scr, *, cond_row):
    cond = cond_row(pl.program_id(0) * x_ref.shape[0])
    gt1 = _mod_vec(mod_ref, cond, 2)
    sh2 = _mod_vec(mod_ref, cond, 3)
    sc2 = _mod_vec(mod_ref, cond, 4)
    gt2 = _mod_vec(mod_ref, cond, 5)
    mixed = _dot(ret_ref[...], wo_ref[0:RET_WIDTH, :]) + _dot(attn_ref[...], wo_ref[RET_WIDTH:, :])
    x1 = x_ref[...] + gt1 * mixed
    h = (_rms(x1, gffn_ref[...]) * (1.0 + sc2) + sh2).astype(BF16)
    for c0 in range(0, D_FF, FF_CHUNK):
        cw = min(FF_CHUNK, D_FF - c0)
        gate = _dot(h, wfi_ref[:, c0:c0 + cw])
        up = _dot(h, wfi_ref[:, D_FF + c0:D_FF + c0 + cw])
        act_scr[:, c0:c0 + cw] = (_silu(gate) * up).astype(BF16)
    o_ref[...] = x1 + gt2 * _dot(act_scr[...], wfo_ref[...])


def _out(x, ret, attn, mod, layer, cond_row, wts, big_w):
    m = x.shape[0]
    tm = ROW_TILE

    def row(i):
        return (i, 0)

    def lay3(i):
        return (layer, 0, 0)

    def whole(i):
        return (0, 0)

    once = pl.Buffered(1)
    return pl.pallas_call(
        functools.partial(_out_kernel, cond_row=cond_row),
        grid=(m // tm,),
        in_specs=[
            pl.BlockSpec((tm, D_MODEL), row),
            pl.BlockSpec((tm, RET_WIDTH), row),
            pl.BlockSpec((tm, RET_WIDTH), row),
            pl.BlockSpec((None, 8, N_MOD * D_MODEL), lay3),
            pl.BlockSpec((None, 1, D_MODEL), lay3),
            pl.BlockSpec((D_MODEL, D_MODEL), whole, pipeline_mode=once),
            pl.BlockSpec((D_MODEL, 2 * D_FF), whole, pipeline_mode=once),
            pl.BlockSpec((D_FF, D_MODEL), whole, pipeline_mode=once),
        ],
        out_specs=pl.BlockSpec((tm, D_MODEL), row),
        out_shape=jax.ShapeDtypeStruct((m, D_MODEL), F32),
        scratch_shapes=[pltpu.VMEM((tm, D_FF), BF16)],
        compiler_params=pltpu.CompilerParams(
            dimension_semantics=("arbitrary",), vmem_limit_bytes=VMEM_LIMIT),
        name="out_ffn",
    )(x, ret, attn, mod, wts["g_ffn"], *big_w)


def _rope_tables(n_lat):
    pos = np.arange(n_lat)
    row = (pos // GRID_W).astype(np.float32)[:, None]
    col = (pos % GRID_W).astype(np.float32)[:, None]
    lane = np.arange(LANES)[None, :]

    def tables(d, start, period):
        rel = (lane - start) % period
        active = np.logical_and(lane >= start, rel < d)
        half = d // 2
        nf = half // 2
        inv = np.float32(ROPE_BASE) ** (-((rel % nf).astype(np.float32)) / np.float32(nf))
        ang = (np.where(rel < half, row, col) * inv).astype(np.float32)
        cos, sin = np.cos(ang), np.sin(ang)
        first = (rel % half) < nf
        c = np.where(active, cos, 1.0)
        s = np.where(active, np.where(first, -sin, sin), 0.0)
        return tuple(jnp.asarray(t, F32) for t in (c, s))

    return tables(RET_DK, 0, RET_DK) + tables(MLA_ROPE, MLA_NOPE, LANES)


def _swap_partners(a, axis, d):
    shape = a.shape
    split = shape[:axis] + (shape[axis] // d, 2, 2, d // 4) + shape[axis + 1:]
    return jnp.flip(a.reshape(split), axis=axis + 2).reshape(shape)


def _prepare_weights(g_norm_mix, g_norm_ffn, w_in, g_q_a, w_q_b, g_kv_a, w_kv_b, g_qn, g_qr, g_kn,
                     g_kr):
    depth = w_in.shape[0]
    n_in = w_in.shape[2]
    wt = jnp.swapaxes(w_in, 1, 2)
    w_in_t = jnp.pad(wt.astype(BF16), ((0, 0), (0, IN_COLS_PAD - n_in), (0, 0)))
    w_in_sw = jnp.concatenate([
        _swap_partners(wt[:, 0:RET_WIDTH], 1, RET_DK),
        _swap_partners(wt[:, RET_WIDTH:2 * RET_WIDTH], 1, RET_DK),
        _swap_partners(wt[:, n_in - MLA_ROPE:], 1, MLA_ROPE),
        jnp.zeros((depth, LANES - MLA_ROPE, D_MODEL), F32)], axis=1).astype(BF16)
    w_qb = w_q_b.reshape(depth, Q_RANK, MLA_HEADS, MLA_NOPE + MLA_ROPE)
    w_qb_sw = jnp.pad(_swap_partners(w_qb[..., MLA_NOPE:], 3, MLA_ROPE),
                      ((0, 0), (0, 0), (0, 0), (MLA_NOPE, HEAD_PAD - MLA_NOPE - MLA_ROPE)))
    w_qb = jnp.pad(w_qb, ((0, 0), (0, 0), (0, 0), (0, HEAD_PAD - MLA_NOPE - MLA_ROPE)))
    zeros32 = jnp.zeros((depth, MLA_ROPE), F32)
    zeros64 = jnp.zeros((depth, MLA_NOPE), F32)
    return {
        "g_mix": g_norm_mix[:, None, :],
        "g_ffn": g_norm_ffn[:, None, :],
        "w_in_t": w_in_t,
        "w_in_sw": w_in_sw,
        "g_qa": g_q_a[:, None, :],
        "w_qb": w_qb.reshape(depth, Q_RANK, MLA_HEADS * HEAD_PAD).astype(BF16),
        "g_kva": g_kv_a[:, None, :],
        "w_kvb": w_kv_b.astype(BF16),
        "g_q": (jnp.concatenate([g_qn, g_qr, zeros32], axis=-1) * Q_FOLD)[:, None, :],
        "g_kn": jnp.concatenate([g_kn, jnp.ones((depth, MLA_V), F32)], axis=-1)[:, None, :],
        "g_kr": jnp.pad(g_kr, ((0, 0), (0, LANES - MLA_ROPE)))[:, None, :],
        "w_qb_sw": w_qb_sw.reshape(depth, Q_RANK, MLA_HEADS * HEAD_PAD).astype(BF16),
        "g_q_sw": (jnp.concatenate([zeros64, _swap_partners(g_qr, 1, MLA_ROPE), zeros32], axis=-1)
                   * Q_FOLD)[:, None, :],
        "g_kr_sw": jnp.pad(_swap_partners(g_kr, 1, MLA_ROPE),
                           ((0, 0), (0, LANES - MLA_ROPE)))[:, None, :],
    }


def _blockdiag_states(s):
    b, l = s.shape[:2]
    s = s.reshape(b, l, N_PAIRS, 2, RET_DK, RET_DK)
    z = jnp.zeros_like(s[:, :, :, 0])
    top = jnp.concatenate([s[:, :, :, 0], z], axis=-1)
    bot = jnp.concatenate([z, s[:, :, :, 1]], axis=-1)
    return jnp.concatenate([top, bot], axis=-2)


def kernel(x_prompt, x_sample, cache_ckv, cache_krope, state_ret_fwd, state_ret_bwd, c, c_ctx,
           w_mod, b_mod, g_norm_mix, g_norm_ffn, w_in, g_q_a, w_q_b, g_kv_a, w_kv_b,
           g_qn, g_qr, g_kn, g_kr, ret_p_fwd, ret_p_bwd, g_ret_gn, b_ret_gn, w_o,
           w_ffn_in, w_ffn_out):
    batch, seq, _ = x_prompt.shape
    dec_batch, dec_seq, _ = x_sample.shape
    depth = w_in.shape[0]

    wts = _prepare_weights(g_norm_mix, g_norm_ffn, w_in, g_q_a, w_q_b, g_kv_a, w_kv_b, g_qn, g_qr,
                           g_kn, g_kr)
    conds = jnp.concatenate([c_ctx[None], c, jnp.zeros((8 - 1 - dec_batch, D_MODEL), F32)], axis=0)
    mod = _modulation(conds, w_mod, b_mod)
    dmask, dvec = _decay_tables(jnp.stack([ret_p_fwd, ret_p_bwd], axis=1))
    gn_g = g_ret_gn[:, None, :]
    gn_b = b_ret_gn[:, None, :]

    rope_tabs = _rope_tables(dec_seq)
    kvn_cache = _cache_up(cache_ckv, wts["w_kvb"], wts["g_kn"])
    krp_cache = jnp.pad(cache_krope, ((0, 0), (0, 0), (0, 0), (MLA_NOPE, LANES - MLA_NOPE - MLA_ROPE)))
    krp_cache = krp_cache.astype(BF16)
    s_f0 = _blockdiag_states(state_ret_fwd)
    s_b0 = _blockdiag_states(state_ret_bwd)

    def ctx_row(row):
        return 0

    def lat_row(row):
        return 1 + row // dec_seq

    x = x_prompt.reshape(batch * seq, D_MODEL)
    y = x_sample.reshape(dec_batch * dec_seq, D_MODEL)
    caches = (jax.ShapeDtypeStruct((batch, depth, seq, KV_RANK), F32),
              jax.ShapeDtypeStruct((batch, depth, MLA_ROPE, seq), F32))
    states = (jax.ShapeDtypeStruct((batch, depth, RET_HEADS, RET_DK, RET_DK), F32),) * 2
    for l in range(depth):
        pr = _proj(x, mod, l, ctx_row, wts, None, caches)
        caches = tuple(pr[7:9])
        casts = [(w_o, l, D_MODEL, 1), (w_ffn_in, l, 2 * D_FF, 1), (w_ffn_out, l, D_MODEL, 2)]
        mixed = _mix(pr, dmask, dvec, gn_g, gn_b, l, batch, seq, None, states, casts)
        ret, attn = mixed[:2]
        states = tuple(mixed[2:4])
        big_w = tuple(mixed[4:7])
        x = _out(x, ret, attn, mod, l, ctx_row, wts, big_w)

        pr = _proj(y, mod, l, lat_row, wts, rope_tabs, None)
        ret, attn = _mix(pr, dmask, dvec, gn_g, gn_b, l, dec_batch, dec_seq,
                         (kvn_cache, krp_cache, s_f0, s_b0), None)
        y = _out(y, ret, attn, mod, l, lat_row, wts, big_w)

    return (x.reshape(batch, seq, D_MODEL), y.reshape(dec_batch, dec_seq, D_MODEL),
            caches[0], jnp.swapaxes(caches[1], 2, 3), states[0], states[1])
```

```python
import functools

import jax
import jax.numpy as jnp
import numpy as np
from jax import lax
from jax.experimental import pallas as pl
from jax.experimental.pallas import tpu as pltpu

D_MODEL = 1024
N_MOD = 6
RET_HEADS = 8
RET_DK = 64
RET_WIDTH = 512
MLA_HEADS = 8
MLA_NOPE = 64
MLA_ROPE = 32
MLA_V = 64
Q_RANK = 256
KV_RANK = 128
D_FF = 2816
GRID_W = 64
ROPE_BASE = 10000.0
EPS = 1e-6

LANES = 128
HEAD_PAD = LANES
N_PAIRS = RET_HEADS // 2
IN_COLS_PAD = 4 * RET_WIDTH + Q_RANK + KV_RANK + LANES
ROW_TILE = 512
PROJ_TILE = 1024
LATENT_PROJ_TILE = 512
SUB_TILE = 256
FF_CHUNK = 256
MIX_CHUNK = 256
MIX_SEQS = 4
Q_DEC_F, K_DEC_F, Q_DEC_B, K_DEC_B, C_DEC_F, C_DEC_B = range(6)
N_DVEC = 6
Q_FOLD = (MLA_NOPE + MLA_ROPE) ** -0.5 * 1.4426950408889634
VMEM_LIMIT = 56 * 1024 * 1024

BF16 = jnp.bfloat16
F32 = jnp.float32
_NT = (((1,), (1,)), ((), ()))


def _dot(a, b):
    return jnp.dot(a, b, preferred_element_type=F32)


def _dot_nt(a, b):
    return lax.dot_general(a, b, _NT, preferred_element_type=F32)


def _rms(x, g):
    return x * lax.rsqrt(jnp.mean(x * x, axis=-1, keepdims=True) + EPS) * g


def _silu(x):
    return x * jax.nn.sigmoid(x)


def _masked_mean_sq(x, mask, n):
    return jnp.sum(jnp.where(mask, x * x, 0.0), axis=-1, keepdims=True) * (1.0 / n)


def _mod_kernel(c_ref, w_ref, b_ref, o_ref):
    a = _silu(c_ref[...])
    w = w_ref[...]
    a_hi = a.astype(BF16)
    a_lo = (a - a_hi.astype(F32)).astype(BF16)
    w_hi = w.astype(BF16)
    w_lo = (w - w_hi.astype(F32)).astype(BF16)
    both = _dot(jnp.concatenate([a_hi, a_lo], axis=0), w_hi)
    o_ref[...] = both[0:8] + both[8:16] + _dot(a_hi, w_lo) + b_ref[...]


def _modulation(conds, w_mod, b_mod):
    depth, _, n = w_mod.shape
    tn = 1536
    return pl.pallas_call(
        _mod_kernel,
        grid=(depth, n // tn),
        in_specs=[
            pl.BlockSpec((8, D_MODEL), lambda l, j: (0, 0)),
            pl.BlockSpec((None, D_MODEL, tn), lambda l, j: (l, 0, j)),
            pl.BlockSpec((None, 1, tn), lambda l, j: (l, 0, j)),
        ],
        out_specs=pl.BlockSpec((None, 8, tn), lambda l, j: (l, 0, j)),
        out_shape=jax.ShapeDtypeStruct((depth, 8, n), F32),
        compiler_params=pltpu.CompilerParams(
            dimension_semantics=("arbitrary", "arbitrary"), vmem_limit_bytes=VMEM_LIMIT),
        name="modulation",
    )(conds, w_mod, b_mod.reshape(depth, 1, n))


def _norm_kn(kv, gkn):
    lane = lax.broadcasted_iota(jnp.int32, (kv.shape[0], LANES), 1)
    lo = lane < MLA_NOPE
    out = []
    for h in range(MLA_HEADS):
        kvh = kv[:, h * HEAD_PAD:(h + 1) * HEAD_PAD]
        rs = lax.rsqrt(_masked_mean_sq(kvh, lo, MLA_NOPE) + EPS)
        out.append((kvh * jnp.where(lo, rs * gkn, 1.0)).astype(BF16))
    return out


def _cache_kernel(ckv_ref, wkvb_ref, gkn_ref, kvn_ref):
    kv = _dot(ckv_ref[...].astype(BF16), wkvb_ref[...])
    for h, kvh in enumerate(_norm_kn(kv, gkn_ref[...])):
        kvn_ref[:, h * HEAD_PAD:(h + 1) * HEAD_PAD] = kvh


def _cache_up(cache_ckv, wkvb, gkn):
    nb, depth, past, _ = cache_ckv.shape
    width = MLA_HEADS * HEAD_PAD
    return pl.pallas_call(
        _cache_kernel,
        grid=(depth, nb),
        in_specs=[
            pl.BlockSpec((None, None, past, KV_RANK), lambda l, b: (b, l, 0, 0)),
            pl.BlockSpec((None, KV_RANK, width), lambda l, b: (l, 0, 0)),
            pl.BlockSpec((None, 1, LANES), lambda l, b: (l, 0, 0)),
        ],
        out_specs=pl.BlockSpec((None, None, past, width), lambda l, b: (l, b, 0, 0)),
        out_shape=jax.ShapeDtypeStruct((depth, nb, past, width), BF16),
        compiler_params=pltpu.CompilerParams(
            dimension_semantics=("arbitrary", "arbitrary"), vmem_limit_bytes=VMEM_LIMIT),
        name="cache_up",
    )(cache_ckv, wkvb, gkn)


def _mod_vec(mod_ref, cond, k):
    return mod_ref[pl.ds(cond, 1), k * D_MODEL:(k + 1) * D_MODEL]


def _is_array(x):
    return not isinstance(x, jax.ShapeDtypeStruct)


def _layer_spec(n, shape, layer, aliased):
    rest = tuple(shape[2:])
    zeros = (0,) * len(rest)
    if aliased:
        return pl.BlockSpec((n, None) + rest, lambda i: (i, layer) + zeros)
    return pl.BlockSpec((n, shape[1]) + rest, lambda i: (i, 0) + zeros)


def _put_layer(ref, i, tail, layer, fresh, value):
    if not fresh:
        ref[(i,) + tail] = value
        return
    for l in range(ref.shape[1]):
        ref[(i, l) + tail] = value if l == layer else jnp.zeros_like(value)


def _proj_kernel(*refs, rope, cond_row, n_alias, layer):
    (x_ref, mod_ref, gmix_ref, win_ref, gqa_ref, wqb_ref, gkva_ref, wkvb_ref,
     gq_ref, gkn_ref, gkr_ref) = refs[:11]
    if rope:
        wqb_sw_ref, gq_sw_ref, c64_ref, s64_ref, c32_ref, s32_ref = refs[11:17]
        rq_ref, rk_ref, rv_ref, rg_ref, qcat_ref, kvn_ref, krp_ref = refs[17:]
    else:
        (rq_ref, rk_ref, rv_ref, rg_ref, qcat_ref, kvn_ref, krp_ref, ckv_ref,
         kro_ref) = refs[11 + n_alias:]

    cond = cond_row(pl.program_id(0) * x_ref.shape[0])
    sh1 = _mod_vec(mod_ref, cond, 0)
    sc1 = _mod_vec(mod_ref, cond, 1)
    w = RET_WIDTH
    lane = lax.broadcasted_iota(jnp.int32, (SUB_TILE, LANES), 1)
    nope = lane < MLA_NOPE
    is_rope = jnp.logical_and(lane >= MLA_NOPE, lane < MLA_NOPE + MLA_ROPE)

    for r in range(x_ref.shape[0] // SUB_TILE):
        rs = slice(r * SUB_TILE, (r + 1) * SUB_TILE)
        h = (_rms(x_ref[rs, :], gmix_ref[...]) * (1.0 + sc1) + sh1).astype(BF16)
        z = _dot_nt(h, win_ref[...])
        if rope:
            grp = RET_DK // 4
            z_sw = _dot_nt(h, jnp.concatenate(
                [win_ref[(g ^ 1) * grp:((g ^ 1) + 1) * grp, :] for g in range(2 * w // grp)], 0))

        for j in range(w // LANES):
            sl = slice(j * LANES, (j + 1) * LANES)
            q = z[:, j * LANES:(j + 1) * LANES]
            k = z[:, w + j * LANES:w + (j + 1) * LANES] * (RET_DK ** -0.5)
            if rope:
                q_sw = z_sw[:, j * LANES:(j + 1) * LANES]
                k_sw = z_sw[:, w + j * LANES:w + (j + 1) * LANES] * (RET_DK ** -0.5)
                q = q * c64_ref[rs, :] + q_sw * s64_ref[rs, :]
                k = k * c64_ref[rs, :] + k_sw * s64_ref[rs, :]
            rq_ref[rs, sl] = q.astype(BF16)
            rk_ref[rs, sl] = k.astype(BF16)
        rv_ref[rs, :] = z[:, 2 * w:3 * w].astype(BF16)
        rg_ref[rs, :] = z[:, 3 * w:4 * w]

        o = 4 * w
        qa = z[:, o:o + Q_RANK]
        kva = z[:, o + Q_RANK:o + Q_RANK + KV_RANK]
        kr2 = z[:, o + Q_RANK + KV_RANK:]

        qa_n = _rms(qa, gqa_ref[...]).astype(BF16)
        q = _dot(qa_n, wqb_ref[...])
        if rope:
            q_sw = _dot(qa_n, wqb_sw_ref[...])
        for hh in range(MLA_HEADS):
            hs = slice(hh * HEAD_PAD, (hh + 1) * HEAD_PAD)
            qh = q[:, hs]
            rs_n = lax.rsqrt(_masked_mean_sq(qh, nope, MLA_NOPE) + EPS)
            rs_r = lax.rsqrt(_masked_mean_sq(qh, is_rope, MLA_ROPE) + EPS)
            qn = qh * jnp.where(nope, rs_n, rs_r) * gq_ref[...]
            if rope:
                qn = qn * c32_ref[rs, :] + q_sw[:, hs] * rs_r * gq_sw_ref[...] * s32_ref[rs, :]
            qcat_ref[rs, hs] = qn.astype(BF16)

        ckv = _rms(kva, gkva_ref[...])
        kv = _dot(ckv.astype(BF16), wkvb_ref[...])
        for hh, kvh in enumerate(_norm_kn(kv, gkn_ref[...])):
            kvn_ref[rs, hh * HEAD_PAD:(hh + 1) * HEAD_PAD] = kvh

        rs_k = lax.rsqrt(_masked_mean_sq(kr2, lane < MLA_ROPE, MLA_ROPE) + EPS)
        krn = kr2 * rs_k * gkr_ref[...]
        krp = pltpu.roll(krn, MLA_NOPE, 1)
        if rope:
            x1_pos = ((lane - MLA_NOPE) % (MLA_ROPE // 2)) < MLA_ROPE // 4
            partner = jnp.where(x1_pos, pltpu.roll(krp, LANES - MLA_ROPE // 4, 1),
                                pltpu.roll(krp, MLA_ROPE // 4, 1))
            krp = krp * c32_ref[rs, :] + partner * s32_ref[rs, :]
        else:
            _put_layer(ckv_ref, r, (), layer, n_alias == 0, ckv)
            _put_layer(kro_ref, r, (), layer, n_alias == 0, krn.T[0:MLA_ROPE, :])
        krp_ref[rs, :] = krp.astype(BF16)


def _proj(x, mod, layer, cond_row, wts, rope_tabs, caches):
    m = x.shape[0]
    rope = rope_tabs is not None
    tm = LATENT_PROJ_TILE if rope else PROJ_TILE
    width = MLA_HEADS * HEAD_PAD

    def row(i):
        return (i, 0)

    def lay3(i):
        return (layer, 0, 0)

    in_specs = [
        pl.BlockSpec((tm, D_MODEL), row),
        pl.BlockSpec((None, 8, N_MOD * D_MODEL), lay3),
        pl.BlockSpec((None, 1, D_MODEL), lay3),
        pl.BlockSpec((None, IN_COLS_PAD, D_MODEL), lay3),
        pl.BlockSpec((None, 1, Q_RANK), lay3),
        pl.BlockSpec((None, Q_RANK, width), lay3),
        pl.BlockSpec((None, 1, KV_RANK), lay3),
        pl.BlockSpec((None, KV_RANK, width), lay3),
        pl.BlockSpec((None, 1, LANES), lay3),
        pl.BlockSpec((None, 1, LANES), lay3),
        pl.BlockSpec((None, 1, LANES), lay3),
    ]
    args = [x, mod, wts["g_mix"], wts["w_in_t"], wts["g_qa"], wts["w_qb"], wts["g_kva"],
            wts["w_kvb"], wts["g_q"], wts["g_kn"], wts["g_kr"]]
    out_shape = [
        jax.ShapeDtypeStruct((m, RET_WIDTH), BF16),
        jax.ShapeDtypeStruct((m, RET_WIDTH), BF16),
        jax.ShapeDtypeStruct((m, RET_WIDTH), BF16),
        jax.ShapeDtypeStruct((m, RET_WIDTH), F32),
        jax.ShapeDtypeStruct((m, width), BF16),
        jax.ShapeDtypeStruct((m, width), BF16),
        jax.ShapeDtypeStruct((m, LANES), BF16),
    ]
    out_specs = [
        pl.BlockSpec((tm, RET_WIDTH), row), pl.BlockSpec((tm, RET_WIDTH), row),
        pl.BlockSpec((tm, RET_WIDTH), row), pl.BlockSpec((tm, RET_WIDTH), row),
        pl.BlockSpec((tm, width), row), pl.BlockSpec((tm, width), row),
        pl.BlockSpec((tm, LANES), row),
    ]
    aliases = {}
    if rope:
        n_lat = rope_tabs[0].shape[0]
        tiles = n_lat // tm
        in_specs += [pl.BlockSpec((None, Q_RANK, width), lay3),
                     pl.BlockSpec((None, 1, LANES), lay3)]
        args += [wts["w_qb_sw"], wts["g_q_sw"]]
        in_specs += [pl.BlockSpec((tm, LANES), lambda i: (i % tiles, 0))] * len(rope_tabs)
        args += list(rope_tabs)
    else:
        seq = caches[0].shape[2]
        assert seq == SUB_TILE
        nb = tm // seq
        if _is_array(caches[0]):
            in_specs += [pl.BlockSpec(memory_space=pl.ANY)] * 2
            aliases = {len(args): len(out_shape), len(args) + 1: len(out_shape) + 1}
            args += list(caches)
        out_shape += [jax.ShapeDtypeStruct(c.shape, c.dtype) for c in caches]
        out_specs += [_layer_spec(nb, c.shape, layer, bool(aliases)) for c in caches]

    return pl.pallas_call(
        functools.partial(_proj_kernel, rope=rope, cond_row=cond_row, n_alias=len(aliases),
                          layer=layer),
        grid=(m // tm,),
        in_specs=in_specs,
        out_specs=out_specs,
        out_shape=out_shape,
        input_output_aliases=aliases,
        compiler_params=pltpu.CompilerParams(
            dimension_semantics=("arbitrary",), vmem_limit_bytes=VMEM_LIMIT),
        name="proj_latent" if rope else "proj_context",
    )(*args)


def _log_gamma(p):
    return jnp.log1p(-jnp.exp2(-p))


def _decay_kernel(p_ref, dmask_ref, dvec_ref):
    c = MIX_CHUNK
    base = pl.program_id(0) * (2 * RET_HEADS)
    pair = pl.program_id(1)
    ri = lax.broadcasted_iota(jnp.int32, (c, c), 0)
    ci = lax.broadcasted_iota(jnp.int32, (c, c), 1)
    dif = (ri - ci).astype(F32)
    for e in range(2):
        h = 2 * pair + e
        lg_f = _log_gamma(jnp.full((c, c), p_ref[base + h], F32))
        lg_b = _log_gamma(jnp.full((c, c), p_ref[base + RET_HEADS + h], F32))
        fwd = jnp.where(dif >= 0, jnp.exp(jnp.maximum(dif, 0.0) * lg_f), 0.0)
        bwd = jnp.where(dif <= 0, jnp.exp(jnp.maximum(-dif, 0.0) * lg_b), 0.0)
        dmask_ref[e] = fwd + bwd
    lane = lax.broadcasted_iota(jnp.int32, (c, LANES), 1)
    rowf = lax.broadcasted_iota(jnp.int32, (c, LANES), 0).astype(F32)
    lo = lane < RET_DK
    lg_f = _log_gamma(jnp.where(lo, p_ref[base + 2 * pair], p_ref[base + 2 * pair + 1]))
    lg_b = _log_gamma(jnp.where(lo, p_ref[base + RET_HEADS + 2 * pair],
                                p_ref[base + RET_HEADS + 2 * pair + 1]))
    dvec_ref[Q_DEC_F] = jnp.exp((rowf + 1.0) * lg_f)
    dvec_ref[K_DEC_F] = jnp.exp((c - 1.0 - rowf) * lg_f)
    dvec_ref[Q_DEC_B] = jnp.exp((c - rowf) * lg_b)
    dvec_ref[K_DEC_B] = jnp.exp(rowf * lg_b)
    dvec_ref[C_DEC_F] = jnp.exp(c * lg_f)
    dvec_ref[C_DEC_B] = jnp.exp(c * lg_b)


def _decay_tables(decay_p):
    depth = decay_p.shape[0]
    c = MIX_CHUNK
    return pl.pallas_call(
        _decay_kernel,
        grid=(depth, N_PAIRS),
        in_specs=[pl.BlockSpec(memory_space=pltpu.SMEM)],
        out_specs=[
            pl.BlockSpec((None, 2, c, c), lambda l, p: (l, p, 0, 0)),
            pl.BlockSpec((None, None, N_DVEC, c, LANES), lambda l, p: (l, p, 0, 0, 0)),
        ],
        out_shape=[
            jax.ShapeDtypeStruct((depth, RET_HEADS, c, c), F32),
            jax.ShapeDtypeStruct((depth, N_PAIRS, N_DVEC, c, LANES), F32),
        ],
        compiler_params=pltpu.CompilerParams(
            dimension_semantics=("arbitrary", "arbitrary"), vmem_limit_bytes=VMEM_LIMIT),
        name="decay_tables",
    )(decay_p.reshape(-1))


def _mix_kernel(*refs, n_seq, n_sub, latent, n_alias, n_cast, layer):
    c = MIX_CHUNK
    nc = n_seq // c
    (rq_ref, rk_ref, rv_ref, rg_ref, qcat_ref, kvn_ref, krp_ref, gng_ref, gnb_ref,
     dmask_ref, dvec_ref) = refs[:11]
    if latent:
        kvc_ref, krc_ref, sf0_ref, sb0_ref, ret_ref, attn_ref, st_scr = refs[11:]
    else:
        n_in = 11 + n_alias
        cast_src = refs[n_in:n_in + n_cast]
        ret_ref, attn_ref, sf_ref, sb_ref = refs[n_in + n_cast:n_in + n_cast + 4]
        cast_dst = refs[n_in + n_cast + 4:]
        for src, dst in zip(cast_src, cast_dst):
            w = src.shape[1]
            dst[:, 0:w] = src[...].astype(BF16)
            if dst.shape[1] > w:
                dst[:, w:] = jnp.zeros((dst.shape[0], dst.shape[1] - w), BF16)

    lane = lax.broadcasted_iota(jnp.int32, (c, LANES), 1)
    lo = lane < RET_DK
    sq_r = lax.broadcasted_iota(jnp.int32, (LANES, LANES), 0)
    sq_c = lax.broadcasted_iota(jnp.int32, (LANES, LANES), 1)
    blockdiag = (sq_r < RET_DK) == (sq_c < RET_DK)

    def cols(j):
        return slice(j * LANES, (j + 1) * LANES)

    def state_update(pair, d, rows):
        kp = rk_ref[rows, cols(pair)]
        vp = rv_ref[rows, cols(pair)]
        kdt = (kp.astype(F32) * dvec_ref[pair, K_DEC_B if d else K_DEC_F]).T.astype(BF16)
        return jnp.where(blockdiag, _dot(kdt, vp), 0.0)

    nope_n = lax.broadcasted_iota(jnp.int32, (n_seq, LANES), 1) < MLA_NOPE
    if latent:
        nope_c = lax.broadcasted_iota(jnp.int32, (kvc_ref.shape[0], LANES), 1) < MLA_NOPE

    def states(sub):
        base = sub * n_seq
        for pair in range(N_PAIRS):
            for d in range(2):
                if latent:
                    s = (sb0_ref if d else sf0_ref)[pair]
                    cdec = dvec_ref[pair, C_DEC_B if d else C_DEC_F][0:LANES, :]
                    order = list(range(nc - 1, -1, -1)) if d else list(range(nc))
                    for idx, ch in enumerate(order):
                        st_scr[d, pair, ch] = s.astype(BF16)
                        if idx < nc - 1:
                            s = s * cdec + state_update(pair, d, pl.ds(base + ch * c, c))
                else:
                    s = state_update(pair, d, pl.ds(base, c))
                    st_ref = sb_ref if d else sf_ref
                    fresh = n_alias == 0
                    _put_layer(st_ref, sub, (2 * pair,), layer, fresh, s[0:RET_DK, 0:RET_DK])
                    _put_layer(st_ref, sub, (2 * pair + 1,), layer, fresh, s[RET_DK:, RET_DK:])

    def chunk_body(ch, base):
        rows = pl.ds(pl.multiple_of(base + ch * c, c), c)
        keys = pl.ds(base, n_seq)

        for pair in range(N_PAIRS):
            qp = rq_ref[rows, cols(pair)]
            kp = rk_ref[rows, cols(pair)]
            vp = rv_ref[rows, cols(pair)]
            zero = jnp.zeros_like(qp)
            a0 = (_dot_nt(jnp.where(lo, qp, zero), kp) * dmask_ref[2 * pair]).astype(BF16)
            a1 = (_dot_nt(jnp.where(lo, zero, qp), kp) * dmask_ref[2 * pair + 1]).astype(BF16)
            tot = jnp.where(lo, _dot(a0, vp), _dot(a1, vp))
            if latent:
                tot = (tot + _dot(qp, st_scr[0, pair, ch]) * dvec_ref[pair, Q_DEC_F]
                       + _dot(qp, st_scr[1, pair, ch]) * dvec_ref[pair, Q_DEC_B])
            inv = 1.0 / RET_DK
            m0 = jnp.sum(jnp.where(lo, tot, 0.0), axis=-1, keepdims=True) * inv
            m1 = jnp.sum(jnp.where(lo, 0.0, tot), axis=-1, keepdims=True) * inv
            y = tot - jnp.where(lo, m0, m1)
            v0 = jnp.sum(jnp.where(lo, y * y, 0.0), axis=-1, keepdims=True) * inv
            v1 = jnp.sum(jnp.where(lo, 0.0, y * y), axis=-1, keepdims=True) * inv
            yn = (y * lax.rsqrt(jnp.where(lo, v0, v1) + EPS) * gng_ref[:, cols(pair)]
                  + gnb_ref[:, cols(pair)])
            ret_ref[rows, cols(pair)] = (yn * _silu(rg_ref[rows, cols(pair)])).astype(BF16)

        krp = krp_ref[keys, :]
        for pair in range(N_PAIRS):
            outs = []
            for e in range(2):
                h = 2 * pair + e
                qc = qcat_ref[rows, cols(h)]
                kv = kvn_ref[keys, cols(h)]
                s = _dot_nt(qc, jnp.where(nope_n, kv, krp))
                m = jnp.max(s, axis=-1, keepdims=True)
                if latent:
                    kv_c = kvc_ref[:, cols(h)]
                    s2 = _dot_nt(qc, jnp.where(nope_c, kv_c, krc_ref[...]))
                    m = jnp.maximum(m, jnp.max(s2, axis=-1, keepdims=True))
                    p2 = jnp.exp2(s2 - m)
                p = jnp.exp2(s - m)
                den = jnp.sum(p, axis=-1, keepdims=True)
                acc = _dot(p.astype(BF16), kv)
                if latent:
                    den = den + jnp.sum(p2, axis=-1, keepdims=True)
                    acc = acc + _dot(p2.astype(BF16), kv_c)
                outs.append(acc / den)
            attn_ref[rows, cols(pair)] = jnp.where(
                lo, pltpu.roll(outs[0], MLA_V, 1), outs[1]).astype(BF16)
        return base

    for sub in range(n_sub):
        states(sub)
    for sub in range(n_sub):
        if nc == 1:
            chunk_body(0, sub * n_seq)
        else:
            lax.fori_loop(0, nc, chunk_body, sub * n_seq)


def _mix(proj_out, dmask, dvec, gn_g, gn_b, layer, n_batch, n_seq, latent_in, states_out,
         casts=()):
    rq, rk, rv, rg, qcat, kvn, krp = proj_out[:7]
    latent = latent_in is not None
    aliases = {}
    m = n_batch * n_seq
    c = MIX_CHUNK
    width = MLA_HEADS * HEAD_PAD

    def row(b):
        return (b, 0)

    def lay3(b):
        return (layer, 0, 0)

    once = pl.Buffered(1)
    n_sub = 1 if latent else MIX_SEQS
    n_steps = n_batch // n_sub
    blk = n_sub * n_seq
    in_specs = [
        pl.BlockSpec((blk, RET_WIDTH), row), pl.BlockSpec((blk, RET_WIDTH), row),
        pl.BlockSpec((blk, RET_WIDTH), row), pl.BlockSpec((blk, RET_WIDTH), row),
        pl.BlockSpec((blk, width), row), pl.BlockSpec((blk, width), row),
        pl.BlockSpec((blk, LANES), row),
        pl.BlockSpec((None, 1, RET_WIDTH), lay3), pl.BlockSpec((None, 1, RET_WIDTH), lay3),
        pl.BlockSpec((None, RET_HEADS, c, c), lambda b: (layer, 0, 0, 0), pipeline_mode=once),
        pl.BlockSpec((None, N_PAIRS, N_DVEC, c, LANES), lambda b: (layer, 0, 0, 0, 0),
                     pipeline_mode=once),
    ]
    args = [rq, rk, rv, rg, qcat, kvn, krp, gn_g, gn_b, dmask, dvec]
    out_shape = [jax.ShapeDtypeStruct((m, RET_WIDTH), BF16),
                 jax.ShapeDtypeStruct((m, MLA_HEADS * MLA_V), BF16)]
    out_specs = [pl.BlockSpec((blk, RET_WIDTH), row), pl.BlockSpec((blk, RET_WIDTH), row)]
    scratch = []
    if latent:
        kvn_c, krp_c, s_f0, s_b0 = latent_in
        past = kvn_c.shape[2]
        st_spec = pl.BlockSpec((None, None, N_PAIRS, LANES, LANES), lambda b: (b, layer, 0, 0, 0))
        in_specs += [
            pl.BlockSpec((None, None, past, width), lambda b: (layer, b, 0, 0)),
            pl.BlockSpec((None, None, past, LANES), lambda b: (b, layer, 0, 0)),
            st_spec, st_spec,
        ]
        args += [kvn_c, krp_c, s_f0, s_b0]
        scratch = [pltpu.VMEM((2, N_PAIRS, n_seq // c, LANES, LANES), BF16)]
    else:
        if _is_array(states_out[0]):
            in_specs += [pl.BlockSpec(memory_space=pl.ANY)] * 2
            aliases = {len(args): len(out_shape), len(args) + 1: len(out_shape) + 1}
            args += list(states_out)
        out_shape += [jax.ShapeDtypeStruct(s.shape, s.dtype) for s in states_out]
        out_specs += [_layer_spec(n_sub, s.shape, layer, bool(aliases)) for s in states_out]
        for w, w_layer, out_cols, stride in casts:
            blocks = n_steps // stride
            rows = w.shape[1] // blocks
            assert rows * blocks == w.shape[1] and rows % 16 == 0 and stride * blocks == n_steps
            in_specs.append(pl.BlockSpec(
                (None, rows, w.shape[2]),
                lambda b, w_layer=w_layer, stride=stride: (w_layer, b // stride, 0)))
            args.append(w)
            out_shape.append(jax.ShapeDtypeStruct((w.shape[1], out_cols), BF16))
            out_specs.append(pl.BlockSpec((rows, out_cols),
                                          lambda b, stride=stride: (b // stride, 0)))

    return pl.pallas_call(
        functools.partial(_mix_kernel, n_seq=n_seq, n_sub=n_sub, latent=latent, layer=layer,
                          n_alias=len(aliases), n_cast=len(casts)),
        grid=(n_steps,),
        in_specs=in_specs,
        out_specs=out_specs,
        out_shape=out_shape,
        input_output_aliases=aliases,
        scratch_shapes=scratch,
        compiler_params=pltpu.CompilerParams(
            dimension_semantics=("arbitrary",), vmem_limit_bytes=VMEM_LIMIT),
        name="mix_latent" if latent else "mix_context",
    )(*args)


def _out_kernel(x_ref, ret_ref, attn_ref, mod_ref, gffn_ref, wo_ref, wfi_ref, wfo_ref, o_ref,
                act_scr, *, cond_row):
    cond = cond_row(pl.program_id(0) * x_ref.shape[0])
    gt1 = _mod_vec(mod_ref, cond, 2)
    sh2 = _mod_vec(mod_ref, cond, 3)
    sc2 = _mod_vec(mod_ref, cond, 4)
    gt2 = _mod_vec(mod_ref, cond, 5)
    mixed = _dot(ret_ref[...], wo_ref[0:RET_WIDTH, :]) + _dot(attn_ref[...], wo_ref[RET_WIDTH:, :])
    x1 = x_ref[...] + gt1 * mixed
    h = (_rms(x1, gffn_ref[...]) * (1.0 + sc2) + sh2).astype(BF16)
    for c0 in range(0, D_FF, FF_CHUNK):
        cw = min(FF_CHUNK, D_FF - c0)
        gate = _dot(h, wfi_ref[:, c0:c0 + cw])
        up = _dot(h, wfi_ref[:, D_FF + c0:D_FF + c0 + cw])
        act_scr[:, c0:c0 + cw] = (_silu(gate) * up).astype(BF16)
    o_ref[...] = x1 + gt2 * _dot(act_scr[...], wfo_ref[...])


def _out(x, ret, attn, mod, layer, cond_row, wts, big_w):
    m = x.shape[0]
    tm = ROW_TILE

    def row(i):
        return (i, 0)

    def lay3(i):
        return (layer, 0, 0)

    def whole(i):
        return (0, 0)

    once = pl.Buffered(1)
    return pl.pallas_call(
        functools.partial(_out_kernel, cond_row=cond_row),
        grid=(m // tm,),
        in_specs=[
            pl.BlockSpec((tm, D_MODEL), row),
            pl.BlockSpec((tm, RET_WIDTH), row),
            pl.BlockSpec((tm, RET_WIDTH), row),
            pl.BlockSpec((None, 8, N_MOD * D_MODEL), lay3),
            pl.BlockSpec((None, 1, D_MODEL), lay3),
            pl.BlockSpec((D_MODEL, D_MODEL), whole, pipeline_mode=once),
            pl.BlockSpec((D_MODEL, 2 * D_FF), whole, pipeline_mode=once),
            pl.BlockSpec((D_FF, D_MODEL), whole, pipeline_mode=once),
        ],
        out_specs=pl.BlockSpec((tm, D_MODEL), row),
        out_shape=jax.ShapeDtypeStruct((m, D_MODEL), F32),
        scratch_shapes=[pltpu.VMEM((tm, D_FF), BF16)],
        compiler_params=pltpu.CompilerParams(
            dimension_semantics=("arbitrary",), vmem_limit_bytes=VMEM_LIMIT),
        name="out_ffn",
    )(x, ret, attn, mod, wts["g_ffn"], *big_w)


def _rope_tables(n_lat):
    pos = np.arange(n_lat)
    row = (pos // GRID_W).astype(np.float32)[:, None]
    col = (pos % GRID_W).astype(np.float32)[:, None]
    lane = np.arange(LANES)[None, :]

    def tables(d, start, period):
        rel = (lane - start) % period
        active = np.logical_and(lane >= start, rel < d)
        half = d // 2
        nf = half // 2
        inv = np.float32(ROPE_BASE) ** (-((rel % nf).astype(np.float32)) / np.float32(nf))
        ang = (np.where(rel < half, row, col) * inv).astype(np.float32)
        cos, sin = np.cos(ang), np.sin(ang)
        first = (rel % half) < nf
        c = np.where(active, cos, 1.0)
        s = np.where(active, np.where(first, -sin, sin), 0.0)
        return tuple(jnp.asarray(t, F32) for t in (c, s))

    return tables(RET_DK, 0, RET_DK) + tables(MLA_ROPE, MLA_NOPE, LANES)


def _swap_partners(a, axis, d):
    shape = a.shape
    split = shape[:axis] + (shape[axis] // d, 2, 2, d // 4) + shape[axis + 1:]
    return jnp.flip(a.reshape(split), axis=axis + 2).reshape(shape)


def _prepare_weights(g_norm_mix, g_norm_ffn, w_in, g_q_a, w_q_b, g_kv_a, w_kv_b, g_qn, g_qr, g_kn,
                     g_kr):
    depth = w_in.shape[0]
    n_in = w_in.shape[2]
    w_in_t = jnp.pad(jnp.swapaxes(w_in, 1, 2).astype(BF16),
                     ((0, 0), (0, IN_COLS_PAD - n_in), (0, 0)))
    w_qb = w_q_b.reshape(depth, Q_RANK, MLA_HEADS, MLA_NOPE + MLA_ROPE)
    w_qb_sw = jnp.pad(_swap_partners(w_qb[..., MLA_NOPE:], 3, MLA_ROPE),
                      ((0, 0), (0, 0), (0, 0), (MLA_NOPE, HEAD_PAD - MLA_NOPE - MLA_ROPE)))
    w_qb = jnp.pad(w_qb, ((0, 0), (0, 0), (0, 0), (0, HEAD_PAD - MLA_NOPE - MLA_ROPE)))
    zeros32 = jnp.zeros((depth, MLA_ROPE), F32)
    zeros64 = jnp.zeros((depth, MLA_NOPE), F32)
    return {
        "g_mix": g_norm_mix[:, None, :],
        "g_ffn": g_norm_ffn[:, None, :],
        "w_in_t": w_in_t,
        "g_qa": g_q_a[:, None, :],
        "w_qb": w_qb.reshape(depth, Q_RANK, MLA_HEADS * HEAD_PAD).astype(BF16),
        "g_kva": g_kv_a[:, None, :],
        "w_kvb": w_kv_b.astype(BF16),
        "g_q": (jnp.concatenate([g_qn, g_qr, zeros32], axis=-1) * Q_FOLD)[:, None, :],
        "g_kn": jnp.concatenate([g_kn, jnp.ones((depth, MLA_V), F32)], axis=-1)[:, None, :],
        "g_kr": jnp.pad(g_kr, ((0, 0), (0, LANES - MLA_ROPE)))[:, None, :],
        "w_qb_sw": w_qb_sw.reshape(depth, Q_RANK, MLA_HEADS * HEAD_PAD).astype(BF16),
        "g_q_sw": (jnp.concatenate([zeros64, _swap_partners(g_qr, 1, MLA_ROPE), zeros32], axis=-1)
                   * Q_FOLD)[:, None, :],
    }


def _blockdiag_states(s):
    b, l = s.shape[:2]
    s = s.reshape(b, l, N_PAIRS, 2, RET_DK, RET_DK)
    z = jnp.zeros_like(s[:, :, :, 0])
    top = jnp.concatenate([s[:, :, :, 0], z], axis=-1)
    bot = jnp.concatenate([z, s[:, :, :, 1]], axis=-1)
    return jnp.concatenate([top, bot], axis=-2)


def kernel(x_prompt, x_sample, cache_ckv, cache_krope, state_ret_fwd, state_ret_bwd, c, c_ctx,
           w_mod, b_mod, g_norm_mix, g_norm_ffn, w_in, g_q_a, w_q_b, g_kv_a, w_kv_b,
           g_qn, g_qr, g_kn, g_kr, ret_p_fwd, ret_p_bwd, g_ret_gn, b_ret_gn, w_o,
           w_ffn_in, w_ffn_out):
    batch, seq, _ = x_prompt.shape
    dec_batch, dec_seq, _ = x_sample.shape
    depth = w_in.shape[0]

    wts = _prepare_weights(g_norm_mix, g_norm_ffn, w_in, g_q_a, w_q_b, g_kv_a, w_kv_b, g_qn, g_qr,
                           g_kn, g_kr)
    conds = jnp.concatenate([c_ctx[None], c, jnp.zeros((8 - 1 - dec_batch, D_MODEL), F32)], axis=0)
    mod = _modulation(conds, w_mod, b_mod)
    dmask, dvec = _decay_tables(jnp.stack([ret_p_fwd, ret_p_bwd], axis=1))
    gn_g = g_ret_gn[:, None, :]
    gn_b = b_ret_gn[:, None, :]

    rope_tabs = _rope_tables(dec_seq)
    kvn_cache = _cache_up(cache_ckv, wts["w_kvb"], wts["g_kn"])
    krp_cache = jnp.pad(cache_krope, ((0, 0), (0, 0), (0, 0), (MLA_NOPE, LANES - MLA_NOPE - MLA_ROPE)))
    krp_cache = krp_cache.astype(BF16)
    s_f0 = _blockdiag_states(state_ret_fwd)
    s_b0 = _blockdiag_states(state_ret_bwd)

    def ctx_row(row):
        return 0

    def lat_row(row):
        return 1 + row // dec_seq

    x = x_prompt.reshape(batch * seq, D_MODEL)
    y = x_sample.reshape(dec_batch * dec_seq, D_MODEL)
    caches = (jax.ShapeDtypeStruct((batch, depth, seq, KV_RANK), F32),
              jax.ShapeDtypeStruct((batch, depth, MLA_ROPE, seq), F32))
    states = (jax.ShapeDtypeStruct((batch, depth, RET_HEADS, RET_DK, RET_DK), F32),) * 2
    for l in range(depth):
        pr = _proj(x, mod, l, ctx_row, wts, None, caches)
        caches = tuple(pr[7:9])
        casts = [(w_o, l, D_MODEL, 1), (w_ffn_in, l, 2 * D_FF, 1), (w_ffn_out, l, D_MODEL, 2)]
        mixed = _mix(pr, dmask, dvec, gn_g, gn_b, l, batch, seq, None, states, casts)
        ret, attn = mixed[:2]
        states = tuple(mixed[2:4])
        big_w = tuple(mixed[4:7])
        x = _out(x, ret, attn, mod, l, ctx_row, wts, big_w)

        pr = _proj(y, mod, l, lat_row, wts, rope_tabs, None)
        ret, attn = _mix(pr, dmask, dvec, gn_g, gn_b, l, dec_batch, dec_seq,
                         (kvn_cache, krp_cache, s_f0, s_b0), None)
        y = _out(y, ret, attn, mod, l, lat_row, wts, big_w)

    return (x.reshape(batch, seq, D_MODEL), y.reshape(dec_batch, dec_seq, D_MODEL),
            caches[0], jnp.swapaxes(caches[1], 2, 3), states[0], states[1])
```

```python
import functools

import jax
import jax.numpy as jnp
import numpy as np
from jax import lax
from jax.experimental import pallas as pl
from jax.experimental.pallas import tpu as pltpu

D_MODEL = 1024
N_MOD = 6
RET_HEADS = 8
RET_DK = 64
RET_WIDTH = 512
MLA_HEADS = 8
MLA_NOPE = 64
MLA_ROPE = 32
MLA_V = 64
Q_RANK = 256
KV_RANK = 128
D_FF = 2816
GRID_W = 64
ROPE_BASE = 10000.0
EPS = 1e-6

LANES = 128
HEAD_PAD = LANES
N_PAIRS = RET_HEADS // 2
IN_COLS_PAD = 4 * RET_WIDTH + Q_RANK + KV_RANK + LANES
ROW_TILE = 512
PROJ_TILE = 1024
LATENT_PROJ_TILE = 512
SUB_TILE = 256
FF_CHUNK = 256
MIX_CHUNK = 256
MIX_SEQS = 4
Q_DEC_F, K_DEC_F, Q_DEC_B, K_DEC_B, C_DEC_F, C_DEC_B = range(6)
N_DVEC = 6
Q_FOLD = (MLA_NOPE + MLA_ROPE) ** -0.5 * 1.4426950408889634
VMEM_LIMIT = 56 * 1024 * 1024

BF16 = jnp.bfloat16
F32 = jnp.float32
_NT = (((1,), (1,)), ((), ()))


def _dot(a, b):
    return jnp.dot(a, b, preferred_element_type=F32)


def _dot_nt(a, b):
    return lax.dot_general(a, b, _NT, preferred_element_type=F32)


def _rms(x, g):
    return x * lax.rsqrt(jnp.mean(x * x, axis=-1, keepdims=True) + EPS) * g


def _silu(x):
    return x * jax.nn.sigmoid(x)


def _masked_mean_sq(x, mask, n):
    return jnp.sum(jnp.where(mask, x * x, 0.0), axis=-1, keepdims=True) * (1.0 / n)


def _mod_kernel(c_ref, w_ref, b_ref, o_ref):
    a = _silu(c_ref[...])
    w = w_ref[...]
    a_hi = a.astype(BF16)
    a_lo = (a - a_hi.astype(F32)).astype(BF16)
    w_hi = w.astype(BF16)
    w_lo = (w - w_hi.astype(F32)).astype(BF16)
    both = _dot(jnp.concatenate([a_hi, a_lo], axis=0), w_hi)
    bias = b_ref[pl.ds(pl.program_id(0), 1), :]
    o_ref[...] = both[0:8] + both[8:16] + _dot(a_hi, w_lo) + bias


def _modulation(conds, w_mod, b_mod):
    depth, _, n = w_mod.shape
    tn = 1536
    return pl.pallas_call(
        _mod_kernel,
        grid=(depth, n // tn),
        in_specs=[
            pl.BlockSpec((8, D_MODEL), lambda l, j: (0, 0)),
            pl.BlockSpec((None, D_MODEL, tn), lambda l, j: (l, 0, j)),
            pl.BlockSpec((depth, tn), lambda l, j: (0, j)),
        ],
        out_specs=pl.BlockSpec((None, 8, tn), lambda l, j: (l, 0, j)),
        out_shape=jax.ShapeDtypeStruct((depth, 8, n), F32),
        compiler_params=pltpu.CompilerParams(
            dimension_semantics=("arbitrary", "arbitrary"), vmem_limit_bytes=VMEM_LIMIT),
        name="modulation",
    )(conds, w_mod, b_mod)


def _norm_kn(kv, gkn):
    lane = lax.broadcasted_iota(jnp.int32, (kv.shape[0], LANES), 1)
    lo = lane < MLA_NOPE
    out = []
    for h in range(MLA_HEADS):
        kvh = kv[:, h * HEAD_PAD:(h + 1) * HEAD_PAD]
        rs = lax.rsqrt(_masked_mean_sq(kvh, lo, MLA_NOPE) + EPS)
        out.append((kvh * jnp.where(lo, rs * gkn, 1.0)).astype(BF16))
    return out


def _cache_kernel(ckv_ref, wkvb_ref, gkn_ref, kvn_ref):
    kv = _dot(ckv_ref[...].astype(BF16), wkvb_ref[...])
    gkn = gkn_ref[pl.ds(pl.program_id(0), 1), :]
    for h, kvh in enumerate(_norm_kn(kv, gkn)):
        kvn_ref[:, h * HEAD_PAD:(h + 1) * HEAD_PAD] = kvh


def _cache_up(cache_ckv, wkvb, gkn):
    nb, depth, past, _ = cache_ckv.shape
    width = MLA_HEADS * HEAD_PAD
    return pl.pallas_call(
        _cache_kernel,
        grid=(depth, nb),
        in_specs=[
            pl.BlockSpec((None, None, past, KV_RANK), lambda l, b: (b, l, 0, 0)),
            pl.BlockSpec((None, KV_RANK, width), lambda l, b: (l, 0, 0)),
            pl.BlockSpec((depth, LANES), lambda l, b: (0, 0)),
        ],
        out_specs=pl.BlockSpec((None, None, past, width), lambda l, b: (l, b, 0, 0)),
        out_shape=jax.ShapeDtypeStruct((depth, nb, past, width), BF16),
        compiler_params=pltpu.CompilerParams(
            dimension_semantics=("arbitrary", "arbitrary"), vmem_limit_bytes=VMEM_LIMIT),
        name="cache_up",
    )(cache_ckv, wkvb, gkn)


def _mod_vec(mod_ref, cond, k):
    return mod_ref[pl.ds(cond, 1), k * D_MODEL:(k + 1) * D_MODEL]


def _gain_spec(g):
    return pl.BlockSpec(g.shape, lambda *_: (0, 0))


def _is_array(x):
    return not isinstance(x, jax.ShapeDtypeStruct)


def _layer_spec(n, shape, layer, aliased):
    rest = tuple(shape[2:])
    zeros = (0,) * len(rest)
    if aliased:
        return pl.BlockSpec((n, None) + rest, lambda i: (i, layer) + zeros)
    return pl.BlockSpec((n, shape[1]) + rest, lambda i: (i, 0) + zeros)


def _put_layer(ref, i, tail, layer, fresh, value):
    if not fresh:
        ref[(i,) + tail] = value
        return
    for l in range(ref.shape[1]):
        ref[(i, l) + tail] = value if l == layer else jnp.zeros_like(value)


def _proj_kernel(*refs, rope, cond_row, n_alias, layer):
    (x_ref, mod_ref, gmix_ref, win_ref, gqa_ref, wqb_ref, gkva_ref, wkvb_ref,
     gq_ref, gkn_ref, gkr_ref) = refs[:11]
    if rope:
        wqb_sw_ref, gq_sw_ref, c64_ref, s64_ref, c32_ref, s32_ref = refs[11:17]
        rq_ref, rk_ref, rv_ref, rg_ref, qcat_ref, kvn_ref, krp_ref = refs[17:]
    else:
        (rq_ref, rk_ref, rv_ref, rg_ref, qcat_ref, kvn_ref, krp_ref, ckv_ref,
         kro_ref) = refs[11 + n_alias:]

    cond = cond_row(pl.program_id(0) * x_ref.shape[0])
    sh1 = _mod_vec(mod_ref, cond, 0)
    sc1 = _mod_vec(mod_ref, cond, 1)
    lrow = slice(layer, layer + 1)
    w = RET_WIDTH
    lane = lax.broadcasted_iota(jnp.int32, (SUB_TILE, LANES), 1)
    nope = lane < MLA_NOPE
    is_rope = jnp.logical_and(lane >= MLA_NOPE, lane < MLA_NOPE + MLA_ROPE)

    for r in range(x_ref.shape[0] // SUB_TILE):
        rs = slice(r * SUB_TILE, (r + 1) * SUB_TILE)
        h = (_rms(x_ref[rs, :], gmix_ref[lrow, :]) * (1.0 + sc1) + sh1).astype(BF16)
        z = _dot_nt(h, win_ref[...])
        if rope:
            grp = RET_DK // 4
            z_sw = _dot_nt(h, jnp.concatenate(
                [win_ref[(g ^ 1) * grp:((g ^ 1) + 1) * grp, :] for g in range(2 * w // grp)], 0))

        for j in range(w // LANES):
            sl = slice(j * LANES, (j + 1) * LANES)
            q = z[:, j * LANES:(j + 1) * LANES]
            k = z[:, w + j * LANES:w + (j + 1) * LANES] * (RET_DK ** -0.5)
            if rope:
                q_sw = z_sw[:, j * LANES:(j + 1) * LANES]
                k_sw = z_sw[:, w + j * LANES:w + (j + 1) * LANES] * (RET_DK ** -0.5)
                q = q * c64_ref[rs, :] + q_sw * s64_ref[rs, :]
                k = k * c64_ref[rs, :] + k_sw * s64_ref[rs, :]
            rq_ref[rs, sl] = q.astype(BF16)
            rk_ref[rs, sl] = k.astype(BF16)
        rv_ref[rs, :] = z[:, 2 * w:3 * w].astype(BF16)
        rg_ref[rs, :] = z[:, 3 * w:4 * w]

        o = 4 * w
        qa = z[:, o:o + Q_RANK]
        kva = z[:, o + Q_RANK:o + Q_RANK + KV_RANK]
        kr2 = z[:, o + Q_RANK + KV_RANK:]

        qa_n = _rms(qa, gqa_ref[lrow, :]).astype(BF16)
        q = _dot(qa_n, wqb_ref[...])
        if rope:
            q_sw = _dot(qa_n, wqb_sw_ref[...])
        for hh in range(MLA_HEADS):
            hs = slice(hh * HEAD_PAD, (hh + 1) * HEAD_PAD)
            qh = q[:, hs]
            rs_n = lax.rsqrt(_masked_mean_sq(qh, nope, MLA_NOPE) + EPS)
            rs_r = lax.rsqrt(_masked_mean_sq(qh, is_rope, MLA_ROPE) + EPS)
            qn = qh * jnp.where(nope, rs_n, rs_r) * gq_ref[lrow, :]
            if rope:
                qn = (qn * c32_ref[rs, :]
                      + q_sw[:, hs] * rs_r * gq_sw_ref[lrow, :] * s32_ref[rs, :])
            qcat_ref[rs, hs] = qn.astype(BF16)

        ckv = _rms(kva, gkva_ref[lrow, :])
        kv = _dot(ckv.astype(BF16), wkvb_ref[...])
        for hh, kvh in enumerate(_norm_kn(kv, gkn_ref[lrow, :])):
            kvn_ref[rs, hh * HEAD_PAD:(hh + 1) * HEAD_PAD] = kvh

        rs_k = lax.rsqrt(_masked_mean_sq(kr2, lane < MLA_ROPE, MLA_ROPE) + EPS)
        krn = kr2 * rs_k * gkr_ref[lrow, :]
        krp = pltpu.roll(krn, MLA_NOPE, 1)
        if rope:
            x1_pos = ((lane - MLA_NOPE) % (MLA_ROPE // 2)) < MLA_ROPE // 4
            partner = jnp.where(x1_pos, pltpu.roll(krp, LANES - MLA_ROPE // 4, 1),
                                pltpu.roll(krp, MLA_ROPE // 4, 1))
            krp = krp * c32_ref[rs, :] + partner * s32_ref[rs, :]
        else:
            seq = ckv_ref.shape[-2]
            krt = krn.T[0:MLA_ROPE, :]
            for t in range(SUB_TILE // seq):
                i = r * (SUB_TILE // seq) + t
                _put_layer(ckv_ref, i, (), layer, n_alias == 0, ckv[t * seq:(t + 1) * seq])
                _put_layer(kro_ref, i, (), layer, n_alias == 0, krt[:, t * seq:(t + 1) * seq])
        krp_ref[rs, :] = krp.astype(BF16)


def _proj(x, mod, layer, cond_row, wts, rope_tabs, caches):
    m = x.shape[0]
    rope = rope_tabs is not None
    tm = LATENT_PROJ_TILE if rope else PROJ_TILE
    width = MLA_HEADS * HEAD_PAD

    def row(i):
        return (i, 0)

    def lay3(i):
        return (layer, 0, 0)

    in_specs = [
        pl.BlockSpec((tm, D_MODEL), row),
        pl.BlockSpec((None, 8, N_MOD * D_MODEL), lay3),
        _gain_spec(wts["g_mix"]),
        pl.BlockSpec((None, IN_COLS_PAD, D_MODEL), lay3),
        _gain_spec(wts["g_qa"]),
        pl.BlockSpec((None, Q_RANK, width), lay3),
        _gain_spec(wts["g_kva"]),
        pl.BlockSpec((None, KV_RANK, width), lay3),
        _gain_spec(wts["g_q"]), _gain_spec(wts["g_kn"]), _gain_spec(wts["g_kr"]),
    ]
    args = [x, mod, wts["g_mix"], wts["w_in_t"], wts["g_qa"], wts["w_qb"], wts["g_kva"],
            wts["w_kvb"], wts["g_q"], wts["g_kn"], wts["g_kr"]]
    out_shape = [
        jax.ShapeDtypeStruct((m, RET_WIDTH), BF16),
        jax.ShapeDtypeStruct((m, RET_WIDTH), BF16),
        jax.ShapeDtypeStruct((m, RET_WIDTH), BF16),
        jax.ShapeDtypeStruct((m, RET_WIDTH), F32),
        jax.ShapeDtypeStruct((m, width), BF16),
        jax.ShapeDtypeStruct((m, width), BF16),
        jax.ShapeDtypeStruct((m, LANES), BF16),
    ]
    out_specs = [
        pl.BlockSpec((tm, RET_WIDTH), row), pl.BlockSpec((tm, RET_WIDTH), row),
        pl.BlockSpec((tm, RET_WIDTH), row), pl.BlockSpec((tm, RET_WIDTH), row),
        pl.BlockSpec((tm, width), row), pl.BlockSpec((tm, width), row),
        pl.BlockSpec((tm, LANES), row),
    ]
    aliases = {}
    if rope:
        n_lat = rope_tabs[0].shape[0]
        tiles = n_lat // tm
        in_specs += [pl.BlockSpec((None, Q_RANK, width), lay3), _gain_spec(wts["g_q_sw"])]
        args += [wts["w_qb_sw"], wts["g_q_sw"]]
        in_specs += [pl.BlockSpec((tm, LANES), lambda i: (i % tiles, 0))] * len(rope_tabs)
        args += list(rope_tabs)
    else:
        seq = caches[0].shape[2]
        assert SUB_TILE % seq == 0
        nb = tm // seq
        if _is_array(caches[0]):
            in_specs += [pl.BlockSpec(memory_space=pl.ANY)] * 2
            aliases = {len(args): len(out_shape), len(args) + 1: len(out_shape) + 1}
            args += list(caches)
        out_shape += [jax.ShapeDtypeStruct(c.shape, c.dtype) for c in caches]
        out_specs += [_layer_spec(nb, c.shape, layer, bool(aliases)) for c in caches]

    return pl.pallas_call(
        functools.partial(_proj_kernel, rope=rope, cond_row=cond_row, n_alias=len(aliases),
                          layer=layer),
        grid=(m // tm,),
        in_specs=in_specs,
        out_specs=out_specs,
        out_shape=out_shape,
        input_output_aliases=aliases,
        compiler_params=pltpu.CompilerParams(
            dimension_semantics=("arbitrary",), vmem_limit_bytes=VMEM_LIMIT),
        name="proj_latent" if rope else "proj_context",
    )(*args)


def _log_gamma(p):
    return jnp.log1p(-jnp.exp2(-p))


def _decay_kernel(p_ref, dmask_ref, dvec_ref):
    c = MIX_CHUNK
    base = pl.program_id(0) * (2 * RET_HEADS)
    pair = pl.program_id(1)
    ri = lax.broadcasted_iota(jnp.int32, (c, c), 0)
    ci = lax.broadcasted_iota(jnp.int32, (c, c), 1)
    dif = (ri - ci).astype(F32)
    for e in range(2):
        h = 2 * pair + e
        lg_f = _log_gamma(jnp.full((c, c), p_ref[base + h], F32))
        lg_b = _log_gamma(jnp.full((c, c), p_ref[base + RET_HEADS + h], F32))
        fwd = jnp.where(dif >= 0, jnp.exp(jnp.maximum(dif, 0.0) * lg_f), 0.0)
        bwd = jnp.where(dif <= 0, jnp.exp(jnp.maximum(-dif, 0.0) * lg_b), 0.0)
        dmask_ref[e] = fwd + bwd
    lane = lax.broadcasted_iota(jnp.int32, (c, LANES), 1)
    rowf = lax.broadcasted_iota(jnp.int32, (c, LANES), 0).astype(F32)
    lo = lane < RET_DK
    lg_f = _log_gamma(jnp.where(lo, p_ref[base + 2 * pair], p_ref[base + 2 * pair + 1]))
    lg_b = _log_gamma(jnp.where(lo, p_ref[base + RET_HEADS + 2 * pair],
                                p_ref[base + RET_HEADS + 2 * pair + 1]))
    dvec_ref[Q_DEC_F] = jnp.exp((rowf + 1.0) * lg_f)
    dvec_ref[K_DEC_F] = jnp.exp((c - 1.0 - rowf) * lg_f)
    dvec_ref[Q_DEC_B] = jnp.exp((c - rowf) * lg_b)
    dvec_ref[K_DEC_B] = jnp.exp(rowf * lg_b)
    dvec_ref[C_DEC_F] = jnp.exp(c * lg_f)
    dvec_ref[C_DEC_B] = jnp.exp(c * lg_b)


def _decay_tables(decay_p):
    depth = decay_p.shape[0]
    c = MIX_CHUNK
    return pl.pallas_call(
        _decay_kernel,
        grid=(depth, N_PAIRS),
        in_specs=[pl.BlockSpec(memory_space=pltpu.SMEM)],
        out_specs=[
            pl.BlockSpec((None, 2, c, c), lambda l, p: (l, p, 0, 0)),
            pl.BlockSpec((None, None, N_DVEC, c, LANES), lambda l, p: (l, p, 0, 0, 0)),
        ],
        out_shape=[
            jax.ShapeDtypeStruct((depth, RET_HEADS, c, c), F32),
            jax.ShapeDtypeStruct((depth, N_PAIRS, N_DVEC, c, LANES), F32),
        ],
        compiler_params=pltpu.CompilerParams(
            dimension_semantics=("arbitrary", "arbitrary"), vmem_limit_bytes=VMEM_LIMIT),
        name="decay_tables",
    )(decay_p.reshape(-1))


def _mix_kernel(*refs, n_seq, n_sub, latent, n_alias, n_cast, layer):
    c = MIX_CHUNK
    nc = n_seq // c
    (rq_ref, rk_ref, rv_ref, rg_ref, qcat_ref, kvn_ref, krp_ref, gng_ref, gnb_ref,
     dmask_ref, dvec_ref) = refs[:11]
    if latent:
        kvc_ref, krc_ref, sf0_ref, sb0_ref, ret_ref, attn_ref, st_scr = refs[11:]
    else:
        n_in = 11 + n_alias
        cast_src = refs[n_in:n_in + n_cast]
        ret_ref, attn_ref, sf_ref, sb_ref = refs[n_in + n_cast:n_in + n_cast + 4]
        cast_dst = refs[n_in + n_cast + 4:]
        for src, dst in zip(cast_src, cast_dst):
            w = src.shape[1]
            dst[:, 0:w] = src[...].astype(BF16)
            if dst.shape[1] > w:
                dst[:, w:] = jnp.zeros((dst.shape[0], dst.shape[1] - w), BF16)

    lane = lax.broadcasted_iota(jnp.int32, (c, LANES), 1)
    lo = lane < RET_DK
    lrow = slice(layer, layer + 1)
    sq_r = lax.broadcasted_iota(jnp.int32, (LANES, LANES), 0)
    sq_c = lax.broadcasted_iota(jnp.int32, (LANES, LANES), 1)
    blockdiag = (sq_r < RET_DK) == (sq_c < RET_DK)

    def cols(j):
        return slice(j * LANES, (j + 1) * LANES)

    def state_update(pair, d, rows):
        kp = rk_ref[rows, cols(pair)]
        vp = rv_ref[rows, cols(pair)]
        kdt = (kp.astype(F32) * dvec_ref[pair, K_DEC_B if d else K_DEC_F]).T.astype(BF16)
        return jnp.where(blockdiag, _dot(kdt, vp), 0.0)

    nope_n = lax.broadcasted_iota(jnp.int32, (n_seq, LANES), 1) < MLA_NOPE
    if latent:
        nope_c = lax.broadcasted_iota(jnp.int32, (kvc_ref.shape[0], LANES), 1) < MLA_NOPE

    def states(sub):
        base = sub * n_seq
        for pair in range(N_PAIRS):
            for d in range(2):
                if latent:
                    s = (sb0_ref if d else sf0_ref)[pair]
                    cdec = dvec_ref[pair, C_DEC_B if d else C_DEC_F][0:LANES, :]
                    order = list(range(nc - 1, -1, -1)) if d else list(range(nc))
                    for idx, ch in enumerate(order):
                        st_scr[d, pair, ch] = s.astype(BF16)
                        if idx < nc - 1:
                            s = s * cdec + state_update(pair, d, pl.ds(base + ch * c, c))
                else:
                    s = state_update(pair, d, pl.ds(base, c))
                    st_ref = sb_ref if d else sf_ref
                    fresh = n_alias == 0
                    _put_layer(st_ref, sub, (2 * pair,), layer, fresh, s[0:RET_DK, 0:RET_DK])
                    _put_layer(st_ref, sub, (2 * pair + 1,), layer, fresh, s[RET_DK:, RET_DK:])

    def chunk_body(ch, base):
        rows = pl.ds(pl.multiple_of(base + ch * c, c), c)
        keys = pl.ds(base, n_seq)

        for pair in range(N_PAIRS):
            qp = rq_ref[rows, cols(pair)]
            kp = rk_ref[rows, cols(pair)]
            vp = rv_ref[rows, cols(pair)]
            zero = jnp.zeros_like(qp)
            a0 = (_dot_nt(jnp.where(lo, qp, zero), kp) * dmask_ref[2 * pair]).astype(BF16)
            a1 = (_dot_nt(jnp.where(lo, zero, qp), kp) * dmask_ref[2 * pair + 1]).astype(BF16)
            tot = jnp.where(lo, _dot(a0, vp), _dot(a1, vp))
            if latent:
                tot = (tot + _dot(qp, st_scr[0, pair, ch]) * dvec_ref[pair, Q_DEC_F]
                       + _dot(qp, st_scr[1, pair, ch]) * dvec_ref[pair, Q_DEC_B])
            inv = 1.0 / RET_DK
            m0 = jnp.sum(jnp.where(lo, tot, 0.0), axis=-1, keepdims=True) * inv
            m1 = jnp.sum(jnp.where(lo, 0.0, tot), axis=-1, keepdims=True) * inv
            y = tot - jnp.where(lo, m0, m1)
            v0 = jnp.sum(jnp.where(lo, y * y, 0.0), axis=-1, keepdims=True) * inv
            v1 = jnp.sum(jnp.where(lo, 0.0, y * y), axis=-1, keepdims=True) * inv
            yn = (y * lax.rsqrt(jnp.where(lo, v0, v1) + EPS) * gng_ref[lrow, cols(pair)]
                  + gnb_ref[lrow, cols(pair)])
            ret_ref[rows, cols(pair)] = (yn * _silu(rg_ref[rows, cols(pair)])).astype(BF16)

        krp = krp_ref[keys, :]
        for pair in range(N_PAIRS):
            outs = []
            for e in range(2):
                h = 2 * pair + e
                qc = qcat_ref[rows, cols(h)]
                kv = kvn_ref[keys, cols(h)]
                s = _dot_nt(qc, jnp.where(nope_n, kv, krp))
                m = jnp.max(s, axis=-1, keepdims=True)
                if latent:
                    kv_c = kvc_ref[:, cols(h)]
                    s2 = _dot_nt(qc, jnp.where(nope_c, kv_c, krc_ref[...]))
                    m = jnp.maximum(m, jnp.max(s2, axis=-1, keepdims=True))
                    p2 = jnp.exp2(s2 - m)
                p = jnp.exp2(s - m)
                den = jnp.sum(p, axis=-1, keepdims=True)
                acc = _dot(p.astype(BF16), kv)
                if latent:
                    den = den + jnp.sum(p2, axis=-1, keepdims=True)
                    acc = acc + _dot(p2.astype(BF16), kv_c)
                outs.append(acc / den)
            attn_ref[rows, cols(pair)] = jnp.where(
                lo, pltpu.roll(outs[0], MLA_V, 1), outs[1]).astype(BF16)
        return base

    for sub in range(n_sub):
        states(sub)
    for sub in range(n_sub):
        if nc == 1:
            chunk_body(0, sub * n_seq)
        else:
            lax.fori_loop(0, nc, chunk_body, sub * n_seq)


def _mix(proj_out, dmask, dvec, gn_g, gn_b, layer, n_batch, n_seq, latent_in, states_out,
         casts=()):
    rq, rk, rv, rg, qcat, kvn, krp = proj_out[:7]
    latent = latent_in is not None
    aliases = {}
    m = n_batch * n_seq
    c = MIX_CHUNK
    width = MLA_HEADS * HEAD_PAD

    def row(b):
        return (b, 0)

    def lay3(b):
        return (layer, 0, 0)

    once = pl.Buffered(1)
    n_sub = 1 if latent else MIX_SEQS
    n_steps = n_batch // n_sub
    blk = n_sub * n_seq
    in_specs = [
        pl.BlockSpec((blk, RET_WIDTH), row), pl.BlockSpec((blk, RET_WIDTH), row),
        pl.BlockSpec((blk, RET_WIDTH), row), pl.BlockSpec((blk, RET_WIDTH), row),
        pl.BlockSpec((blk, width), row), pl.BlockSpec((blk, width), row),
        pl.BlockSpec((blk, LANES), row),
        _gain_spec(gn_g), _gain_spec(gn_b),
        pl.BlockSpec((None, RET_HEADS, c, c), lambda b: (layer, 0, 0, 0), pipeline_mode=once),
        pl.BlockSpec((None, N_PAIRS, N_DVEC, c, LANES), lambda b: (layer, 0, 0, 0, 0),
                     pipeline_mode=once),
    ]
    args = [rq, rk, rv, rg, qcat, kvn, krp, gn_g, gn_b, dmask, dvec]
    out_shape = [jax.ShapeDtypeStruct((m, RET_WIDTH), BF16),
                 jax.ShapeDtypeStruct((m, MLA_HEADS * MLA_V), BF16)]
    out_specs = [pl.BlockSpec((blk, RET_WIDTH), row), pl.BlockSpec((blk, RET_WIDTH), row)]
    scratch = []
    if latent:
        kvn_c, krp_c, s_f0, s_b0 = latent_in
        past = kvn_c.shape[2]
        st_spec = pl.BlockSpec((None, None, N_PAIRS, LANES, LANES), lambda b: (b, layer, 0, 0, 0))
        in_specs += [
            pl.BlockSpec((None, None, past, width), lambda b: (layer, b, 0, 0)),
            pl.BlockSpec((None, None, past, LANES), lambda b: (b, layer, 0, 0)),
            st_spec, st_spec,
        ]
        args += [kvn_c, krp_c, s_f0, s_b0]
        scratch = [pltpu.VMEM((2, N_PAIRS, n_seq // c, LANES, LANES), BF16)]
    else:
        if _is_array(states_out[0]):
            in_specs += [pl.BlockSpec(memory_space=pl.ANY)] * 2
            aliases = {len(args): len(out_shape), len(args) + 1: len(out_shape) + 1}
            args += list(states_out)
        out_shape += [jax.ShapeDtypeStruct(s.shape, s.dtype) for s in states_out]
        out_specs += [_layer_spec(n_sub, s.shape, layer, bool(aliases)) for s in states_out]
        for w, w_layer, out_cols, stride in casts:
            blocks = n_steps // stride
            rows = w.shape[1] // blocks
            assert rows * blocks == w.shape[1] and rows % 16 == 0 and stride * blocks == n_steps
            in_specs.append(pl.BlockSpec(
                (None, rows, w.shape[2]),
                lambda b, w_layer=w_layer, stride=stride: (w_layer, b // stride, 0)))
            args.append(w)
            out_shape.append(jax.ShapeDtypeStruct((w.shape[1], out_cols), BF16))
            out_specs.append(pl.BlockSpec((rows, out_cols),
                                          lambda b, stride=stride: (b // stride, 0)))

    return pl.pallas_call(
        functools.partial(_mix_kernel, n_seq=n_seq, n_sub=n_sub, latent=latent, layer=layer,
                          n_alias=len(aliases), n_cast=len(casts)),
        grid=(n_steps,),
        in_specs=in_specs,
        out_specs=out_specs,
        out_shape=out_shape,
        input_output_aliases=aliases,
        scratch_shapes=scratch,
        compiler_params=pltpu.CompilerParams(
            dimension_semantics=("arbitrary",), vmem_limit_bytes=VMEM_LIMIT),
        name="mix_latent" if latent else "mix_context",
    )(*args)


def _out_kernel(x_ref, ret_ref, attn_ref, mod_ref, gffn_ref, wo_ref, wfi_ref, wfo_ref, o_ref,
                act_scr, *, cond_row, layer):
    cond = cond_row(pl.program_id(0) * x_ref.shape[0])
    gt1 = _mod_vec(mod_ref, cond, 2)
    sh2 = _mod_vec(mod_ref, cond, 3)
    sc2 = _mod_vec(mod_ref, cond, 4)
    gt2 = _mod_vec(mod_ref, cond, 5)
    mixed = _dot(ret_ref[...], wo_ref[0:RET_WIDTH, :]) + _dot(attn_ref[...], wo_ref[RET_WIDTH:, :])
    x1 = x_ref[...] + gt1 * mixed
    h = (_rms(x1, gffn_ref[layer:layer + 1, :]) * (1.0 + sc2) + sh2).astype(BF16)
    for c0 in range(0, D_FF, FF_CHUNK):
        cw = min(FF_CHUNK, D_FF - c0)
        gate = _dot(h, wfi_ref[:, c0:c0 + cw])
        up = _dot(h, wfi_ref[:, D_FF + c0:D_FF + c0 + cw])
        act_scr[:, c0:c0 + cw] = (_silu(gate) * up).astype(BF16)
    o_ref[...] = x1 + gt2 * _dot(act_scr[...], wfo_ref[...])


def _out(x, ret, attn, mod, layer, cond_row, wts, big_w):
    m = x.shape[0]
    tm = ROW_TILE

    def row(i):
        return (i, 0)

    def lay3(i):
        return (layer, 0, 0)

    def whole(i):
        return (0, 0)

    once = pl.Buffered(1)
    return pl.pallas_call(
        functools.partial(_out_kernel, cond_row=cond_row, layer=layer),
        grid=(m // tm,),
        in_specs=[
            pl.BlockSpec((tm, D_MODEL), row),
            pl.BlockSpec((tm, RET_WIDTH), row),
            pl.BlockSpec((tm, RET_WIDTH), row),
            pl.BlockSpec((None, 8, N_MOD * D_MODEL), lay3),
            _gain_spec(wts["g_ffn"]),
            pl.BlockSpec((D_MODEL, D_MODEL), whole, pipeline_mode=once),
            pl.BlockSpec((D_MODEL, 2 * D_FF), whole, pipeline_mode=once),
            pl.BlockSpec((D_FF, D_MODEL), whole, pipeline_mode=once),
        ],
        out_specs=pl.BlockSpec((tm, D_MODEL), row),
        out_shape=jax.ShapeDtypeStruct((m, D_MODEL), F32),
        scratch_shapes=[pltpu.VMEM((tm, D_FF), BF16)],
        compiler_params=pltpu.CompilerParams(
            dimension_semantics=("arbitrary",), vmem_limit_bytes=VMEM_LIMIT),
        name="out_ffn",
    )(x, ret, attn, mod, wts["g_ffn"], *big_w)


def _rope_tables(n_lat):
    pos = np.arange(n_lat)
    row = (pos // GRID_W).astype(np.float32)[:, None]
    col = (pos % GRID_W).astype(np.float32)[:, None]
    lane = np.arange(LANES)[None, :]

    def tables(d, start, period):
        rel = (lane - start) % period
        active = np.logical_and(lane >= start, rel < d)
        half = d // 2
        nf = half // 2
        inv = np.float32(ROPE_BASE) ** (-((rel % nf).astype(np.float32)) / np.float32(nf))
        ang = (np.where(rel < half, row, col) * inv).astype(np.float32)
        cos, sin = np.cos(ang), np.sin(ang)
        first = (rel % half) < nf
        c = np.where(active, cos, 1.0)
        s = np.where(active, np.where(first, -sin, sin), 0.0)
        return tuple(jnp.asarray(t, F32) for t in (c, s))

    return tables(RET_DK, 0, RET_DK) + tables(MLA_ROPE, MLA_NOPE, LANES)


def _swap_partners(a, axis, d):
    shape = a.shape
    split = shape[:axis] + (shape[axis] // d, 2, 2, d // 4) + shape[axis + 1:]
    return jnp.flip(a.reshape(split), axis=axis + 2).reshape(shape)


def _prepare_weights(g_norm_mix, g_norm_ffn, w_in, g_q_a, w_q_b, g_kv_a, w_kv_b, g_qn, g_qr, g_kn,
                     g_kr):
    depth = w_in.shape[0]
    n_in = w_in.shape[2]
    w_in_t = jnp.pad(jnp.swapaxes(w_in, 1, 2).astype(BF16),
                     ((0, 0), (0, IN_COLS_PAD - n_in), (0, 0)))
    w_qb = w_q_b.reshape(depth, Q_RANK, MLA_HEADS, MLA_NOPE + MLA_ROPE)
    w_qb_sw = jnp.pad(_swap_partners(w_qb[..., MLA_NOPE:], 3, MLA_ROPE),
                      ((0, 0), (0, 0), (0, 0), (MLA_NOPE, HEAD_PAD - MLA_NOPE - MLA_ROPE)))
    w_qb = jnp.pad(w_qb, ((0, 0), (0, 0), (0, 0), (0, HEAD_PAD - MLA_NOPE - MLA_ROPE)))
    zeros32 = jnp.zeros((depth, MLA_ROPE), F32)
    zeros64 = jnp.zeros((depth, MLA_NOPE), F32)
    return {
        "g_mix": g_norm_mix,
        "g_ffn": g_norm_ffn,
        "w_in_t": w_in_t,
        "g_qa": g_q_a,
        "w_qb": w_qb.reshape(depth, Q_RANK, MLA_HEADS * HEAD_PAD).astype(BF16),
        "g_kva": g_kv_a,
        "w_kvb": w_kv_b.astype(BF16),
        "g_q": jnp.concatenate([g_qn, g_qr, zeros32], axis=-1) * Q_FOLD,
        "g_kn": jnp.concatenate([g_kn, jnp.ones((depth, MLA_V), F32)], axis=-1),
        "g_kr": jnp.pad(g_kr, ((0, 0), (0, LANES - MLA_ROPE))),
        "w_qb_sw": w_qb_sw.reshape(depth, Q_RANK, MLA_HEADS * HEAD_PAD).astype(BF16),
        "g_q_sw": jnp.concatenate([zeros64, _swap_partners(g_qr, 1, MLA_ROPE), zeros32],
                                  axis=-1) * Q_FOLD,
    }


def _blockdiag_states(s):
    b, l = s.shape[:2]
    s = s.reshape(b, l, N_PAIRS, 2, RET_DK, RET_DK)
    z = jnp.zeros_like(s[:, :, :, 0])
    top = jnp.concatenate([s[:, :, :, 0], z], axis=-1)
    bot = jnp.concatenate([z, s[:, :, :, 1]], axis=-1)
    return jnp.concatenate([top, bot], axis=-2)


def kernel(x_prompt, x_sample, cache_ckv, cache_krope, state_ret_fwd, state_ret_bwd, c, c_ctx,
           w_mod, b_mod, g_norm_mix, g_norm_ffn, w_in, g_q_a, w_q_b, g_kv_a, w_kv_b,
           g_qn, g_qr, g_kn, g_kr, ret_p_fwd, ret_p_bwd, g_ret_gn, b_ret_gn, w_o,
           w_ffn_in, w_ffn_out):
    batch, seq, _ = x_prompt.shape
    dec_batch, dec_seq, _ = x_sample.shape
    depth = w_in.shape[0]

    wts = _prepare_weights(g_norm_mix, g_norm_ffn, w_in, g_q_a, w_q_b, g_kv_a, w_kv_b, g_qn, g_qr,
                           g_kn, g_kr)
    conds = jnp.concatenate([c_ctx[None], c, jnp.zeros((8 - 1 - dec_batch, D_MODEL), F32)], axis=0)
    mod = _modulation(conds, w_mod, b_mod)
    dmask, dvec = _decay_tables(jnp.stack([ret_p_fwd, ret_p_bwd], axis=1))
    gn_g, gn_b = g_ret_gn, b_ret_gn

    rope_tabs = _rope_tables(dec_seq)
    kvn_cache = _cache_up(cache_ckv, wts["w_kvb"], wts["g_kn"])
    krp_cache = jnp.pad(cache_krope, ((0, 0), (0, 0), (0, 0), (MLA_NOPE, LANES - MLA_NOPE - MLA_ROPE)))
    krp_cache = krp_cache.astype(BF16)
    s_f0 = _blockdiag_states(state_ret_fwd)
    s_b0 = _blockdiag_states(state_ret_bwd)

    def ctx_row(row):
        return 0

    def lat_row(row):
        return 1 + row // dec_seq

    x = x_prompt.reshape(batch * seq, D_MODEL)
    y = x_sample.reshape(dec_batch * dec_seq, D_MODEL)
    caches = (jax.ShapeDtypeStruct((batch, depth, seq, KV_RANK), F32),
              jax.ShapeDtypeStruct((batch, depth, MLA_ROPE, seq), F32))
    states = (jax.ShapeDtypeStruct((batch, depth, RET_HEADS, RET_DK, RET_DK), F32),) * 2
    for l in range(depth):
        pr = _proj(x, mod, l, ctx_row, wts, None, caches)
        caches = tuple(pr[7:9])
        casts = [(w_o, l, D_MODEL, 1), (w_ffn_in, l, 2 * D_FF, 1), (w_ffn_out, l, D_MODEL, 2)]
        mixed = _mix(pr, dmask, dvec, gn_g, gn_b, l, batch, seq, None, states, casts)
        ret, attn = mixed[:2]
        states = tuple(mixed[2:4])
        big_w = tuple(mixed[4:7])
        x = _out(x, ret, attn, mod, l, ctx_row, wts, big_w)

        pr = _proj(y, mod, l, lat_row, wts, rope_tabs, None)
        ret, attn = _mix(pr, dmask, dvec, gn_g, gn_b, l, dec_batch, dec_seq,
                         (kvn_cache, krp_cache, s_f0, s_b0), None)
        y = _out(y, ret, attn, mod, l, lat_row, wts, big_w)

    return (x.reshape(batch, seq, D_MODEL), y.reshape(dec_batch, dec_seq, D_MODEL),
            caches[0], jnp.swapaxes(caches[1], 2, 3), states[0], states[1])
```

```python
import functools

import jax
import jax.numpy as jnp
import numpy as np
from jax import lax
from jax.experimental import pallas as pl
from jax.experimental.pallas import tpu as pltpu

D_MODEL = 1024
N_MOD = 6
RET_HEADS = 8
RET_DK = 64
RET_WIDTH = 512
MLA_HEADS = 8
MLA_NOPE = 64
MLA_ROPE = 32
MLA_V = 64
Q_RANK = 256
KV_RANK = 128
D_FF = 2816
GRID_W = 64
ROPE_BASE = 10000.0
EPS = 1e-6

LANES = 128
HEAD_PAD = LANES
N_PAIRS = RET_HEADS // 2
IN_COLS_PAD = 4 * RET_WIDTH + Q_RANK + KV_RANK + LANES
ROW_TILE = 512
PROJ_TILE = 1024
LATENT_PROJ_TILE = 512
SUB_TILE = 256
FF_CHUNK = 256
MIX_CHUNK = 256
MIX_SEQS = 4
Q_DEC_F, K_DEC_F, Q_DEC_B, K_DEC_B, C_DEC_F, C_DEC_B = range(6)
N_DVEC = 6
K_SCALE = RET_DK ** -0.5
Q_FOLD = (MLA_NOPE + MLA_ROPE) ** -0.5 * 1.4426950408889634
VMEM_LIMIT = 56 * 1024 * 1024

BF16 = jnp.bfloat16
F32 = jnp.float32
_NT = (((1,), (1,)), ((), ()))


def _dot(a, b):
    return jnp.dot(a, b, preferred_element_type=F32)


def _dot_nt(a, b):
    return lax.dot_general(a, b, _NT, preferred_element_type=F32)


def _rms(x, g):
    return x * lax.rsqrt(jnp.mean(x * x, axis=-1, keepdims=True) + EPS) * g


def _silu(x):
    return x * jax.nn.sigmoid(x)


def _masked_mean_sq(x, mask, n):
    return jnp.sum(jnp.where(mask, x * x, 0.0), axis=-1, keepdims=True) * (1.0 / n)


def _mod_kernel(c_ref, w_ref, b_ref, o_ref):
    a = _silu(c_ref[...])
    w = w_ref[...]
    a_hi = a.astype(BF16)
    a_lo = (a - a_hi.astype(F32)).astype(BF16)
    w_hi = w.astype(BF16)
    w_lo = (w - w_hi.astype(F32)).astype(BF16)
    both = _dot(jnp.concatenate([a_hi, a_lo], axis=0), w_hi)
    bias = b_ref[pl.ds(pl.program_id(0), 1), :]
    o_ref[...] = both[0:8] + both[8:16] + _dot(a_hi, w_lo) + bias


def _modulation(conds, w_mod, b_mod):
    depth, _, n = w_mod.shape
    tn = 1536
    return pl.pallas_call(
        _mod_kernel,
        grid=(depth, n // tn),
        in_specs=[
            pl.BlockSpec((8, D_MODEL), lambda l, j: (0, 0)),
            pl.BlockSpec((None, D_MODEL, tn), lambda l, j: (l, 0, j)),
            pl.BlockSpec((depth, tn), lambda l, j: (0, j)),
        ],
        out_specs=pl.BlockSpec((None, 8, tn), lambda l, j: (l, 0, j)),
        out_shape=jax.ShapeDtypeStruct((depth, 8, n), F32),
        compiler_params=pltpu.CompilerParams(
            dimension_semantics=("arbitrary", "arbitrary"), vmem_limit_bytes=VMEM_LIMIT),
        name="modulation",
    )(conds, w_mod, b_mod)


def _norm_kn(kv, gkn):
    lane = lax.broadcasted_iota(jnp.int32, (kv.shape[0], LANES), 1)
    lo = lane < MLA_NOPE
    out = []
    for h in range(MLA_HEADS):
        kvh = kv[:, h * HEAD_PAD:(h + 1) * HEAD_PAD]
        rs = lax.rsqrt(_masked_mean_sq(kvh, lo, MLA_NOPE) + EPS)
        out.append((kvh * jnp.where(lo, rs * gkn, 1.0)).astype(BF16))
    return out


def _cache_kernel(ckv_ref, wkvb_ref, gkn_ref, kvn_ref):
    kv = _dot(ckv_ref[...].astype(BF16), wkvb_ref[...])
    gkn = gkn_ref[pl.ds(pl.program_id(0), 1), :]
    for h, kvh in enumerate(_norm_kn(kv, gkn)):
        kvn_ref[:, h * HEAD_PAD:(h + 1) * HEAD_PAD] = kvh


def _cache_up(cache_ckv, wkvb, gkn):
    nb, depth, past, _ = cache_ckv.shape
    width = MLA_HEADS * HEAD_PAD
    return pl.pallas_call(
        _cache_kernel,
        grid=(depth, nb),
        in_specs=[
            pl.BlockSpec((None, None, past, KV_RANK), lambda l, b: (b, l, 0, 0)),
            pl.BlockSpec((None, KV_RANK, width), lambda l, b: (l, 0, 0)),
            pl.BlockSpec((depth, LANES), lambda l, b: (0, 0)),
        ],
        out_specs=pl.BlockSpec((None, None, past, width), lambda l, b: (l, b, 0, 0)),
        out_shape=jax.ShapeDtypeStruct((depth, nb, past, width), BF16),
        compiler_params=pltpu.CompilerParams(
            dimension_semantics=("arbitrary", "arbitrary"), vmem_limit_bytes=VMEM_LIMIT),
        name="cache_up",
    )(cache_ckv, wkvb, gkn)


def _mod_vec(mod_ref, cond, k):
    return mod_ref[pl.ds(cond, 1), k * D_MODEL:(k + 1) * D_MODEL]


def _gain_spec(g):
    return pl.BlockSpec(g.shape, lambda *_: (0, 0))


def _is_array(x):
    return not isinstance(x, jax.ShapeDtypeStruct)


def _layer_spec(n, shape, layer, aliased):
    rest = tuple(shape[2:])
    zeros = (0,) * len(rest)
    if aliased:
        return pl.BlockSpec((n, None) + rest, lambda i: (i, layer) + zeros)
    return pl.BlockSpec((n, shape[1]) + rest, lambda i: (i, 0) + zeros)


def _put_layer(ref, i, tail, layer, fresh, value):
    if not fresh:
        ref[(i,) + tail] = value
        return
    for l in range(ref.shape[1]):
        ref[(i, l) + tail] = value if l == layer else jnp.zeros_like(value)


def _proj_kernel(*refs, rope, cond_row, n_alias, layer):
    (x_ref, mod_ref, gmix_ref, win_ref, gqa_ref, wqb_ref, gkva_ref, wkvb_ref,
     gq_ref, gkn_ref, gkr_ref) = refs[:11]
    if rope:
        wqb_sw_ref, gq_sw_ref, c64_ref, s64_ref, c32_ref, s32_ref = refs[11:17]
        rq_ref, rk_ref, rv_ref, rg_ref, qcat_ref, kvn_ref, krp_ref = refs[17:]
    else:
        (rq_ref, rk_ref, rv_ref, rg_ref, qcat_ref, kvn_ref, krp_ref, ckv_ref,
         kro_ref) = refs[11 + n_alias:]

    cond = cond_row(pl.program_id(0) * x_ref.shape[0])
    sh1 = _mod_vec(mod_ref, cond, 0)
    sc1 = _mod_vec(mod_ref, cond, 1)
    lrow = slice(layer, layer + 1)
    g_mod = gmix_ref[lrow, :] * (1.0 + sc1)
    w = RET_WIDTH
    lane = lax.broadcasted_iota(jnp.int32, (SUB_TILE, LANES), 1)
    nope = lane < MLA_NOPE
    is_rope = jnp.logical_and(lane >= MLA_NOPE, lane < MLA_NOPE + MLA_ROPE)

    for r in range(x_ref.shape[0] // SUB_TILE):
        rs = slice(r * SUB_TILE, (r + 1) * SUB_TILE)
        h = (_rms(x_ref[rs, :], g_mod) + sh1).astype(BF16)
        z = _dot_nt(h, win_ref[...])
        if rope:
            grp = RET_DK // 4
            z_sw = _dot_nt(h, jnp.concatenate(
                [win_ref[(g ^ 1) * grp:((g ^ 1) + 1) * grp, :] for g in range(2 * w // grp)], 0))

        for j in range(w // LANES):
            sl = slice(j * LANES, (j + 1) * LANES)
            q = z[:, j * LANES:(j + 1) * LANES]
            k = z[:, w + j * LANES:w + (j + 1) * LANES]
            if rope:
                q_sw = z_sw[:, j * LANES:(j + 1) * LANES]
                k_sw = z_sw[:, w + j * LANES:w + (j + 1) * LANES]
                q = q * c64_ref[rs, :] + q_sw * s64_ref[rs, :]
                k = k * c64_ref[rs, :] + k_sw * s64_ref[rs, :]
            rq_ref[rs, sl] = q.astype(BF16)
            rk_ref[rs, sl] = k.astype(BF16)
        rv_ref[rs, :] = z[:, 2 * w:3 * w].astype(BF16)
        rg_ref[rs, :] = z[:, 3 * w:4 * w]

        o = 4 * w
        qa = z[:, o:o + Q_RANK]
        kva = z[:, o + Q_RANK:o + Q_RANK + KV_RANK]
        kr2 = z[:, o + Q_RANK + KV_RANK:]

        qa_n = _rms(qa, gqa_ref[lrow, :]).astype(BF16)
        q = _dot(qa_n, wqb_ref[...])
        if rope:
            q_sw = _dot(qa_n, wqb_sw_ref[...])
        for hh in range(MLA_HEADS):
            hs = slice(hh * HEAD_PAD, (hh + 1) * HEAD_PAD)
            qh = q[:, hs]
            rs_n = lax.rsqrt(_masked_mean_sq(qh, nope, MLA_NOPE) + EPS)
            rs_r = lax.rsqrt(_masked_mean_sq(qh, is_rope, MLA_ROPE) + EPS)
            qn = qh * jnp.where(nope, rs_n, rs_r) * gq_ref[lrow, :]
            if rope:
                qn = (qn * c32_ref[rs, :]
                      + q_sw[:, hs] * rs_r * gq_sw_ref[lrow, :] * s32_ref[rs, :])
            qcat_ref[rs, hs] = qn.astype(BF16)

        ckv = _rms(kva, gkva_ref[lrow, :])
        kv = _dot(ckv.astype(BF16), wkvb_ref[...])
        for hh, kvh in enumerate(_norm_kn(kv, gkn_ref[lrow, :])):
            kvn_ref[rs, hh * HEAD_PAD:(hh + 1) * HEAD_PAD] = kvh

        rs_k = lax.rsqrt(_masked_mean_sq(kr2, lane < MLA_ROPE, MLA_ROPE) + EPS)
        krn = kr2 * rs_k * gkr_ref[lrow, :]
        krp = pltpu.roll(krn, MLA_NOPE, 1)
        if rope:
            x1_pos = ((lane - MLA_NOPE) % (MLA_ROPE // 2)) < MLA_ROPE // 4
            partner = jnp.where(x1_pos, pltpu.roll(krp, LANES - MLA_ROPE // 4, 1),
                                pltpu.roll(krp, MLA_ROPE // 4, 1))
            krp = krp * c32_ref[rs, :] + partner * s32_ref[rs, :]
        else:
            seq = ckv_ref.shape[-2]
            krt = krn.T[0:MLA_ROPE, :]
            for t in range(SUB_TILE // seq):
                i = r * (SUB_TILE // seq) + t
                _put_layer(ckv_ref, i, (), layer, n_alias == 0, ckv[t * seq:(t + 1) * seq])
                _put_layer(kro_ref, i, (), layer, n_alias == 0, krt[:, t * seq:(t + 1) * seq])
        krp_ref[rs, :] = krp.astype(BF16)


def _proj(x, mod, layer, cond_row, wts, rope_tabs, caches):
    m = x.shape[0]
    rope = rope_tabs is not None
    tm = LATENT_PROJ_TILE if rope else PROJ_TILE
    width = MLA_HEADS * HEAD_PAD

    def row(i):
        return (i, 0)

    def lay3(i):
        return (layer, 0, 0)

    in_specs = [
        pl.BlockSpec((tm, D_MODEL), row),
        pl.BlockSpec((None, 8, N_MOD * D_MODEL), lay3),
        _gain_spec(wts["g_mix"]),
        pl.BlockSpec((None, IN_COLS_PAD, D_MODEL), lay3),
        _gain_spec(wts["g_qa"]),
        pl.BlockSpec((None, Q_RANK, width), lay3),
        _gain_spec(wts["g_kva"]),
        pl.BlockSpec((None, KV_RANK, width), lay3),
        _gain_spec(wts["g_q"]), _gain_spec(wts["g_kn"]), _gain_spec(wts["g_kr"]),
    ]
    args = [x, mod, wts["g_mix"], wts["w_in_t"], wts["g_qa"], wts["w_qb"], wts["g_kva"],
            wts["w_kvb"], wts["g_q"], wts["g_kn"], wts["g_kr"]]
    out_shape = [
        jax.ShapeDtypeStruct((m, RET_WIDTH), BF16),
        jax.ShapeDtypeStruct((m, RET_WIDTH), BF16),
        jax.ShapeDtypeStruct((m, RET_WIDTH), BF16),
        jax.ShapeDtypeStruct((m, RET_WIDTH), F32),
        jax.ShapeDtypeStruct((m, width), BF16),
        jax.ShapeDtypeStruct((m, width), BF16),
        jax.ShapeDtypeStruct((m, LANES), BF16),
    ]
    out_specs = [
        pl.BlockSpec((tm, RET_WIDTH), row), pl.BlockSpec((tm, RET_WIDTH), row),
        pl.BlockSpec((tm, RET_WIDTH), row), pl.BlockSpec((tm, RET_WIDTH), row),
        pl.BlockSpec((tm, width), row), pl.BlockSpec((tm, width), row),
        pl.BlockSpec((tm, LANES), row),
    ]
    aliases = {}
    if rope:
        n_lat = rope_tabs[0].shape[0]
        tiles = n_lat // tm
        in_specs += [pl.BlockSpec((None, Q_RANK, width), lay3), _gain_spec(wts["g_q_sw"])]
        args += [wts["w_qb_sw"], wts["g_q_sw"]]
        in_specs += [pl.BlockSpec((tm, LANES), lambda i: (i % tiles, 0))] * len(rope_tabs)
        args += list(rope_tabs)
    else:
        seq = caches[0].shape[2]
        assert SUB_TILE % seq == 0
        nb = tm // seq
        if _is_array(caches[0]):
            in_specs += [pl.BlockSpec(memory_space=pl.ANY)] * 2
            aliases = {len(args): len(out_shape), len(args) + 1: len(out_shape) + 1}
            args += list(caches)
        out_shape += [jax.ShapeDtypeStruct(c.shape, c.dtype) for c in caches]
        out_specs += [_layer_spec(nb, c.shape, layer, bool(aliases)) for c in caches]

    return pl.pallas_call(
        functools.partial(_proj_kernel, rope=rope, cond_row=cond_row, n_alias=len(aliases),
                          layer=layer),
        grid=(m // tm,),
        in_specs=in_specs,
        out_specs=out_specs,
        out_shape=out_shape,
        input_output_aliases=aliases,
        compiler_params=pltpu.CompilerParams(
            dimension_semantics=("arbitrary",), vmem_limit_bytes=VMEM_LIMIT),
        name="proj_latent" if rope else "proj_context",
    )(*args)


def _log_gamma(p):
    return jnp.log1p(-jnp.exp2(-p))


def _decay_kernel(p_ref, dmask_ref, dvec_ref):
    c = MIX_CHUNK
    base = pl.program_id(0) * (2 * RET_HEADS)
    pair = pl.program_id(1)
    ri = lax.broadcasted_iota(jnp.int32, (c, c), 0)
    ci = lax.broadcasted_iota(jnp.int32, (c, c), 1)
    dif = (ri - ci).astype(F32)
    for e in range(2):
        h = 2 * pair + e
        lg_f = _log_gamma(jnp.full((c, c), p_ref[base + h], F32))
        lg_b = _log_gamma(jnp.full((c, c), p_ref[base + RET_HEADS + h], F32))
        fwd = jnp.where(dif >= 0, jnp.exp(jnp.maximum(dif, 0.0) * lg_f), 0.0)
        bwd = jnp.where(dif <= 0, jnp.exp(jnp.maximum(-dif, 0.0) * lg_b), 0.0)
        dmask_ref[e] = (fwd + bwd) * K_SCALE
    lane = lax.broadcasted_iota(jnp.int32, (c, LANES), 1)
    rowf = lax.broadcasted_iota(jnp.int32, (c, LANES), 0).astype(F32)
    lo = lane < RET_DK
    lg_f = _log_gamma(jnp.where(lo, p_ref[base + 2 * pair], p_ref[base + 2 * pair + 1]))
    lg_b = _log_gamma(jnp.where(lo, p_ref[base + RET_HEADS + 2 * pair],
                                p_ref[base + RET_HEADS + 2 * pair + 1]))
    dvec_ref[Q_DEC_F] = jnp.exp((rowf + 1.0) * lg_f)
    dvec_ref[K_DEC_F] = jnp.exp((c - 1.0 - rowf) * lg_f) * K_SCALE
    dvec_ref[Q_DEC_B] = jnp.exp((c - rowf) * lg_b)
    dvec_ref[K_DEC_B] = jnp.exp(rowf * lg_b) * K_SCALE
    dvec_ref[C_DEC_F] = jnp.exp(c * lg_f)
    dvec_ref[C_DEC_B] = jnp.exp(c * lg_b)


def _decay_tables(decay_p):
    depth = decay_p.shape[0]
    c = MIX_CHUNK
    return pl.pallas_call(
        _decay_kernel,
        grid=(depth, N_PAIRS),
        in_specs=[pl.BlockSpec(memory_space=pltpu.SMEM)],
        out_specs=[
            pl.BlockSpec((None, 2, c, c), lambda l, p: (l, p, 0, 0)),
            pl.BlockSpec((None, None, N_DVEC, c, LANES), lambda l, p: (l, p, 0, 0, 0)),
        ],
        out_shape=[
            jax.ShapeDtypeStruct((depth, RET_HEADS, c, c), F32),
            jax.ShapeDtypeStruct((depth, N_PAIRS, N_DVEC, c, LANES), F32),
        ],
        compiler_params=pltpu.CompilerParams(
            dimension_semantics=("arbitrary", "arbitrary"), vmem_limit_bytes=VMEM_LIMIT),
        name="decay_tables",
    )(decay_p.reshape(-1))


def _mix_kernel(*refs, n_seq, n_sub, latent, n_alias, n_cast, layer):
    c = MIX_CHUNK
    nc = n_seq // c
    (rq_ref, rk_ref, rv_ref, rg_ref, qcat_ref, kvn_ref, krp_ref, gng_ref, gnb_ref,
     dmask_ref, dvec_ref) = refs[:11]
    if latent:
        kvc_ref, krc_ref, sf0_ref, sb0_ref, ret_ref, attn_ref, st_scr = refs[11:]
    else:
        n_in = 11 + n_alias
        cast_src = refs[n_in:n_in + n_cast]
        ret_ref, attn_ref, sf_ref, sb_ref = refs[n_in + n_cast:n_in + n_cast + 4]
        cast_dst = refs[n_in + n_cast + 4:]
        for src, dst in zip(cast_src, cast_dst):
            w = src.shape[1]
            dst[:, 0:w] = src[...].astype(BF16)
            if dst.shape[1] > w:
                dst[:, w:] = jnp.zeros((dst.shape[0], dst.shape[1] - w), BF16)

    lane = lax.broadcasted_iota(jnp.int32, (c, LANES), 1)
    lo = lane < RET_DK
    lrow = slice(layer, layer + 1)
    sq_r = lax.broadcasted_iota(jnp.int32, (LANES, LANES), 0)
    sq_c = lax.broadcasted_iota(jnp.int32, (LANES, LANES), 1)
    blockdiag = (sq_r < RET_DK) == (sq_c < RET_DK)

    def cols(j):
        return slice(j * LANES, (j + 1) * LANES)

    def state_update(pair, d, rows):
        kp = rk_ref[rows, cols(pair)]
        vp = rv_ref[rows, cols(pair)]
        kdt = (kp.astype(F32) * dvec_ref[pair, K_DEC_B if d else K_DEC_F]).T.astype(BF16)
        return jnp.where(blockdiag, _dot(kdt, vp), 0.0)

    nope_n = lax.broadcasted_iota(jnp.int32, (n_seq, LANES), 1) < MLA_NOPE
    if latent:
        nope_c = lax.broadcasted_iota(jnp.int32, (kvc_ref.shape[0], LANES), 1) < MLA_NOPE

    def states(sub):
        base = sub * n_seq
        for pair in range(N_PAIRS):
            for d in range(2):
                if latent:
                    s = (sb0_ref if d else sf0_ref)[pair]
                    cdec = dvec_ref[pair, C_DEC_B if d else C_DEC_F][0:LANES, :]
                    order = list(range(nc - 1, -1, -1)) if d else list(range(nc))
                    for idx, ch in enumerate(order):
                        st_scr[d, pair, ch] = s.astype(BF16)
                        if idx < nc - 1:
                            s = s * cdec + state_update(pair, d, pl.ds(base + ch * c, c))
                else:
                    s = state_update(pair, d, pl.ds(base, c))
                    st_ref = sb_ref if d else sf_ref
                    fresh = n_alias == 0
                    _put_layer(st_ref, sub, (2 * pair,), layer, fresh, s[0:RET_DK, 0:RET_DK])
                    _put_layer(st_ref, sub, (2 * pair + 1,), layer, fresh, s[RET_DK:, RET_DK:])

    def chunk_body(ch, base):
        rows = pl.ds(pl.multiple_of(base + ch * c, c), c)
        keys = pl.ds(base, n_seq)

        for pair in range(N_PAIRS):
            qp = rq_ref[rows, cols(pair)]
            kp = rk_ref[rows, cols(pair)]
            vp = rv_ref[rows, cols(pair)]
            zero = jnp.zeros_like(qp)
            a0 = (_dot_nt(jnp.where(lo, qp, zero), kp) * dmask_ref[2 * pair]).astype(BF16)
            a1 = (_dot_nt(jnp.where(lo, zero, qp), kp) * dmask_ref[2 * pair + 1]).astype(BF16)
            tot = jnp.where(lo, _dot(a0, vp), _dot(a1, vp))
            if latent:
                tot = (tot + _dot(qp, st_scr[0, pair, ch]) * dvec_ref[pair, Q_DEC_F]
                       + _dot(qp, st_scr[1, pair, ch]) * dvec_ref[pair, Q_DEC_B])
            inv = 1.0 / RET_DK
            m0 = jnp.sum(jnp.where(lo, tot, 0.0), axis=-1, keepdims=True) * inv
            m1 = jnp.sum(jnp.where(lo, 0.0, tot), axis=-1, keepdims=True) * inv
            y = tot - jnp.where(lo, m0, m1)
            v0 = jnp.sum(jnp.where(lo, y * y, 0.0), axis=-1, keepdims=True) * inv
            v1 = jnp.sum(jnp.where(lo, 0.0, y * y), axis=-1, keepdims=True) * inv
            yn = (y * lax.rsqrt(jnp.where(lo, v0, v1) + EPS) * gng_ref[lrow, cols(pair)]
                  + gnb_ref[lrow, cols(pair)])
            ret_ref[rows, cols(pair)] = (yn * _silu(rg_ref[rows, cols(pair)])).astype(BF16)

        krp = krp_ref[keys, :]
        for pair in range(N_PAIRS):
            outs = []
            for e in range(2):
                h = 2 * pair + e
                qc = qcat_ref[rows, cols(h)]
                kv = kvn_ref[keys, cols(h)]
                s = _dot_nt(qc, jnp.where(nope_n, kv, krp))
                m = jnp.max(s, axis=-1, keepdims=True)
                if latent:
                    kv_c = kvc_ref[:, cols(h)]
                    s2 = _dot_nt(qc, jnp.where(nope_c, kv_c, krc_ref[...]))
                    m = jnp.maximum(m, jnp.max(s2, axis=-1, keepdims=True))
                    p2 = jnp.exp2(s2 - m)
                p = jnp.exp2(s - m)
                den = jnp.sum(p, axis=-1, keepdims=True)
                acc = _dot(p.astype(BF16), kv)
                if latent:
                    den = den + jnp.sum(p2, axis=-1, keepdims=True)
                    acc = acc + _dot(p2.astype(BF16), kv_c)
                outs.append(acc / den)
            attn_ref[rows, cols(pair)] = jnp.where(
                lo, pltpu.roll(outs[0], MLA_V, 1), outs[1]).astype(BF16)
        return base

    for sub in range(n_sub):
        states(sub)
    for sub in range(n_sub):
        if nc == 1:
            chunk_body(0, sub * n_seq)
        else:
            lax.fori_loop(0, nc, chunk_body, sub * n_seq)


def _mix(proj_out, dmask, dvec, gn_g, gn_b, layer, n_batch, n_seq, latent_in, states_out,
         casts=()):
    rq, rk, rv, rg, qcat, kvn, krp = proj_out[:7]
    latent = latent_in is not None
    aliases = {}
    m = n_batch * n_seq
    c = MIX_CHUNK
    width = MLA_HEADS * HEAD_PAD

    def row(b):
        return (b, 0)

    def lay3(b):
        return (layer, 0, 0)

    once = pl.Buffered(1)
    n_sub = 1 if latent else MIX_SEQS
    n_steps = n_batch // n_sub
    blk = n_sub * n_seq
    in_specs = [
        pl.BlockSpec((blk, RET_WIDTH), row), pl.BlockSpec((blk, RET_WIDTH), row),
        pl.BlockSpec((blk, RET_WIDTH), row), pl.BlockSpec((blk, RET_WIDTH), row),
        pl.BlockSpec((blk, width), row), pl.BlockSpec((blk, width), row),
        pl.BlockSpec((blk, LANES), row),
        _gain_spec(gn_g), _gain_spec(gn_b),
        pl.BlockSpec((None, RET_HEADS, c, c), lambda b: (layer, 0, 0, 0), pipeline_mode=once),
        pl.BlockSpec((None, N_PAIRS, N_DVEC, c, LANES), lambda b: (layer, 0, 0, 0, 0),
                     pipeline_mode=once),
    ]
    args = [rq, rk, rv, rg, qcat, kvn, krp, gn_g, gn_b, dmask, dvec]
    out_shape = [jax.ShapeDtypeStruct((m, RET_WIDTH), BF16),
                 jax.ShapeDtypeStruct((m, MLA_HEADS * MLA_V), BF16)]
    out_specs = [pl.BlockSpec((blk, RET_WIDTH), row), pl.BlockSpec((blk, RET_WIDTH), row)]
    scratch = []
    if latent:
        kvn_c, krp_c, s_f0, s_b0 = latent_in
        past = kvn_c.shape[2]
        st_spec = pl.BlockSpec((None, None, N_PAIRS, LANES, LANES), lambda b: (b, layer, 0, 0, 0))
        in_specs += [
            pl.BlockSpec((None, None, past, width), lambda b: (layer, b, 0, 0)),
            pl.BlockSpec((None, None, past, LANES), lambda b: (b, layer, 0, 0)),
            st_spec, st_spec,
        ]
        args += [kvn_c, krp_c, s_f0, s_b0]
        scratch = [pltpu.VMEM((2, N_PAIRS, n_seq // c, LANES, LANES), BF16)]
    else:
        if _is_array(states_out[0]):
            in_specs += [pl.BlockSpec(memory_space=pl.ANY)] * 2
            aliases = {len(args): len(out_shape), len(args) + 1: len(out_shape) + 1}
            args += list(states_out)
        out_shape += [jax.ShapeDtypeStruct(s.shape, s.dtype) for s in states_out]
        out_specs += [_layer_spec(n_sub, s.shape, layer, bool(aliases)) for s in states_out]
        for w, w_layer, out_cols, stride in casts:
            blocks = n_steps // stride
            rows = w.shape[1] // blocks
            assert rows * blocks == w.shape[1] and rows % 16 == 0 and stride * blocks == n_steps
            in_specs.append(pl.BlockSpec(
                (None, rows, w.shape[2]),
                lambda b, w_layer=w_layer, stride=stride: (w_layer, b // stride, 0)))
            args.append(w)
            out_shape.append(jax.ShapeDtypeStruct((w.shape[1], out_cols), BF16))
            out_specs.append(pl.BlockSpec((rows, out_cols),
                                          lambda b, stride=stride: (b // stride, 0)))

    return pl.pallas_call(
        functools.partial(_mix_kernel, n_seq=n_seq, n_sub=n_sub, latent=latent, layer=layer,
                          n_alias=len(aliases), n_cast=len(casts)),
        grid=(n_steps,),
        in_specs=in_specs,
        out_specs=out_specs,
        out_shape=out_shape,
        input_output_aliases=aliases,
        scratch_shapes=scratch,
        compiler_params=pltpu.CompilerParams(
            dimension_semantics=("arbitrary",), vmem_limit_bytes=VMEM_LIMIT),
        name="mix_latent" if latent else "mix_context",
    )(*args)


def _out_kernel(x_ref, ret_ref, attn_ref, mod_ref, gffn_ref, wo_ref, wfi_ref, wfo_ref, o_ref,
                act_scr, *, cond_row, layer):
    cond = cond_row(pl.program_id(0) * x_ref.shape[0])
    gt1 = _mod_vec(mod_ref, cond, 2)
    sh2 = _mod_vec(mod_ref, cond, 3)
    sc2 = _mod_vec(mod_ref, cond, 4)
    gt2 = _mod_vec(mod_ref, cond, 5)
    mixed = _dot(ret_ref[...], wo_ref[0:RET_WIDTH, :]) + _dot(attn_ref[...], wo_ref[RET_WIDTH:, :])
    x1 = x_ref[...] + gt1 * mixed
    h = (_rms(x1, gffn_ref[layer:layer + 1, :] * (1.0 + sc2)) + sh2).astype(BF16)
    for c0 in range(0, D_FF, FF_CHUNK):
        cw = min(FF_CHUNK, D_FF - c0)
        gate = _dot(h, wfi_ref[:, c0:c0 + cw])
        up = _dot(h, wfi_ref[:, D_FF + c0:D_FF + c0 + cw])
        act_scr[:, c0:c0 + cw] = (_silu(gate) * up).astype(BF16)
    o_ref[...] = x1 + gt2 * _dot(act_scr[...], wfo_ref[...])


def _out(x, ret, attn, mod, layer, cond_row, wts, big_w):
    m = x.shape[0]
    tm = ROW_TILE

    def row(i):
        return (i, 0)

    def lay3(i):
        return (layer, 0, 0)

    def whole(i):
        return (0, 0)

    once = pl.Buffered(1)
    return pl.pallas_call(
        functools.partial(_out_kernel, cond_row=cond_row, layer=layer),
        grid=(m // tm,),
        in_specs=[
            pl.BlockSpec((tm, D_MODEL), row),
            pl.BlockSpec((tm, RET_WIDTH), row),
            pl.BlockSpec((tm, RET_WIDTH), row),
            pl.BlockSpec((None, 8, N_MOD * D_MODEL), lay3),
            _gain_spec(wts["g_ffn"]),
            pl.BlockSpec((D_MODEL, D_MODEL), whole, pipeline_mode=once),
            pl.BlockSpec((D_MODEL, 2 * D_FF), whole, pipeline_mode=once),
            pl.BlockSpec((D_FF, D_MODEL), whole, pipeline_mode=once),
        ],
        out_specs=pl.BlockSpec((tm, D_MODEL), row),
        out_shape=jax.ShapeDtypeStruct((m, D_MODEL), F32),
        scratch_shapes=[pltpu.VMEM((tm, D_FF), BF16)],
        compiler_params=pltpu.CompilerParams(
            dimension_semantics=("arbitrary",), vmem_limit_bytes=VMEM_LIMIT),
        name="out_ffn",
    )(x, ret, attn, mod, wts["g_ffn"], *big_w)


def _rope_tables(n_lat):
    pos = np.arange(n_lat)
    row = (pos // GRID_W).astype(np.float32)[:, None]
    col = (pos % GRID_W).astype(np.float32)[:, None]
    lane = np.arange(LANES)[None, :]

    def tables(d, start, period):
        rel = (lane - start) % period
        active = np.logical_and(lane >= start, rel < d)
        half = d // 2
        nf = half // 2
        inv = np.float32(ROPE_BASE) ** (-((rel % nf).astype(np.float32)) / np.float32(nf))
        ang = (np.where(rel < half, row, col) * inv).astype(np.float32)
        cos, sin = np.cos(ang), np.sin(ang)
        first = (rel % half) < nf
        c = np.where(active, cos, 1.0)
        s = np.where(active, np.where(first, -sin, sin), 0.0)
        return tuple(jnp.asarray(t, F32) for t in (c, s))

    return tables(RET_DK, 0, RET_DK) + tables(MLA_ROPE, MLA_NOPE, LANES)


def _swap_partners(a, axis, d):
    shape = a.shape
    split = shape[:axis] + (shape[axis] // d, 2, 2, d // 4) + shape[axis + 1:]
    return jnp.flip(a.reshape(split), axis=axis + 2).reshape(shape)


def _prepare_weights(g_norm_mix, g_norm_ffn, w_in, g_q_a, w_q_b, g_kv_a, w_kv_b, g_qn, g_qr, g_kn,
                     g_kr):
    depth = w_in.shape[0]
    n_in = w_in.shape[2]
    w_in_t = jnp.pad(jnp.swapaxes(w_in, 1, 2).astype(BF16),
                     ((0, 0), (0, IN_COLS_PAD - n_in), (0, 0)))
    w_qb = w_q_b.reshape(depth, Q_RANK, MLA_HEADS, MLA_NOPE + MLA_ROPE)
    w_qb_sw = jnp.pad(_swap_partners(w_qb[..., MLA_NOPE:], 3, MLA_ROPE),
                      ((0, 0), (0, 0), (0, 0), (MLA_NOPE, HEAD_PAD - MLA_NOPE - MLA_ROPE)))
    w_qb = jnp.pad(w_qb, ((0, 0), (0, 0), (0, 0), (0, HEAD_PAD - MLA_NOPE - MLA_ROPE)))
    zeros32 = jnp.zeros((depth, MLA_ROPE), F32)
    zeros64 = jnp.zeros((depth, MLA_NOPE), F32)
    return {
        "g_mix": g_norm_mix,
        "g_ffn": g_norm_ffn,
        "w_in_t": w_in_t,
        "g_qa": g_q_a,
        "w_qb": w_qb.reshape(depth, Q_RANK, MLA_HEADS * HEAD_PAD).astype(BF16),
        "g_kva": g_kv_a,
        "w_kvb": w_kv_b.astype(BF16),
        "g_q": jnp.concatenate([g_qn, g_qr, zeros32], axis=-1) * Q_FOLD,
        "g_kn": jnp.concatenate([g_kn, jnp.ones((depth, MLA_V), F32)], axis=-1),
        "g_kr": jnp.pad(g_kr, ((0, 0), (0, LANES - MLA_ROPE))),
        "w_qb_sw": w_qb_sw.reshape(depth, Q_RANK, MLA_HEADS * HEAD_PAD).astype(BF16),
        "g_q_sw": jnp.concatenate([zeros64, _swap_partners(g_qr, 1, MLA_ROPE), zeros32],
                                  axis=-1) * Q_FOLD,
    }


def _blockdiag_states(s):
    b, l = s.shape[:2]
    s = s.reshape(b, l, N_PAIRS, 2, RET_DK, RET_DK)
    z = jnp.zeros_like(s[:, :, :, 0])
    top = jnp.concatenate([s[:, :, :, 0], z], axis=-1)
    bot = jnp.concatenate([z, s[:, :, :, 1]], axis=-1)
    return jnp.concatenate([top, bot], axis=-2)


def kernel(x_prompt, x_sample, cache_ckv, cache_krope, state_ret_fwd, state_ret_bwd, c, c_ctx,
           w_mod, b_mod, g_norm_mix, g_norm_ffn, w_in, g_q_a, w_q_b, g_kv_a, w_kv_b,
           g_qn, g_qr, g_kn, g_kr, ret_p_fwd, ret_p_bwd, g_ret_gn, b_ret_gn, w_o,
           w_ffn_in, w_ffn_out):
    batch, seq, _ = x_prompt.shape
    dec_batch, dec_seq, _ = x_sample.shape
    depth = w_in.shape[0]

    wts = _prepare_weights(g_norm_mix, g_norm_ffn, w_in, g_q_a, w_q_b, g_kv_a, w_kv_b, g_qn, g_qr,
                           g_kn, g_kr)
    conds = jnp.concatenate([c_ctx[None], c, jnp.zeros((8 - 1 - dec_batch, D_MODEL), F32)], axis=0)
    mod = _modulation(conds, w_mod, b_mod)
    dmask, dvec = _decay_tables(jnp.stack([ret_p_fwd, ret_p_bwd], axis=1))
    gn_g, gn_b = g_ret_gn, b_ret_gn

    rope_tabs = _rope_tables(dec_seq)
    kvn_cache = _cache_up(cache_ckv, wts["w_kvb"], wts["g_kn"])
    krp_cache = jnp.pad(cache_krope, ((0, 0), (0, 0), (0, 0), (MLA_NOPE, LANES - MLA_NOPE - MLA_ROPE)))
    krp_cache = krp_cache.astype(BF16)
    s_f0 = _blockdiag_states(state_ret_fwd)
    s_b0 = _blockdiag_states(state_ret_bwd)

    def ctx_row(row):
        return 0

    def lat_row(row):
        return 1 + row // dec_seq

    x = x_prompt.reshape(batch * seq, D_MODEL)
    y = x_sample.reshape(dec_batch * dec_seq, D_MODEL)
    caches = (jax.ShapeDtypeStruct((batch, depth, seq, KV_RANK), F32),
              jax.ShapeDtypeStruct((batch, depth, MLA_ROPE, seq), F32))
    states = (jax.ShapeDtypeStruct((batch, depth, RET_HEADS, RET_DK, RET_DK), F32),) * 2
    for l in range(depth):
        pr = _proj(x, mod, l, ctx_row, wts, None, caches)
        caches = tuple(pr[7:9])
        casts = [(w_o, l, D_MODEL, 1), (w_ffn_in, l, 2 * D_FF, 1), (w_ffn_out, l, D_MODEL, 2)]
        mixed = _mix(pr, dmask, dvec, gn_g, gn_b, l, batch, seq, None, states, casts)
        ret, attn = mixed[:2]
        states = tuple(mixed[2:4])
        big_w = tuple(mixed[4:7])
        x = _out(x, ret, attn, mod, l, ctx_row, wts, big_w)

        pr = _proj(y, mod, l, lat_row, wts, rope_tabs, None)
        ret, attn = _mix(pr, dmask, dvec, gn_g, gn_b, l, dec_batch, dec_seq,
                         (kvn_cache, krp_cache, s_f0, s_b0), None)
        y = _out(y, ret, attn, mod, l, lat_row, wts, big_w)

    return (x.reshape(batch, seq, D_MODEL), y.reshape(dec_batch, dec_seq, D_MODEL),
            caches[0], jnp.swapaxes(caches[1], 2, 3), states[0], states[1])
```

```python
import functools

import jax
import jax.numpy as jnp
import numpy as np
from jax import lax
from jax.experimental import pallas as pl
from jax.experimental.pallas import tpu as pltpu

D_MODEL = 1024
N_MOD = 6
RET_HEADS = 8
RET_DK = 64
RET_WIDTH = 512
MLA_HEADS = 8
MLA_NOPE = 64
MLA_ROPE = 32
MLA_V = 64
Q_RANK = 256
KV_RANK = 128
D_FF = 2816
GRID_W = 64
ROPE_BASE = 10000.0
EPS = 1e-6

LANES = 128
HEAD_PAD = LANES
N_PAIRS = RET_HEADS // 2
IN_COLS_PAD = 4 * RET_WIDTH + Q_RANK + KV_RANK + LANES
ROW_TILE = 512
PROJ_TILE = 1024
LATENT_PROJ_TILE = 512
SUB_TILE = 256
FF_CHUNK = 256
MIX_CHUNK = 256
MIX_SEQS = 4
Q_DEC_F, K_DEC_F, Q_DEC_B, K_DEC_B, C_DEC_F, C_DEC_B = range(6)
N_DVEC = 6
K_SCALE = RET_DK ** -0.5
Q_FOLD = (MLA_NOPE + MLA_ROPE) ** -0.5 * 1.4426950408889634
VMEM_LIMIT = 56 * 1024 * 1024

BF16 = jnp.bfloat16
F32 = jnp.float32
_NT = (((1,), (1,)), ((), ()))


def _dot(a, b):
    return jnp.dot(a, b, preferred_element_type=F32)


def _dot_nt(a, b):
    return lax.dot_general(a, b, _NT, preferred_element_type=F32)


def _rms(x, g):
    return x * lax.rsqrt(jnp.mean(x * x, axis=-1, keepdims=True) + EPS) * g


def _silu(x):
    return x * jax.nn.sigmoid(x)


def _masked_mean_sq(x, mask, n):
    return jnp.sum(jnp.where(mask, x * x, 0.0), axis=-1, keepdims=True) * (1.0 / n)


def _mod_kernel(c_ref, w_ref, b_ref, o_ref):
    a = _silu(c_ref[...])
    w = w_ref[...]
    a_hi = a.astype(BF16)
    a_lo = (a - a_hi.astype(F32)).astype(BF16)
    w_hi = w.astype(BF16)
    w_lo = (w - w_hi.astype(F32)).astype(BF16)
    both = _dot(jnp.concatenate([a_hi, a_lo], axis=0), w_hi)
    bias = b_ref[pl.ds(pl.program_id(0), 1), :]
    o_ref[...] = both[0:8] + both[8:16] + _dot(a_hi, w_lo) + bias


def _modulation(conds, w_mod, b_mod):
    depth, _, n = w_mod.shape
    tn = 768
    return pl.pallas_call(
        _mod_kernel,
        grid=(depth, n // tn),
        in_specs=[
            pl.BlockSpec((8, D_MODEL), lambda l, j: (0, 0)),
            pl.BlockSpec((None, D_MODEL, tn), lambda l, j: (l, 0, j)),
            pl.BlockSpec((depth, tn), lambda l, j: (0, j)),
        ],
        out_specs=pl.BlockSpec((None, 8, tn), lambda l, j: (l, 0, j)),
        out_shape=jax.ShapeDtypeStruct((depth, 8, n), F32),
        compiler_params=pltpu.CompilerParams(
            dimension_semantics=("arbitrary", "arbitrary"), vmem_limit_bytes=VMEM_LIMIT),
        name="modulation",
    )(conds, w_mod, b_mod)


def _norm_kn(kv, gkn):
    lane = lax.broadcasted_iota(jnp.int32, (kv.shape[0], LANES), 1)
    lo = lane < MLA_NOPE
    out = []
    for h in range(MLA_HEADS):
        kvh = kv[:, h * HEAD_PAD:(h + 1) * HEAD_PAD]
        rs = lax.rsqrt(_masked_mean_sq(kvh, lo, MLA_NOPE) + EPS)
        out.append((kvh * jnp.where(lo, rs * gkn, 1.0)).astype(BF16))
    return out


def _cache_kernel(ckv_ref, wkvb_ref, gkn_ref, kvn_ref):
    kv = _dot(ckv_ref[...].astype(BF16), wkvb_ref[...])
    gkn = gkn_ref[pl.ds(pl.program_id(0), 1), :]
    for h, kvh in enumerate(_norm_kn(kv, gkn)):
        kvn_ref[:, h * HEAD_PAD:(h + 1) * HEAD_PAD] = kvh


def _cache_up(cache_ckv, wkvb, gkn):
    nb, depth, past, _ = cache_ckv.shape
    width = MLA_HEADS * HEAD_PAD
    return pl.pallas_call(
        _cache_kernel,
        grid=(depth, nb),
        in_specs=[
            pl.BlockSpec((None, None, past, KV_RANK), lambda l, b: (b, l, 0, 0)),
            pl.BlockSpec((None, KV_RANK, width), lambda l, b: (l, 0, 0)),
            pl.BlockSpec((depth, LANES), lambda l, b: (0, 0)),
        ],
        out_specs=pl.BlockSpec((None, None, past, width), lambda l, b: (l, b, 0, 0)),
        out_shape=jax.ShapeDtypeStruct((depth, nb, past, width), BF16),
        compiler_params=pltpu.CompilerParams(
            dimension_semantics=("arbitrary", "arbitrary"), vmem_limit_bytes=VMEM_LIMIT),
        name="cache_up",
    )(cache_ckv, wkvb, gkn)


def _mod_vec(mod_ref, cond, k):
    return mod_ref[pl.ds(cond, 1), k * D_MODEL:(k + 1) * D_MODEL]


def _gain_spec(g):
    return pl.BlockSpec(g.shape, lambda *_: (0, 0))


def _is_array(x):
    return not isinstance(x, jax.ShapeDtypeStruct)


def _layer_spec(n, shape, layer, aliased):
    rest = tuple(shape[2:])
    zeros = (0,) * len(rest)
    if aliased:
        return pl.BlockSpec((n, None) + rest, lambda i: (i, layer) + zeros)
    return pl.BlockSpec((n, shape[1]) + rest, lambda i: (i, 0) + zeros)


def _put_layer(ref, i, tail, layer, fresh, value):
    if not fresh:
        ref[(i,) + tail] = value
        return
    for l in range(ref.shape[1]):
        ref[(i, l) + tail] = value if l == layer else jnp.zeros_like(value)


def _proj_kernel(*refs, rope, cond_row, n_alias, layer):
    (x_ref, mod_ref, gmix_ref, win_ref, gqa_ref, wqb_ref, gkva_ref, wkvb_ref,
     gq_ref, gkn_ref, gkr_ref) = refs[:11]
    if rope:
        wqb_sw_ref, gq_sw_ref, c64_ref, s64_ref, c32_ref, s32_ref = refs[11:17]
        rq_ref, rk_ref, rv_ref, rg_ref, qcat_ref, kvn_ref, krp_ref = refs[17:]
    else:
        (rq_ref, rk_ref, rv_ref, rg_ref, qcat_ref, kvn_ref, krp_ref, ckv_ref,
         kro_ref) = refs[11 + n_alias:]

    cond = cond_row(pl.program_id(0) * x_ref.shape[0])
    sh1 = _mod_vec(mod_ref, cond, 0)
    sc1 = _mod_vec(mod_ref, cond, 1)
    lrow = slice(layer, layer + 1)
    g_mod = gmix_ref[lrow, :] * (1.0 + sc1)
    w = RET_WIDTH
    lane = lax.broadcasted_iota(jnp.int32, (SUB_TILE, LANES), 1)
    nope = lane < MLA_NOPE
    is_rope = jnp.logical_and(lane >= MLA_NOPE, lane < MLA_NOPE + MLA_ROPE)

    for r in range(x_ref.shape[0] // SUB_TILE):
        rs = slice(r * SUB_TILE, (r + 1) * SUB_TILE)
        h = (_rms(x_ref[rs, :], g_mod) + sh1).astype(BF16)
        z = _dot_nt(h, win_ref[...])
        if rope:
            grp = RET_DK // 4
            z_sw = _dot_nt(h, jnp.concatenate(
                [win_ref[(g ^ 1) * grp:((g ^ 1) + 1) * grp, :] for g in range(2 * w // grp)], 0))

        for j in range(w // LANES):
            sl = slice(j * LANES, (j + 1) * LANES)
            q = z[:, j * LANES:(j + 1) * LANES]
            k = z[:, w + j * LANES:w + (j + 1) * LANES]
            if rope:
                q_sw = z_sw[:, j * LANES:(j + 1) * LANES]
                k_sw = z_sw[:, w + j * LANES:w + (j + 1) * LANES]
                q = q * c64_ref[rs, :] + q_sw * s64_ref[rs, :]
                k = k * c64_ref[rs, :] + k_sw * s64_ref[rs, :]
            rq_ref[rs, sl] = q.astype(BF16)
            rk_ref[rs, sl] = k.astype(BF16)
        rv_ref[rs, :] = z[:, 2 * w:3 * w].astype(BF16)
        rg_ref[rs, :] = z[:, 3 * w:4 * w]

        o = 4 * w
        qa = z[:, o:o + Q_RANK]
        kva = z[:, o + Q_RANK:o + Q_RANK + KV_RANK]
        kr2 = z[:, o + Q_RANK + KV_RANK:]

        qa_n = _rms(qa, gqa_ref[lrow, :]).astype(BF16)
        q = _dot(qa_n, wqb_ref[...])
        if rope:
            q_sw = _dot(qa_n, wqb_sw_ref[...])
        for hh in range(MLA_HEADS):
            hs = slice(hh * HEAD_PAD, (hh + 1) * HEAD_PAD)
            qh = q[:, hs]
            rs_n = lax.rsqrt(_masked_mean_sq(qh, nope, MLA_NOPE) + EPS)
            rs_r = lax.rsqrt(_masked_mean_sq(qh, is_rope, MLA_ROPE) + EPS)
            qn = qh * jnp.where(nope, rs_n, rs_r) * gq_ref[lrow, :]
            if rope:
                qn = (qn * c32_ref[rs, :]
                      + q_sw[:, hs] * rs_r * gq_sw_ref[lrow, :] * s32_ref[rs, :])
            qcat_ref[rs, hs] = qn.astype(BF16)

        ckv = _rms(kva, gkva_ref[lrow, :])
        kv = _dot(ckv.astype(BF16), wkvb_ref[...])
        for hh, kvh in enumerate(_norm_kn(kv, gkn_ref[lrow, :])):
            kvn_ref[rs, hh * HEAD_PAD:(hh + 1) * HEAD_PAD] = kvh

        rs_k = lax.rsqrt(_masked_mean_sq(kr2, lane < MLA_ROPE, MLA_ROPE) + EPS)
        krn = kr2 * rs_k * gkr_ref[lrow, :]
        krp = pltpu.roll(krn, MLA_NOPE, 1)
        if rope:
            x1_pos = ((lane - MLA_NOPE) % (MLA_ROPE // 2)) < MLA_ROPE // 4
            partner = jnp.where(x1_pos, pltpu.roll(krp, LANES - MLA_ROPE // 4, 1),
                                pltpu.roll(krp, MLA_ROPE // 4, 1))
            krp = krp * c32_ref[rs, :] + partner * s32_ref[rs, :]
        else:
            seq = ckv_ref.shape[-2]
            krt = krn.T[0:MLA_ROPE, :]
            for t in range(SUB_TILE // seq):
                i = r * (SUB_TILE // seq) + t
                _put_layer(ckv_ref, i, (), layer, n_alias == 0, ckv[t * seq:(t + 1) * seq])
                _put_layer(kro_ref, i, (), layer, n_alias == 0, krt[:, t * seq:(t + 1) * seq])
        krp_ref[rs, :] = krp.astype(BF16)


def _proj(x, mod, layer, cond_row, wts, rope_tabs, caches):
    m = x.shape[0]
    rope = rope_tabs is not None
    tm = LATENT_PROJ_TILE if rope else PROJ_TILE
    width = MLA_HEADS * HEAD_PAD

    def row(i):
        return (i, 0)

    def lay3(i):
        return (layer, 0, 0)

    in_specs = [
        pl.BlockSpec((tm, D_MODEL), row),
        pl.BlockSpec((None, 8, N_MOD * D_MODEL), lay3),
        _gain_spec(wts["g_mix"]),
        pl.BlockSpec((None, IN_COLS_PAD, D_MODEL), lay3),
        _gain_spec(wts["g_qa"]),
        pl.BlockSpec((None, Q_RANK, width), lay3),
        _gain_spec(wts["g_kva"]),
        pl.BlockSpec((None, KV_RANK, width), lay3),
        _gain_spec(wts["g_q"]), _gain_spec(wts["g_kn"]), _gain_spec(wts["g_kr"]),
    ]
    args = [x, mod, wts["g_mix"], wts["w_in_t"], wts["g_qa"], wts["w_qb"], wts["g_kva"],
            wts["w_kvb"], wts["g_q"], wts["g_kn"], wts["g_kr"]]
    out_shape = [
        jax.ShapeDtypeStruct((m, RET_WIDTH), BF16),
        jax.ShapeDtypeStruct((m, RET_WIDTH), BF16),
        jax.ShapeDtypeStruct((m, RET_WIDTH), BF16),
        jax.ShapeDtypeStruct((m, RET_WIDTH), F32),
        jax.ShapeDtypeStruct((m, width), BF16),
        jax.ShapeDtypeStruct((m, width), BF16),
        jax.ShapeDtypeStruct((m, LANES), BF16),
    ]
    out_specs = [
        pl.BlockSpec((tm, RET_WIDTH), row), pl.BlockSpec((tm, RET_WIDTH), row),
        pl.BlockSpec((tm, RET_WIDTH), row), pl.BlockSpec((tm, RET_WIDTH), row),
        pl.BlockSpec((tm, width), row), pl.BlockSpec((tm, width), row),
        pl.BlockSpec((tm, LANES), row),
    ]
    aliases = {}
    if rope:
        n_lat = rope_tabs[0].shape[0]
        tiles = n_lat // tm
        in_specs += [pl.BlockSpec((None, Q_RANK, width), lay3), _gain_spec(wts["g_q_sw"])]
        args += [wts["w_qb_sw"], wts["g_q_sw"]]
        in_specs += [pl.BlockSpec((tm, LANES), lambda i: (i % tiles, 0))] * len(rope_tabs)
        args += list(rope_tabs)
    else:
        seq = caches[0].shape[2]
        assert SUB_TILE % seq == 0
        nb = tm // seq
        if _is_array(caches[0]):
            in_specs += [pl.BlockSpec(memory_space=pl.ANY)] * 2
            aliases = {len(args): len(out_shape), len(args) + 1: len(out_shape) + 1}
            args += list(caches)
        out_shape += [jax.ShapeDtypeStruct(c.shape, c.dtype) for c in caches]
        out_specs += [_layer_spec(nb, c.shape, layer, bool(aliases)) for c in caches]

    return pl.pallas_call(
        functools.partial(_proj_kernel, rope=rope, cond_row=cond_row, n_alias=len(aliases),
                          layer=layer),
        grid=(m // tm,),
        in_specs=in_specs,
        out_specs=out_specs,
        out_shape=out_shape,
        input_output_aliases=aliases,
        compiler_params=pltpu.CompilerParams(
            dimension_semantics=("arbitrary",), vmem_limit_bytes=VMEM_LIMIT),
        name="proj_latent" if rope else "proj_context",
    )(*args)


def _log_gamma(p):
    return jnp.log1p(-jnp.exp2(-p))


def _decay_kernel(p_ref, dmask_ref, dvec_ref):
    c = MIX_CHUNK
    base = pl.program_id(0) * (2 * RET_HEADS)
    pair = pl.program_id(1)
    ri = lax.broadcasted_iota(jnp.int32, (c, c), 0)
    ci = lax.broadcasted_iota(jnp.int32, (c, c), 1)
    dif = (ri - ci).astype(F32)
    for e in range(2):
        h = 2 * pair + e
        lg_f = _log_gamma(jnp.full((c, c), p_ref[base + h], F32))
        lg_b = _log_gamma(jnp.full((c, c), p_ref[base + RET_HEADS + h], F32))
        fwd = jnp.where(dif >= 0, jnp.exp(jnp.maximum(dif, 0.0) * lg_f), 0.0)
        bwd = jnp.where(dif <= 0, jnp.exp(jnp.maximum(-dif, 0.0) * lg_b), 0.0)
        dmask_ref[e] = (fwd + bwd) * K_SCALE
    lane = lax.broadcasted_iota(jnp.int32, (c, LANES), 1)
    rowf = lax.broadcasted_iota(jnp.int32, (c, LANES), 0).astype(F32)
    lo = lane < RET_DK
    lg_f = _log_gamma(jnp.where(lo, p_ref[base + 2 * pair], p_ref[base + 2 * pair + 1]))
    lg_b = _log_gamma(jnp.where(lo, p_ref[base + RET_HEADS + 2 * pair],
                                p_ref[base + RET_HEADS + 2 * pair + 1]))
    dvec_ref[Q_DEC_F] = jnp.exp((rowf + 1.0) * lg_f)
    dvec_ref[K_DEC_F] = jnp.exp((c - 1.0 - rowf) * lg_f) * K_SCALE
    dvec_ref[Q_DEC_B] = jnp.exp((c - rowf) * lg_b)
    dvec_ref[K_DEC_B] = jnp.exp(rowf * lg_b) * K_SCALE
    dvec_ref[C_DEC_F] = jnp.exp(c * lg_f)
    dvec_ref[C_DEC_B] = jnp.exp(c * lg_b)


def _decay_tables(decay_p):
    depth = decay_p.shape[0]
    c = MIX_CHUNK
    return pl.pallas_call(
        _decay_kernel,
        grid=(depth, N_PAIRS),
        in_specs=[pl.BlockSpec(memory_space=pltpu.SMEM)],
        out_specs=[
            pl.BlockSpec((None, 2, c, c), lambda l, p: (l, p, 0, 0)),
            pl.BlockSpec((None, None, N_DVEC, c, LANES), lambda l, p: (l, p, 0, 0, 0)),
        ],
        out_shape=[
            jax.ShapeDtypeStruct((depth, RET_HEADS, c, c), F32),
            jax.ShapeDtypeStruct((depth, N_PAIRS, N_DVEC, c, LANES), F32),
        ],
        compiler_params=pltpu.CompilerParams(
            dimension_semantics=("arbitrary", "arbitrary"), vmem_limit_bytes=VMEM_LIMIT),
        name="decay_tables",
    )(decay_p.reshape(-1))


def _mix_kernel(*refs, n_seq, n_sub, latent, n_alias, n_cast, layer):
    c = MIX_CHUNK
    nc = n_seq // c
    (rq_ref, rk_ref, rv_ref, rg_ref, qcat_ref, kvn_ref, krp_ref, gng_ref, gnb_ref,
     dmask_ref, dvec_ref) = refs[:11]
    if latent:
        kvc_ref, krc_ref, sf0_ref, sb0_ref, ret_ref, attn_ref, st_scr = refs[11:]
    else:
        n_in = 11 + n_alias
        cast_src = refs[n_in:n_in + n_cast]
        ret_ref, attn_ref, sf_ref, sb_ref = refs[n_in + n_cast:n_in + n_cast + 4]
        cast_dst = refs[n_in + n_cast + 4:]
        for src, dst in zip(cast_src, cast_dst):
            w = src.shape[1]
            dst[:, 0:w] = src[...].astype(BF16)
            if dst.shape[1] > w:
                dst[:, w:] = jnp.zeros((dst.shape[0], dst.shape[1] - w), BF16)

    lane = lax.broadcasted_iota(jnp.int32, (c, LANES), 1)
    lo = lane < RET_DK
    lrow = slice(layer, layer + 1)
    sq_r = lax.broadcasted_iota(jnp.int32, (LANES, LANES), 0)
    sq_c = lax.broadcasted_iota(jnp.int32, (LANES, LANES), 1)
    blockdiag = (sq_r < RET_DK) == (sq_c < RET_DK)

    def cols(j):
        return slice(j * LANES, (j + 1) * LANES)

    def state_update(pair, d, rows):
        kp = rk_ref[rows, cols(pair)]
        vp = rv_ref[rows, cols(pair)]
        kdt = (kp.astype(F32) * dvec_ref[pair, K_DEC_B if d else K_DEC_F]).T.astype(BF16)
        return jnp.where(blockdiag, _dot(kdt, vp), 0.0)

    nope_n = lax.broadcasted_iota(jnp.int32, (n_seq, LANES), 1) < MLA_NOPE
    if latent:
        nope_c = lax.broadcasted_iota(jnp.int32, (kvc_ref.shape[0], LANES), 1) < MLA_NOPE

    def states(sub):
        base = sub * n_seq
        for pair in range(N_PAIRS):
            for d in range(2):
                if latent:
                    s = (sb0_ref if d else sf0_ref)[pair]
                    cdec = dvec_ref[pair, C_DEC_B if d else C_DEC_F][0:LANES, :]
                    order = list(range(nc - 1, -1, -1)) if d else list(range(nc))
                    for idx, ch in enumerate(order):
                        st_scr[d, pair, ch] = s.astype(BF16)
                        if idx < nc - 1:
                            s = s * cdec + state_update(pair, d, pl.ds(base + ch * c, c))
                else:
                    s = state_update(pair, d, pl.ds(base, c))
                    st_ref = sb_ref if d else sf_ref
                    fresh = n_alias == 0
                    _put_layer(st_ref, sub, (2 * pair,), layer, fresh, s[0:RET_DK, 0:RET_DK])
                    _put_layer(st_ref, sub, (2 * pair + 1,), layer, fresh, s[RET_DK:, RET_DK:])

    def chunk_body(ch, base):
        rows = pl.ds(pl.multiple_of(base + ch * c, c), c)
        keys = pl.ds(base, n_seq)

        for pair in range(N_PAIRS):
            qp = rq_ref[rows, cols(pair)]
            kp = rk_ref[rows, cols(pair)]
            vp = rv_ref[rows, cols(pair)]
            zero = jnp.zeros_like(qp)
            a0 = (_dot_nt(jnp.where(lo, qp, zero), kp) * dmask_ref[2 * pair]).astype(BF16)
            a1 = (_dot_nt(jnp.where(lo, zero, qp), kp) * dmask_ref[2 * pair + 1]).astype(BF16)
            tot = jnp.where(lo, _dot(a0, vp), _dot(a1, vp))
            if latent:
                tot = (tot + _dot(qp, st_scr[0, pair, ch]) * dvec_ref[pair, Q_DEC_F]
                       + _dot(qp, st_scr[1, pair, ch]) * dvec_ref[pair, Q_DEC_B])
            inv = 1.0 / RET_DK
            m0 = jnp.sum(jnp.where(lo, tot, 0.0), axis=-1, keepdims=True) * inv
            m1 = jnp.sum(jnp.where(lo, 0.0, tot), axis=-1, keepdims=True) * inv
            y = tot - jnp.where(lo, m0, m1)
            v0 = jnp.sum(jnp.where(lo, y * y, 0.0), axis=-1, keepdims=True) * inv
            v1 = jnp.sum(jnp.where(lo, 0.0, y * y), axis=-1, keepdims=True) * inv
            yn = (y * lax.rsqrt(jnp.where(lo, v0, v1) + EPS) * gng_ref[lrow, cols(pair)]
                  + gnb_ref[lrow, cols(pair)])
            ret_ref[rows, cols(pair)] = (yn * _silu(rg_ref[rows, cols(pair)])).astype(BF16)

        krp = krp_ref[keys, :]
        for pair in range(N_PAIRS):
            outs = []
            for e in range(2):
                h = 2 * pair + e
                qc = qcat_ref[rows, cols(h)]
                kv = kvn_ref[keys, cols(h)]
                s = _dot_nt(qc, jnp.where(nope_n, kv, krp))
                m = jnp.max(s, axis=-1, keepdims=True)
                if latent:
                    kv_c = kvc_ref[:, cols(h)]
                    s2 = _dot_nt(qc, jnp.where(nope_c, kv_c, krc_ref[...]))
                    m = jnp.maximum(m, jnp.max(s2, axis=-1, keepdims=True))
                    p2 = jnp.exp2(s2 - m)
                p = jnp.exp2(s - m)
                den = jnp.sum(p, axis=-1, keepdims=True)
                acc = _dot(p.astype(BF16), kv)
                if latent:
                    den = den + jnp.sum(p2, axis=-1, keepdims=True)
                    acc = acc + _dot(p2.astype(BF16), kv_c)
                outs.append(acc / den)
            attn_ref[rows, cols(pair)] = jnp.where(
                lo, pltpu.roll(outs[0], MLA_V, 1), outs[1]).astype(BF16)
        return base

    for sub in range(n_sub):
        states(sub)
    for sub in range(n_sub):
        if nc == 1:
            chunk_body(0, sub * n_seq)
        else:
            lax.fori_loop(0, nc, chunk_body, sub * n_seq)


def _mix(proj_out, dmask, dvec, gn_g, gn_b, layer, n_batch, n_seq, latent_in, states_out,
         casts=()):
    rq, rk, rv, rg, qcat, kvn, krp = proj_out[:7]
    latent = latent_in is not None
    aliases = {}
    m = n_batch * n_seq
    c = MIX_CHUNK
    width = MLA_HEADS * HEAD_PAD

    def row(b):
        return (b, 0)

    def lay3(b):
        return (layer, 0, 0)

    once = pl.Buffered(1)
    n_sub = 1 if latent else MIX_SEQS
    n_steps = n_batch // n_sub
    blk = n_sub * n_seq
    in_specs = [
        pl.BlockSpec((blk, RET_WIDTH), row), pl.BlockSpec((blk, RET_WIDTH), row),
        pl.BlockSpec((blk, RET_WIDTH), row), pl.BlockSpec((blk, RET_WIDTH), row),
        pl.BlockSpec((blk, width), row), pl.BlockSpec((blk, width), row),
        pl.BlockSpec((blk, LANES), row),
        _gain_spec(gn_g), _gain_spec(gn_b),
        pl.BlockSpec((None, RET_HEADS, c, c), lambda b: (layer, 0, 0, 0), pipeline_mode=once),
        pl.BlockSpec((None, N_PAIRS, N_DVEC, c, LANES), lambda b: (layer, 0, 0, 0, 0),
                     pipeline_mode=once),
    ]
    args = [rq, rk, rv, rg, qcat, kvn, krp, gn_g, gn_b, dmask, dvec]
    out_shape = [jax.ShapeDtypeStruct((m, RET_WIDTH), BF16),
                 jax.ShapeDtypeStruct((m, MLA_HEADS * MLA_V), BF16)]
    out_specs = [pl.BlockSpec((blk, RET_WIDTH), row), pl.BlockSpec((blk, RET_WIDTH), row)]
    scratch = []
    if latent:
        kvn_c, krp_c, s_f0, s_b0 = latent_in
        past = kvn_c.shape[2]
        st_spec = pl.BlockSpec((None, None, N_PAIRS, LANES, LANES), lambda b: (b, layer, 0, 0, 0))
        in_specs += [
            pl.BlockSpec((None, None, past, width), lambda b: (layer, b, 0, 0)),
            pl.BlockSpec((None, None, past, LANES), lambda b: (b, layer, 0, 0)),
            st_spec, st_spec,
        ]
        args += [kvn_c, krp_c, s_f0, s_b0]
        scratch = [pltpu.VMEM((2, N_PAIRS, n_seq // c, LANES, LANES), BF16)]
    else:
        if _is_array(states_out[0]):
            in_specs += [pl.BlockSpec(memory_space=pl.ANY)] * 2
            aliases = {len(args): len(out_shape), len(args) + 1: len(out_shape) + 1}
            args += list(states_out)
        out_shape += [jax.ShapeDtypeStruct(s.shape, s.dtype) for s in states_out]
        out_specs += [_layer_spec(n_sub, s.shape, layer, bool(aliases)) for s in states_out]
        for w, w_layer, out_cols, stride in casts:
            blocks = n_steps // stride
            rows = w.shape[1] // blocks
            assert rows * blocks == w.shape[1] and rows % 16 == 0 and stride * blocks == n_steps
            in_specs.append(pl.BlockSpec(
                (None, rows, w.shape[2]),
                lambda b, w_layer=w_layer, stride=stride: (w_layer, b // stride, 0)))
            args.append(w)
            out_shape.append(jax.ShapeDtypeStruct((w.shape[1], out_cols), BF16))
            out_specs.append(pl.BlockSpec((rows, out_cols),
                                          lambda b, stride=stride: (b // stride, 0)))

    return pl.pallas_call(
        functools.partial(_mix_kernel, n_seq=n_seq, n_sub=n_sub, latent=latent, layer=layer,
                          n_alias=len(aliases), n_cast=len(casts)),
        grid=(n_steps,),
        in_specs=in_specs,
        out_specs=out_specs,
        out_shape=out_shape,
        input_output_aliases=aliases,
        scratch_shapes=scratch,
        compiler_params=pltpu.CompilerParams(
            dimension_semantics=("arbitrary",), vmem_limit_bytes=VMEM_LIMIT),
        name="mix_latent" if latent else "mix_context",
    )(*args)


def _out_kernel(x_ref, ret_ref, attn_ref, mod_ref, gffn_ref, wo_ref, wfi_ref, wfo_ref, o_ref,
                act_scr, *, cond_row, layer):
    cond = cond_row(pl.program_id(0) * x_ref.shape[0])
    gt1 = _mod_vec(mod_ref, cond, 2)
    sh2 = _mod_vec(mod_ref, cond, 3)
    sc2 = _mod_vec(mod_ref, cond, 4)
    gt2 = _mod_vec(mod_ref, cond, 5)
    mixed = _dot(ret_ref[...], wo_ref[0:RET_WIDTH, :]) + _dot(attn_ref[...], wo_ref[RET_WIDTH:, :])
    x1 = x_ref[...] + gt1 * mixed
    h = (_rms(x1, gffn_ref[layer:layer + 1, :] * (1.0 + sc2)) + sh2).astype(BF16)
    for c0 in range(0, D_FF, FF_CHUNK):
        cw = min(FF_CHUNK, D_FF - c0)
        gate = _dot(h, wfi_ref[:, c0:c0 + cw])
        up = _dot(h, wfi_ref[:, D_FF + c0:D_FF + c0 + cw])
        act_scr[:, c0:c0 + cw] = (_silu(gate) * up).astype(BF16)
    o_ref[...] = x1 + gt2 * _dot(act_scr[...], wfo_ref[...])


def _out(x, ret, attn, mod, layer, cond_row, wts, big_w):
    m = x.shape[0]
    tm = ROW_TILE

    def row(i):
        return (i, 0)

    def lay3(i):
        return (layer, 0, 0)

    def whole(i):
        return (0, 0)

    once = pl.Buffered(1)
    return pl.pallas_call(
        functools.partial(_out_kernel, cond_row=cond_row, layer=layer),
        grid=(m // tm,),
        in_specs=[
            pl.BlockSpec((tm, D_MODEL), row),
            pl.BlockSpec((tm, RET_WIDTH), row),
            pl.BlockSpec((tm, RET_WIDTH), row),
            pl.BlockSpec((None, 8, N_MOD * D_MODEL), lay3),
            _gain_spec(wts["g_ffn"]),
            pl.BlockSpec((D_MODEL, D_MODEL), whole, pipeline_mode=once),
            pl.BlockSpec((D_MODEL, 2 * D_FF), whole, pipeline_mode=once),
            pl.BlockSpec((D_FF, D_MODEL), whole, pipeline_mode=once),
        ],
        out_specs=pl.BlockSpec((tm, D_MODEL), row),
        out_shape=jax.ShapeDtypeStruct((m, D_MODEL), F32),
        scratch_shapes=[pltpu.VMEM((tm, D_FF), BF16)],
        compiler_params=pltpu.CompilerParams(
            dimension_semantics=("arbitrary",), vmem_limit_bytes=VMEM_LIMIT),
        name="out_ffn",
    )(x, ret, attn, mod, wts["g_ffn"], *big_w)


def _rope_tables(n_lat):
    pos = np.arange(n_lat)
    row = (pos // GRID_W).astype(np.float32)[:, None]
    col = (pos % GRID_W).astype(np.float32)[:, None]
    lane = np.arange(LANES)[None, :]

    def tables(d, start, period):
        rel = (lane - start) % period
        active = np.logical_and(lane >= start, rel < d)
        half = d // 2
        nf = half // 2
        inv = np.float32(ROPE_BASE) ** (-((rel % nf).astype(np.float32)) / np.float32(nf))
        ang = (np.where(rel < half, row, col) * inv).astype(np.float32)
        cos, sin = np.cos(ang), np.sin(ang)
        first = (rel % half) < nf
        c = np.where(active, cos, 1.0)
        s = np.where(active, np.where(first, -sin, sin), 0.0)
        return tuple(jnp.asarray(t, F32) for t in (c, s))

    return tables(RET_DK, 0, RET_DK) + tables(MLA_ROPE, MLA_NOPE, LANES)


def _swap_partners(a, axis, d):
    shape = a.shape
    split = shape[:axis] + (shape[axis] // d, 2, 2, d // 4) + shape[axis + 1:]
    return jnp.flip(a.reshape(split), axis=axis + 2).reshape(shape)


def _prepare_weights(g_norm_mix, g_norm_ffn, w_in, g_q_a, w_q_b, g_kv_a, w_kv_b, g_qn, g_qr, g_kn,
                     g_kr):
    depth = w_in.shape[0]
    n_in = w_in.shape[2]
    w_in_t = jnp.pad(jnp.swapaxes(w_in, 1, 2).astype(BF16),
                     ((0, 0), (0, IN_COLS_PAD - n_in), (0, 0)))
    w_qb = w_q_b.reshape(depth, Q_RANK, MLA_HEADS, MLA_NOPE + MLA_ROPE)
    w_qb_sw = jnp.pad(_swap_partners(w_qb[..., MLA_NOPE:], 3, MLA_ROPE),
                      ((0, 0), (0, 0), (0, 0), (MLA_NOPE, HEAD_PAD - MLA_NOPE - MLA_ROPE)))
    w_qb = jnp.pad(w_qb, ((0, 0), (0, 0), (0, 0), (0, HEAD_PAD - MLA_NOPE - MLA_ROPE)))
    zeros32 = jnp.zeros((depth, MLA_ROPE), F32)
    zeros64 = jnp.zeros((depth, MLA_NOPE), F32)
    return {
        "g_mix": g_norm_mix,
        "g_ffn": g_norm_ffn,
        "w_in_t": w_in_t,
        "g_qa": g_q_a,
        "w_qb": w_qb.reshape(depth, Q_RANK, MLA_HEADS * HEAD_PAD).astype(BF16),
        "g_kva": g_kv_a,
        "w_kvb": w_kv_b.astype(BF16),
        "g_q": jnp.concatenate([g_qn, g_qr, zeros32], axis=-1) * Q_FOLD,
        "g_kn": jnp.concatenate([g_kn, jnp.ones((depth, MLA_V), F32)], axis=-1),
        "g_kr": jnp.pad(g_kr, ((0, 0), (0, LANES - MLA_ROPE))),
        "w_qb_sw": w_qb_sw.reshape(depth, Q_RANK, MLA_HEADS * HEAD_PAD).astype(BF16),
        "g_q_sw": jnp.concatenate([zeros64, _swap_partners(g_qr, 1, MLA_ROPE), zeros32],
                                  axis=-1) * Q_FOLD,
    }


def _blockdiag_states(s):
    b, l = s.shape[:2]
    s = s.reshape(b, l, N_PAIRS, 2, RET_DK, RET_DK)
    z = jnp.zeros_like(s[:, :, :, 0])
    top = jnp.concatenate([s[:, :, :, 0], z], axis=-1)
    bot = jnp.concatenate([z, s[:, :, :, 1]], axis=-1)
    return jnp.concatenate([top, bot], axis=-2)


def kernel(x_prompt, x_sample, cache_ckv, cache_krope, state_ret_fwd, state_ret_bwd, c, c_ctx,
           w_mod, b_mod, g_norm_mix, g_norm_ffn, w_in, g_q_a, w_q_b, g_kv_a, w_kv_b,
           g_qn, g_qr, g_kn, g_kr, ret_p_fwd, ret_p_bwd, g_ret_gn, b_ret_gn, w_o,
           w_ffn_in, w_ffn_out):
    batch, seq, _ = x_prompt.shape
    dec_batch, dec_seq, _ = x_sample.shape
    depth = w_in.shape[0]

    wts = _prepare_weights(g_norm_mix, g_norm_ffn, w_in, g_q_a, w_q_b, g_kv_a, w_kv_b, g_qn, g_qr,
                           g_kn, g_kr)
    conds = jnp.concatenate([c_ctx[None], c, jnp.zeros((8 - 1 - dec_batch, D_MODEL), F32)], axis=0)
    mod = _modulation(conds, w_mod, b_mod)
    dmask, dvec = _decay_tables(jnp.stack([ret_p_fwd, ret_p_bwd], axis=1))
    gn_g, gn_b = g_ret_gn, b_ret_gn

    rope_tabs = _rope_tables(dec_seq)
    kvn_cache = _cache_up(cache_ckv, wts["w_kvb"], wts["g_kn"])
    krp_cache = jnp.pad(cache_krope, ((0, 0), (0, 0), (0, 0), (MLA_NOPE, LANES - MLA_NOPE - MLA_ROPE)))
    krp_cache = krp_cache.astype(BF16)
    s_f0 = _blockdiag_states(state_ret_fwd)
    s_b0 = _blockdiag_states(state_ret_bwd)

    def ctx_row(row):
        return 0

    def lat_row(row):
        return 1 + row // dec_seq

    x = x_prompt.reshape(batch * seq, D_MODEL)
    y = x_sample.reshape(dec_batch * dec_seq, D_MODEL)
    caches = (jax.ShapeDtypeStruct((batch, depth, seq, KV_RANK), F32),
              jax.ShapeDtypeStruct((batch, depth, MLA_ROPE, seq), F32))
    states = (jax.ShapeDtypeStruct((batch, depth, RET_HEADS, RET_DK, RET_DK), F32),) * 2
    for l in range(depth):
        pr = _proj(x, mod, l, ctx_row, wts, None, caches)
        caches = tuple(pr[7:9])
        casts = [(w_o, l, D_MODEL, 1), (w_ffn_in, l, 2 * D_FF, 1), (w_ffn_out, l, D_MODEL, 2)]
        mixed = _mix(pr, dmask, dvec, gn_g, gn_b, l, batch, seq, None, states, casts)
        ret, attn = mixed[:2]
        states = tuple(mixed[2:4])
        big_w = tuple(mixed[4:7])
        x = _out(x, ret, attn, mod, l, ctx_row, wts, big_w)

        pr = _proj(y, mod, l, lat_row, wts, rope_tabs, None)
        ret, attn = _mix(pr, dmask, dvec, gn_g, gn_b, l, dec_batch, dec_seq,
                         (kvn_cache, krp_cache, s_f0, s_b0), None)
        y = _out(y, ret, attn, mod, l, lat_row, wts, big_w)

    return (x.reshape(batch, seq, D_MODEL), y.reshape(dec_batch, dec_seq, D_MODEL),
            caches[0], jnp.swapaxes(caches[1], 2, 3), states[0], states[1])
```

```python
import functools

import jax
import jax.numpy as jnp
import numpy as np
from jax import lax
from jax.experimental import pallas as pl
from jax.experimental.pallas import tpu as pltpu

D_MODEL = 1024
N_MOD = 6
RET_HEADS = 8
RET_DK = 64
RET_WIDTH = 512
MLA_HEADS = 8
MLA_NOPE = 64
MLA_ROPE = 32
MLA_V = 64
Q_RANK = 256
KV_RANK = 128
D_FF = 2816
GRID_W = 64
ROPE_BASE = 10000.0
EPS = 1e-6

LANES = 128
HEAD_PAD = LANES
N_PAIRS = RET_HEADS // 2
IN_COLS_PAD = 4 * RET_WIDTH + Q_RANK + KV_RANK + LANES
ROW_TILE = 512
PROJ_TILE = 1024
LATENT_PROJ_TILE = 512
SUB_TILE = 256
FF_CHUNK = 256
MIX_CHUNK = 256
MIX_SEQS = 4
Q_DEC_F, K_DEC_F, Q_DEC_B, K_DEC_B, C_DEC_F, C_DEC_B = range(6)
N_DVEC = 6
K_SCALE = RET_DK ** -0.5
Q_FOLD = (MLA_NOPE + MLA_ROPE) ** -0.5 * 1.4426950408889634
VMEM_LIMIT = 56 * 1024 * 1024

BF16 = jnp.bfloat16
F32 = jnp.float32
_NT = (((1,), (1,)), ((), ()))


def _dot(a, b):
    return jnp.dot(a, b, preferred_element_type=F32)


def _dot_nt(a, b):
    return lax.dot_general(a, b, _NT, preferred_element_type=F32)


def _rms(x, g):
    return x * lax.rsqrt(jnp.mean(x * x, axis=-1, keepdims=True) + EPS) * g


def _silu(x):
    return x * jax.nn.sigmoid(x)


def _masked_mean_sq(x, mask, n):
    return jnp.sum(jnp.where(mask, x * x, 0.0), axis=-1, keepdims=True) * (1.0 / n)


def _mod_kernel(c_ref, w_ref, b_ref, o_ref):
    a = _silu(c_ref[...])
    w = w_ref[...]
    a_hi = a.astype(BF16)
    a_lo = (a - a_hi.astype(F32)).astype(BF16)
    w_hi = w.astype(BF16)
    w_lo = (w - w_hi.astype(F32)).astype(BF16)
    both = _dot(jnp.concatenate([a_hi, a_lo], axis=0), w_hi)
    part = both[0:8] + both[8:16] + _dot(a_hi, w_lo)

    @pl.when(pl.program_id(1) == 0)
    def _():
        o_ref[...] = part + b_ref[pl.ds(pl.program_id(0), 1), :]

    @pl.when(pl.program_id(1) > 0)
    def _():
        o_ref[...] += part


def _modulation(conds, w_mod, b_mod):
    depth, d, n = w_mod.shape
    tk = 256
    return pl.pallas_call(
        _mod_kernel,
        grid=(depth, d // tk),
        in_specs=[
            pl.BlockSpec((8, tk), lambda l, k: (0, k)),
            pl.BlockSpec((None, tk, n), lambda l, k: (l, k, 0)),
            pl.BlockSpec((depth, n), lambda l, k: (0, 0)),
        ],
        out_specs=pl.BlockSpec((None, 8, n), lambda l, k: (l, 0, 0)),
        out_shape=jax.ShapeDtypeStruct((depth, 8, n), F32),
        compiler_params=pltpu.CompilerParams(
            dimension_semantics=("arbitrary", "arbitrary"), vmem_limit_bytes=VMEM_LIMIT),
        name="modulation",
    )(conds, w_mod, b_mod)


def _norm_kn(kv, gkn):
    lane = lax.broadcasted_iota(jnp.int32, (kv.shape[0], LANES), 1)
    lo = lane < MLA_NOPE
    out = []
    for h in range(MLA_HEADS):
        kvh = kv[:, h * HEAD_PAD:(h + 1) * HEAD_PAD]
        rs = lax.rsqrt(_masked_mean_sq(kvh, lo, MLA_NOPE) + EPS)
        out.append((kvh * jnp.where(lo, rs * gkn, 1.0)).astype(BF16))
    return out


def _cache_kernel(ckv_ref, wkvb_ref, gkn_ref, kvn_ref):
    kv = _dot(ckv_ref[...].astype(BF16), wkvb_ref[...])
    gkn = gkn_ref[pl.ds(pl.program_id(0), 1), :]
    for h, kvh in enumerate(_norm_kn(kv, gkn)):
        kvn_ref[:, h * HEAD_PAD:(h + 1) * HEAD_PAD] = kvh


def _cache_up(cache_ckv, wkvb, gkn):
    nb, depth, past, _ = cache_ckv.shape
    width = MLA_HEADS * HEAD_PAD
    return pl.pallas_call(
        _cache_kernel,
        grid=(depth, nb),
        in_specs=[
            pl.BlockSpec((None, None, past, KV_RANK), lambda l, b: (b, l, 0, 0)),
            pl.BlockSpec((None, KV_RANK, width), lambda l, b: (l, 0, 0)),
            pl.BlockSpec((depth, LANES), lambda l, b: (0, 0)),
        ],
        out_specs=pl.BlockSpec((None, None, past, width), lambda l, b: (l, b, 0, 0)),
        out_shape=jax.ShapeDtypeStruct((depth, nb, past, width), BF16),
        compiler_params=pltpu.CompilerParams(
            dimension_semantics=("arbitrary", "arbitrary"), vmem_limit_bytes=VMEM_LIMIT),
        name="cache_up",
    )(cache_ckv, wkvb, gkn)


def _mod_vec(mod_ref, cond, k):
    return mod_ref[pl.ds(cond, 1), k * D_MODEL:(k + 1) * D_MODEL]


def _gain_spec(g):
    return pl.BlockSpec(g.shape, lambda *_: (0, 0))


def _is_array(x):
    return not isinstance(x, jax.ShapeDtypeStruct)


def _layer_spec(n, shape, layer, aliased):
    rest = tuple(shape[2:])
    zeros = (0,) * len(rest)
    if aliased:
        return pl.BlockSpec((n, None) + rest, lambda i: (i, layer) + zeros)
    return pl.BlockSpec((n, shape[1]) + rest, lambda i: (i, 0) + zeros)


def _put_layer(ref, i, tail, layer, fresh, value):
    if not fresh:
        ref[(i,) + tail] = value
        return
    for l in range(ref.shape[1]):
        ref[(i, l) + tail] = value if l == layer else jnp.zeros_like(value)


def _proj_kernel(*refs, rope, cond_row, n_alias, layer):
    (x_ref, mod_ref, gmix_ref, win_ref, gqa_ref, wqb_ref, gkva_ref, wkvb_ref,
     gq_ref, gkn_ref, gkr_ref) = refs[:11]
    if rope:
        wqb_sw_ref, gq_sw_ref, c64_ref, s64_ref, c32_ref, s32_ref = refs[11:17]
        rq_ref, rk_ref, rv_ref, rg_ref, qcat_ref, kvn_ref, krp_ref = refs[17:]
    else:
        (rq_ref, rk_ref, rv_ref, rg_ref, qcat_ref, kvn_ref, krp_ref, ckv_ref,
         kro_ref) = refs[11 + n_alias:]

    cond = cond_row(pl.program_id(0) * x_ref.shape[0])
    sh1 = _mod_vec(mod_ref, cond, 0)
    sc1 = _mod_vec(mod_ref, cond, 1)
    lrow = slice(layer, layer + 1)
    g_mod = gmix_ref[lrow, :] * (1.0 + sc1)
    w = RET_WIDTH
    lane = lax.broadcasted_iota(jnp.int32, (SUB_TILE, LANES), 1)
    nope = lane < MLA_NOPE
    is_rope = jnp.logical_and(lane >= MLA_NOPE, lane < MLA_NOPE + MLA_ROPE)

    for r in range(x_ref.shape[0] // SUB_TILE):
        rs = slice(r * SUB_TILE, (r + 1) * SUB_TILE)
        h = (_rms(x_ref[rs, :], g_mod) + sh1).astype(BF16)
        z = _dot_nt(h, win_ref[...])
        if rope:
            grp = RET_DK // 4
            z_sw = _dot_nt(h, jnp.concatenate(
                [win_ref[(g ^ 1) * grp:((g ^ 1) + 1) * grp, :] for g in range(2 * w // grp)], 0))

        for j in range(w // LANES):
            sl = slice(j * LANES, (j + 1) * LANES)
            q = z[:, j * LANES:(j + 1) * LANES]
            k = z[:, w + j * LANES:w + (j + 1) * LANES]
            if rope:
                q_sw = z_sw[:, j * LANES:(j + 1) * LANES]
                k_sw = z_sw[:, w + j * LANES:w + (j + 1) * LANES]
                q = q * c64_ref[rs, :] + q_sw * s64_ref[rs, :]
                k = k * c64_ref[rs, :] + k_sw * s64_ref[rs, :]
            rq_ref[rs, sl] = q.astype(BF16)
            rk_ref[rs, sl] = k.astype(BF16)
        rv_ref[rs, :] = z[:, 2 * w:3 * w].astype(BF16)
        rg_ref[rs, :] = z[:, 3 * w:4 * w]

        o = 4 * w
        qa = z[:, o:o + Q_RANK]
        kva = z[:, o + Q_RANK:o + Q_RANK + KV_RANK]
        kr2 = z[:, o + Q_RANK + KV_RANK:]

        qa_n = _rms(qa, gqa_ref[lrow, :]).astype(BF16)
        q = _dot(qa_n, wqb_ref[...])
        if rope:
            q_sw = _dot(qa_n, wqb_sw_ref[...])
        for hh in range(MLA_HEADS):
            hs = slice(hh * HEAD_PAD, (hh + 1) * HEAD_PAD)
            qh = q[:, hs]
            rs_n = lax.rsqrt(_masked_mean_sq(qh, nope, MLA_NOPE) + EPS)
            rs_r = lax.rsqrt(_masked_mean_sq(qh, is_rope, MLA_ROPE) + EPS)
            qn = qh * jnp.where(nope, rs_n, rs_r) * gq_ref[lrow, :]
            if rope:
                qn = (qn * c32_ref[rs, :]
                      + q_sw[:, hs] * rs_r * gq_sw_ref[lrow, :] * s32_ref[rs, :])
            qcat_ref[rs, hs] = qn.astype(BF16)

        ckv = _rms(kva, gkva_ref[lrow, :])
        kv = _dot(ckv.astype(BF16), wkvb_ref[...])
        for hh, kvh in enumerate(_norm_kn(kv, gkn_ref[lrow, :])):
            kvn_ref[rs, hh * HEAD_PAD:(hh + 1) * HEAD_PAD] = kvh

        rs_k = lax.rsqrt(_masked_mean_sq(kr2, lane < MLA_ROPE, MLA_ROPE) + EPS)
        krn = kr2 * rs_k * gkr_ref[lrow, :]
        krp = pltpu.roll(krn, MLA_NOPE, 1)
        if rope:
            x1_pos = ((lane - MLA_NOPE) % (MLA_ROPE // 2)) < MLA_ROPE // 4
            partner = jnp.where(x1_pos, pltpu.roll(krp, LANES - MLA_ROPE // 4, 1),
                                pltpu.roll(krp, MLA_ROPE // 4, 1))
            krp = krp * c32_ref[rs, :] + partner * s32_ref[rs, :]
        else:
            seq = ckv_ref.shape[-2]
            krt = krn.T[0:MLA_ROPE, :]
            for t in range(SUB_TILE // seq):
                i = r * (SUB_TILE // seq) + t
                _put_layer(ckv_ref, i, (), layer, n_alias == 0, ckv[t * seq:(t + 1) * seq])
                _put_layer(kro_ref, i, (), layer, n_alias == 0, krt[:, t * seq:(t + 1) * seq])
        krp_ref[rs, :] = krp.astype(BF16)


def _proj(x, mod, layer, cond_row, wts, rope_tabs, caches):
    m = x.shape[0]
    rope = rope_tabs is not None
    tm = LATENT_PROJ_TILE if rope else PROJ_TILE
    width = MLA_HEADS * HEAD_PAD

    def row(i):
        return (i, 0)

    def lay3(i):
        return (layer, 0, 0)

    in_specs = [
        pl.BlockSpec((tm, D_MODEL), row),
        pl.BlockSpec((None, 8, N_MOD * D_MODEL), lay3),
        _gain_spec(wts["g_mix"]),
        pl.BlockSpec((None, IN_COLS_PAD, D_MODEL), lay3),
        _gain_spec(wts["g_qa"]),
        pl.BlockSpec((None, Q_RANK, width), lay3),
        _gain_spec(wts["g_kva"]),
        pl.BlockSpec((None, KV_RANK, width), lay3),
        _gain_spec(wts["g_q"]), _gain_spec(wts["g_kn"]), _gain_spec(wts["g_kr"]),
    ]
    args = [x, mod, wts["g_mix"], wts["w_in_t"], wts["g_qa"], wts["w_qb"], wts["g_kva"],
            wts["w_kvb"], wts["g_q"], wts["g_kn"], wts["g_kr"]]
    out_shape = [
        jax.ShapeDtypeStruct((m, RET_WIDTH), BF16),
        jax.ShapeDtypeStruct((m, RET_WIDTH), BF16),
        jax.ShapeDtypeStruct((m, RET_WIDTH), BF16),
        jax.ShapeDtypeStruct((m, RET_WIDTH), F32),
        jax.ShapeDtypeStruct((m, width), BF16),
        jax.ShapeDtypeStruct((m, width), BF16),
        jax.ShapeDtypeStruct((m, LANES), BF16),
    ]
    out_specs = [
        pl.BlockSpec((tm, RET_WIDTH), row), pl.BlockSpec((tm, RET_WIDTH), row),
        pl.BlockSpec((tm, RET_WIDTH), row), pl.BlockSpec((tm, RET_WIDTH), row),
        pl.BlockSpec((tm, width), row), pl.BlockSpec((tm, width), row),
        pl.BlockSpec((tm, LANES), row),
    ]
    aliases = {}
    if rope:
        n_lat = rope_tabs[0].shape[0]
        tiles = n_lat // tm
        in_specs += [pl.BlockSpec((None, Q_RANK, width), lay3), _gain_spec(wts["g_q_sw"])]
        args += [wts["w_qb_sw"], wts["g_q_sw"]]
        in_specs += [pl.BlockSpec((tm, LANES), lambda i: (i % tiles, 0))] * len(rope_tabs)
        args += list(rope_tabs)
    else:
        seq = caches[0].shape[2]
        assert SUB_TILE % seq == 0
        nb = tm // seq
        if _is_array(caches[0]):
            in_specs += [pl.BlockSpec(memory_space=pl.ANY)] * 2
            aliases = {len(args): len(out_shape), len(args) + 1: len(out_shape) + 1}
            args += list(caches)
        out_shape += [jax.ShapeDtypeStruct(c.shape, c.dtype) for c in caches]
        out_specs += [_layer_spec(nb, c.shape, layer, bool(aliases)) for c in caches]

    return pl.pallas_call(
        functools.partial(_proj_kernel, rope=rope, cond_row=cond_row, n_alias=len(aliases),
                          layer=layer),
        grid=(m // tm,),
        in_specs=in_specs,
        out_specs=out_specs,
        out_shape=out_shape,
        input_output_aliases=aliases,
        compiler_params=pltpu.CompilerParams(
            dimension_semantics=("arbitrary",), vmem_limit_bytes=VMEM_LIMIT),
        name="proj_latent" if rope else "proj_context",
    )(*args)


def _log_gamma(p):
    return jnp.log1p(-jnp.exp2(-p))


def _decay_kernel(p_ref, dmask_ref, dvec_ref):
    c = MIX_CHUNK
    base = pl.program_id(0) * (2 * RET_HEADS)
    pair = pl.program_id(1)
    ri = lax.broadcasted_iota(jnp.int32, (c, c), 0)
    ci = lax.broadcasted_iota(jnp.int32, (c, c), 1)
    dif = (ri - ci).astype(F32)
    for e in range(2):
        h = 2 * pair + e
        lg_f = _log_gamma(jnp.full((c, c), p_ref[base + h], F32))
        lg_b = _log_gamma(jnp.full((c, c), p_ref[base + RET_HEADS + h], F32))
        fwd = jnp.where(dif >= 0, jnp.exp(jnp.maximum(dif, 0.0) * lg_f), 0.0)
        bwd = jnp.where(dif <= 0, jnp.exp(jnp.maximum(-dif, 0.0) * lg_b), 0.0)
        dmask_ref[e] = (fwd + bwd) * K_SCALE
    lane = lax.broadcasted_iota(jnp.int32, (c, LANES), 1)
    rowf = lax.broadcasted_iota(jnp.int32, (c, LANES), 0).astype(F32)
    lo = lane < RET_DK
    lg_f = _log_gamma(jnp.where(lo, p_ref[base + 2 * pair], p_ref[base + 2 * pair + 1]))
    lg_b = _log_gamma(jnp.where(lo, p_ref[base + RET_HEADS + 2 * pair],
                                p_ref[base + RET_HEADS + 2 * pair + 1]))
    dvec_ref[Q_DEC_F] = jnp.exp((rowf + 1.0) * lg_f)
    dvec_ref[K_DEC_F] = jnp.exp((c - 1.0 - rowf) * lg_f) * K_SCALE
    dvec_ref[Q_DEC_B] = jnp.exp((c - rowf) * lg_b)
    dvec_ref[K_DEC_B] = jnp.exp(rowf * lg_b) * K_SCALE
    dvec_ref[C_DEC_F] = jnp.exp(c * lg_f)
    dvec_ref[C_DEC_B] = jnp.exp(c * lg_b)


def _decay_tables(decay_p):
    depth = decay_p.shape[0]
    c = MIX_CHUNK
    return pl.pallas_call(
        _decay_kernel,
        grid=(depth, N_PAIRS),
        in_specs=[pl.BlockSpec(memory_space=pltpu.SMEM)],
        out_specs=[
            pl.BlockSpec((None, 2, c, c), lambda l, p: (l, p, 0, 0)),
            pl.BlockSpec((None, None, N_DVEC, c, LANES), lambda l, p: (l, p, 0, 0, 0)),
        ],
        out_shape=[
            jax.ShapeDtypeStruct((depth, RET_HEADS, c, c), F32),
            jax.ShapeDtypeStruct((depth, N_PAIRS, N_DVEC, c, LANES), F32),
        ],
        compiler_params=pltpu.CompilerParams(
            dimension_semantics=("arbitrary", "arbitrary"), vmem_limit_bytes=VMEM_LIMIT),
        name="decay_tables",
    )(decay_p.reshape(-1))


def _mix_kernel(*refs, n_seq, n_sub, latent, n_alias, n_cast, layer):
    c = MIX_CHUNK
    nc = n_seq // c
    (rq_ref, rk_ref, rv_ref, rg_ref, qcat_ref, kvn_ref, krp_ref, gng_ref, gnb_ref,
     dmask_ref, dvec_ref) = refs[:11]
    if latent:
        kvc_ref, krc_ref, sf0_ref, sb0_ref, ret_ref, attn_ref, st_scr = refs[11:]
    else:
        n_in = 11 + n_alias
        cast_src = refs[n_in:n_in + n_cast]
        ret_ref, attn_ref, sf_ref, sb_ref = refs[n_in + n_cast:n_in + n_cast + 4]
        cast_dst = refs[n_in + n_cast + 4:]
        for src, dst in zip(cast_src, cast_dst):
            w = src.shape[1]
            dst[:, 0:w] = src[...].astype(BF16)
            if dst.shape[1] > w:
                dst[:, w:] = jnp.zeros((dst.shape[0], dst.shape[1] - w), BF16)

    lane = lax.broadcasted_iota(jnp.int32, (c, LANES), 1)
    lo = lane < RET_DK
    lrow = slice(layer, layer + 1)
    sq_r = lax.broadcasted_iota(jnp.int32, (LANES, LANES), 0)
    sq_c = lax.broadcasted_iota(jnp.int32, (LANES, LANES), 1)
    blockdiag = (sq_r < RET_DK) == (sq_c < RET_DK)

    def cols(j):
        return slice(j * LANES, (j + 1) * LANES)

    def state_update(pair, d, rows):
        kp = rk_ref[rows, cols(pair)]
        vp = rv_ref[rows, cols(pair)]
        kdt = (kp.astype(F32) * dvec_ref[pair, K_DEC_B if d else K_DEC_F]).T.astype(BF16)
        return jnp.where(blockdiag, _dot(kdt, vp), 0.0)

    nope_n = lax.broadcasted_iota(jnp.int32, (n_seq, LANES), 1) < MLA_NOPE
    if latent:
        nope_c = lax.broadcasted_iota(jnp.int32, (kvc_ref.shape[0], LANES), 1) < MLA_NOPE

    def states(sub):
        base = sub * n_seq
        for pair in range(N_PAIRS):
            for d in range(2):
                if latent:
                    s = (sb0_ref if d else sf0_ref)[pair]
                    cdec = dvec_ref[pair, C_DEC_B if d else C_DEC_F][0:LANES, :]
                    order = list(range(nc - 1, -1, -1)) if d else list(range(nc))
                    for idx, ch in enumerate(order):
                        st_scr[d, pair, ch] = s.astype(BF16)
                        if idx < nc - 1:
                            s = s * cdec + state_update(pair, d, pl.ds(base + ch * c, c))
                else:
                    s = state_update(pair, d, pl.ds(base, c))
                    st_ref = sb_ref if d else sf_ref
                    fresh = n_alias == 0
                    _put_layer(st_ref, sub, (2 * pair,), layer, fresh, s[0:RET_DK, 0:RET_DK])
                    _put_layer(st_ref, sub, (2 * pair + 1,), layer, fresh, s[RET_DK:, RET_DK:])

    def chunk_body(ch, base):
        rows = pl.ds(pl.multiple_of(base + ch * c, c), c)
        keys = pl.ds(base, n_seq)

        for pair in range(N_PAIRS):
            qp = rq_ref[rows, cols(pair)]
            kp = rk_ref[rows, cols(pair)]
            vp = rv_ref[rows, cols(pair)]
            zero = jnp.zeros_like(qp)
            a0 = (_dot_nt(jnp.where(lo, qp, zero), kp) * dmask_ref[2 * pair]).astype(BF16)
            a1 = (_dot_nt(jnp.where(lo, zero, qp), kp) * dmask_ref[2 * pair + 1]).astype(BF16)
            tot = jnp.where(lo, _dot(a0, vp), _dot(a1, vp))
            if latent:
                tot = (tot + _dot(qp, st_scr[0, pair, ch]) * dvec_ref[pair, Q_DEC_F]
                       + _dot(qp, st_scr[1, pair, ch]) * dvec_ref[pair, Q_DEC_B])
            inv = 1.0 / RET_DK
            m0 = jnp.sum(jnp.where(lo, tot, 0.0), axis=-1, keepdims=True) * inv
            m1 = jnp.sum(jnp.where(lo, 0.0, tot), axis=-1, keepdims=True) * inv
            y = tot - jnp.where(lo, m0, m1)
            v0 = jnp.sum(jnp.where(lo, y * y, 0.0), axis=-1, keepdims=True) * inv
            v1 = jnp.sum(jnp.where(lo, 0.0, y * y), axis=-1, keepdims=True) * inv
            yn = (y * lax.rsqrt(jnp.where(lo, v0, v1) + EPS) * gng_ref[lrow, cols(pair)]
                  + gnb_ref[lrow, cols(pair)])
            ret_ref[rows, cols(pair)] = (yn * _silu(rg_ref[rows, cols(pair)])).astype(BF16)

        krp = krp_ref[keys, :]
        for pair in range(N_PAIRS):
            outs = []
            for e in range(2):
                h = 2 * pair + e
                qc = qcat_ref[rows, cols(h)]
                kv = kvn_ref[keys, cols(h)]
                s = _dot_nt(qc, jnp.where(nope_n, kv, krp))
                m = jnp.max(s, axis=-1, keepdims=True)
                if latent:
                    kv_c = kvc_ref[:, cols(h)]
                    s2 = _dot_nt(qc, jnp.where(nope_c, kv_c, krc_ref[...]))
                    m = jnp.maximum(m, jnp.max(s2, axis=-1, keepdims=True))
                    p2 = jnp.exp2(s2 - m)
                p = jnp.exp2(s - m)
                den = jnp.sum(p, axis=-1, keepdims=True)
                acc = _dot(p.astype(BF16), kv)
                if latent:
                    den = den + jnp.sum(p2, axis=-1, keepdims=True)
                    acc = acc + _dot(p2.astype(BF16), kv_c)
                outs.append(acc / den)
            attn_ref[rows, cols(pair)] = jnp.where(
                lo, pltpu.roll(outs[0], MLA_V, 1), outs[1]).astype(BF16)
        return base

    for sub in range(n_sub):
        states(sub)
    for sub in range(n_sub):
        if nc == 1:
            chunk_body(0, sub * n_seq)
        else:
            lax.fori_loop(0, nc, chunk_body, sub * n_seq)


def _mix(proj_out, dmask, dvec, gn_g, gn_b, layer, n_batch, n_seq, latent_in, states_out,
         casts=()):
    rq, rk, rv, rg, qcat, kvn, krp = proj_out[:7]
    latent = latent_in is not None
    aliases = {}
    m = n_batch * n_seq
    c = MIX_CHUNK
    width = MLA_HEADS * HEAD_PAD

    def row(b):
        return (b, 0)

    def lay3(b):
        return (layer, 0, 0)

    once = pl.Buffered(1)
    n_sub = 1 if latent else MIX_SEQS
    n_steps = n_batch // n_sub
    blk = n_sub * n_seq
    in_specs = [
        pl.BlockSpec((blk, RET_WIDTH), row), pl.BlockSpec((blk, RET_WIDTH), row),
        pl.BlockSpec((blk, RET_WIDTH), row), pl.BlockSpec((blk, RET_WIDTH), row),
        pl.BlockSpec((blk, width), row), pl.BlockSpec((blk, width), row),
        pl.BlockSpec((blk, LANES), row),
        _gain_spec(gn_g), _gain_spec(gn_b),
        pl.BlockSpec((None, RET_HEADS, c, c), lambda b: (layer, 0, 0, 0), pipeline_mode=once),
        pl.BlockSpec((None, N_PAIRS, N_DVEC, c, LANES), lambda b: (layer, 0, 0, 0, 0),
                     pipeline_mode=once),
    ]
    args = [rq, rk, rv, rg, qcat, kvn, krp, gn_g, gn_b, dmask, dvec]
    out_shape = [jax.ShapeDtypeStruct((m, RET_WIDTH), BF16),
                 jax.ShapeDtypeStruct((m, MLA_HEADS * MLA_V), BF16)]
    out_specs = [pl.BlockSpec((blk, RET_WIDTH), row), pl.BlockSpec((blk, RET_WIDTH), row)]
    scratch = []
    if latent:
        kvn_c, krp_c, s_f0, s_b0 = latent_in
        past = kvn_c.shape[2]
        st_spec = pl.BlockSpec((None, None, N_PAIRS, LANES, LANES), lambda b: (b, layer, 0, 0, 0))
        in_specs += [
            pl.BlockSpec((None, None, past, width), lambda b: (layer, b, 0, 0)),
            pl.BlockSpec((None, None, past, LANES), lambda b: (b, layer, 0, 0)),
            st_spec, st_spec,
        ]
        args += [kvn_c, krp_c, s_f0, s_b0]
        scratch = [pltpu.VMEM((2, N_PAIRS, n_seq // c, LANES, LANES), BF16)]
    else:
        if _is_array(states_out[0]):
            in_specs += [pl.BlockSpec(memory_space=pl.ANY)] * 2
            aliases = {len(args): len(out_shape), len(args) + 1: len(out_shape) + 1}
            args += list(states_out)
        out_shape += [jax.ShapeDtypeStruct(s.shape, s.dtype) for s in states_out]
        out_specs += [_layer_spec(n_sub, s.shape, layer, bool(aliases)) for s in states_out]
        for w, w_layer, out_cols, stride in casts:
            blocks = n_steps // stride
            rows = w.shape[1] // blocks
            assert rows * blocks == w.shape[1] and rows % 16 == 0 and stride * blocks == n_steps
            in_specs.append(pl.BlockSpec(
                (None, rows, w.shape[2]),
                lambda b, w_layer=w_layer, stride=stride: (w_layer, b // stride, 0)))
            args.append(w)
            out_shape.append(jax.ShapeDtypeStruct((w.shape[1], out_cols), BF16))
            out_specs.append(pl.BlockSpec((rows, out_cols),
                                          lambda b, stride=stride: (b // stride, 0)))

    return pl.pallas_call(
        functools.partial(_mix_kernel, n_seq=n_seq, n_sub=n_sub, latent=latent, layer=layer,
                          n_alias=len(aliases), n_cast=len(casts)),
        grid=(n_steps,),
        in_specs=in_specs,
        out_specs=out_specs,
        out_shape=out_shape,
        input_output_aliases=aliases,
        scratch_shapes=scratch,
        compiler_params=pltpu.CompilerParams(
            dimension_semantics=("arbitrary",), vmem_limit_bytes=VMEM_LIMIT),
        name="mix_latent" if latent else "mix_context",
    )(*args)


def _out_kernel(x_ref, ret_ref, attn_ref, mod_ref, gffn_ref, wo_ref, wfi_ref, wfo_ref, o_ref,
                act_scr, *, cond_row, layer):
    cond = cond_row(pl.program_id(0) * x_ref.shape[0])
    gt1 = _mod_vec(mod_ref, cond, 2)
    sh2 = _mod_vec(mod_ref, cond, 3)
    sc2 = _mod_vec(mod_ref, cond, 4)
    gt2 = _mod_vec(mod_ref, cond, 5)
    mixed = _dot(ret_ref[...], wo_ref[0:RET_WIDTH, :]) + _dot(attn_ref[...], wo_ref[RET_WIDTH:, :])
    x1 = x_ref[...] + gt1 * mixed
    h = (_rms(x1, gffn_ref[layer:layer + 1, :] * (1.0 + sc2)) + sh2).astype(BF16)
    for c0 in range(0, D_FF, FF_CHUNK):
        cw = min(FF_CHUNK, D_FF - c0)
        gate = _dot(h, wfi_ref[:, c0:c0 + cw])
        up = _dot(h, wfi_ref[:, D_FF + c0:D_FF + c0 + cw])
        act_scr[:, c0:c0 + cw] = (_silu(gate) * up).astype(BF16)
    o_ref[...] = x1 + gt2 * _dot(act_scr[...], wfo_ref[...])


def _out(x, ret, attn, mod, layer, cond_row, wts, big_w):
    m = x.shape[0]
    tm = ROW_TILE

    def row(i):
        return (i, 0)

    def lay3(i):
        return (layer, 0, 0)

    def whole(i):
        return (0, 0)

    once = pl.Buffered(1)
    return pl.pallas_call(
        functools.partial(_out_kernel, cond_row=cond_row, layer=layer),
        grid=(m // tm,),
        in_specs=[
            pl.BlockSpec((tm, D_MODEL), row),
            pl.BlockSpec((tm, RET_WIDTH), row),
            pl.BlockSpec((tm, RET_WIDTH), row),
            pl.BlockSpec((None, 8, N_MOD * D_MODEL), lay3),
            _gain_spec(wts["g_ffn"]),
            pl.BlockSpec((D_MODEL, D_MODEL), whole, pipeline_mode=once),
            pl.BlockSpec((D_MODEL, 2 * D_FF), whole, pipeline_mode=once),
            pl.BlockSpec((D_FF, D_MODEL), whole, pipeline_mode=once),
        ],
        out_specs=pl.BlockSpec((tm, D_MODEL), row),
        out_shape=jax.ShapeDtypeStruct((m, D_MODEL), F32),
        scratch_shapes=[pltpu.VMEM((tm, D_FF), BF16)],
        compiler_params=pltpu.CompilerParams(
            dimension_semantics=("arbitrary",), vmem_limit_bytes=VMEM_LIMIT),
        name="out_ffn",
    )(x, ret, attn, mod, wts["g_ffn"], *big_w)


def _rope_tables(n_lat):
    pos = np.arange(n_lat)
    row = (pos // GRID_W).astype(np.float32)[:, None]
    col = (pos % GRID_W).astype(np.float32)[:, None]
    lane = np.arange(LANES)[None, :]

    def tables(d, start, period):
        rel = (lane - start) % period
        active = np.logical_and(lane >= start, rel < d)
        half = d // 2
        nf = half // 2
        inv = np.float32(ROPE_BASE) ** (-((rel % nf).astype(np.float32)) / np.float32(nf))
        ang = (np.where(rel < half, row, col) * inv).astype(np.float32)
        cos, sin = np.cos(ang), np.sin(ang)
        first = (rel % half) < nf
        c = np.where(active, cos, 1.0)
        s = np.where(active, np.where(first, -sin, sin), 0.0)
        return tuple(jnp.asarray(t, F32) for t in (c, s))

    return tables(RET_DK, 0, RET_DK) + tables(MLA_ROPE, MLA_NOPE, LANES)


def _swap_partners(a, axis, d):
    shape = a.shape
    split = shape[:axis] + (shape[axis] // d, 2, 2, d // 4) + shape[axis + 1:]
    return jnp.flip(a.reshape(split), axis=axis + 2).reshape(shape)


def _prepare_weights(g_norm_mix, g_norm_ffn, w_in, g_q_a, w_q_b, g_kv_a, w_kv_b, g_qn, g_qr, g_kn,
                     g_kr):
    depth = w_in.shape[0]
    n_in = w_in.shape[2]
    w_in_t = jnp.pad(jnp.swapaxes(w_in, 1, 2).astype(BF16),
                     ((0, 0), (0, IN_COLS_PAD - n_in), (0, 0)))
    w_qb = w_q_b.reshape(depth, Q_RANK, MLA_HEADS, MLA_NOPE + MLA_ROPE)
    w_qb_sw = jnp.pad(_swap_partners(w_qb[..., MLA_NOPE:], 3, MLA_ROPE),
                      ((0, 0), (0, 0), (0, 0), (MLA_NOPE, HEAD_PAD - MLA_NOPE - MLA_ROPE)))
    w_qb = jnp.pad(w_qb, ((0, 0), (0, 0), (0, 0), (0, HEAD_PAD - MLA_NOPE - MLA_ROPE)))
    zeros32 = jnp.zeros((depth, MLA_ROPE), F32)
    zeros64 = jnp.zeros((depth, MLA_NOPE), F32)
    return {
        "g_mix": g_norm_mix,
        "g_ffn": g_norm_ffn,
        "w_in_t": w_in_t,
        "g_qa": g_q_a,
        "w_qb": w_qb.reshape(depth, Q_RANK, MLA_HEADS * HEAD_PAD).astype(BF16),
        "g_kva": g_kv_a,
        "w_kvb": w_kv_b.astype(BF16),
        "g_q": jnp.concatenate([g_qn, g_qr, zeros32], axis=-1) * Q_FOLD,
        "g_kn": jnp.concatenate([g_kn, jnp.ones((depth, MLA_V), F32)], axis=-1),
        "g_kr": jnp.pad(g_kr, ((0, 0), (0, LANES - MLA_ROPE))),
        "w_qb_sw": w_qb_sw.reshape(depth, Q_RANK, MLA_HEADS * HEAD_PAD).astype(BF16),
        "g_q_sw": jnp.concatenate([zeros64, _swap_partners(g_qr, 1, MLA_ROPE), zeros32],
                                  axis=-1) * Q_FOLD,
    }


def _blockdiag_states(s):
    b, l = s.shape[:2]
    s = s.reshape(b, l, N_PAIRS, 2, RET_DK, RET_DK)
    z = jnp.zeros_like(s[:, :, :, 0])
    top = jnp.concatenate([s[:, :, :, 0], z], axis=-1)
    bot = jnp.concatenate([z, s[:, :, :, 1]], axis=-1)
    return jnp.concatenate([top, bot], axis=-2)


def kernel(x_prompt, x_sample, cache_ckv, cache_krope, state_ret_fwd, state_ret_bwd, c, c_ctx,
           w_mod, b_mod, g_norm_mix, g_norm_ffn, w_in, g_q_a, w_q_b, g_kv_a, w_kv_b,
           g_qn, g_qr, g_kn, g_kr, ret_p_fwd, ret_p_bwd, g_ret_gn, b_ret_gn, w_o,
           w_ffn_in, w_ffn_out):
    batch, seq, _ = x_prompt.shape
    dec_batch, dec_seq, _ = x_sample.shape
    depth = w_in.shape[0]

    wts = _prepare_weights(g_norm_mix, g_norm_ffn, w_in, g_q_a, w_q_b, g_kv_a, w_kv_b, g_qn, g_qr,
                           g_kn, g_kr)
    conds = jnp.concatenate([c_ctx[None], c, jnp.zeros((8 - 1 - dec_batch, D_MODEL), F32)], axis=0)
    mod = _modulation(conds, w_mod, b_mod)
    dmask, dvec = _decay_tables(jnp.stack([ret_p_fwd, ret_p_bwd], axis=1))
    gn_g, gn_b = g_ret_gn, b_ret_gn

    rope_tabs = _rope_tables(dec_seq)
    kvn_cache = _cache_up(cache_ckv, wts["w_kvb"], wts["g_kn"])
    krp_cache = jnp.pad(cache_krope, ((0, 0), (0, 0), (0, 0), (MLA_NOPE, LANES - MLA_NOPE - MLA_ROPE)))
    krp_cache = krp_cache.astype(BF16)
    s_f0 = _blockdiag_states(state_ret_fwd)
    s_b0 = _blockdiag_states(state_ret_bwd)

    def ctx_row(row):
        return 0

    def lat_row(row):
        return 1 + row // dec_seq

    x = x_prompt.reshape(batch * seq, D_MODEL)
    y = x_sample.reshape(dec_batch * dec_seq, D_MODEL)
    caches = (jax.ShapeDtypeStruct((batch, depth, seq, KV_RANK), F32),
              jax.ShapeDtypeStruct((batch, depth, MLA_ROPE, seq), F32))
    states = (jax.ShapeDtypeStruct((batch, depth, RET_HEADS, RET_DK, RET_DK), F32),) * 2
    for l in range(depth):
        pr = _proj(x, mod, l, ctx_row, wts, None, caches)
        caches = tuple(pr[7:9])
        casts = [(w_o, l, D_MODEL, 1), (w_ffn_in, l, 2 * D_FF, 1), (w_ffn_out, l, D_MODEL, 2)]
        mixed = _mix(pr, dmask, dvec, gn_g, gn_b, l, batch, seq, None, states, casts)
        ret, attn = mixed[:2]
        states = tuple(mixed[2:4])
        big_w = tuple(mixed[4:7])
        x = _out(x, ret, attn, mod, l, ctx_row, wts, big_w)

        pr = _proj(y, mod, l, lat_row, wts, rope_tabs, None)
        ret, attn = _mix(pr, dmask, dvec, gn_g, gn_b, l, dec_batch, dec_seq,
                         (kvn_cache, krp_cache, s_f0, s_b0), None)
        y = _out(y, ret, attn, mod, l, lat_row, wts, big_w)

    return (x.reshape(batch, seq, D_MODEL), y.reshape(dec_batch, dec_seq, D_MODEL),
            caches[0], jnp.swapaxes(caches[1], 2, 3), states[0], states[1])
```

```python
import functools

import jax
import jax.numpy as jnp
import numpy as np
from jax import lax
from jax.experimental import pallas as pl
from jax.experimental.pallas import tpu as pltpu

D_MODEL = 1024
N_MOD = 6
RET_HEADS = 8
RET_DK = 64
RET_WIDTH = 512
MLA_HEADS = 8
MLA_NOPE = 64
MLA_ROPE = 32
MLA_V = 64
Q_RANK = 256
KV_RANK = 128
D_FF = 2816
GRID_W = 64
ROPE_BASE = 10000.0
EPS = 1e-6

LANES = 128
HEAD_PAD = LANES
N_PAIRS = RET_HEADS // 2
IN_COLS_PAD = 4 * RET_WIDTH + Q_RANK + KV_RANK + LANES
OUT_TILE = 1024
LATENT_OUT_TILE = 512
PROJ_TILE = 1024
LATENT_PROJ_TILE = 512
SUB_TILE = 256
FF_CHUNK = 256
MIX_CHUNK = 256
MIX_SEQS = 4
Q_DEC_F, K_DEC_F, Q_DEC_B, K_DEC_B, C_DEC_F, C_DEC_B = range(6)
N_DVEC = 6
K_SCALE = RET_DK ** -0.5
Q_FOLD = (MLA_NOPE + MLA_ROPE) ** -0.5 * 1.4426950408889634
VMEM_LIMIT = 56 * 1024 * 1024

BF16 = jnp.bfloat16
F32 = jnp.float32
_NT = (((1,), (1,)), ((), ()))


def _dot(a, b):
    return jnp.dot(a, b, preferred_element_type=F32)


def _dot_nt(a, b):
    return lax.dot_general(a, b, _NT, preferred_element_type=F32)


def _rms(x, g):
    return x * lax.rsqrt(jnp.mean(x * x, axis=-1, keepdims=True) + EPS) * g


def _silu(x):
    return x * jax.nn.sigmoid(x)


def _masked_mean_sq(x, mask, n):
    return jnp.sum(jnp.where(mask, x * x, 0.0), axis=-1, keepdims=True) * (1.0 / n)


def _mod_kernel(c_ref, w_ref, b_ref, o_ref):
    a = _silu(c_ref[...])
    w = w_ref[...]
    a_hi = a.astype(BF16)
    a_lo = (a - a_hi.astype(F32)).astype(BF16)
    w_hi = w.astype(BF16)
    w_lo = (w - w_hi.astype(F32)).astype(BF16)
    both = _dot(jnp.concatenate([a_hi, a_lo], axis=0), w_hi)
    part = both[0:8] + both[8:16] + _dot(a_hi, w_lo)

    @pl.when(pl.program_id(1) == 0)
    def _():
        o_ref[...] = part + b_ref[pl.ds(pl.program_id(0), 1), :]

    @pl.when(pl.program_id(1) > 0)
    def _():
        o_ref[...] += part


def _modulation(conds, w_mod, b_mod):
    depth, d, n = w_mod.shape
    tk = 256
    return pl.pallas_call(
        _mod_kernel,
        grid=(depth, d // tk),
        in_specs=[
            pl.BlockSpec((8, tk), lambda l, k: (0, k)),
            pl.BlockSpec((None, tk, n), lambda l, k: (l, k, 0)),
            pl.BlockSpec((depth, n), lambda l, k: (0, 0)),
        ],
        out_specs=pl.BlockSpec((None, 8, n), lambda l, k: (l, 0, 0)),
        out_shape=jax.ShapeDtypeStruct((depth, 8, n), F32),
        compiler_params=pltpu.CompilerParams(
            dimension_semantics=("arbitrary", "arbitrary"), vmem_limit_bytes=VMEM_LIMIT),
        name="modulation",
    )(conds, w_mod, b_mod)


def _norm_kn(kv, gkn):
    lane = lax.broadcasted_iota(jnp.int32, (kv.shape[0], LANES), 1)
    lo = lane < MLA_NOPE
    out = []
    for h in range(MLA_HEADS):
        kvh = kv[:, h * HEAD_PAD:(h + 1) * HEAD_PAD]
        rs = lax.rsqrt(_masked_mean_sq(kvh, lo, MLA_NOPE) + EPS)
        out.append((kvh * jnp.where(lo, rs * gkn, 1.0)).astype(BF16))
    return out


def _cache_kernel(ckv_ref, wkvb_ref, gkn_ref, kvn_ref):
    kv = _dot(ckv_ref[...].astype(BF16), wkvb_ref[...])
    gkn = gkn_ref[pl.ds(pl.program_id(0), 1), :]
    for h, kvh in enumerate(_norm_kn(kv, gkn)):
        kvn_ref[:, h * HEAD_PAD:(h + 1) * HEAD_PAD] = kvh


def _cache_up(cache_ckv, wkvb, gkn):
    nb, depth, past, _ = cache_ckv.shape
    width = MLA_HEADS * HEAD_PAD
    return pl.pallas_call(
        _cache_kernel,
        grid=(depth, nb),
        in_specs=[
            pl.BlockSpec((None, None, past, KV_RANK), lambda l, b: (b, l, 0, 0)),
            pl.BlockSpec((None, KV_RANK, width), lambda l, b: (l, 0, 0)),
            pl.BlockSpec((depth, LANES), lambda l, b: (0, 0)),
        ],
        out_specs=pl.BlockSpec((None, None, past, width), lambda l, b: (l, b, 0, 0)),
        out_shape=jax.ShapeDtypeStruct((depth, nb, past, width), BF16),
        compiler_params=pltpu.CompilerParams(
            dimension_semantics=("arbitrary", "arbitrary"), vmem_limit_bytes=VMEM_LIMIT),
        name="cache_up",
    )(cache_ckv, wkvb, gkn)


def _mod_vec(mod_ref, cond, k):
    return mod_ref[pl.ds(cond, 1), k * D_MODEL:(k + 1) * D_MODEL]


def _gain_spec(g):
    return pl.BlockSpec(g.shape, lambda *_: (0, 0))


def _is_array(x):
    return not isinstance(x, jax.ShapeDtypeStruct)


def _layer_spec(n, shape, layer, aliased):
    rest = tuple(shape[2:])
    zeros = (0,) * len(rest)
    if aliased:
        return pl.BlockSpec((n, None) + rest, lambda i: (i, layer) + zeros)
    return pl.BlockSpec((n, shape[1]) + rest, lambda i: (i, 0) + zeros)


def _put_layer(ref, i, tail, layer, fresh, value):
    if not fresh:
        ref[(i,) + tail] = value
        return
    for l in range(ref.shape[1]):
        ref[(i, l) + tail] = value if l == layer else jnp.zeros_like(value)


def _proj_kernel(*refs, rope, cond_row, n_alias, layer):
    (x_ref, mod_ref, gmix_ref, win_ref, gqa_ref, wqb_ref, gkva_ref, wkvb_ref,
     gq_ref, gkn_ref, gkr_ref) = refs[:11]
    if rope:
        wqb_sw_ref, gq_sw_ref, c64_ref, s64_ref, c32_ref, s32_ref = refs[11:17]
        rq_ref, rk_ref, rv_ref, rg_ref, qcat_ref, kvn_ref, krp_ref = refs[17:]
    else:
        (rq_ref, rk_ref, rv_ref, rg_ref, qcat_ref, kvn_ref, krp_ref, ckv_ref,
         kro_ref) = refs[11 + n_alias:]

    cond = cond_row(pl.program_id(0) * x_ref.shape[0])
    sh1 = _mod_vec(mod_ref, cond, 0)
    sc1 = _mod_vec(mod_ref, cond, 1)
    lrow = slice(layer, layer + 1)
    g_mod = gmix_ref[lrow, :] * (1.0 + sc1)
    w = RET_WIDTH
    lane = lax.broadcasted_iota(jnp.int32, (SUB_TILE, LANES), 1)
    nope = lane < MLA_NOPE
    is_rope = jnp.logical_and(lane >= MLA_NOPE, lane < MLA_NOPE + MLA_ROPE)

    for r in range(x_ref.shape[0] // SUB_TILE):
        rs = slice(r * SUB_TILE, (r + 1) * SUB_TILE)
        h = (_rms(x_ref[rs, :], g_mod) + sh1).astype(BF16)
        z = _dot_nt(h, win_ref[...])
        if rope:
            grp = RET_DK // 4
            z_sw = _dot_nt(h, jnp.concatenate(
                [win_ref[(g ^ 1) * grp:((g ^ 1) + 1) * grp, :] for g in range(2 * w // grp)], 0))

        for j in range(w // LANES):
            sl = slice(j * LANES, (j + 1) * LANES)
            q = z[:, j * LANES:(j + 1) * LANES]
            k = z[:, w + j * LANES:w + (j + 1) * LANES]
            if rope:
                q_sw = z_sw[:, j * LANES:(j + 1) * LANES]
                k_sw = z_sw[:, w + j * LANES:w + (j + 1) * LANES]
                q = q * c64_ref[rs, :] + q_sw * s64_ref[rs, :]
                k = k * c64_ref[rs, :] + k_sw * s64_ref[rs, :]
            rq_ref[rs, sl] = q.astype(BF16)
            rk_ref[rs, sl] = k.astype(BF16)
        rv_ref[rs, :] = z[:, 2 * w:3 * w].astype(BF16)
        rg_ref[rs, :] = z[:, 3 * w:4 * w]

        o = 4 * w
        qa = z[:, o:o + Q_RANK]
        kva = z[:, o + Q_RANK:o + Q_RANK + KV_RANK]
        kr2 = z[:, o + Q_RANK + KV_RANK:]

        qa_n = _rms(qa, gqa_ref[lrow, :]).astype(BF16)
        q = _dot(qa_n, wqb_ref[...])
        if rope:
            q_sw = _dot(qa_n, wqb_sw_ref[...])
        for hh in range(MLA_HEADS):
            hs = slice(hh * HEAD_PAD, (hh + 1) * HEAD_PAD)
            qh = q[:, hs]
            rs_n = lax.rsqrt(_masked_mean_sq(qh, nope, MLA_NOPE) + EPS)
            rs_r = lax.rsqrt(_masked_mean_sq(qh, is_rope, MLA_ROPE) + EPS)
            qn = qh * jnp.where(nope, rs_n, rs_r) * gq_ref[lrow, :]
            if rope:
                qn = (qn * c32_ref[rs, :]
                      + q_sw[:, hs] * rs_r * gq_sw_ref[lrow, :] * s32_ref[rs, :])
            qcat_ref[rs, hs] = qn.astype(BF16)

        ckv = _rms(kva, gkva_ref[lrow, :])
        kv = _dot(ckv.astype(BF16), wkvb_ref[...])
        for hh, kvh in enumerate(_norm_kn(kv, gkn_ref[lrow, :])):
            kvn_ref[rs, hh * HEAD_PAD:(hh + 1) * HEAD_PAD] = kvh

        rs_k = lax.rsqrt(_masked_mean_sq(kr2, lane < MLA_ROPE, MLA_ROPE) + EPS)
        krn = kr2 * rs_k * gkr_ref[lrow, :]
        krp = pltpu.roll(krn, MLA_NOPE, 1)
        if rope:
            x1_pos = ((lane - MLA_NOPE) % (MLA_ROPE // 2)) < MLA_ROPE // 4
            partner = jnp.where(x1_pos, pltpu.roll(krp, LANES - MLA_ROPE // 4, 1),
                                pltpu.roll(krp, MLA_ROPE // 4, 1))
            krp = krp * c32_ref[rs, :] + partner * s32_ref[rs, :]
        else:
            seq = ckv_ref.shape[-2]
            krt = krn.T[0:MLA_ROPE, :]
            for t in range(SUB_TILE // seq):
                i = r * (SUB_TILE // seq) + t
                _put_layer(ckv_ref, i, (), layer, n_alias == 0, ckv[t * seq:(t + 1) * seq])
                _put_layer(kro_ref, i, (), layer, n_alias == 0, krt[:, t * seq:(t + 1) * seq])
        krp_ref[rs, :] = krp.astype(BF16)


def _proj(x, mod, layer, cond_row, wts, rope_tabs, caches):
    m = x.shape[0]
    rope = rope_tabs is not None
    tm = LATENT_PROJ_TILE if rope else PROJ_TILE
    width = MLA_HEADS * HEAD_PAD

    def row(i):
        return (i, 0)

    def lay3(i):
        return (layer, 0, 0)

    in_specs = [
        pl.BlockSpec((tm, D_MODEL), row),
        pl.BlockSpec((None, 8, N_MOD * D_MODEL), lay3),
        _gain_spec(wts["g_mix"]),
        pl.BlockSpec((None, IN_COLS_PAD, D_MODEL), lay3),
        _gain_spec(wts["g_qa"]),
        pl.BlockSpec((None, Q_RANK, width), lay3),
        _gain_spec(wts["g_kva"]),
        pl.BlockSpec((None, KV_RANK, width), lay3),
        _gain_spec(wts["g_q"]), _gain_spec(wts["g_kn"]), _gain_spec(wts["g_kr"]),
    ]
    args = [x, mod, wts["g_mix"], wts["w_in_t"], wts["g_qa"], wts["w_qb"], wts["g_kva"],
            wts["w_kvb"], wts["g_q"], wts["g_kn"], wts["g_kr"]]
    out_shape = [
        jax.ShapeDtypeStruct((m, RET_WIDTH), BF16),
        jax.ShapeDtypeStruct((m, RET_WIDTH), BF16),
        jax.ShapeDtypeStruct((m, RET_WIDTH), BF16),
        jax.ShapeDtypeStruct((m, RET_WIDTH), F32),
        jax.ShapeDtypeStruct((m, width), BF16),
        jax.ShapeDtypeStruct((m, width), BF16),
        jax.ShapeDtypeStruct((m, LANES), BF16),
    ]
    out_specs = [
        pl.BlockSpec((tm, RET_WIDTH), row), pl.BlockSpec((tm, RET_WIDTH), row),
        pl.BlockSpec((tm, RET_WIDTH), row), pl.BlockSpec((tm, RET_WIDTH), row),
        pl.BlockSpec((tm, width), row), pl.BlockSpec((tm, width), row),
        pl.BlockSpec((tm, LANES), row),
    ]
    aliases = {}
    if rope:
        n_lat = rope_tabs[0].shape[0]
        tiles = n_lat // tm
        in_specs += [pl.BlockSpec((None, Q_RANK, width), lay3), _gain_spec(wts["g_q_sw"])]
        args += [wts["w_qb_sw"], wts["g_q_sw"]]
        in_specs += [pl.BlockSpec((tm, LANES), lambda i: (i % tiles, 0))] * len(rope_tabs)
        args += list(rope_tabs)
    else:
        seq = caches[0].shape[2]
        assert SUB_TILE % seq == 0
        nb = tm // seq
        if _is_array(caches[0]):
            in_specs += [pl.BlockSpec(memory_space=pl.ANY)] * 2
            aliases = {len(args): len(out_shape), len(args) + 1: len(out_shape) + 1}
            args += list(caches)
        out_shape += [jax.ShapeDtypeStruct(c.shape, c.dtype) for c in caches]
        out_specs += [_layer_spec(nb, c.shape, layer, bool(aliases)) for c in caches]

    return pl.pallas_call(
        functools.partial(_proj_kernel, rope=rope, cond_row=cond_row, n_alias=len(aliases),
                          layer=layer),
        grid=(m // tm,),
        in_specs=in_specs,
        out_specs=out_specs,
        out_shape=out_shape,
        input_output_aliases=aliases,
        compiler_params=pltpu.CompilerParams(
            dimension_semantics=("arbitrary",), vmem_limit_bytes=VMEM_LIMIT),
        name="proj_latent" if rope else "proj_context",
    )(*args)


def _log_gamma(p):
    return jnp.log1p(-jnp.exp2(-p))


def _decay_kernel(p_ref, dmask_ref, dvec_ref):
    c = MIX_CHUNK
    base = pl.program_id(0) * (2 * RET_HEADS)
    pair = pl.program_id(1)
    ri = lax.broadcasted_iota(jnp.int32, (c, c), 0)
    ci = lax.broadcasted_iota(jnp.int32, (c, c), 1)
    dif = (ri - ci).astype(F32)
    for e in range(2):
        h = 2 * pair + e
        lg_f = _log_gamma(jnp.full((c, c), p_ref[base + h], F32))
        lg_b = _log_gamma(jnp.full((c, c), p_ref[base + RET_HEADS + h], F32))
        fwd = jnp.where(dif >= 0, jnp.exp(jnp.maximum(dif, 0.0) * lg_f), 0.0)
        bwd = jnp.where(dif <= 0, jnp.exp(jnp.maximum(-dif, 0.0) * lg_b), 0.0)
        dmask_ref[e] = (fwd + bwd) * K_SCALE
    lane = lax.broadcasted_iota(jnp.int32, (c, LANES), 1)
    rowf = lax.broadcasted_iota(jnp.int32, (c, LANES), 0).astype(F32)
    lo = lane < RET_DK
    lg_f = _log_gamma(jnp.where(lo, p_ref[base + 2 * pair], p_ref[base + 2 * pair + 1]))
    lg_b = _log_gamma(jnp.where(lo, p_ref[base + RET_HEADS + 2 * pair],
                                p_ref[base + RET_HEADS + 2 * pair + 1]))
    dvec_ref[Q_DEC_F] = jnp.exp((rowf + 1.0) * lg_f)
    dvec_ref[K_DEC_F] = jnp.exp((c - 1.0 - rowf) * lg_f) * K_SCALE
    dvec_ref[Q_DEC_B] = jnp.exp((c - rowf) * lg_b)
    dvec_ref[K_DEC_B] = jnp.exp(rowf * lg_b) * K_SCALE
    dvec_ref[C_DEC_F] = jnp.exp(c * lg_f)
    dvec_ref[C_DEC_B] = jnp.exp(c * lg_b)


def _decay_tables(decay_p):
    depth = decay_p.shape[0]
    c = MIX_CHUNK
    return pl.pallas_call(
        _decay_kernel,
        grid=(depth, N_PAIRS),
        in_specs=[pl.BlockSpec(memory_space=pltpu.SMEM)],
        out_specs=[
            pl.BlockSpec((None, 2, c, c), lambda l, p: (l, p, 0, 0)),
            pl.BlockSpec((None, None, N_DVEC, c, LANES), lambda l, p: (l, p, 0, 0, 0)),
        ],
        out_shape=[
            jax.ShapeDtypeStruct((depth, RET_HEADS, c, c), F32),
            jax.ShapeDtypeStruct((depth, N_PAIRS, N_DVEC, c, LANES), F32),
        ],
        compiler_params=pltpu.CompilerParams(
            dimension_semantics=("arbitrary", "arbitrary"), vmem_limit_bytes=VMEM_LIMIT),
        name="decay_tables",
    )(decay_p.reshape(-1))


def _mix_kernel(*refs, n_seq, n_sub, latent, n_alias, n_cast, layer):
    c = MIX_CHUNK
    nc = n_seq // c
    (rq_ref, rk_ref, rv_ref, rg_ref, qcat_ref, kvn_ref, krp_ref, gng_ref, gnb_ref,
     dmask_ref, dvec_ref) = refs[:11]
    if latent:
        kvc_ref, krc_ref, sf0_ref, sb0_ref, ret_ref, attn_ref, st_scr = refs[11:]
    else:
        n_in = 11 + n_alias
        cast_src = refs[n_in:n_in + n_cast]
        ret_ref, attn_ref, sf_ref, sb_ref = refs[n_in + n_cast:n_in + n_cast + 4]
        cast_dst = refs[n_in + n_cast + 4:]
        for src, dst in zip(cast_src, cast_dst):
            w = src.shape[1]
            dst[:, 0:w] = src[...].astype(BF16)
            if dst.shape[1] > w:
                dst[:, w:] = jnp.zeros((dst.shape[0], dst.shape[1] - w), BF16)

    lane = lax.broadcasted_iota(jnp.int32, (c, LANES), 1)
    lo = lane < RET_DK
    lrow = slice(layer, layer + 1)
    sq_r = lax.broadcasted_iota(jnp.int32, (LANES, LANES), 0)
    sq_c = lax.broadcasted_iota(jnp.int32, (LANES, LANES), 1)
    blockdiag = (sq_r < RET_DK) == (sq_c < RET_DK)

    def cols(j):
        return slice(j * LANES, (j + 1) * LANES)

    def state_update(pair, d, rows):
        kp = rk_ref[rows, cols(pair)]
        vp = rv_ref[rows, cols(pair)]
        kdt = (kp.astype(F32) * dvec_ref[pair, K_DEC_B if d else K_DEC_F]).T.astype(BF16)
        return jnp.where(blockdiag, _dot(kdt, vp), 0.0)

    nope_n = lax.broadcasted_iota(jnp.int32, (n_seq, LANES), 1) < MLA_NOPE
    if latent:
        nope_c = lax.broadcasted_iota(jnp.int32, (kvc_ref.shape[0], LANES), 1) < MLA_NOPE

    def states(sub):
        base = sub * n_seq
        for pair in range(N_PAIRS):
            for d in range(2):
                if latent:
                    s = (sb0_ref if d else sf0_ref)[pair]
                    cdec = dvec_ref[pair, C_DEC_B if d else C_DEC_F][0:LANES, :]
                    order = list(range(nc - 1, -1, -1)) if d else list(range(nc))
                    for idx, ch in enumerate(order):
                        st_scr[d, pair, ch] = s.astype(BF16)
                        if idx < nc - 1:
                            s = s * cdec + state_update(pair, d, pl.ds(base + ch * c, c))
                else:
                    s = state_update(pair, d, pl.ds(base, c))
                    st_ref = sb_ref if d else sf_ref
                    fresh = n_alias == 0
                    _put_layer(st_ref, sub, (2 * pair,), layer, fresh, s[0:RET_DK, 0:RET_DK])
                    _put_layer(st_ref, sub, (2 * pair + 1,), layer, fresh, s[RET_DK:, RET_DK:])

    def chunk_body(ch, base):
        rows = pl.ds(pl.multiple_of(base + ch * c, c), c)
        keys = pl.ds(base, n_seq)

        for pair in range(N_PAIRS):
            qp = rq_ref[rows, cols(pair)]
            kp = rk_ref[rows, cols(pair)]
            vp = rv_ref[rows, cols(pair)]
            zero = jnp.zeros_like(qp)
            a0 = (_dot_nt(jnp.where(lo, qp, zero), kp) * dmask_ref[2 * pair]).astype(BF16)
            a1 = (_dot_nt(jnp.where(lo, zero, qp), kp) * dmask_ref[2 * pair + 1]).astype(BF16)
            tot = jnp.where(lo, _dot(a0, vp), _dot(a1, vp))
            if latent:
                tot = (tot + _dot(qp, st_scr[0, pair, ch]) * dvec_ref[pair, Q_DEC_F]
                       + _dot(qp, st_scr[1, pair, ch]) * dvec_ref[pair, Q_DEC_B])
            inv = 1.0 / RET_DK
            m0 = jnp.sum(jnp.where(lo, tot, 0.0), axis=-1, keepdims=True) * inv
            m1 = jnp.sum(jnp.where(lo, 0.0, tot), axis=-1, keepdims=True) * inv
            y = tot - jnp.where(lo, m0, m1)
            v0 = jnp.sum(jnp.where(lo, y * y, 0.0), axis=-1, keepdims=True) * inv
            v1 = jnp.sum(jnp.where(lo, 0.0, y * y), axis=-1, keepdims=True) * inv
            yn = (y * lax.rsqrt(jnp.where(lo, v0, v1) + EPS) * gng_ref[lrow, cols(pair)]
                  + gnb_ref[lrow, cols(pair)])
            ret_ref[rows, cols(pair)] = (yn * _silu(rg_ref[rows, cols(pair)])).astype(BF16)

        krp = krp_ref[keys, :]
        for pair in range(N_PAIRS):
            outs = []
            for e in range(2):
                h = 2 * pair + e
                qc = qcat_ref[rows, cols(h)]
                kv = kvn_ref[keys, cols(h)]
                s = _dot_nt(qc, jnp.where(nope_n, kv, krp))
                m = jnp.max(s, axis=-1, keepdims=True)
                if latent:
                    kv_c = kvc_ref[:, cols(h)]
                    s2 = _dot_nt(qc, jnp.where(nope_c, kv_c, krc_ref[...]))
                    m = jnp.maximum(m, jnp.max(s2, axis=-1, keepdims=True))
                    p2 = jnp.exp2(s2 - m)
                p = jnp.exp2(s - m)
                den = jnp.sum(p, axis=-1, keepdims=True)
                acc = _dot(p.astype(BF16), kv)
                if latent:
                    den = den + jnp.sum(p2, axis=-1, keepdims=True)
                    acc = acc + _dot(p2.astype(BF16), kv_c)
                outs.append(acc / den)
            attn_ref[rows, cols(pair)] = jnp.where(
                lo, pltpu.roll(outs[0], MLA_V, 1), outs[1]).astype(BF16)
        return base

    for sub in range(n_sub):
        states(sub)
    for sub in range(n_sub):
        if nc == 1:
            chunk_body(0, sub * n_seq)
        else:
            lax.fori_loop(0, nc, chunk_body, sub * n_seq)


def _mix(proj_out, dmask, dvec, gn_g, gn_b, layer, n_batch, n_seq, latent_in, states_out,
         casts=()):
    rq, rk, rv, rg, qcat, kvn, krp = proj_out[:7]
    latent = latent_in is not None
    aliases = {}
    m = n_batch * n_seq
    c = MIX_CHUNK
    width = MLA_HEADS * HEAD_PAD

    def row(b):
        return (b, 0)

    def lay3(b):
        return (layer, 0, 0)

    once = pl.Buffered(1)
    n_sub = 1 if latent else MIX_SEQS
    n_steps = n_batch // n_sub
    blk = n_sub * n_seq
    in_specs = [
        pl.BlockSpec((blk, RET_WIDTH), row), pl.BlockSpec((blk, RET_WIDTH), row),
        pl.BlockSpec((blk, RET_WIDTH), row), pl.BlockSpec((blk, RET_WIDTH), row),
        pl.BlockSpec((blk, width), row), pl.BlockSpec((blk, width), row),
        pl.BlockSpec((blk, LANES), row),
        _gain_spec(gn_g), _gain_spec(gn_b),
        pl.BlockSpec((None, RET_HEADS, c, c), lambda b: (layer, 0, 0, 0), pipeline_mode=once),
        pl.BlockSpec((None, N_PAIRS, N_DVEC, c, LANES), lambda b: (layer, 0, 0, 0, 0),
                     pipeline_mode=once),
    ]
    args = [rq, rk, rv, rg, qcat, kvn, krp, gn_g, gn_b, dmask, dvec]
    out_shape = [jax.ShapeDtypeStruct((m, RET_WIDTH), BF16),
                 jax.ShapeDtypeStruct((m, MLA_HEADS * MLA_V), BF16)]
    out_specs = [pl.BlockSpec((blk, RET_WIDTH), row), pl.BlockSpec((blk, RET_WIDTH), row)]
    scratch = []
    if latent:
        kvn_c, krp_c, s_f0, s_b0 = latent_in
        past = kvn_c.shape[2]
        st_spec = pl.BlockSpec((None, None, N_PAIRS, LANES, LANES), lambda b: (b, layer, 0, 0, 0))
        in_specs += [
            pl.BlockSpec((None, None, past, width), lambda b: (layer, b, 0, 0)),
            pl.BlockSpec((None, None, past, LANES), lambda b: (b, layer, 0, 0)),
            st_spec, st_spec,
        ]
        args += [kvn_c, krp_c, s_f0, s_b0]
        scratch = [pltpu.VMEM((2, N_PAIRS, n_seq // c, LANES, LANES), BF16)]
    else:
        if _is_array(states_out[0]):
            in_specs += [pl.BlockSpec(memory_space=pl.ANY)] * 2
            aliases = {len(args): len(out_shape), len(args) + 1: len(out_shape) + 1}
            args += list(states_out)
        out_shape += [jax.ShapeDtypeStruct(s.shape, s.dtype) for s in states_out]
        out_specs += [_layer_spec(n_sub, s.shape, layer, bool(aliases)) for s in states_out]
        for w, w_layer, out_cols, stride in casts:
            blocks = n_steps // stride
            rows = w.shape[1] // blocks
            assert rows * blocks == w.shape[1] and rows % 16 == 0 and stride * blocks == n_steps
            in_specs.append(pl.BlockSpec(
                (None, rows, w.shape[2]),
                lambda b, w_layer=w_layer, stride=stride: (w_layer, b // stride, 0)))
            args.append(w)
            out_shape.append(jax.ShapeDtypeStruct((w.shape[1], out_cols), BF16))
            out_specs.append(pl.BlockSpec((rows, out_cols),
                                          lambda b, stride=stride: (b // stride, 0)))

    return pl.pallas_call(
        functools.partial(_mix_kernel, n_seq=n_seq, n_sub=n_sub, latent=latent, layer=layer,
                          n_alias=len(aliases), n_cast=len(casts)),
        grid=(n_steps,),
        in_specs=in_specs,
        out_specs=out_specs,
        out_shape=out_shape,
        input_output_aliases=aliases,
        scratch_shapes=scratch,
        compiler_params=pltpu.CompilerParams(
            dimension_semantics=("arbitrary",), vmem_limit_bytes=VMEM_LIMIT),
        name="mix_latent" if latent else "mix_context",
    )(*args)


def _out_kernel(x_ref, ret_ref, attn_ref, mod_ref, gffn_ref, wo_ref, wfi_ref, wfo_ref, o_ref,
                act_scr, *, cond_row, layer):
    cond = cond_row(pl.program_id(0) * x_ref.shape[0])
    gt1 = _mod_vec(mod_ref, cond, 2)
    sh2 = _mod_vec(mod_ref, cond, 3)
    sc2 = _mod_vec(mod_ref, cond, 4)
    gt2 = _mod_vec(mod_ref, cond, 5)
    mixed = _dot(ret_ref[...], wo_ref[0:RET_WIDTH, :]) + _dot(attn_ref[...], wo_ref[RET_WIDTH:, :])
    x1 = x_ref[...] + gt1 * mixed
    h = (_rms(x1, gffn_ref[layer:layer + 1, :] * (1.0 + sc2)) + sh2).astype(BF16)
    for c0 in range(0, D_FF, FF_CHUNK):
        cw = min(FF_CHUNK, D_FF - c0)
        gate = _dot(h, wfi_ref[:, c0:c0 + cw])
        up = _dot(h, wfi_ref[:, D_FF + c0:D_FF + c0 + cw])
        act_scr[:, c0:c0 + cw] = (_silu(gate) * up).astype(BF16)
    o_ref[...] = x1 + gt2 * _dot(act_scr[...], wfo_ref[...])


def _out(x, ret, attn, mod, layer, cond_row, wts, big_w, tm):
    m = x.shape[0]

    def row(i):
        return (i, 0)

    def lay3(i):
        return (layer, 0, 0)

    def whole(i):
        return (0, 0)

    once = pl.Buffered(1)
    return pl.pallas_call(
        functools.partial(_out_kernel, cond_row=cond_row, layer=layer),
        grid=(m // tm,),
        in_specs=[
            pl.BlockSpec((tm, D_MODEL), row),
            pl.BlockSpec((tm, RET_WIDTH), row),
            pl.BlockSpec((tm, RET_WIDTH), row),
            pl.BlockSpec((None, 8, N_MOD * D_MODEL), lay3),
            _gain_spec(wts["g_ffn"]),
            pl.BlockSpec((D_MODEL, D_MODEL), whole, pipeline_mode=once),
            pl.BlockSpec((D_MODEL, 2 * D_FF), whole, pipeline_mode=once),
            pl.BlockSpec((D_FF, D_MODEL), whole, pipeline_mode=once),
        ],
        out_specs=pl.BlockSpec((tm, D_MODEL), row),
        out_shape=jax.ShapeDtypeStruct((m, D_MODEL), F32),
        scratch_shapes=[pltpu.VMEM((tm, D_FF), BF16)],
        compiler_params=pltpu.CompilerParams(
            dimension_semantics=("arbitrary",), vmem_limit_bytes=VMEM_LIMIT),
        name="out_ffn",
    )(x, ret, attn, mod, wts["g_ffn"], *big_w)


def _rope_tables(n_lat):
    pos = np.arange(n_lat)
    row = (pos // GRID_W).astype(np.float32)[:, None]
    col = (pos % GRID_W).astype(np.float32)[:, None]
    lane = np.arange(LANES)[None, :]

    def tables(d, start, period):
        rel = (lane - start) % period
        active = np.logical_and(lane >= start, rel < d)
        half = d // 2
        nf = half // 2
        inv = np.float32(ROPE_BASE) ** (-((rel % nf).astype(np.float32)) / np.float32(nf))
        ang = (np.where(rel < half, row, col) * inv).astype(np.float32)
        cos, sin = np.cos(ang), np.sin(ang)
        first = (rel % half) < nf
        c = np.where(active, cos, 1.0)
        s = np.where(active, np.where(first, -sin, sin), 0.0)
        return tuple(jnp.asarray(t, F32) for t in (c, s))

    return tables(RET_DK, 0, RET_DK) + tables(MLA_ROPE, MLA_NOPE, LANES)


def _swap_partners(a, axis, d):
    shape = a.shape
    split = shape[:axis] + (shape[axis] // d, 2, 2, d // 4) + shape[axis + 1:]
    return jnp.flip(a.reshape(split), axis=axis + 2).reshape(shape)


def _prepare_weights(g_norm_mix, g_norm_ffn, w_in, g_q_a, w_q_b, g_kv_a, w_kv_b, g_qn, g_qr, g_kn,
                     g_kr):
    depth = w_in.shape[0]
    n_in = w_in.shape[2]
    w_in_t = jnp.pad(jnp.swapaxes(w_in, 1, 2).astype(BF16),
                     ((0, 0), (0, IN_COLS_PAD - n_in), (0, 0)))
    w_qb = w_q_b.reshape(depth, Q_RANK, MLA_HEADS, MLA_NOPE + MLA_ROPE)
    w_qb_sw = jnp.pad(_swap_partners(w_qb[..., MLA_NOPE:], 3, MLA_ROPE),
                      ((0, 0), (0, 0), (0, 0), (MLA_NOPE, HEAD_PAD - MLA_NOPE - MLA_ROPE)))
    w_qb = jnp.pad(w_qb, ((0, 0), (0, 0), (0, 0), (0, HEAD_PAD - MLA_NOPE - MLA_ROPE)))
    zeros32 = jnp.zeros((depth, MLA_ROPE), F32)
    zeros64 = jnp.zeros((depth, MLA_NOPE), F32)
    return {
        "g_mix": g_norm_mix,
        "g_ffn": g_norm_ffn,
        "w_in_t": w_in_t,
        "g_qa": g_q_a,
        "w_qb": w_qb.reshape(depth, Q_RANK, MLA_HEADS * HEAD_PAD).astype(BF16),
        "g_kva": g_kv_a,
        "w_kvb": w_kv_b.astype(BF16),
        "g_q": jnp.concatenate([g_qn, g_qr, zeros32], axis=-1) * Q_FOLD,
        "g_kn": jnp.concatenate([g_kn, jnp.ones((depth, MLA_V), F32)], axis=-1),
        "g_kr": jnp.pad(g_kr, ((0, 0), (0, LANES - MLA_ROPE))),
        "w_qb_sw": w_qb_sw.reshape(depth, Q_RANK, MLA_HEADS * HEAD_PAD).astype(BF16),
        "g_q_sw": jnp.concatenate([zeros64, _swap_partners(g_qr, 1, MLA_ROPE), zeros32],
                                  axis=-1) * Q_FOLD,
    }


def _blockdiag_states(s):
    b, l = s.shape[:2]
    s = s.reshape(b, l, N_PAIRS, 2, RET_DK, RET_DK)
    z = jnp.zeros_like(s[:, :, :, 0])
    top = jnp.concatenate([s[:, :, :, 0], z], axis=-1)
    bot = jnp.concatenate([z, s[:, :, :, 1]], axis=-1)
    return jnp.concatenate([top, bot], axis=-2)


def kernel(x_prompt, x_sample, cache_ckv, cache_krope, state_ret_fwd, state_ret_bwd, c, c_ctx,
           w_mod, b_mod, g_norm_mix, g_norm_ffn, w_in, g_q_a, w_q_b, g_kv_a, w_kv_b,
           g_qn, g_qr, g_kn, g_kr, ret_p_fwd, ret_p_bwd, g_ret_gn, b_ret_gn, w_o,
           w_ffn_in, w_ffn_out):
    batch, seq, _ = x_prompt.shape
    dec_batch, dec_seq, _ = x_sample.shape
    depth = w_in.shape[0]

    wts = _prepare_weights(g_norm_mix, g_norm_ffn, w_in, g_q_a, w_q_b, g_kv_a, w_kv_b, g_qn, g_qr,
                           g_kn, g_kr)
    conds = jnp.concatenate([c_ctx[None], c, jnp.zeros((8 - 1 - dec_batch, D_MODEL), F32)], axis=0)
    mod = _modulation(conds, w_mod, b_mod)
    dmask, dvec = _decay_tables(jnp.stack([ret_p_fwd, ret_p_bwd], axis=1))
    gn_g, gn_b = g_ret_gn, b_ret_gn

    rope_tabs = _rope_tables(dec_seq)
    kvn_cache = _cache_up(cache_ckv, wts["w_kvb"], wts["g_kn"])
    krp_cache = jnp.pad(cache_krope, ((0, 0), (0, 0), (0, 0), (MLA_NOPE, LANES - MLA_NOPE - MLA_ROPE)))
    krp_cache = krp_cache.astype(BF16)
    s_f0 = _blockdiag_states(state_ret_fwd)
    s_b0 = _blockdiag_states(state_ret_bwd)

    def ctx_row(row):
        return 0

    def lat_row(row):
        return 1 + row // dec_seq

    x = x_prompt.reshape(batch * seq, D_MODEL)
    y = x_sample.reshape(dec_batch * dec_seq, D_MODEL)
    caches = (jax.ShapeDtypeStruct((batch, depth, seq, KV_RANK), F32),
              jax.ShapeDtypeStruct((batch, depth, MLA_ROPE, seq), F32))
    states = (jax.ShapeDtypeStruct((batch, depth, RET_HEADS, RET_DK, RET_DK), F32),) * 2
    for l in range(depth):
        pr = _proj(x, mod, l, ctx_row, wts, None, caches)
        caches = tuple(pr[7:9])
        casts = [(w_o, l, D_MODEL, 1), (w_ffn_in, l, 2 * D_FF, 1), (w_ffn_out, l, D_MODEL, 2)]
        mixed = _mix(pr, dmask, dvec, gn_g, gn_b, l, batch, seq, None, states, casts)
        ret, attn = mixed[:2]
        states = tuple(mixed[2:4])
        big_w = tuple(mixed[4:7])
        x = _out(x, ret, attn, mod, l, ctx_row, wts, big_w, OUT_TILE)

        pr = _proj(y, mod, l, lat_row, wts, rope_tabs, None)
        ret, attn = _mix(pr, dmask, dvec, gn_g, gn_b, l, dec_batch, dec_seq,
                         (kvn_cache, krp_cache, s_f0, s_b0), None)
        y = _out(y, ret, attn, mod, l, lat_row, wts, big_w, LATENT_OUT_TILE)

    return (x.reshape(batch, seq, D_MODEL), y.reshape(dec_batch, dec_seq, D_MODEL),
            caches[0], jnp.swapaxes(caches[1], 2, 3), states[0], states[1])
```

```python
import functools

import jax
import jax.numpy as jnp
import numpy as np
from jax import lax
from jax.experimental import pallas as pl
from jax.experimental.pallas import tpu as pltpu

D_MODEL = 1024
N_MOD = 6
RET_HEADS = 8
RET_DK = 64
RET_WIDTH = 512
MLA_HEADS = 8
MLA_NOPE = 64
MLA_ROPE = 32
MLA_V = 64
Q_RANK = 256
KV_RANK = 128
D_FF = 2816
GRID_W = 64
ROPE_BASE = 10000.0
EPS = 1e-6

LANES = 128
HEAD_PAD = LANES
N_PAIRS = RET_HEADS // 2
IN_COLS_PAD = 4 * RET_WIDTH + Q_RANK + KV_RANK + LANES
OUT_TILE = 1024
LATENT_OUT_TILE = 512
PROJ_TILE = 1024
LATENT_PROJ_TILE = 512
SUB_TILE = 256
FF_CHUNK = 256
MIX_CHUNK = 256
MIX_SEQS = 4
Q_DEC_F, K_DEC_F, Q_DEC_B, K_DEC_B, C_DEC_F, C_DEC_B = range(6)
N_DVEC = 6
K_SCALE = RET_DK ** -0.5
Q_FOLD = (MLA_NOPE + MLA_ROPE) ** -0.5 * 1.4426950408889634
VMEM_LIMIT = 56 * 1024 * 1024

BF16 = jnp.bfloat16
F32 = jnp.float32
_NT = (((1,), (1,)), ((), ()))


def _dot(a, b):
    return jnp.dot(a, b, preferred_element_type=F32)


def _dot_nt(a, b):
    return lax.dot_general(a, b, _NT, preferred_element_type=F32)


def _rms(x, g):
    return x * lax.rsqrt(jnp.mean(x * x, axis=-1, keepdims=True) + EPS) * g


def _silu(x):
    return x * jax.nn.sigmoid(x)


def _masked_mean_sq(x, mask, n):
    return jnp.sum(jnp.where(mask, x * x, 0.0), axis=-1, keepdims=True) * (1.0 / n)


def _mod_kernel(c_ref, w_ref, b_ref, p_ref, o_ref, dmask_ref, dvec_ref):
    _decay_kernel(p_ref, dmask_ref, dvec_ref)

    a = _silu(c_ref[...])
    w = w_ref[...]
    a_hi = a.astype(BF16)
    a_lo = (a - a_hi.astype(F32)).astype(BF16)
    w_hi = w.astype(BF16)
    w_lo = (w - w_hi.astype(F32)).astype(BF16)
    both = _dot(jnp.concatenate([a_hi, a_lo], axis=0), w_hi)
    part = both[0:8] + both[8:16] + _dot(a_hi, w_lo)

    @pl.when(pl.program_id(1) == 0)
    def _():
        o_ref[...] = part + b_ref[pl.ds(pl.program_id(0), 1), :]

    @pl.when(pl.program_id(1) > 0)
    def _():
        o_ref[...] += part


def _modulation(conds, w_mod, b_mod, decay_p):
    depth, d, n = w_mod.shape
    tk = d // N_PAIRS
    c = MIX_CHUNK
    return pl.pallas_call(
        _mod_kernel,
        grid=(depth, N_PAIRS),
        in_specs=[
            pl.BlockSpec((8, tk), lambda l, k: (0, k)),
            pl.BlockSpec((None, tk, n), lambda l, k: (l, k, 0)),
            pl.BlockSpec((depth, n), lambda l, k: (0, 0)),
            pl.BlockSpec(memory_space=pltpu.SMEM),
        ],
        out_specs=[
            pl.BlockSpec((None, 8, n), lambda l, k: (l, 0, 0)),
            pl.BlockSpec((None, 2, c, c), lambda l, p: (l, p, 0, 0)),
            pl.BlockSpec((None, None, N_DVEC, c, LANES), lambda l, p: (l, p, 0, 0, 0)),
        ],
        out_shape=[
            jax.ShapeDtypeStruct((depth, 8, n), F32),
            jax.ShapeDtypeStruct((depth, RET_HEADS, c, c), F32),
            jax.ShapeDtypeStruct((depth, N_PAIRS, N_DVEC, c, LANES), F32),
        ],
        compiler_params=pltpu.CompilerParams(
            dimension_semantics=("arbitrary", "arbitrary"), vmem_limit_bytes=VMEM_LIMIT),
        name="modulation",
    )(conds, w_mod, b_mod, decay_p.reshape(-1))


def _norm_kn(kv, gkn):
    lane = lax.broadcasted_iota(jnp.int32, (kv.shape[0], LANES), 1)
    lo = lane < MLA_NOPE
    out = []
    for h in range(MLA_HEADS):
        kvh = kv[:, h * HEAD_PAD:(h + 1) * HEAD_PAD]
        rs = lax.rsqrt(_masked_mean_sq(kvh, lo, MLA_NOPE) + EPS)
        out.append((kvh * jnp.where(lo, rs * gkn, 1.0)).astype(BF16))
    return out


def _cache_kernel(ckv_ref, wkvb_ref, gkn_ref, kvn_ref):
    kv = _dot(ckv_ref[...].astype(BF16), wkvb_ref[...])
    gkn = gkn_ref[pl.ds(pl.program_id(0), 1), :]
    for h, kvh in enumerate(_norm_kn(kv, gkn)):
        kvn_ref[:, h * HEAD_PAD:(h + 1) * HEAD_PAD] = kvh


def _cache_up(cache_ckv, wkvb, gkn):
    nb, depth, past, _ = cache_ckv.shape
    width = MLA_HEADS * HEAD_PAD
    return pl.pallas_call(
        _cache_kernel,
        grid=(depth, nb),
        in_specs=[
            pl.BlockSpec((None, None, past, KV_RANK), lambda l, b: (b, l, 0, 0)),
            pl.BlockSpec((None, KV_RANK, width), lambda l, b: (l, 0, 0)),
            pl.BlockSpec((depth, LANES), lambda l, b: (0, 0)),
        ],
        out_specs=pl.BlockSpec((None, None, past, width), lambda l, b: (l, b, 0, 0)),
        out_shape=jax.ShapeDtypeStruct((depth, nb, past, width), BF16),
        compiler_params=pltpu.CompilerParams(
            dimension_semantics=("arbitrary", "arbitrary"), vmem_limit_bytes=VMEM_LIMIT),
        name="cache_up",
    )(cache_ckv, wkvb, gkn)


def _mod_vec(mod_ref, cond, k):
    return mod_ref[pl.ds(cond, 1), k * D_MODEL:(k + 1) * D_MODEL]


def _gain_spec(g):
    return pl.BlockSpec(g.shape, lambda *_: (0, 0))


def _is_array(x):
    return not isinstance(x, jax.ShapeDtypeStruct)


def _layer_spec(n, shape, layer, aliased):
    rest = tuple(shape[2:])
    zeros = (0,) * len(rest)
    if aliased:
        return pl.BlockSpec((n, None) + rest, lambda i: (i, layer) + zeros)
    return pl.BlockSpec((n, shape[1]) + rest, lambda i: (i, 0) + zeros)


def _put_layer(ref, i, tail, layer, fresh, value):
    if not fresh:
        ref[(i,) + tail] = value
        return
    for l in range(ref.shape[1]):
        ref[(i, l) + tail] = value if l == layer else jnp.zeros_like(value)


def _proj_kernel(*refs, rope, cond_row, n_alias, layer):
    (x_ref, mod_ref, gmix_ref, win_ref, gqa_ref, wqb_ref, gkva_ref, wkvb_ref,
     gq_ref, gkn_ref, gkr_ref) = refs[:11]
    if rope:
        wqb_sw_ref, gq_sw_ref, c64_ref, s64_ref, c32_ref, s32_ref = refs[11:17]
        rq_ref, rk_ref, rv_ref, rg_ref, qcat_ref, kvn_ref, krp_ref = refs[17:]
    else:
        (rq_ref, rk_ref, rv_ref, rg_ref, qcat_ref, kvn_ref, krp_ref, ckv_ref,
         kro_ref) = refs[11 + n_alias:]

    cond = cond_row(pl.program_id(0) * x_ref.shape[0])
    sh1 = _mod_vec(mod_ref, cond, 0)
    sc1 = _mod_vec(mod_ref, cond, 1)
    lrow = slice(layer, layer + 1)
    g_mod = gmix_ref[lrow, :] * (1.0 + sc1)
    w = RET_WIDTH
    lane = lax.broadcasted_iota(jnp.int32, (SUB_TILE, LANES), 1)
    nope = lane < MLA_NOPE
    is_rope = jnp.logical_and(lane >= MLA_NOPE, lane < MLA_NOPE + MLA_ROPE)

    for r in range(x_ref.shape[0] // SUB_TILE):
        rs = slice(r * SUB_TILE, (r + 1) * SUB_TILE)
        h = (_rms(x_ref[rs, :], g_mod) + sh1).astype(BF16)
        z = _dot_nt(h, win_ref[...])
        if rope:
            grp = RET_DK // 4
            z_sw = _dot_nt(h, jnp.concatenate(
                [win_ref[(g ^ 1) * grp:((g ^ 1) + 1) * grp, :] for g in range(2 * w // grp)], 0))

        for j in range(w // LANES):
            sl = slice(j * LANES, (j + 1) * LANES)
            q = z[:, j * LANES:(j + 1) * LANES]
            k = z[:, w + j * LANES:w + (j + 1) * LANES]
            if rope:
                q_sw = z_sw[:, j * LANES:(j + 1) * LANES]
                k_sw = z_sw[:, w + j * LANES:w + (j + 1) * LANES]
                q = q * c64_ref[rs, :] + q_sw * s64_ref[rs, :]
                k = k * c64_ref[rs, :] + k_sw * s64_ref[rs, :]
            rq_ref[rs, sl] = q.astype(BF16)
            rk_ref[rs, sl] = k.astype(BF16)
        rv_ref[rs, :] = z[:, 2 * w:3 * w].astype(BF16)
        rg_ref[rs, :] = z[:, 3 * w:4 * w]

        o = 4 * w
        qa = z[:, o:o + Q_RANK]
        kva = z[:, o + Q_RANK:o + Q_RANK + KV_RANK]
        kr2 = z[:, o + Q_RANK + KV_RANK:]

        qa_n = _rms(qa, gqa_ref[lrow, :]).astype(BF16)
        q = _dot(qa_n, wqb_ref[...])
        if rope:
            q_sw = _dot(qa_n, wqb_sw_ref[...])
        for hh in range(MLA_HEADS):
            hs = slice(hh * HEAD_PAD, (hh + 1) * HEAD_PAD)
            qh = q[:, hs]
            rs_n = lax.rsqrt(_masked_mean_sq(qh, nope, MLA_NOPE) + EPS)
            rs_r = lax.rsqrt(_masked_mean_sq(qh, is_rope, MLA_ROPE) + EPS)
            qn = qh * jnp.where(nope, rs_n, rs_r) * gq_ref[lrow, :]
            if rope:
                qn = (qn * c32_ref[rs, :]
                      + q_sw[:, hs] * rs_r * gq_sw_ref[lrow, :] * s32_ref[rs, :])
            qcat_ref[rs, hs] = qn.astype(BF16)

        ckv = _rms(kva, gkva_ref[lrow, :])
        kv = _dot(ckv.astype(BF16), wkvb_ref[...])
        for hh, kvh in enumerate(_norm_kn(kv, gkn_ref[lrow, :])):
            kvn_ref[rs, hh * HEAD_PAD:(hh + 1) * HEAD_PAD] = kvh

        rs_k = lax.rsqrt(_masked_mean_sq(kr2, lane < MLA_ROPE, MLA_ROPE) + EPS)
        krn = kr2 * rs_k * gkr_ref[lrow, :]
        krp = pltpu.roll(krn, MLA_NOPE, 1)
        if rope:
            x1_pos = ((lane - MLA_NOPE) % (MLA_ROPE // 2)) < MLA_ROPE // 4
            partner = jnp.where(x1_pos, pltpu.roll(krp, LANES - MLA_ROPE // 4, 1),
                                pltpu.roll(krp, MLA_ROPE // 4, 1))
            krp = krp * c32_ref[rs, :] + partner * s32_ref[rs, :]
        else:
            seq = ckv_ref.shape[-2]
            krt = krn.T[0:MLA_ROPE, :]
            for t in range(SUB_TILE // seq):
                i = r * (SUB_TILE // seq) + t
                _put_layer(ckv_ref, i, (), layer, n_alias == 0, ckv[t * seq:(t + 1) * seq])
                _put_layer(kro_ref, i, (), layer, n_alias == 0, krt[:, t * seq:(t + 1) * seq])
        krp_ref[rs, :] = krp.astype(BF16)


def _proj(x, mod, layer, cond_row, wts, rope_tabs, caches):
    m = x.shape[0]
    rope = rope_tabs is not None
    tm = LATENT_PROJ_TILE if rope else PROJ_TILE
    width = MLA_HEADS * HEAD_PAD

    def row(i):
        return (i, 0)

    def lay3(i):
        return (layer, 0, 0)

    in_specs = [
        pl.BlockSpec((tm, D_MODEL), row),
        pl.BlockSpec((None, 8, N_MOD * D_MODEL), lay3),
        _gain_spec(wts["g_mix"]),
        pl.BlockSpec((None, IN_COLS_PAD, D_MODEL), lay3),
        _gain_spec(wts["g_qa"]),
        pl.BlockSpec((None, Q_RANK, width), lay3),
        _gain_spec(wts["g_kva"]),
        pl.BlockSpec((None, KV_RANK, width), lay3),
        _gain_spec(wts["g_q"]), _gain_spec(wts["g_kn"]), _gain_spec(wts["g_kr"]),
    ]
    args = [x, mod, wts["g_mix"], wts["w_in_t"], wts["g_qa"], wts["w_qb"], wts["g_kva"],
            wts["w_kvb"], wts["g_q"], wts["g_kn"], wts["g_kr"]]
    out_shape = [
        jax.ShapeDtypeStruct((m, RET_WIDTH), BF16),
        jax.ShapeDtypeStruct((m, RET_WIDTH), BF16),
        jax.ShapeDtypeStruct((m, RET_WIDTH), BF16),
        jax.ShapeDtypeStruct((m, RET_WIDTH), F32),
        jax.ShapeDtypeStruct((m, width), BF16),
        jax.ShapeDtypeStruct((m, width), BF16),
        jax.ShapeDtypeStruct((m, LANES), BF16),
    ]
    out_specs = [
        pl.BlockSpec((tm, RET_WIDTH), row), pl.BlockSpec((tm, RET_WIDTH), row),
        pl.BlockSpec((tm, RET_WIDTH), row), pl.BlockSpec((tm, RET_WIDTH), row),
        pl.BlockSpec((tm, width), row), pl.BlockSpec((tm, width), row),
        pl.BlockSpec((tm, LANES), row),
    ]
    aliases = {}
    if rope:
        n_lat = rope_tabs[0].shape[0]
        tiles = n_lat // tm
        in_specs += [pl.BlockSpec((None, Q_RANK, width), lay3), _gain_spec(wts["g_q_sw"])]
        args += [wts["w_qb_sw"], wts["g_q_sw"]]
        in_specs += [pl.BlockSpec((tm, LANES), lambda i: (i % tiles, 0))] * len(rope_tabs)
        args += list(rope_tabs)
    else:
        seq = caches[0].shape[2]
        assert SUB_TILE % seq == 0
        nb = tm // seq
        if _is_array(caches[0]):
            in_specs += [pl.BlockSpec(memory_space=pl.ANY)] * 2
            aliases = {len(args): len(out_shape), len(args) + 1: len(out_shape) + 1}
            args += list(caches)
        out_shape += [jax.ShapeDtypeStruct(c.shape, c.dtype) for c in caches]
        out_specs += [_layer_spec(nb, c.shape, layer, bool(aliases)) for c in caches]

    return pl.pallas_call(
        functools.partial(_proj_kernel, rope=rope, cond_row=cond_row, n_alias=len(aliases),
                          layer=layer),
        grid=(m // tm,),
        in_specs=in_specs,
        out_specs=out_specs,
        out_shape=out_shape,
        input_output_aliases=aliases,
        compiler_params=pltpu.CompilerParams(
            dimension_semantics=("arbitrary",), vmem_limit_bytes=VMEM_LIMIT),
        name="proj_latent" if rope else "proj_context",
    )(*args)


def _log_gamma(p):
    return jnp.log1p(-jnp.exp2(-p))


def _decay_kernel(p_ref, dmask_ref, dvec_ref):
    c = MIX_CHUNK
    base = pl.program_id(0) * (2 * RET_HEADS)
    pair = pl.program_id(1)
    ri = lax.broadcasted_iota(jnp.int32, (c, c), 0)
    ci = lax.broadcasted_iota(jnp.int32, (c, c), 1)
    dif = (ri - ci).astype(F32)
    for e in range(2):
        h = 2 * pair + e
        lg_f = _log_gamma(jnp.full((c, c), p_ref[base + h], F32))
        lg_b = _log_gamma(jnp.full((c, c), p_ref[base + RET_HEADS + h], F32))
        fwd = jnp.where(dif >= 0, jnp.exp(jnp.maximum(dif, 0.0) * lg_f), 0.0)
        bwd = jnp.where(dif <= 0, jnp.exp(jnp.maximum(-dif, 0.0) * lg_b), 0.0)
        dmask_ref[e] = (fwd + bwd) * K_SCALE
    lane = lax.broadcasted_iota(jnp.int32, (c, LANES), 1)
    rowf = lax.broadcasted_iota(jnp.int32, (c, LANES), 0).astype(F32)
    lo = lane < RET_DK
    lg_f = _log_gamma(jnp.where(lo, p_ref[base + 2 * pair], p_ref[base + 2 * pair + 1]))
    lg_b = _log_gamma(jnp.where(lo, p_ref[base + RET_HEADS + 2 * pair],
                                p_ref[base + RET_HEADS + 2 * pair + 1]))
    dvec_ref[Q_DEC_F] = jnp.exp((rowf + 1.0) * lg_f)
    dvec_ref[K_DEC_F] = jnp.exp((c - 1.0 - rowf) * lg_f) * K_SCALE
    dvec_ref[Q_DEC_B] = jnp.exp((c - rowf) * lg_b)
    dvec_ref[K_DEC_B] = jnp.exp(rowf * lg_b) * K_SCALE
    dvec_ref[C_DEC_F] = jnp.exp(c * lg_f)
    dvec_ref[C_DEC_B] = jnp.exp(c * lg_b)


def _mix_kernel(*refs, n_seq, n_sub, latent, n_alias, n_cast, layer):
    c = MIX_CHUNK
    nc = n_seq // c
    (rq_ref, rk_ref, rv_ref, rg_ref, qcat_ref, kvn_ref, krp_ref, gng_ref, gnb_ref,
     dmask_ref, dvec_ref) = refs[:11]
    if latent:
        kvc_ref, krc_ref, sf0_ref, sb0_ref, ret_ref, attn_ref, st_scr = refs[11:]
    else:
        n_in = 11 + n_alias
        cast_src = refs[n_in:n_in + n_cast]
        ret_ref, attn_ref, sf_ref, sb_ref = refs[n_in + n_cast:n_in + n_cast + 4]
        cast_dst = refs[n_in + n_cast + 4:]
        for src, dst in zip(cast_src, cast_dst):
            w = src.shape[1]
            dst[:, 0:w] = src[...].astype(BF16)
            if dst.shape[1] > w:
                dst[:, w:] = jnp.zeros((dst.shape[0], dst.shape[1] - w), BF16)

    lane = lax.broadcasted_iota(jnp.int32, (c, LANES), 1)
    lo = lane < RET_DK
    lrow = slice(layer, layer + 1)
    sq_r = lax.broadcasted_iota(jnp.int32, (LANES, LANES), 0)
    sq_c = lax.broadcasted_iota(jnp.int32, (LANES, LANES), 1)
    blockdiag = (sq_r < RET_DK) == (sq_c < RET_DK)

    def cols(j):
        return slice(j * LANES, (j + 1) * LANES)

    def state_update(pair, d, rows):
        kp = rk_ref[rows, cols(pair)]
        vp = rv_ref[rows, cols(pair)]
        kdt = (kp.astype(F32) * dvec_ref[pair, K_DEC_B if d else K_DEC_F]).T.astype(BF16)
        return jnp.where(blockdiag, _dot(kdt, vp), 0.0)

    nope_n = lax.broadcasted_iota(jnp.int32, (n_seq, LANES), 1) < MLA_NOPE
    if latent:
        nope_c = lax.broadcasted_iota(jnp.int32, (kvc_ref.shape[0], LANES), 1) < MLA_NOPE

    def states(sub):
        base = sub * n_seq
        for pair in range(N_PAIRS):
            for d in range(2):
                if latent:
                    s = (sb0_ref if d else sf0_ref)[pair]
                    cdec = dvec_ref[pair, C_DEC_B if d else C_DEC_F][0:LANES, :]
                    order = list(range(nc - 1, -1, -1)) if d else list(range(nc))
                    for idx, ch in enumerate(order):
                        st_scr[d, pair, ch] = s.astype(BF16)
                        if idx < nc - 1:
                            s = s * cdec + state_update(pair, d, pl.ds(base + ch * c, c))
                else:
                    s = state_update(pair, d, pl.ds(base, c))
                    st_ref = sb_ref if d else sf_ref
                    fresh = n_alias == 0
                    _put_layer(st_ref, sub, (2 * pair,), layer, fresh, s[0:RET_DK, 0:RET_DK])
                    _put_layer(st_ref, sub, (2 * pair + 1,), layer, fresh, s[RET_DK:, RET_DK:])

    def chunk_body(ch, base):
        rows = pl.ds(pl.multiple_of(base + ch * c, c), c)
        keys = pl.ds(base, n_seq)

        for pair in range(N_PAIRS):
            qp = rq_ref[rows, cols(pair)]
            kp = rk_ref[rows, cols(pair)]
            vp = rv_ref[rows, cols(pair)]
            zero = jnp.zeros_like(qp)
            a0 = (_dot_nt(jnp.where(lo, qp, zero), kp) * dmask_ref[2 * pair]).astype(BF16)
            a1 = (_dot_nt(jnp.where(lo, zero, qp), kp) * dmask_ref[2 * pair + 1]).astype(BF16)
            tot = jnp.where(lo, _dot(a0, vp), _dot(a1, vp))
            if latent:
                tot = (tot + _dot(qp, st_scr[0, pair, ch]) * dvec_ref[pair, Q_DEC_F]
                       + _dot(qp, st_scr[1, pair, ch]) * dvec_ref[pair, Q_DEC_B])
            inv = 1.0 / RET_DK
            m0 = jnp.sum(jnp.where(lo, tot, 0.0), axis=-1, keepdims=True) * inv
            m1 = jnp.sum(jnp.where(lo, 0.0, tot), axis=-1, keepdims=True) * inv
            y = tot - jnp.where(lo, m0, m1)
            v0 = jnp.sum(jnp.where(lo, y * y, 0.0), axis=-1, keepdims=True) * inv
            v1 = jnp.sum(jnp.where(lo, 0.0, y * y), axis=-1, keepdims=True) * inv
            yn = (y * lax.rsqrt(jnp.where(lo, v0, v1) + EPS) * gng_ref[lrow, cols(pair)]
                  + gnb_ref[lrow, cols(pair)])
            ret_ref[rows, cols(pair)] = (yn * _silu(rg_ref[rows, cols(pair)])).astype(BF16)

        krp = krp_ref[keys, :]
        for pair in range(N_PAIRS):
            outs = []
            for e in range(2):
                h = 2 * pair + e
                qc = qcat_ref[rows, cols(h)]
                kv = kvn_ref[keys, cols(h)]
                s = _dot_nt(qc, jnp.where(nope_n, kv, krp))
                m = jnp.max(s, axis=-1, keepdims=True)
                if latent:
                    kv_c = kvc_ref[:, cols(h)]
                    s2 = _dot_nt(qc, jnp.where(nope_c, kv_c, krc_ref[...]))
                    m = jnp.maximum(m, jnp.max(s2, axis=-1, keepdims=True))
                    p2 = jnp.exp2(s2 - m)
                p = jnp.exp2(s - m)
                den = jnp.sum(p, axis=-1, keepdims=True)
                acc = _dot(p.astype(BF16), kv)
                if latent:
                    den = den + jnp.sum(p2, axis=-1, keepdims=True)
                    acc = acc + _dot(p2.astype(BF16), kv_c)
                outs.append(acc / den)
            attn_ref[rows, cols(pair)] = jnp.where(
                lo, pltpu.roll(outs[0], MLA_V, 1), outs[1]).astype(BF16)
        return base

    for sub in range(n_sub):
        states(sub)
    for sub in range(n_sub):
        if nc == 1:
            chunk_body(0, sub * n_seq)
        else:
            lax.fori_loop(0, nc, chunk_body, sub * n_seq)


def _mix(proj_out, dmask, dvec, gn_g, gn_b, layer, n_batch, n_seq, latent_in, states_out,
         casts=()):
    rq, rk, rv, rg, qcat, kvn, krp = proj_out[:7]
    latent = latent_in is not None
    aliases = {}
    m = n_batch * n_seq
    c = MIX_CHUNK
    width = MLA_HEADS * HEAD_PAD

    def row(b):
        return (b, 0)

    def lay3(b):
        return (layer, 0, 0)

    once = pl.Buffered(1)
    n_sub = 1 if latent else MIX_SEQS
    n_steps = n_batch // n_sub
    blk = n_sub * n_seq
    in_specs = [
        pl.BlockSpec((blk, RET_WIDTH), row), pl.BlockSpec((blk, RET_WIDTH), row),
        pl.BlockSpec((blk, RET_WIDTH), row), pl.BlockSpec((blk, RET_WIDTH), row),
        pl.BlockSpec((blk, width), row), pl.BlockSpec((blk, width), row),
        pl.BlockSpec((blk, LANES), row),
        _gain_spec(gn_g), _gain_spec(gn_b),
        pl.BlockSpec((None, RET_HEADS, c, c), lambda b: (layer, 0, 0, 0), pipeline_mode=once),
        pl.BlockSpec((None, N_PAIRS, N_DVEC, c, LANES), lambda b: (layer, 0, 0, 0, 0),
                     pipeline_mode=once),
    ]
    args = [rq, rk, rv, rg, qcat, kvn, krp, gn_g, gn_b, dmask, dvec]
    out_shape = [jax.ShapeDtypeStruct((m, RET_WIDTH), BF16),
                 jax.ShapeDtypeStruct((m, MLA_HEADS * MLA_V), BF16)]
    out_specs = [pl.BlockSpec((blk, RET_WIDTH), row), pl.BlockSpec((blk, RET_WIDTH), row)]
    scratch = []
    if latent:
        kvn_c, krp_c, s_f0, s_b0 = latent_in
        past = kvn_c.shape[2]
        st_spec = pl.BlockSpec((None, None, N_PAIRS, LANES, LANES), lambda b: (b, layer, 0, 0, 0))
        in_specs += [
            pl.BlockSpec((None, None, past, width), lambda b: (layer, b, 0, 0)),
            pl.BlockSpec((None, None, past, LANES), lambda b: (b, layer, 0, 0)),
            st_spec, st_spec,
        ]
        args += [kvn_c, krp_c, s_f0, s_b0]
        scratch = [pltpu.VMEM((2, N_PAIRS, n_seq // c, LANES, LANES), BF16)]
    else:
        if _is_array(states_out[0]):
            in_specs += [pl.BlockSpec(memory_space=pl.ANY)] * 2
            aliases = {len(args): len(out_shape), len(args) + 1: len(out_shape) + 1}
            args += list(states_out)
        out_shape += [jax.ShapeDtypeStruct(s.shape, s.dtype) for s in states_out]
        out_specs += [_layer_spec(n_sub, s.shape, layer, bool(aliases)) for s in states_out]
        for w, w_layer, out_cols, stride in casts:
            blocks = n_steps // stride
            rows = w.shape[1] // blocks
            assert rows * blocks == w.shape[1] and rows % 16 == 0 and stride * blocks == n_steps
            in_specs.append(pl.BlockSpec(
                (None, rows, w.shape[2]),
                lambda b, w_layer=w_layer, stride=stride: (w_layer, b // stride, 0)))
            args.append(w)
            out_shape.append(jax.ShapeDtypeStruct((w.shape[1], out_cols), BF16))
            out_specs.append(pl.BlockSpec((rows, out_cols),
                                          lambda b, stride=stride: (b // stride, 0)))

    return pl.pallas_call(
        functools.partial(_mix_kernel, n_seq=n_seq, n_sub=n_sub, latent=latent, layer=layer,
                          n_alias=len(aliases), n_cast=len(casts)),
        grid=(n_steps,),
        in_specs=in_specs,
        out_specs=out_specs,
        out_shape=out_shape,
        input_output_aliases=aliases,
        scratch_shapes=scratch,
        compiler_params=pltpu.CompilerParams(
            dimension_semantics=("arbitrary",), vmem_limit_bytes=VMEM_LIMIT),
        name="mix_latent" if latent else "mix_context",
    )(*args)


def _out_kernel(x_ref, ret_ref, attn_ref, mod_ref, gffn_ref, wo_ref, wfi_ref, wfo_ref, o_ref,
                act_scr, *, cond_row, layer):
    cond = cond_row(pl.program_id(0) * x_ref.shape[0])
    gt1 = _mod_vec(mod_ref, cond, 2)
    sh2 = _mod_vec(mod_ref, cond, 3)
    sc2 = _mod_vec(mod_ref, cond, 4)
    gt2 = _mod_vec(mod_ref, cond, 5)
    mixed = _dot(ret_ref[...], wo_ref[0:RET_WIDTH, :]) + _dot(attn_ref[...], wo_ref[RET_WIDTH:, :])
    x1 = x_ref[...] + gt1 * mixed
    h = (_rms(x1, gffn_ref[layer:layer + 1, :] * (1.0 + sc2)) + sh2).astype(BF16)
    for c0 in range(0, D_FF, FF_CHUNK):
        cw = min(FF_CHUNK, D_FF - c0)
        gate = _dot(h, wfi_ref[:, c0:c0 + cw])
        up = _dot(h, wfi_ref[:, D_FF + c0:D_FF + c0 + cw])
        act_scr[:, c0:c0 + cw] = (_silu(gate) * up).astype(BF16)
    o_ref[...] = x1 + gt2 * _dot(act_scr[...], wfo_ref[...])


def _out(x, ret, attn, mod, layer, cond_row, wts, big_w, tm):
    m = x.shape[0]

    def row(i):
        return (i, 0)

    def lay3(i):
        return (layer, 0, 0)

    def whole(i):
        return (0, 0)

    once = pl.Buffered(1)
    return pl.pallas_call(
        functools.partial(_out_kernel, cond_row=cond_row, layer=layer),
        grid=(m // tm,),
        in_specs=[
            pl.BlockSpec((tm, D_MODEL), row),
            pl.BlockSpec((tm, RET_WIDTH), row),
            pl.BlockSpec((tm, RET_WIDTH), row),
            pl.BlockSpec((None, 8, N_MOD * D_MODEL), lay3),
            _gain_spec(wts["g_ffn"]),
            pl.BlockSpec((D_MODEL, D_MODEL), whole, pipeline_mode=once),
            pl.BlockSpec((D_MODEL, 2 * D_FF), whole, pipeline_mode=once),
            pl.BlockSpec((D_FF, D_MODEL), whole, pipeline_mode=once),
        ],
        out_specs=pl.BlockSpec((tm, D_MODEL), row),
        out_shape=jax.ShapeDtypeStruct((m, D_MODEL), F32),
        scratch_shapes=[pltpu.VMEM((tm, D_FF), BF16)],
        compiler_params=pltpu.CompilerParams(
            dimension_semantics=("arbitrary",), vmem_limit_bytes=VMEM_LIMIT),
        name="out_ffn",
    )(x, ret, attn, mod, wts["g_ffn"], *big_w)


def _rope_tables(n_lat):
    pos = np.arange(n_lat)
    row = (pos // GRID_W).astype(np.float32)[:, None]
    col = (pos % GRID_W).astype(np.float32)[:, None]
    lane = np.arange(LANES)[None, :]

    def tables(d, start, period):
        rel = (lane - start) % period
        active = np.logical_and(lane >= start, rel < d)
        half = d // 2
        nf = half // 2
        inv = np.float32(ROPE_BASE) ** (-((rel % nf).astype(np.float32)) / np.float32(nf))
        ang = (np.where(rel < half, row, col) * inv).astype(np.float32)
        cos, sin = np.cos(ang), np.sin(ang)
        first = (rel % half) < nf
        c = np.where(active, cos, 1.0)
        s = np.where(active, np.where(first, -sin, sin), 0.0)
        return tuple(jnp.asarray(t, F32) for t in (c, s))

    return tables(RET_DK, 0, RET_DK) + tables(MLA_ROPE, MLA_NOPE, LANES)


def _swap_partners(a, axis, d):
    shape = a.shape
    split = shape[:axis] + (shape[axis] // d, 2, 2, d // 4) + shape[axis + 1:]
    return jnp.flip(a.reshape(split), axis=axis + 2).reshape(shape)


def _prepare_weights(g_norm_mix, g_norm_ffn, w_in, g_q_a, w_q_b, g_kv_a, w_kv_b, g_qn, g_qr, g_kn,
                     g_kr):
    depth = w_in.shape[0]
    n_in = w_in.shape[2]
    w_in_t = jnp.pad(jnp.swapaxes(w_in, 1, 2).astype(BF16),
                     ((0, 0), (0, IN_COLS_PAD - n_in), (0, 0)))
    w_qb = w_q_b.reshape(depth, Q_RANK, MLA_HEADS, MLA_NOPE + MLA_ROPE)
    w_qb_sw = jnp.pad(_swap_partners(w_qb[..., MLA_NOPE:], 3, MLA_ROPE),
                      ((0, 0), (0, 0), (0, 0), (MLA_NOPE, HEAD_PAD - MLA_NOPE - MLA_ROPE)))
    w_qb = jnp.pad(w_qb, ((0, 0), (0, 0), (0, 0), (0, HEAD_PAD - MLA_NOPE - MLA_ROPE)))
    zeros32 = jnp.zeros((depth, MLA_ROPE), F32)
    zeros64 = jnp.zeros((depth, MLA_NOPE), F32)
    return {
        "g_mix": g_norm_mix,
        "g_ffn": g_norm_ffn,
        "w_in_t": w_in_t,
        "g_qa": g_q_a,
        "w_qb": w_qb.reshape(depth, Q_RANK, MLA_HEADS * HEAD_PAD).astype(BF16),
        "g_kva": g_kv_a,
        "w_kvb": w_kv_b.astype(BF16),
        "g_q": jnp.concatenate([g_qn, g_qr, zeros32], axis=-1) * Q_FOLD,
        "g_kn": jnp.concatenate([g_kn, jnp.ones((depth, MLA_V), F32)], axis=-1),
        "g_kr": jnp.pad(g_kr, ((0, 0), (0, LANES - MLA_ROPE))),
        "w_qb_sw": w_qb_sw.reshape(depth, Q_RANK, MLA_HEADS * HEAD_PAD).astype(BF16),
        "g_q_sw": jnp.concatenate([zeros64, _swap_partners(g_qr, 1, MLA_ROPE), zeros32],
                                  axis=-1) * Q_FOLD,
    }


def _blockdiag_states(s):
    b, l = s.shape[:2]
    s = s.reshape(b, l, N_PAIRS, 2, RET_DK, RET_DK)
    z = jnp.zeros_like(s[:, :, :, 0])
    top = jnp.concatenate([s[:, :, :, 0], z], axis=-1)
    bot = jnp.concatenate([z, s[:, :, :, 1]], axis=-1)
    return jnp.concatenate([top, bot], axis=-2)


def kernel(x_prompt, x_sample, cache_ckv, cache_krope, state_ret_fwd, state_ret_bwd, c, c_ctx,
           w_mod, b_mod, g_norm_mix, g_norm_ffn, w_in, g_q_a, w_q_b, g_kv_a, w_kv_b,
           g_qn, g_qr, g_kn, g_kr, ret_p_fwd, ret_p_bwd, g_ret_gn, b_ret_gn, w_o,
           w_ffn_in, w_ffn_out):
    batch, seq, _ = x_prompt.shape
    dec_batch, dec_seq, _ = x_sample.shape
    depth = w_in.shape[0]

    wts = _prepare_weights(g_norm_mix, g_norm_ffn, w_in, g_q_a, w_q_b, g_kv_a, w_kv_b, g_qn, g_qr,
                           g_kn, g_kr)
    conds = jnp.concatenate([c_ctx[None], c, jnp.zeros((8 - 1 - dec_batch, D_MODEL), F32)], axis=0)
    mod, dmask, dvec = _modulation(conds, w_mod, b_mod,
                                   jnp.stack([ret_p_fwd, ret_p_bwd], axis=1))
    gn_g, gn_b = g_ret_gn, b_ret_gn

    rope_tabs = _rope_tables(dec_seq)
    kvn_cache = _cache_up(cache_ckv, wts["w_kvb"], wts["g_kn"])
    krp_cache = jnp.pad(cache_krope, ((0, 0), (0, 0), (0, 0), (MLA_NOPE, LANES - MLA_NOPE - MLA_ROPE)))
    krp_cache = krp_cache.astype(BF16)
    s_f0 = _blockdiag_states(state_ret_fwd)
    s_b0 = _blockdiag_states(state_ret_bwd)

    def ctx_row(row):
        return 0

    def lat_row(row):
        return 1 + row // dec_seq

    x = x_prompt.reshape(batch * seq, D_MODEL)
    y = x_sample.reshape(dec_batch * dec_seq, D_MODEL)
    caches = (jax.ShapeDtypeStruct((batch, depth, seq, KV_RANK), F32),
              jax.ShapeDtypeStruct((batch, depth, MLA_ROPE, seq), F32))
    states = (jax.ShapeDtypeStruct((batch, depth, RET_HEADS, RET_DK, RET_DK), F32),) * 2
    for l in range(depth):
        pr = _proj(x, mod, l, ctx_row, wts, None, caches)
        caches = tuple(pr[7:9])
        casts = [(w_o, l, D_MODEL, 1), (w_ffn_in, l, 2 * D_FF, 1), (w_ffn_out, l, D_MODEL, 2)]
        mixed = _mix(pr, dmask, dvec, gn_g, gn_b, l, batch, seq, None, states, casts)
        ret, attn = mixed[:2]
        states = tuple(mixed[2:4])
        big_w = tuple(mixed[4:7])
        x = _out(x, ret, attn, mod, l, ctx_row, wts, big_w, OUT_TILE)

        pr = _proj(y, mod, l, lat_row, wts, rope_tabs, None)
        ret, attn = _mix(pr, dmask, dvec, gn_g, gn_b, l, dec_batch, dec_seq,
                         (kvn_cache, krp_cache, s_f0, s_b0), None)
        y = _out(y, ret, attn, mod, l, lat_row, wts, big_w, LATENT_OUT_TILE)

    return (x.reshape(batch, seq, D_MODEL), y.reshape(dec_batch, dec_seq, D_MODEL),
            caches[0], jnp.swapaxes(caches[1], 2, 3), states[0], states[1])
```

```python
import functools

import jax
import jax.numpy as jnp
import numpy as np
from jax import lax
from jax.experimental import pallas as pl
from jax.experimental.pallas import tpu as pltpu

D_MODEL = 1024
N_MOD = 6
RET_HEADS = 8
RET_DK = 64
RET_WIDTH = 512
MLA_HEADS = 8
MLA_NOPE = 64
MLA_ROPE = 32
MLA_V = 64
Q_RANK = 256
KV_RANK = 128
D_FF = 2816
GRID_W = 64
ROPE_BASE = 10000.0
EPS = 1e-6

LANES = 128
HEAD_PAD = LANES
N_PAIRS = RET_HEADS // 2
IN_COLS_PAD = 4 * RET_WIDTH + Q_RANK + KV_RANK + LANES
OUT_TILE = 1024
LATENT_OUT_TILE = 512
PROJ_TILE = 1024
LATENT_PROJ_TILE = 512
SUB_TILE = 256
FF_CHUNK = 256
MIX_CHUNK = 256
MIX_SEQS = 4
Q_DEC_F, K_DEC_F, Q_DEC_B, K_DEC_B, C_DEC_F, C_DEC_B = range(6)
N_DVEC = 6
K_SCALE = RET_DK ** -0.5
Q_FOLD = (MLA_NOPE + MLA_ROPE) ** -0.5 * 1.4426950408889634
VMEM_LIMIT = 56 * 1024 * 1024

BF16 = jnp.bfloat16
F32 = jnp.float32
_NT = (((1,), (1,)), ((), ()))


def _dot(a, b):
    return jnp.dot(a, b, preferred_element_type=F32)


def _dot_nt(a, b):
    return lax.dot_general(a, b, _NT, preferred_element_type=F32)


def _rms(x, g):
    return x * lax.rsqrt(jnp.mean(x * x, axis=-1, keepdims=True) + EPS) * g


def _silu(x):
    return x * jax.nn.sigmoid(x)


def _masked_mean_sq(x, mask, n):
    return jnp.sum(jnp.where(mask, x * x, 0.0), axis=-1, keepdims=True) * (1.0 / n)


def _mod_kernel(c_ref, w_ref, b_ref, o_ref):
    a = _silu(c_ref[...])
    w = w_ref[...]
    a_hi = a.astype(BF16)
    a_lo = (a - a_hi.astype(F32)).astype(BF16)
    w_hi = w.astype(BF16)
    w_lo = (w - w_hi.astype(F32)).astype(BF16)
    both = _dot(jnp.concatenate([a_hi, a_lo], axis=0), w_hi)
    part = both[0:8] + both[8:16] + _dot(a_hi, w_lo)

    @pl.when(pl.program_id(1) == 0)
    def _():
        o_ref[...] = part + b_ref[pl.ds(pl.program_id(0), 1), :]

    @pl.when(pl.program_id(1) > 0)
    def _():
        o_ref[...] += part


def _modulation(conds, w_mod, b_mod):
    depth, d, n = w_mod.shape
    tk = 256
    return pl.pallas_call(
        _mod_kernel,
        grid=(depth, d // tk),
        in_specs=[
            pl.BlockSpec((8, tk), lambda l, k: (0, k)),
            pl.BlockSpec((None, tk, n), lambda l, k: (l, k, 0)),
            pl.BlockSpec((depth, n), lambda l, k: (0, 0)),
        ],
        out_specs=pl.BlockSpec((None, 8, n), lambda l, k: (l, 0, 0)),
        out_shape=jax.ShapeDtypeStruct((depth, 8, n), F32),
        compiler_params=pltpu.CompilerParams(
            dimension_semantics=("arbitrary", "arbitrary"), vmem_limit_bytes=VMEM_LIMIT),
        name="modulation",
    )(conds, w_mod, b_mod)


def _norm_kn(kv, gkn):
    lane = lax.broadcasted_iota(jnp.int32, (kv.shape[0], LANES), 1)
    lo = lane < MLA_NOPE
    out = []
    for h in range(MLA_HEADS):
        kvh = kv[:, h * HEAD_PAD:(h + 1) * HEAD_PAD]
        rs = lax.rsqrt(_masked_mean_sq(kvh, lo, MLA_NOPE) + EPS)
        out.append((kvh * jnp.where(lo, rs * gkn, 1.0)).astype(BF16))
    return out


def _cache_kernel(ckv_ref, wkvb_ref, gkn_ref, kvn_ref):
    kv = _dot(ckv_ref[...].astype(BF16), wkvb_ref[...])
    gkn = gkn_ref[pl.ds(pl.program_id(0), 1), :]
    for h, kvh in enumerate(_norm_kn(kv, gkn)):
        kvn_ref[:, h * HEAD_PAD:(h + 1) * HEAD_PAD] = kvh


def _cache_up(cache_ckv, wkvb, gkn):
    nb, depth, past, _ = cache_ckv.shape
    width = MLA_HEADS * HEAD_PAD
    return pl.pallas_call(
        _cache_kernel,
        grid=(depth, nb),
        in_specs=[
            pl.BlockSpec((None, None, past, KV_RANK), lambda l, b: (b, l, 0, 0)),
            pl.BlockSpec((None, KV_RANK, width), lambda l, b: (l, 0, 0)),
            pl.BlockSpec((depth, LANES), lambda l, b: (0, 0)),
        ],
        out_specs=pl.BlockSpec((None, None, past, width), lambda l, b: (l, b, 0, 0)),
        out_shape=jax.ShapeDtypeStruct((depth, nb, past, width), BF16),
        compiler_params=pltpu.CompilerParams(
            dimension_semantics=("arbitrary", "arbitrary"), vmem_limit_bytes=VMEM_LIMIT),
        name="cache_up",
    )(cache_ckv, wkvb, gkn)


def _mod_vec(mod_ref, cond, k):
    return mod_ref[pl.ds(cond, 1), k * D_MODEL:(k + 1) * D_MODEL]


def _gain_spec(g):
    return pl.BlockSpec(g.shape, lambda *_: (0, 0))


def _is_array(x):
    return not isinstance(x, jax.ShapeDtypeStruct)


def _layer_spec(n, shape, layer, aliased):
    rest = tuple(shape[2:])
    zeros = (0,) * len(rest)
    if aliased:
        return pl.BlockSpec((n, None) + rest, lambda i: (i, layer) + zeros)
    return pl.BlockSpec((n, shape[1]) + rest, lambda i: (i, 0) + zeros)


def _put_layer(ref, i, tail, layer, fresh, value):
    if not fresh:
        ref[(i,) + tail] = value
        return
    for l in range(ref.shape[1]):
        ref[(i, l) + tail] = value if l == layer else jnp.zeros_like(value)


def _proj_kernel(*refs, rope, cond_row, n_alias, layer):
    (x_ref, mod_ref, gmix_ref, win_ref, gqa_ref, wqb_ref, gkva_ref, wkvb_ref,
     gq_ref, gkn_ref, gkr_ref) = refs[:11]
    if rope:
        wqb_sw_ref, gq_sw_ref, c64_ref, s64_ref, c32_ref, s32_ref = refs[11:17]
        rq_ref, rk_ref, rv_ref, rg_ref, qcat_ref, kvn_ref, krp_ref = refs[17:]
    else:
        (rq_ref, rk_ref, rv_ref, rg_ref, qcat_ref, kvn_ref, krp_ref, ckv_ref,
         kro_ref) = refs[11 + n_alias:]

    cond = cond_row(pl.program_id(0) * x_ref.shape[0])
    sh1 = _mod_vec(mod_ref, cond, 0)
    sc1 = _mod_vec(mod_ref, cond, 1)
    lrow = slice(layer, layer + 1)
    g_mod = gmix_ref[lrow, :] * (1.0 + sc1)
    w = RET_WIDTH
    lane = lax.broadcasted_iota(jnp.int32, (SUB_TILE, LANES), 1)
    nope = lane < MLA_NOPE
    is_rope = jnp.logical_and(lane >= MLA_NOPE, lane < MLA_NOPE + MLA_ROPE)

    for r in range(x_ref.shape[0] // SUB_TILE):
        rs = slice(r * SUB_TILE, (r + 1) * SUB_TILE)
        h = (_rms(x_ref[rs, :], g_mod) + sh1).astype(BF16)
        z = _dot_nt(h, win_ref[...])
        if rope:
            grp = RET_DK // 4
            z_sw = _dot_nt(h, jnp.concatenate(
                [win_ref[(g ^ 1) * grp:((g ^ 1) + 1) * grp, :] for g in range(2 * w // grp)], 0))

        for j in range(w // LANES):
            sl = slice(j * LANES, (j + 1) * LANES)
            q = z[:, j * LANES:(j + 1) * LANES]
            k = z[:, w + j * LANES:w + (j + 1) * LANES]
            if rope:
                q_sw = z_sw[:, j * LANES:(j + 1) * LANES]
                k_sw = z_sw[:, w + j * LANES:w + (j + 1) * LANES]
                q = q * c64_ref[rs, :] + q_sw * s64_ref[rs, :]
                k = k * c64_ref[rs, :] + k_sw * s64_ref[rs, :]
            rq_ref[rs, sl] = q.astype(BF16)
            rk_ref[rs, sl] = k.astype(BF16)
        rv_ref[rs, :] = z[:, 2 * w:3 * w].astype(BF16)
        rg_ref[rs, :] = z[:, 3 * w:4 * w]

        o = 4 * w
        qa = z[:, o:o + Q_RANK]
        kva = z[:, o + Q_RANK:o + Q_RANK + KV_RANK]
        kr2 = z[:, o + Q_RANK + KV_RANK:]

        qa_n = _rms(qa, gqa_ref[lrow, :]).astype(BF16)
        q = _dot(qa_n, wqb_ref[...])
        if rope:
            q_sw = _dot(qa_n, wqb_sw_ref[...])
        for hh in range(MLA_HEADS):
            hs = slice(hh * HEAD_PAD, (hh + 1) * HEAD_PAD)
            qh = q[:, hs]
            rs_n = lax.rsqrt(_masked_mean_sq(qh, nope, MLA_NOPE) + EPS)
            rs_r = lax.rsqrt(_masked_mean_sq(qh, is_rope, MLA_ROPE) + EPS)
            qn = qh * jnp.where(nope, rs_n, rs_r) * gq_ref[lrow, :]
            if rope:
                qn = (qn * c32_ref[rs, :]
                      + q_sw[:, hs] * rs_r * gq_sw_ref[lrow, :] * s32_ref[rs, :])
            qcat_ref[rs, hs] = qn.astype(BF16)

        ckv = _rms(kva, gkva_ref[lrow, :])
        kv = _dot(ckv.astype(BF16), wkvb_ref[...])
        for hh, kvh in enumerate(_norm_kn(kv, gkn_ref[lrow, :])):
            kvn_ref[rs, hh * HEAD_PAD:(hh + 1) * HEAD_PAD] = kvh

        rs_k = lax.rsqrt(_masked_mean_sq(kr2, lane < MLA_ROPE, MLA_ROPE) + EPS)
        krn = kr2 * rs_k * gkr_ref[lrow, :]
        krp = pltpu.roll(krn, MLA_NOPE, 1)
        if rope:
            x1_pos = ((lane - MLA_NOPE) % (MLA_ROPE // 2)) < MLA_ROPE // 4
            partner = jnp.where(x1_pos, pltpu.roll(krp, LANES - MLA_ROPE // 4, 1),
                                pltpu.roll(krp, MLA_ROPE // 4, 1))
            krp = krp * c32_ref[rs, :] + partner * s32_ref[rs, :]
        else:
            seq = ckv_ref.shape[-2]
            krt = krn.T[0:MLA_ROPE, :]
            for t in range(SUB_TILE // seq):
                i = r * (SUB_TILE // seq) + t
                _put_layer(ckv_ref, i, (), layer, n_alias == 0, ckv[t * seq:(t + 1) * seq])
                _put_layer(kro_ref, i, (), layer, n_alias == 0, krt[:, t * seq:(t + 1) * seq])
        krp_ref[rs, :] = krp.astype(BF16)


def _proj(x, mod, layer, cond_row, wts, rope_tabs, caches):
    m = x.shape[0]
    rope = rope_tabs is not None
    tm = LATENT_PROJ_TILE if rope else PROJ_TILE
    width = MLA_HEADS * HEAD_PAD

    def row(i):
        return (i, 0)

    def lay3(i):
        return (layer, 0, 0)

    in_specs = [
        pl.BlockSpec((tm, D_MODEL), row),
        pl.BlockSpec((None, 8, N_MOD * D_MODEL), lay3),
        _gain_spec(wts["g_mix"]),
        pl.BlockSpec((None, IN_COLS_PAD, D_MODEL), lay3),
        _gain_spec(wts["g_qa"]),
        pl.BlockSpec((None, Q_RANK, width), lay3),
        _gain_spec(wts["g_kva"]),
        pl.BlockSpec((None, KV_RANK, width), lay3),
        _gain_spec(wts["g_q"]), _gain_spec(wts["g_kn"]), _gain_spec(wts["g_kr"]),
    ]
    args = [x, mod, wts["g_mix"], wts["w_in_t"], wts["g_qa"], wts["w_qb"], wts["g_kva"],
            wts["w_kvb"], wts["g_q"], wts["g_kn"], wts["g_kr"]]
    out_shape = [
        jax.ShapeDtypeStruct((m, RET_WIDTH), BF16),
        jax.ShapeDtypeStruct((m, RET_WIDTH), BF16),
        jax.ShapeDtypeStruct((m, RET_WIDTH), BF16),
        jax.ShapeDtypeStruct((m, RET_WIDTH), F32),
        jax.ShapeDtypeStruct((m, width), BF16),
        jax.ShapeDtypeStruct((m, width), BF16),
        jax.ShapeDtypeStruct((m, LANES), BF16),
    ]
    out_specs = [
        pl.BlockSpec((tm, RET_WIDTH), row), pl.BlockSpec((tm, RET_WIDTH), row),
        pl.BlockSpec((tm, RET_WIDTH), row), pl.BlockSpec((tm, RET_WIDTH), row),
        pl.BlockSpec((tm, width), row), pl.BlockSpec((tm, width), row),
        pl.BlockSpec((tm, LANES), row),
    ]
    aliases = {}
    if rope:
        n_lat = rope_tabs[0].shape[0]
        tiles = n_lat // tm
        in_specs += [pl.BlockSpec((None, Q_RANK, width), lay3), _gain_spec(wts["g_q_sw"])]
        args += [wts["w_qb_sw"], wts["g_q_sw"]]
        in_specs += [pl.BlockSpec((tm, LANES), lambda i: (i % tiles, 0))] * len(rope_tabs)
        args += list(rope_tabs)
    else:
        seq = caches[0].shape[2]
        assert SUB_TILE % seq == 0
        nb = tm // seq
        if _is_array(caches[0]):
            in_specs += [pl.BlockSpec(memory_space=pl.ANY)] * 2
            aliases = {len(args): len(out_shape), len(args) + 1: len(out_shape) + 1}
            args += list(caches)
        out_shape += [jax.ShapeDtypeStruct(c.shape, c.dtype) for c in caches]
        out_specs += [_layer_spec(nb, c.shape, layer, bool(aliases)) for c in caches]

    return pl.pallas_call(
        functools.partial(_proj_kernel, rope=rope, cond_row=cond_row, n_alias=len(aliases),
                          layer=layer),
        grid=(m // tm,),
        in_specs=in_specs,
        out_specs=out_specs,
        out_shape=out_shape,
        input_output_aliases=aliases,
        compiler_params=pltpu.CompilerParams(
            dimension_semantics=("arbitrary",), vmem_limit_bytes=VMEM_LIMIT),
        name="proj_latent" if rope else "proj_context",
    )(*args)


def _log_gamma(p):
    return jnp.log1p(-jnp.exp2(-p))


def _decay_kernel(p_ref, dmask_ref, dvec_ref):
    c = MIX_CHUNK
    base = pl.program_id(0) * (2 * RET_HEADS)
    pair = pl.program_id(1)
    ri = lax.broadcasted_iota(jnp.int32, (c, c), 0)
    ci = lax.broadcasted_iota(jnp.int32, (c, c), 1)
    dif = (ri - ci).astype(F32)
    for e in range(2):
        h = 2 * pair + e
        lg_f = _log_gamma(jnp.full((c, c), p_ref[base + h], F32))
        lg_b = _log_gamma(jnp.full((c, c), p_ref[base + RET_HEADS + h], F32))
        fwd = jnp.where(dif >= 0, jnp.exp(jnp.maximum(dif, 0.0) * lg_f), 0.0)
        bwd = jnp.where(dif <= 0, jnp.exp(jnp.maximum(-dif, 0.0) * lg_b), 0.0)
        dmask_ref[e] = (fwd + bwd) * K_SCALE
    lane = lax.broadcasted_iota(jnp.int32, (c, LANES), 1)
    rowf = lax.broadcasted_iota(jnp.int32, (c, LANES), 0).astype(F32)
    lo = lane < RET_DK
    lg_f = _log_gamma(jnp.where(lo, p_ref[base + 2 * pair], p_ref[base + 2 * pair + 1]))
    lg_b = _log_gamma(jnp.where(lo, p_ref[base + RET_HEADS + 2 * pair],
                                p_ref[base + RET_HEADS + 2 * pair + 1]))
    dvec_ref[Q_DEC_F] = jnp.exp((rowf + 1.0) * lg_f)
    dvec_ref[K_DEC_F] = jnp.exp((c - 1.0 - rowf) * lg_f) * K_SCALE
    dvec_ref[Q_DEC_B] = jnp.exp((c - rowf) * lg_b)
    dvec_ref[K_DEC_B] = jnp.exp(rowf * lg_b) * K_SCALE
    dvec_ref[C_DEC_F] = jnp.exp(c * lg_f)
    dvec_ref[C_DEC_B] = jnp.exp(c * lg_b)


def _decay_tables(decay_p):
    depth = decay_p.shape[0]
    c = MIX_CHUNK
    return pl.pallas_call(
        _decay_kernel,
        grid=(depth, N_PAIRS),
        in_specs=[pl.BlockSpec(memory_space=pltpu.SMEM)],
        out_specs=[
            pl.BlockSpec((None, 2, c, c), lambda l, p: (l, p, 0, 0)),
            pl.BlockSpec((None, None, N_DVEC, c, LANES), lambda l, p: (l, p, 0, 0, 0)),
        ],
        out_shape=[
            jax.ShapeDtypeStruct((depth, RET_HEADS, c, c), F32),
            jax.ShapeDtypeStruct((depth, N_PAIRS, N_DVEC, c, LANES), F32),
        ],
        compiler_params=pltpu.CompilerParams(
            dimension_semantics=("arbitrary", "arbitrary"), vmem_limit_bytes=VMEM_LIMIT),
        name="decay_tables",
    )(decay_p.reshape(-1))


def _mix_kernel(*refs, n_seq, n_sub, latent, n_alias, n_cast, layer):
    c = MIX_CHUNK
    nc = n_seq // c
    (rq_ref, rk_ref, rv_ref, rg_ref, qcat_ref, kvn_ref, krp_ref, gng_ref, gnb_ref,
     dmask_ref, dvec_ref) = refs[:11]
    if latent:
        kvc_ref, krc_ref, sf0_ref, sb0_ref, ret_ref, attn_ref, st_scr = refs[11:]
    else:
        n_in = 11 + n_alias
        cast_src = refs[n_in:n_in + n_cast]
        ret_ref, attn_ref, sf_ref, sb_ref = refs[n_in + n_cast:n_in + n_cast + 4]
        cast_dst = refs[n_in + n_cast + 4:]
        for src, dst in zip(cast_src, cast_dst):
            w = src.shape[1]
            dst[:, 0:w] = src[...].astype(BF16)
            if dst.shape[1] > w:
                dst[:, w:] = jnp.zeros((dst.shape[0], dst.shape[1] - w), BF16)

    lane = lax.broadcasted_iota(jnp.int32, (c, LANES), 1)
    lo = lane < RET_DK
    lrow = slice(layer, layer + 1)
    sq_r = lax.broadcasted_iota(jnp.int32, (LANES, LANES), 0)
    sq_c = lax.broadcasted_iota(jnp.int32, (LANES, LANES), 1)
    blockdiag = (sq_r < RET_DK) == (sq_c < RET_DK)

    def cols(j):
        return slice(j * LANES, (j + 1) * LANES)

    def state_update(pair, d, rows):
        kp = rk_ref[rows, cols(pair)]
        vp = rv_ref[rows, cols(pair)]
        kdt = (kp.astype(F32) * dvec_ref[pair, K_DEC_B if d else K_DEC_F]).T.astype(BF16)
        return jnp.where(blockdiag, _dot(kdt, vp), 0.0)

    nope_n = lax.broadcasted_iota(jnp.int32, (n_seq, LANES), 1) < MLA_NOPE
    if latent:
        nope_c = lax.broadcasted_iota(jnp.int32, (kvc_ref.shape[0], LANES), 1) < MLA_NOPE

    def states(sub):
        base = sub * n_seq
        for pair in range(N_PAIRS):
            for d in range(2):
                if latent:
                    s = (sb0_ref if d else sf0_ref)[pair]
                    cdec = dvec_ref[pair, C_DEC_B if d else C_DEC_F][0:LANES, :]
                    order = list(range(nc - 1, -1, -1)) if d else list(range(nc))
                    for idx, ch in enumerate(order):
                        st_scr[d, pair, ch] = s.astype(BF16)
                        if idx < nc - 1:
                            s = s * cdec + state_update(pair, d, pl.ds(base + ch * c, c))
                else:
                    s = state_update(pair, d, pl.ds(base, c))
                    st_ref = sb_ref if d else sf_ref
                    fresh = n_alias == 0
                    _put_layer(st_ref, sub, (2 * pair,), layer, fresh, s[0:RET_DK, 0:RET_DK])
                    _put_layer(st_ref, sub, (2 * pair + 1,), layer, fresh, s[RET_DK:, RET_DK:])

    def chunk_body(ch, base):
        rows = pl.ds(pl.multiple_of(base + ch * c, c), c)
        keys = pl.ds(base, n_seq)

        for pair in range(N_PAIRS):
            qp = rq_ref[rows, cols(pair)]
            kp = rk_ref[rows, cols(pair)]
            vp = rv_ref[rows, cols(pair)]
            zero = jnp.zeros_like(qp)
            a0 = (_dot_nt(jnp.where(lo, qp, zero), kp) * dmask_ref[2 * pair]).astype(BF16)
            a1 = (_dot_nt(jnp.where(lo, zero, qp), kp) * dmask_ref[2 * pair + 1]).astype(BF16)
            tot = jnp.where(lo, _dot(a0, vp), _dot(a1, vp))
            if latent:
                tot = (tot + _dot(qp, st_scr[0, pair, ch]) * dvec_ref[pair, Q_DEC_F]
                       + _dot(qp, st_scr[1, pair, ch]) * dvec_ref[pair, Q_DEC_B])
            inv = 1.0 / RET_DK
            m0 = jnp.sum(jnp.where(lo, tot, 0.0), axis=-1, keepdims=True) * inv
            m1 = jnp.sum(jnp.where(lo, 0.0, tot), axis=-1, keepdims=True) * inv
            y = tot - jnp.where(lo, m0, m1)
            v0 = jnp.sum(jnp.where(lo, y * y, 0.0), axis=-1, keepdims=True) * inv
            v1 = jnp.sum(jnp.where(lo, 0.0, y * y), axis=-1, keepdims=True) * inv
            yn = (y * lax.rsqrt(jnp.where(lo, v0, v1) + EPS) * gng_ref[lrow, cols(pair)]
                  + gnb_ref[lrow, cols(pair)])
            ret_ref[rows, cols(pair)] = (yn * _silu(rg_ref[rows, cols(pair)])).astype(BF16)

        krp = krp_ref[keys, :]
        for pair in range(N_PAIRS):
            outs = []
            for e in range(2):
                h = 2 * pair + e
                qc = qcat_ref[rows, cols(h)]
                kv = kvn_ref[keys, cols(h)]
                s = _dot_nt(qc, jnp.where(nope_n, kv, krp))
                m = jnp.max(s, axis=-1, keepdims=True)
                if latent:
                    kv_c = kvc_ref[:, cols(h)]
                    s2 = _dot_nt(qc, jnp.where(nope_c, kv_c, krc_ref[...]))
                    m = jnp.maximum(m, jnp.max(s2, axis=-1, keepdims=True))
                    p2 = jnp.exp2(s2 - m)
                p = jnp.exp2(s - m)
                den = jnp.sum(p, axis=-1, keepdims=True)
                acc = _dot(p.astype(BF16), kv)
                if latent:
                    den = den + jnp.sum(p2, axis=-1, keepdims=True)
                    acc = acc + _dot(p2.astype(BF16), kv_c)
                outs.append(acc / den)
            attn_ref[rows, cols(pair)] = jnp.where(
                lo, pltpu.roll(outs[0], MLA_V, 1), outs[1]).astype(BF16)
        return base

    for sub in range(n_sub):
        states(sub)
    for sub in range(n_sub):
        if nc == 1:
            chunk_body(0, sub * n_seq)
        else:
            lax.fori_loop(0, nc, chunk_body, sub * n_seq)


def _mix(proj_out, dmask, dvec, gn_g, gn_b, layer, n_batch, n_seq, latent_in, states_out,
         casts=()):
    rq, rk, rv, rg, qcat, kvn, krp = proj_out[:7]
    latent = latent_in is not None
    aliases = {}
    m = n_batch * n_seq
    c = MIX_CHUNK
    width = MLA_HEADS * HEAD_PAD

    def row(b):
        return (b, 0)

    def lay3(b):
        return (layer, 0, 0)

    once = pl.Buffered(1)
    n_sub = 1 if latent else MIX_SEQS
    n_steps = n_batch // n_sub
    blk = n_sub * n_seq
    in_specs = [
        pl.BlockSpec((blk, RET_WIDTH), row), pl.BlockSpec((blk, RET_WIDTH), row),
        pl.BlockSpec((blk, RET_WIDTH), row), pl.BlockSpec((blk, RET_WIDTH), row),
        pl.BlockSpec((blk, width), row), pl.BlockSpec((blk, width), row),
        pl.BlockSpec((blk, LANES), row),
        _gain_spec(gn_g), _gain_spec(gn_b),
        pl.BlockSpec((None, RET_HEADS, c, c), lambda b: (layer, 0, 0, 0), pipeline_mode=once),
        pl.BlockSpec((None, N_PAIRS, N_DVEC, c, LANES), lambda b: (layer, 0, 0, 0, 0),
                     pipeline_mode=once),
    ]
    args = [rq, rk, rv, rg, qcat, kvn, krp, gn_g, gn_b, dmask, dvec]
    out_shape = [jax.ShapeDtypeStruct((m, RET_WIDTH), BF16),
                 jax.ShapeDtypeStruct((m, MLA_HEADS * MLA_V), BF16)]
    out_specs = [pl.BlockSpec((blk, RET_WIDTH), row), pl.BlockSpec((blk, RET_WIDTH), row)]
    scratch = []
    if latent:
        kvn_c, krp_c, s_f0, s_b0 = latent_in
        past = kvn_c.shape[2]
        st_spec = pl.BlockSpec((None, None, N_PAIRS, LANES, LANES), lambda b: (b, layer, 0, 0, 0))
        in_specs += [
            pl.BlockSpec((None, None, past, width), lambda b: (layer, b, 0, 0)),
            pl.BlockSpec((None, None, past, LANES), lambda b: (b, layer, 0, 0)),
            st_spec, st_spec,
        ]
        args += [kvn_c, krp_c, s_f0, s_b0]
        scratch = [pltpu.VMEM((2, N_PAIRS, n_seq // c, LANES, LANES), BF16)]
    else:
        if _is_array(states_out[0]):
            in_specs += [pl.BlockSpec(memory_space=pl.ANY)] * 2
            aliases = {len(args): len(out_shape), len(args) + 1: len(out_shape) + 1}
            args += list(states_out)
        out_shape += [jax.ShapeDtypeStruct(s.shape, s.dtype) for s in states_out]
        out_specs += [_layer_spec(n_sub, s.shape, layer, bool(aliases)) for s in states_out]
        for w, w_layer, out_cols, stride in casts:
            blocks = n_steps // stride
            rows = w.shape[1] // blocks
            assert rows * blocks == w.shape[1] and rows % 16 == 0 and stride * blocks == n_steps
            in_specs.append(pl.BlockSpec(
                (None, rows, w.shape[2]),
                lambda b, w_layer=w_layer, stride=stride: (w_layer, b // stride, 0)))
            args.append(w)
            out_shape.append(jax.ShapeDtypeStruct((w.shape[1], out_cols), BF16))
            out_specs.append(pl.BlockSpec((rows, out_cols),
                                          lambda b, stride=stride: (b // stride, 0)))

    return pl.pallas_call(
        functools.partial(_mix_kernel, n_seq=n_seq, n_sub=n_sub, latent=latent, layer=layer,
                          n_alias=len(aliases), n_cast=len(casts)),
        grid=(n_steps,),
        in_specs=in_specs,
        out_specs=out_specs,
        out_shape=out_shape,
        input_output_aliases=aliases,
        scratch_shapes=scratch,
        compiler_params=pltpu.CompilerParams(
            dimension_semantics=("arbitrary",), vmem_limit_bytes=VMEM_LIMIT),
        name="mix_latent" if latent else "mix_context",
    )(*args)


def _out_kernel(x_ref, ret_ref, attn_ref, mod_ref, gffn_ref, wo_hbm, wfi_hbm, wfo_hbm, o_ref,
                wo_ref, wfi_ref, wfo_ref, act_scr, sems, *, cond_row, layer):
    first = pl.program_id(0) == 0
    loads = [pltpu.make_async_copy(src, dst, sems.at[k]) for k, (src, dst) in
             enumerate(((wo_hbm, wo_ref), (wfi_hbm, wfi_ref), (wfo_hbm, wfo_ref)))]

    def on_first_step(action):
        @pl.when(first)
        def _():
            action()

    on_first_step(lambda: [load.start() for load in loads])

    cond = cond_row(pl.program_id(0) * x_ref.shape[0])
    gt1 = _mod_vec(mod_ref, cond, 2)
    sh2 = _mod_vec(mod_ref, cond, 3)
    sc2 = _mod_vec(mod_ref, cond, 4)
    gt2 = _mod_vec(mod_ref, cond, 5)
    on_first_step(loads[0].wait)
    mixed = _dot(ret_ref[...], wo_ref[0:RET_WIDTH, :]) + _dot(attn_ref[...], wo_ref[RET_WIDTH:, :])
    x1 = x_ref[...] + gt1 * mixed
    h = (_rms(x1, gffn_ref[layer:layer + 1, :] * (1.0 + sc2)) + sh2).astype(BF16)
    on_first_step(loads[1].wait)
    for c0 in range(0, D_FF, FF_CHUNK):
        cw = min(FF_CHUNK, D_FF - c0)
        gate = _dot(h, wfi_ref[:, c0:c0 + cw])
        up = _dot(h, wfi_ref[:, D_FF + c0:D_FF + c0 + cw])
        act_scr[:, c0:c0 + cw] = (_silu(gate) * up).astype(BF16)
    on_first_step(loads[2].wait)
    o_ref[...] = x1 + gt2 * _dot(act_scr[...], wfo_ref[...])


def _out(x, ret, attn, mod, layer, cond_row, wts, big_w, tm):
    m = x.shape[0]

    def row(i):
        return (i, 0)

    def lay3(i):
        return (layer, 0, 0)

    in_hbm = pl.BlockSpec(memory_space=pl.ANY)
    return pl.pallas_call(
        functools.partial(_out_kernel, cond_row=cond_row, layer=layer),
        grid=(m // tm,),
        in_specs=[
            pl.BlockSpec((tm, D_MODEL), row),
            pl.BlockSpec((tm, RET_WIDTH), row),
            pl.BlockSpec((tm, RET_WIDTH), row),
            pl.BlockSpec((None, 8, N_MOD * D_MODEL), lay3),
            _gain_spec(wts["g_ffn"]),
            in_hbm, in_hbm, in_hbm,
        ],
        out_specs=pl.BlockSpec((tm, D_MODEL), row),
        out_shape=jax.ShapeDtypeStruct((m, D_MODEL), F32),
        scratch_shapes=[pltpu.VMEM(w.shape, BF16) for w in big_w]
        + [pltpu.VMEM((tm, D_FF), BF16), pltpu.SemaphoreType.DMA((len(big_w),))],
        compiler_params=pltpu.CompilerParams(
            dimension_semantics=("arbitrary",), vmem_limit_bytes=VMEM_LIMIT),
        name="out_ffn",
    )(x, ret, attn, mod, wts["g_ffn"], *big_w)


def _rope_tables(n_lat):
    pos = np.arange(n_lat)
    row = (pos // GRID_W).astype(np.float32)[:, None]
    col = (pos % GRID_W).astype(np.float32)[:, None]
    lane = np.arange(LANES)[None, :]

    def tables(d, start, period):
        rel = (lane - start) % period
        active = np.logical_and(lane >= start, rel < d)
        half = d // 2
        nf = half // 2
        inv = np.float32(ROPE_BASE) ** (-((rel % nf).astype(np.float32)) / np.float32(nf))
        ang = (np.where(rel < half, row, col) * inv).astype(np.float32)
        cos, sin = np.cos(ang), np.sin(ang)
        first = (rel % half) < nf
        c = np.where(active, cos, 1.0)
        s = np.where(active, np.where(first, -sin, sin), 0.0)
        return tuple(jnp.asarray(t, F32) for t in (c, s))

    return tables(RET_DK, 0, RET_DK) + tables(MLA_ROPE, MLA_NOPE, LANES)


def _swap_partners(a, axis, d):
    shape = a.shape
    split = shape[:axis] + (shape[axis] // d, 2, 2, d // 4) + shape[axis + 1:]
    return jnp.flip(a.reshape(split), axis=axis + 2).reshape(shape)


def _prepare_weights(g_norm_mix, g_norm_ffn, w_in, g_q_a, w_q_b, g_kv_a, w_kv_b, g_qn, g_qr, g_kn,
                     g_kr):
    depth = w_in.shape[0]
    n_in = w_in.shape[2]
    w_in_t = jnp.pad(jnp.swapaxes(w_in, 1, 2).astype(BF16),
                     ((0, 0), (0, IN_COLS_PAD - n_in), (0, 0)))
    w_qb = w_q_b.reshape(depth, Q_RANK, MLA_HEADS, MLA_NOPE + MLA_ROPE)
    w_qb_sw = jnp.pad(_swap_partners(w_qb[..., MLA_NOPE:], 3, MLA_ROPE),
                      ((0, 0), (0, 0), (0, 0), (MLA_NOPE, HEAD_PAD - MLA_NOPE - MLA_ROPE)))
    w_qb = jnp.pad(w_qb, ((0, 0), (0, 0), (0, 0), (0, HEAD_PAD - MLA_NOPE - MLA_ROPE)))
    zeros32 = jnp.zeros((depth, MLA_ROPE), F32)
    zeros64 = jnp.zeros((depth, MLA_NOPE), F32)
    return {
        "g_mix": g_norm_mix,
        "g_ffn": g_norm_ffn,
        "w_in_t": w_in_t,
        "g_qa": g_q_a,
        "w_qb": w_qb.reshape(depth, Q_RANK, MLA_HEADS * HEAD_PAD).astype(BF16),
        "g_kva": g_kv_a,
        "w_kvb": w_kv_b.astype(BF16),
        "g_q": jnp.concatenate([g_qn, g_qr, zeros32], axis=-1) * Q_FOLD,
        "g_kn": jnp.concatenate([g_kn, jnp.ones((depth, MLA_V), F32)], axis=-1),
        "g_kr": jnp.pad(g_kr, ((0, 0), (0, LANES - MLA_ROPE))),
        "w_qb_sw": w_qb_sw.reshape(depth, Q_RANK, MLA_HEADS * HEAD_PAD).astype(BF16),
        "g_q_sw": jnp.concatenate([zeros64, _swap_partners(g_qr, 1, MLA_ROPE), zeros32],
                                  axis=-1) * Q_FOLD,
    }


def _blockdiag_states(s):
    b, l = s.shape[:2]
    s = s.reshape(b, l, N_PAIRS, 2, RET_DK, RET_DK)
    z = jnp.zeros_like(s[:, :, :, 0])
    top = jnp.concatenate([s[:, :, :, 0], z], axis=-1)
    bot = jnp.concatenate([z, s[:, :, :, 1]], axis=-1)
    return jnp.concatenate([top, bot], axis=-2)


def kernel(x_prompt, x_sample, cache_ckv, cache_krope, state_ret_fwd, state_ret_bwd, c, c_ctx,
           w_mod, b_mod, g_norm_mix, g_norm_ffn, w_in, g_q_a, w_q_b, g_kv_a, w_kv_b,
           g_qn, g_qr, g_kn, g_kr, ret_p_fwd, ret_p_bwd, g_ret_gn, b_ret_gn, w_o,
           w_ffn_in, w_ffn_out):
    batch, seq, _ = x_prompt.shape
    dec_batch, dec_seq, _ = x_sample.shape
    depth = w_in.shape[0]

    wts = _prepare_weights(g_norm_mix, g_norm_ffn, w_in, g_q_a, w_q_b, g_kv_a, w_kv_b, g_qn, g_qr,
                           g_kn, g_kr)
    conds = jnp.concatenate([c_ctx[None], c, jnp.zeros((8 - 1 - dec_batch, D_MODEL), F32)], axis=0)
    mod = _modulation(conds, w_mod, b_mod)
    dmask, dvec = _decay_tables(jnp.stack([ret_p_fwd, ret_p_bwd], axis=1))
    gn_g, gn_b = g_ret_gn, b_ret_gn

    rope_tabs = _rope_tables(dec_seq)
    kvn_cache = _cache_up(cache_ckv, wts["w_kvb"], wts["g_kn"])
    krp_cache = jnp.pad(cache_krope, ((0, 0), (0, 0), (0, 0), (MLA_NOPE, LANES - MLA_NOPE - MLA_ROPE)))
    krp_cache = krp_cache.astype(BF16)
    s_f0 = _blockdiag_states(state_ret_fwd)
    s_b0 = _blockdiag_states(state_ret_bwd)

    def ctx_row(row):
        return 0

    def lat_row(row):
        return 1 + row // dec_seq

    x = x_prompt.reshape(batch * seq, D_MODEL)
    y = x_sample.reshape(dec_batch * dec_seq, D_MODEL)
    caches = (jax.ShapeDtypeStruct((batch, depth, seq, KV_RANK), F32),
              jax.ShapeDtypeStruct((batch, depth, MLA_ROPE, seq), F32))
    states = (jax.ShapeDtypeStruct((batch, depth, RET_HEADS, RET_DK, RET_DK), F32),) * 2
    for l in range(depth):
        pr = _proj(x, mod, l, ctx_row, wts, None, caches)
        caches = tuple(pr[7:9])
        casts = [(w_o, l, D_MODEL, 1), (w_ffn_in, l, 2 * D_FF, 1), (w_ffn_out, l, D_MODEL, 2)]
        mixed = _mix(pr, dmask, dvec, gn_g, gn_b, l, batch, seq, None, states, casts)
        ret, attn = mixed[:2]
        states = tuple(mixed[2:4])
        big_w = tuple(mixed[4:7])
        x = _out(x, ret, attn, mod, l, ctx_row, wts, big_w, OUT_TILE)

        pr = _proj(y, mod, l, lat_row, wts, rope_tabs, None)
        ret, attn = _mix(pr, dmask, dvec, gn_g, gn_b, l, dec_batch, dec_seq,
                         (kvn_cache, krp_cache, s_f0, s_b0), None)
        y = _out(y, ret, attn, mod, l, lat_row, wts, big_w, LATENT_OUT_TILE)

    return (x.reshape(batch, seq, D_MODEL), y.reshape(dec_batch, dec_seq, D_MODEL),
            caches[0], jnp.swapaxes(caches[1], 2, 3), states[0], states[1])
```

```python
import functools

import jax
import jax.numpy as jnp
import numpy as np
from jax import lax
from jax.experimental import pallas as pl
from jax.experimental.pallas import tpu as pltpu

D_MODEL = 1024
N_MOD = 6
RET_HEADS = 8
RET_DK = 64
RET_WIDTH = 512
MLA_HEADS = 8
MLA_NOPE = 64
MLA_ROPE = 32
MLA_V = 64
Q_RANK = 256
KV_RANK = 128
D_FF = 2816
GRID_W = 64
ROPE_BASE = 10000.0
EPS = 1e-6

LANES = 128
HEAD_PAD = LANES
N_PAIRS = RET_HEADS // 2
IN_COLS_PAD = 4 * RET_WIDTH + Q_RANK + KV_RANK + LANES
OUT_TILE = 1024
LATENT_OUT_TILE = 512
PROJ_TILE = 1024
LATENT_PROJ_TILE = 512
SUB_TILE = 256
FF_CHUNK = 256
MIX_CHUNK = 256
MIX_SEQS = 4
Q_DEC_F, K_DEC_F, Q_DEC_B, K_DEC_B, C_DEC_F, C_DEC_B = range(6)
N_DVEC = 6
K_SCALE = RET_DK ** -0.5
Q_FOLD = (MLA_NOPE + MLA_ROPE) ** -0.5 * 1.4426950408889634
VMEM_LIMIT = 56 * 1024 * 1024

BF16 = jnp.bfloat16
F32 = jnp.float32
_NT = (((1,), (1,)), ((), ()))


def _dot(a, b):
    return jnp.dot(a, b, preferred_element_type=F32)


def _dot_nt(a, b):
    return lax.dot_general(a, b, _NT, preferred_element_type=F32)


def _rms(x, g):
    return x * lax.rsqrt(jnp.mean(x * x, axis=-1, keepdims=True) + EPS) * g


def _silu(x):
    return x * jax.nn.sigmoid(x)


def _masked_mean_sq(x, mask, n):
    return jnp.sum(jnp.where(mask, x * x, 0.0), axis=-1, keepdims=True) * (1.0 / n)


def _mod_kernel(c_ref, w_ref, b_ref, o_ref):
    a = _silu(c_ref[...])
    w = w_ref[...]
    a_hi = a.astype(BF16)
    a_lo = (a - a_hi.astype(F32)).astype(BF16)
    w_hi = w.astype(BF16)
    w_lo = (w - w_hi.astype(F32)).astype(BF16)
    both = _dot(jnp.concatenate([a_hi, a_lo], axis=0), w_hi)
    part = both[0:8] + both[8:16] + _dot(a_hi, w_lo)

    @pl.when(pl.program_id(1) == 0)
    def _():
        o_ref[...] = part + b_ref[pl.ds(pl.program_id(0), 1), :]

    @pl.when(pl.program_id(1) > 0)
    def _():
        o_ref[...] += part


def _modulation(conds, w_mod, b_mod):
    depth, d, n = w_mod.shape
    tk = 256
    return pl.pallas_call(
        _mod_kernel,
        grid=(depth, d // tk),
        in_specs=[
            pl.BlockSpec((8, tk), lambda l, k: (0, k)),
            pl.BlockSpec((None, tk, n), lambda l, k: (l, k, 0)),
            pl.BlockSpec((depth, n), lambda l, k: (0, 0)),
        ],
        out_specs=pl.BlockSpec((None, 8, n), lambda l, k: (l, 0, 0)),
        out_shape=jax.ShapeDtypeStruct((depth, 8, n), F32),
        compiler_params=pltpu.CompilerParams(
            dimension_semantics=("arbitrary", "arbitrary"), vmem_limit_bytes=VMEM_LIMIT),
        name="modulation",
    )(conds, w_mod, b_mod)


def _norm_kn(kv, gkn):
    lane = lax.broadcasted_iota(jnp.int32, (kv.shape[0], LANES), 1)
    lo = lane < MLA_NOPE
    out = []
    for h in range(MLA_HEADS):
        kvh = kv[:, h * HEAD_PAD:(h + 1) * HEAD_PAD]
        rs = lax.rsqrt(_masked_mean_sq(kvh, lo, MLA_NOPE) + EPS)
        out.append((kvh * jnp.where(lo, rs * gkn, 1.0)).astype(BF16))
    return out


def _mod_vec(mod_ref, cond, k):
    return mod_ref[pl.ds(cond, 1), k * D_MODEL:(k + 1) * D_MODEL]


def _gain_spec(g):
    return pl.BlockSpec(g.shape, lambda *_: (0, 0))


def _is_array(x):
    return not isinstance(x, jax.ShapeDtypeStruct)


def _layer_spec(n, shape, layer, aliased):
    rest = tuple(shape[2:])
    zeros = (0,) * len(rest)
    if aliased:
        return pl.BlockSpec((n, None) + rest, lambda i: (i, layer) + zeros)
    return pl.BlockSpec((n, shape[1]) + rest, lambda i: (i, 0) + zeros)


def _put_layer(ref, i, tail, layer, fresh, value):
    if not fresh:
        ref[(i,) + tail] = value
        return
    for l in range(ref.shape[1]):
        ref[(i, l) + tail] = value if l == layer else jnp.zeros_like(value)


def _proj_kernel(*refs, rope, cond_row, n_alias, layer):
    (x_ref, mod_ref, gmix_ref, win_ref, gqa_ref, wqb_ref, gkva_ref, wkvb_ref,
     gq_ref, gkn_ref, gkr_ref) = refs[:11]
    if rope:
        wqb_sw_ref, gq_sw_ref, c64_ref, s64_ref, c32_ref, s32_ref = refs[11:17]
        rq_ref, rk_ref, rv_ref, rg_ref, qcat_ref, kvn_ref, krp_ref = refs[17:]
    else:
        (rq_ref, rk_ref, rv_ref, rg_ref, qcat_ref, kvn_ref, krp_ref, ckv_ref,
         kro_ref) = refs[11 + n_alias:]

    cond = cond_row(pl.program_id(0) * x_ref.shape[0])
    sh1 = _mod_vec(mod_ref, cond, 0)
    sc1 = _mod_vec(mod_ref, cond, 1)
    lrow = slice(layer, layer + 1)
    g_mod = gmix_ref[lrow, :] * (1.0 + sc1)
    w = RET_WIDTH
    lane = lax.broadcasted_iota(jnp.int32, (SUB_TILE, LANES), 1)
    nope = lane < MLA_NOPE
    is_rope = jnp.logical_and(lane >= MLA_NOPE, lane < MLA_NOPE + MLA_ROPE)

    for r in range(x_ref.shape[0] // SUB_TILE):
        rs = slice(r * SUB_TILE, (r + 1) * SUB_TILE)
        h = (_rms(x_ref[rs, :], g_mod) + sh1).astype(BF16)
        z = _dot_nt(h, win_ref[...])
        if rope:
            grp = RET_DK // 4
            z_sw = _dot_nt(h, jnp.concatenate(
                [win_ref[(g ^ 1) * grp:((g ^ 1) + 1) * grp, :] for g in range(2 * w // grp)], 0))

        for j in range(w // LANES):
            sl = slice(j * LANES, (j + 1) * LANES)
            q = z[:, j * LANES:(j + 1) * LANES]
            k = z[:, w + j * LANES:w + (j + 1) * LANES]
            if rope:
                q_sw = z_sw[:, j * LANES:(j + 1) * LANES]
                k_sw = z_sw[:, w + j * LANES:w + (j + 1) * LANES]
                q = q * c64_ref[rs, :] + q_sw * s64_ref[rs, :]
                k = k * c64_ref[rs, :] + k_sw * s64_ref[rs, :]
            rq_ref[rs, sl] = q.astype(BF16)
            rk_ref[rs, sl] = k.astype(BF16)
        rv_ref[rs, :] = z[:, 2 * w:3 * w].astype(BF16)
        rg_ref[rs, :] = z[:, 3 * w:4 * w]

        o = 4 * w
        qa = z[:, o:o + Q_RANK]
        kva = z[:, o + Q_RANK:o + Q_RANK + KV_RANK]
        kr2 = z[:, o + Q_RANK + KV_RANK:]

        qa_n = _rms(qa, gqa_ref[lrow, :]).astype(BF16)
        q = _dot(qa_n, wqb_ref[...])
        if rope:
            q_sw = _dot(qa_n, wqb_sw_ref[...])
        for hh in range(MLA_HEADS):
            hs = slice(hh * HEAD_PAD, (hh + 1) * HEAD_PAD)
            qh = q[:, hs]
            rs_n = lax.rsqrt(_masked_mean_sq(qh, nope, MLA_NOPE) + EPS)
            rs_r = lax.rsqrt(_masked_mean_sq(qh, is_rope, MLA_ROPE) + EPS)
            qn = qh * jnp.where(nope, rs_n, rs_r) * gq_ref[lrow, :]
            if rope:
                qn = (qn * c32_ref[rs, :]
                      + q_sw[:, hs] * rs_r * gq_sw_ref[lrow, :] * s32_ref[rs, :])
            qcat_ref[rs, hs] = qn.astype(BF16)

        ckv = _rms(kva, gkva_ref[lrow, :])
        kv = _dot(ckv.astype(BF16), wkvb_ref[...])
        for hh, kvh in enumerate(_norm_kn(kv, gkn_ref[lrow, :])):
            kvn_ref[rs, hh * HEAD_PAD:(hh + 1) * HEAD_PAD] = kvh

        rs_k = lax.rsqrt(_masked_mean_sq(kr2, lane < MLA_ROPE, MLA_ROPE) + EPS)
        krn = kr2 * rs_k * gkr_ref[lrow, :]
        krp = pltpu.roll(krn, MLA_NOPE, 1)
        if rope:
            x1_pos = ((lane - MLA_NOPE) % (MLA_ROPE // 2)) < MLA_ROPE // 4
            partner = jnp.where(x1_pos, pltpu.roll(krp, LANES - MLA_ROPE // 4, 1),
                                pltpu.roll(krp, MLA_ROPE // 4, 1))
            krp = krp * c32_ref[rs, :] + partner * s32_ref[rs, :]
        else:
            seq = ckv_ref.shape[-2]
            krt = krn.T[0:MLA_ROPE, :]
            for t in range(SUB_TILE // seq):
                i = r * (SUB_TILE // seq) + t
                _put_layer(ckv_ref, i, (), layer, n_alias == 0, ckv[t * seq:(t + 1) * seq])
                _put_layer(kro_ref, i, (), layer, n_alias == 0, krt[:, t * seq:(t + 1) * seq])
        krp_ref[rs, :] = krp.astype(BF16)


def _proj(x, mod, layer, cond_row, wts, rope_tabs, caches):
    m = x.shape[0]
    rope = rope_tabs is not None
    tm = LATENT_PROJ_TILE if rope else PROJ_TILE
    width = MLA_HEADS * HEAD_PAD

    def row(i):
        return (i, 0)

    def lay3(i):
        return (layer, 0, 0)

    in_specs = [
        pl.BlockSpec((tm, D_MODEL), row),
        pl.BlockSpec((None, 8, N_MOD * D_MODEL), lay3),
        _gain_spec(wts["g_mix"]),
        pl.BlockSpec((None, IN_COLS_PAD, D_MODEL), lay3),
        _gain_spec(wts["g_qa"]),
        pl.BlockSpec((None, Q_RANK, width), lay3),
        _gain_spec(wts["g_kva"]),
        pl.BlockSpec((None, KV_RANK, width), lay3),
        _gain_spec(wts["g_q"]), _gain_spec(wts["g_kn"]), _gain_spec(wts["g_kr"]),
    ]
    args = [x, mod, wts["g_mix"], wts["w_in_t"], wts["g_qa"], wts["w_qb"], wts["g_kva"],
            wts["w_kvb"], wts["g_q"], wts["g_kn"], wts["g_kr"]]
    out_shape = [
        jax.ShapeDtypeStruct((m, RET_WIDTH), BF16),
        jax.ShapeDtypeStruct((m, RET_WIDTH), BF16),
        jax.ShapeDtypeStruct((m, RET_WIDTH), BF16),
        jax.ShapeDtypeStruct((m, RET_WIDTH), F32),
        jax.ShapeDtypeStruct((m, width), BF16),
        jax.ShapeDtypeStruct((m, width), BF16),
        jax.ShapeDtypeStruct((m, LANES), BF16),
    ]
    out_specs = [
        pl.BlockSpec((tm, RET_WIDTH), row), pl.BlockSpec((tm, RET_WIDTH), row),
        pl.BlockSpec((tm, RET_WIDTH), row), pl.BlockSpec((tm, RET_WIDTH), row),
        pl.BlockSpec((tm, width), row), pl.BlockSpec((tm, width), row),
        pl.BlockSpec((tm, LANES), row),
    ]
    aliases = {}
    if rope:
        n_lat = rope_tabs[0].shape[0]
        tiles = n_lat // tm
        in_specs += [pl.BlockSpec((None, Q_RANK, width), lay3), _gain_spec(wts["g_q_sw"])]
        args += [wts["w_qb_sw"], wts["g_q_sw"]]
        in_specs += [pl.BlockSpec((tm, LANES), lambda i: (i % tiles, 0))] * len(rope_tabs)
        args += list(rope_tabs)
    else:
        seq = caches[0].shape[2]
        assert SUB_TILE % seq == 0
        nb = tm // seq
        if _is_array(caches[0]):
            in_specs += [pl.BlockSpec(memory_space=pl.ANY)] * 2
            aliases = {len(args): len(out_shape), len(args) + 1: len(out_shape) + 1}
            args += list(caches)
        out_shape += [jax.ShapeDtypeStruct(c.shape, c.dtype) for c in caches]
        out_specs += [_layer_spec(nb, c.shape, layer, bool(aliases)) for c in caches]

    return pl.pallas_call(
        functools.partial(_proj_kernel, rope=rope, cond_row=cond_row, n_alias=len(aliases),
                          layer=layer),
        grid=(m // tm,),
        in_specs=in_specs,
        out_specs=out_specs,
        out_shape=out_shape,
        input_output_aliases=aliases,
        compiler_params=pltpu.CompilerParams(
            dimension_semantics=("arbitrary",), vmem_limit_bytes=VMEM_LIMIT),
        name="proj_latent" if rope else "proj_context",
    )(*args)


def _log_gamma(p):
    return jnp.log1p(-jnp.exp2(-p))


def _decay_kernel(p_ref, dmask_ref, dvec_ref):
    c = MIX_CHUNK
    base = pl.program_id(0) * (2 * RET_HEADS)
    pair = pl.program_id(1)
    ri = lax.broadcasted_iota(jnp.int32, (c, c), 0)
    ci = lax.broadcasted_iota(jnp.int32, (c, c), 1)
    dif = (ri - ci).astype(F32)
    for e in range(2):
        h = 2 * pair + e
        lg_f = _log_gamma(jnp.full((c, c), p_ref[base + h], F32))
        lg_b = _log_gamma(jnp.full((c, c), p_ref[base + RET_HEADS + h], F32))
        fwd = jnp.where(dif >= 0, jnp.exp(jnp.maximum(dif, 0.0) * lg_f), 0.0)
        bwd = jnp.where(dif <= 0, jnp.exp(jnp.maximum(-dif, 0.0) * lg_b), 0.0)
        dmask_ref[e] = (fwd + bwd) * K_SCALE
    lane = lax.broadcasted_iota(jnp.int32, (c, LANES), 1)
    rowf = lax.broadcasted_iota(jnp.int32, (c, LANES), 0).astype(F32)
    lo = lane < RET_DK
    lg_f = _log_gamma(jnp.where(lo, p_ref[base + 2 * pair], p_ref[base + 2 * pair + 1]))
    lg_b = _log_gamma(jnp.where(lo, p_ref[base + RET_HEADS + 2 * pair],
                                p_ref[base + RET_HEADS + 2 * pair + 1]))
    dvec_ref[Q_DEC_F] = jnp.exp((rowf + 1.0) * lg_f)
    dvec_ref[K_DEC_F] = jnp.exp((c - 1.0 - rowf) * lg_f) * K_SCALE
    dvec_ref[Q_DEC_B] = jnp.exp((c - rowf) * lg_b)
    dvec_ref[K_DEC_B] = jnp.exp(rowf * lg_b) * K_SCALE
    dvec_ref[C_DEC_F] = jnp.exp(c * lg_f)
    dvec_ref[C_DEC_B] = jnp.exp(c * lg_b)


def _decay_tables(decay_p):
    depth = decay_p.shape[0]
    c = MIX_CHUNK
    return pl.pallas_call(
        _decay_kernel,
        grid=(depth, N_PAIRS),
        in_specs=[pl.BlockSpec(memory_space=pltpu.SMEM)],
        out_specs=[
            pl.BlockSpec((None, 2, c, c), lambda l, p: (l, p, 0, 0)),
            pl.BlockSpec((None, None, N_DVEC, c, LANES), lambda l, p: (l, p, 0, 0, 0)),
        ],
        out_shape=[
            jax.ShapeDtypeStruct((depth, RET_HEADS, c, c), F32),
            jax.ShapeDtypeStruct((depth, N_PAIRS, N_DVEC, c, LANES), F32),
        ],
        compiler_params=pltpu.CompilerParams(
            dimension_semantics=("arbitrary", "arbitrary"), vmem_limit_bytes=VMEM_LIMIT),
        name="decay_tables",
    )(decay_p.reshape(-1))


def _mix_kernel(*refs, n_seq, n_sub, latent, n_alias, n_cast, layer):
    c = MIX_CHUNK
    nc = n_seq // c
    (rq_ref, rk_ref, rv_ref, rg_ref, qcat_ref, kvn_ref, krp_ref, gng_ref, gnb_ref,
     dmask_ref, dvec_ref) = refs[:11]
    if latent:
        (ckvc_ref, wkvb_ref, gkn_ref, krc_ref, sf0_ref, sb0_ref, ret_ref, attn_ref, st_scr,
         kvc_ref) = refs[11:]
        kv_cache = _dot(ckvc_ref[...].astype(BF16), wkvb_ref[...])
        for h, kvh in enumerate(_norm_kn(kv_cache, gkn_ref[layer:layer + 1, :])):
            kvc_ref[:, h * HEAD_PAD:(h + 1) * HEAD_PAD] = kvh
    else:
        n_in = 11 + n_alias
        cast_src = refs[n_in:n_in + n_cast]
        ret_ref, attn_ref, sf_ref, sb_ref = refs[n_in + n_cast:n_in + n_cast + 4]
        cast_dst = refs[n_in + n_cast + 4:]
        for src, dst in zip(cast_src, cast_dst):
            w = src.shape[1]
            dst[:, 0:w] = src[...].astype(BF16)
            if dst.shape[1] > w:
                dst[:, w:] = jnp.zeros((dst.shape[0], dst.shape[1] - w), BF16)

    lane = lax.broadcasted_iota(jnp.int32, (c, LANES), 1)
    lo = lane < RET_DK
    lrow = slice(layer, layer + 1)
    sq_r = lax.broadcasted_iota(jnp.int32, (LANES, LANES), 0)
    sq_c = lax.broadcasted_iota(jnp.int32, (LANES, LANES), 1)
    blockdiag = (sq_r < RET_DK) == (sq_c < RET_DK)

    def cols(j):
        return slice(j * LANES, (j + 1) * LANES)

    def state_update(pair, d, rows):
        kp = rk_ref[rows, cols(pair)]
        vp = rv_ref[rows, cols(pair)]
        kdt = (kp.astype(F32) * dvec_ref[pair, K_DEC_B if d else K_DEC_F]).T.astype(BF16)
        return jnp.where(blockdiag, _dot(kdt, vp), 0.0)

    nope_n = lax.broadcasted_iota(jnp.int32, (n_seq, LANES), 1) < MLA_NOPE
    if latent:
        nope_c = lax.broadcasted_iota(jnp.int32, (kvc_ref.shape[0], LANES), 1) < MLA_NOPE

    def states(sub):
        base = sub * n_seq
        for pair in range(N_PAIRS):
            for d in range(2):
                if latent:
                    s = (sb0_ref if d else sf0_ref)[pair]
                    cdec = dvec_ref[pair, C_DEC_B if d else C_DEC_F][0:LANES, :]
                    order = list(range(nc - 1, -1, -1)) if d else list(range(nc))
                    for idx, ch in enumerate(order):
                        st_scr[d, pair, ch] = s.astype(BF16)
                        if idx < nc - 1:
                            s = s * cdec + state_update(pair, d, pl.ds(base + ch * c, c))
                else:
                    s = state_update(pair, d, pl.ds(base, c))
                    st_ref = sb_ref if d else sf_ref
                    fresh = n_alias == 0
                    _put_layer(st_ref, sub, (2 * pair,), layer, fresh, s[0:RET_DK, 0:RET_DK])
                    _put_layer(st_ref, sub, (2 * pair + 1,), layer, fresh, s[RET_DK:, RET_DK:])

    def chunk_body(ch, base):
        rows = pl.ds(pl.multiple_of(base + ch * c, c), c)
        keys = pl.ds(base, n_seq)

        for pair in range(N_PAIRS):
            qp = rq_ref[rows, cols(pair)]
            kp = rk_ref[rows, cols(pair)]
            vp = rv_ref[rows, cols(pair)]
            zero = jnp.zeros_like(qp)
            a0 = (_dot_nt(jnp.where(lo, qp, zero), kp) * dmask_ref[2 * pair]).astype(BF16)
            a1 = (_dot_nt(jnp.where(lo, zero, qp), kp) * dmask_ref[2 * pair + 1]).astype(BF16)
            tot = jnp.where(lo, _dot(a0, vp), _dot(a1, vp))
            if latent:
                tot = (tot + _dot(qp, st_scr[0, pair, ch]) * dvec_ref[pair, Q_DEC_F]
                       + _dot(qp, st_scr[1, pair, ch]) * dvec_ref[pair, Q_DEC_B])
            inv = 1.0 / RET_DK
            m0 = jnp.sum(jnp.where(lo, tot, 0.0), axis=-1, keepdims=True) * inv
            m1 = jnp.sum(jnp.where(lo, 0.0, tot), axis=-1, keepdims=True) * inv
            y = tot - jnp.where(lo, m0, m1)
            v0 = jnp.sum(jnp.where(lo, y * y, 0.0), axis=-1, keepdims=True) * inv
            v1 = jnp.sum(jnp.where(lo, 0.0, y * y), axis=-1, keepdims=True) * inv
            yn = (y * lax.rsqrt(jnp.where(lo, v0, v1) + EPS) * gng_ref[lrow, cols(pair)]
                  + gnb_ref[lrow, cols(pair)])
            ret_ref[rows, cols(pair)] = (yn * _silu(rg_ref[rows, cols(pair)])).astype(BF16)

        krp = krp_ref[keys, :]
        for pair in range(N_PAIRS):
            outs = []
            for e in range(2):
                h = 2 * pair + e
                qc = qcat_ref[rows, cols(h)]
                kv = kvn_ref[keys, cols(h)]
                s = _dot_nt(qc, jnp.where(nope_n, kv, krp))
                m = jnp.max(s, axis=-1, keepdims=True)
                if latent:
                    kv_c = kvc_ref[:, cols(h)]
                    s2 = _dot_nt(qc, jnp.where(nope_c, kv_c, krc_ref[...]))
                    m = jnp.maximum(m, jnp.max(s2, axis=-1, keepdims=True))
                    p2 = jnp.exp2(s2 - m)
                p = jnp.exp2(s - m)
                den = jnp.sum(p, axis=-1, keepdims=True)
                acc = _dot(p.astype(BF16), kv)
                if latent:
                    den = den + jnp.sum(p2, axis=-1, keepdims=True)
                    acc = acc + _dot(p2.astype(BF16), kv_c)
                outs.append(acc / den)
            attn_ref[rows, cols(pair)] = jnp.where(
                lo, pltpu.roll(outs[0], MLA_V, 1), outs[1]).astype(BF16)
        return base

    for sub in range(n_sub):
        states(sub)
    for sub in range(n_sub):
        if nc == 1:
            chunk_body(0, sub * n_seq)
        else:
            lax.fori_loop(0, nc, chunk_body, sub * n_seq)


def _mix(proj_out, dmask, dvec, gn_g, gn_b, layer, n_batch, n_seq, latent_in, states_out,
         casts=()):
    rq, rk, rv, rg, qcat, kvn, krp = proj_out[:7]
    latent = latent_in is not None
    aliases = {}
    m = n_batch * n_seq
    c = MIX_CHUNK
    width = MLA_HEADS * HEAD_PAD

    def row(b):
        return (b, 0)

    def lay3(b):
        return (layer, 0, 0)

    once = pl.Buffered(1)
    n_sub = 1 if latent else MIX_SEQS
    n_steps = n_batch // n_sub
    blk = n_sub * n_seq
    in_specs = [
        pl.BlockSpec((blk, RET_WIDTH), row), pl.BlockSpec((blk, RET_WIDTH), row),
        pl.BlockSpec((blk, RET_WIDTH), row), pl.BlockSpec((blk, RET_WIDTH), row),
        pl.BlockSpec((blk, width), row), pl.BlockSpec((blk, width), row),
        pl.BlockSpec((blk, LANES), row),
        _gain_spec(gn_g), _gain_spec(gn_b),
        pl.BlockSpec((None, RET_HEADS, c, c), lambda b: (layer, 0, 0, 0), pipeline_mode=once),
        pl.BlockSpec((None, N_PAIRS, N_DVEC, c, LANES), lambda b: (layer, 0, 0, 0, 0),
                     pipeline_mode=once),
    ]
    args = [rq, rk, rv, rg, qcat, kvn, krp, gn_g, gn_b, dmask, dvec]
    out_shape = [jax.ShapeDtypeStruct((m, RET_WIDTH), BF16),
                 jax.ShapeDtypeStruct((m, MLA_HEADS * MLA_V), BF16)]
    out_specs = [pl.BlockSpec((blk, RET_WIDTH), row), pl.BlockSpec((blk, RET_WIDTH), row)]
    scratch = []
    if latent:
        ckv_c, w_kvb, g_kn, krp_c, s_f0, s_b0 = latent_in
        past = ckv_c.shape[2]
        st_spec = pl.BlockSpec((None, None, N_PAIRS, LANES, LANES), lambda b: (b, layer, 0, 0, 0))
        in_specs += [
            pl.BlockSpec((None, None, past, KV_RANK), lambda b: (b, layer, 0, 0)),
            pl.BlockSpec((None, KV_RANK, width), lay3),
            _gain_spec(g_kn),
            pl.BlockSpec((None, None, past, LANES), lambda b: (b, layer, 0, 0)),
            st_spec, st_spec,
        ]
        args += [ckv_c, w_kvb, g_kn, krp_c, s_f0, s_b0]
        scratch = [pltpu.VMEM((2, N_PAIRS, n_seq // c, LANES, LANES), BF16),
                   pltpu.VMEM((past, width), BF16)]
    else:
        if _is_array(states_out[0]):
            in_specs += [pl.BlockSpec(memory_space=pl.ANY)] * 2
            aliases = {len(args): len(out_shape), len(args) + 1: len(out_shape) + 1}
            args += list(states_out)
        out_shape += [jax.ShapeDtypeStruct(s.shape, s.dtype) for s in states_out]
        out_specs += [_layer_spec(n_sub, s.shape, layer, bool(aliases)) for s in states_out]
        for w, w_layer, out_cols, stride in casts:
            blocks = n_steps // stride
            rows = w.shape[1] // blocks
            assert rows * blocks == w.shape[1] and rows % 16 == 0 and stride * blocks == n_steps
            in_specs.append(pl.BlockSpec(
                (None, rows, w.shape[2]),
                lambda b, w_layer=w_layer, stride=stride: (w_layer, b // stride, 0)))
            args.append(w)
            out_shape.append(jax.ShapeDtypeStruct((w.shape[1], out_cols), BF16))
            out_specs.append(pl.BlockSpec((rows, out_cols),
                                          lambda b, stride=stride: (b // stride, 0)))

    return pl.pallas_call(
        functools.partial(_mix_kernel, n_seq=n_seq, n_sub=n_sub, latent=latent, layer=layer,
                          n_alias=len(aliases), n_cast=len(casts)),
        grid=(n_steps,),
        in_specs=in_specs,
        out_specs=out_specs,
        out_shape=out_shape,
        input_output_aliases=aliases,
        scratch_shapes=scratch,
        compiler_params=pltpu.CompilerParams(
            dimension_semantics=("arbitrary",), vmem_limit_bytes=VMEM_LIMIT),
        name="mix_latent" if latent else "mix_context",
    )(*args)


def _out_kernel(x_ref, ret_ref, attn_ref, mod_ref, gffn_ref, wo_ref, wfi_ref, wfo_ref, o_ref,
                act_scr, *, cond_row, layer):
    cond = cond_row(pl.program_id(0) * x_ref.shape[0])
    gt1 = _mod_vec(mod_ref, cond, 2)
    sh2 = _mod_vec(mod_ref, cond, 3)
    sc2 = _mod_vec(mod_ref, cond, 4)
    gt2 = _mod_vec(mod_ref, cond, 5)
    mixed = _dot(ret_ref[...], wo_ref[0:RET_WIDTH, :]) + _dot(attn_ref[...], wo_ref[RET_WIDTH:, :])
    x1 = x_ref[...] + gt1 * mixed
    h = (_rms(x1, gffn_ref[layer:layer + 1, :] * (1.0 + sc2)) + sh2).astype(BF16)
    for c0 in range(0, D_FF, FF_CHUNK):
        cw = min(FF_CHUNK, D_FF - c0)
        gate = _dot(h, wfi_ref[:, c0:c0 + cw])
        up = _dot(h, wfi_ref[:, D_FF + c0:D_FF + c0 + cw])
        act_scr[:, c0:c0 + cw] = (_silu(gate) * up).astype(BF16)
    o_ref[...] = x1 + gt2 * _dot(act_scr[...], wfo_ref[...])


def _out(x, ret, attn, mod, layer, cond_row, wts, big_w, tm):
    m = x.shape[0]

    def row(i):
        return (i, 0)

    def lay3(i):
        return (layer, 0, 0)

    def whole(i):
        return (0, 0)

    once = pl.Buffered(1)
    return pl.pallas_call(
        functools.partial(_out_kernel, cond_row=cond_row, layer=layer),
        grid=(m // tm,),
        in_specs=[
            pl.BlockSpec((tm, D_MODEL), row),
            pl.BlockSpec((tm, RET_WIDTH), row),
            pl.BlockSpec((tm, RET_WIDTH), row),
            pl.BlockSpec((None, 8, N_MOD * D_MODEL), lay3),
            _gain_spec(wts["g_ffn"]),
            pl.BlockSpec((D_MODEL, D_MODEL), whole, pipeline_mode=once),
            pl.BlockSpec((D_MODEL, 2 * D_FF), whole, pipeline_mode=once),
            pl.BlockSpec((D_FF, D_MODEL), whole, pipeline_mode=once),
        ],
        out_specs=pl.BlockSpec((tm, D_MODEL), row),
        out_shape=jax.ShapeDtypeStruct((m, D_MODEL), F32),
        scratch_shapes=[pltpu.VMEM((tm, D_FF), BF16)],
        compiler_params=pltpu.CompilerParams(
            dimension_semantics=("arbitrary",), vmem_limit_bytes=VMEM_LIMIT),
        name="out_ffn",
    )(x, ret, attn, mod, wts["g_ffn"], *big_w)


def _rope_tables(n_lat):
    pos = np.arange(n_lat)
    row = (pos // GRID_W).astype(np.float32)[:, None]
    col = (pos % GRID_W).astype(np.float32)[:, None]
    lane = np.arange(LANES)[None, :]

    def tables(d, start, period):
        rel = (lane - start) % period
        active = np.logical_and(lane >= start, rel < d)
        half = d // 2
        nf = half // 2
        inv = np.float32(ROPE_BASE) ** (-((rel % nf).astype(np.float32)) / np.float32(nf))
        ang = (np.where(rel < half, row, col) * inv).astype(np.float32)
        cos, sin = np.cos(ang), np.sin(ang)
        first = (rel % half) < nf
        c = np.where(active, cos, 1.0)
        s = np.where(active, np.where(first, -sin, sin), 0.0)
        return tuple(jnp.asarray(t, F32) for t in (c, s))

    return tables(RET_DK, 0, RET_DK) + tables(MLA_ROPE, MLA_NOPE, LANES)


def _swap_partners(a, axis, d):
    shape = a.shape
    split = shape[:axis] + (shape[axis] // d, 2, 2, d // 4) + shape[axis + 1:]
    return jnp.flip(a.reshape(split), axis=axis + 2).reshape(shape)


def _prepare_weights(g_norm_mix, g_norm_ffn, w_in, g_q_a, w_q_b, g_kv_a, w_kv_b, g_qn, g_qr, g_kn,
                     g_kr):
    depth = w_in.shape[0]
    n_in = w_in.shape[2]
    w_in_t = jnp.pad(jnp.swapaxes(w_in, 1, 2).astype(BF16),
                     ((0, 0), (0, IN_COLS_PAD - n_in), (0, 0)))
    w_qb = w_q_b.reshape(depth, Q_RANK, MLA_HEADS, MLA_NOPE + MLA_ROPE)
    w_qb_sw = jnp.pad(_swap_partners(w_qb[..., MLA_NOPE:], 3, MLA_ROPE),
                      ((0, 0), (0, 0), (0, 0), (MLA_NOPE, HEAD_PAD - MLA_NOPE - MLA_ROPE)))
    w_qb = jnp.pad(w_qb, ((0, 0), (0, 0), (0, 0), (0, HEAD_PAD - MLA_NOPE - MLA_ROPE)))
    zeros32 = jnp.zeros((depth, MLA_ROPE), F32)
    zeros64 = jnp.zeros((depth, MLA_NOPE), F32)
    return {
        "g_mix": g_norm_mix,
        "g_ffn": g_norm_ffn,
        "w_in_t": w_in_t,
        "g_qa": g_q_a,
        "w_qb": w_qb.reshape(depth, Q_RANK, MLA_HEADS * HEAD_PAD).astype(BF16),
        "g_kva": g_kv_a,
        "w_kvb": w_kv_b.astype(BF16),
        "g_q": jnp.concatenate([g_qn, g_qr, zeros32], axis=-1) * Q_FOLD,
        "g_kn": jnp.concatenate([g_kn, jnp.ones((depth, MLA_V), F32)], axis=-1),
        "g_kr": jnp.pad(g_kr, ((0, 0), (0, LANES - MLA_ROPE))),
        "w_qb_sw": w_qb_sw.reshape(depth, Q_RANK, MLA_HEADS * HEAD_PAD).astype(BF16),
        "g_q_sw": jnp.concatenate([zeros64, _swap_partners(g_qr, 1, MLA_ROPE), zeros32],
                                  axis=-1) * Q_FOLD,
    }


def _blockdiag_states(s):
    b, l = s.shape[:2]
    s = s.reshape(b, l, N_PAIRS, 2, RET_DK, RET_DK)
    z = jnp.zeros_like(s[:, :, :, 0])
    top = jnp.concatenate([s[:, :, :, 0], z], axis=-1)
    bot = jnp.concatenate([z, s[:, :, :, 1]], axis=-1)
    return jnp.concatenate([top, bot], axis=-2)


def kernel(x_prompt, x_sample, cache_ckv, cache_krope, state_ret_fwd, state_ret_bwd, c, c_ctx,
           w_mod, b_mod, g_norm_mix, g_norm_ffn, w_in, g_q_a, w_q_b, g_kv_a, w_kv_b,
           g_qn, g_qr, g_kn, g_kr, ret_p_fwd, ret_p_bwd, g_ret_gn, b_ret_gn, w_o,
           w_ffn_in, w_ffn_out):
    batch, seq, _ = x_prompt.shape
    dec_batch, dec_seq, _ = x_sample.shape
    depth = w_in.shape[0]

    wts = _prepare_weights(g_norm_mix, g_norm_ffn, w_in, g_q_a, w_q_b, g_kv_a, w_kv_b, g_qn, g_qr,
                           g_kn, g_kr)
    conds = jnp.concatenate([c_ctx[None], c, jnp.zeros((8 - 1 - dec_batch, D_MODEL), F32)], axis=0)
    mod = _modulation(conds, w_mod, b_mod)
    dmask, dvec = _decay_tables(jnp.stack([ret_p_fwd, ret_p_bwd], axis=1))
    gn_g, gn_b = g_ret_gn, b_ret_gn

    rope_tabs = _rope_tables(dec_seq)
    krp_cache = jnp.pad(cache_krope, ((0, 0), (0, 0), (0, 0), (MLA_NOPE, LANES - MLA_NOPE - MLA_ROPE)))
    krp_cache = krp_cache.astype(BF16)
    s_f0 = _blockdiag_states(state_ret_fwd)
    s_b0 = _blockdiag_states(state_ret_bwd)

    def ctx_row(row):
        return 0

    def lat_row(row):
        return 1 + row // dec_seq

    x = x_prompt.reshape(batch * seq, D_MODEL)
    y = x_sample.reshape(dec_batch * dec_seq, D_MODEL)
    caches = (jax.ShapeDtypeStruct((batch, depth, seq, KV_RANK), F32),
              jax.ShapeDtypeStruct((batch, depth, MLA_ROPE, seq), F32))
    states = (jax.ShapeDtypeStruct((batch, depth, RET_HEADS, RET_DK, RET_DK), F32),) * 2
    for l in range(depth):
        pr = _proj(x, mod, l, ctx_row, wts, None, caches)
        caches = tuple(pr[7:9])
        casts = [(w_o, l, D_MODEL, 1), (w_ffn_in, l, 2 * D_FF, 1), (w_ffn_out, l, D_MODEL, 2)]
        mixed = _mix(pr, dmask, dvec, gn_g, gn_b, l, batch, seq, None, states, casts)
        ret, attn = mixed[:2]
        states = tuple(mixed[2:4])
        big_w = tuple(mixed[4:7])
        x = _out(x, ret, attn, mod, l, ctx_row, wts, big_w, OUT_TILE)

        pr = _proj(y, mod, l, lat_row, wts, rope_tabs, None)
        ret, attn = _mix(pr, dmask, dvec, gn_g, gn_b, l, dec_batch, dec_seq,
                         (cache_ckv, wts["w_kvb"], wts["g_kn"], krp_cache, s_f0, s_b0), None)
        y = _out(y, ret, attn, mod, l, lat_row, wts, big_w, LATENT_OUT_TILE)

    return (x.reshape(batch, seq, D_MODEL), y.reshape(dec_batch, dec_seq, D_MODEL),
            caches[0], jnp.swapaxes(caches[1], 2, 3), states[0], states[1])
```
